```python
import math
import numpy as np
import jax
import jax.numpy as jnp
from jax import lax

D_MODEL = 1024
BATCH = 16
SEQ = 2048
DEPTH = 2

HEAD_DIM = 64
FOX_HEADS = 8
DSA_HEADS = 8
DSA_KV_HEADS = 2
IDX_HEADS = 4
IDX_DIM = 64
DSA_TOPK = 256
HGRN_HEADS = 8
HGRN_KDIM = 64
HGRN_VDIM = 64
HGRN_CHUNK = 64
DIFF_HEADS = 4
DIFF_DIM = 64
Q_BLOCK = 128
ROPE_THETA = 10000.0
D_FF = 2816
N_EXPERTS = 8
TOP_K = 2
D_FF_EXPERT = 3584
NORM_EPS = 1e-6
NEG_INF = -1e30
F32 = jnp.float32

FOX_W = FOX_HEADS * HEAD_DIM
DSA_W = DSA_HEADS * HEAD_DIM
DSA_KV_W = DSA_KV_HEADS * HEAD_DIM
HGRN_KW = HGRN_HEADS * HGRN_KDIM
HGRN_VW = HGRN_HEADS * HGRN_VDIM
DIFF_QW = DIFF_HEADS * 2 * DIFF_DIM
DIFF_VW = DIFF_HEADS * 2 * DIFF_DIM
EVEN_SPLITS = (FOX_W, FOX_W, FOX_W, FOX_HEADS, DSA_W, DSA_KV_W, DSA_KV_W, IDX_HEADS * IDX_DIM, IDX_DIM, IDX_HEADS)
ODD_SPLITS = (HGRN_KW, HGRN_KW, HGRN_VW, HGRN_VW, DIFF_QW, DIFF_QW, DIFF_VW)
IN_EVEN = sum(EVEN_SPLITS)
IN_ODD = sum(ODD_SPLITS)
MIX_EVEN = FOX_W + DSA_W
MIX_ODD = HGRN_VW + DIFF_VW
N_EVEN = (DEPTH + 1) // 2
N_ODD = DEPTH // 2

kernel_name = "hybrid_fox_dsa_hgrn2_diff_moe_trunk"


def rms_norm(x, g):
    xf = x.astype(F32)
    y = xf * lax.rsqrt(jnp.mean(xf * xf, axis=-1, keepdims=True) + NORM_EPS)
    return (y * g.astype(F32)).astype(x.dtype)


def rope(x, pos):
    d = x.shape[-1]
    half = d // 2
    inv = ROPE_THETA ** (-jnp.arange(half, dtype=F32) * 2.0 / d)
    ang = pos.astype(F32)[:, None] * inv[None, :]
    cos = jnp.cos(ang)[:, None, :]
    sin = jnp.sin(ang)[:, None, :]
    xf = x.astype(F32)
    x1, x2 = xf[..., :half], xf[..., half:]
    return jnp.concatenate([x1 * cos - x2 * sin, x2 * cos + x1 * sin], axis=-1).astype(x.dtype)


def split_cols(h, sizes):
    return jnp.split(h, np.cumsum(sizes)[:-1].tolist(), axis=-1)


def to_blocks(a):
    B, S = a.shape[:2]
    return a.reshape((B, S // Q_BLOCK, Q_BLOCK) + a.shape[2:]).swapaxes(0, 1)


def from_blocks(a):
    a = a.swapaxes(0, 1)
    return a.reshape((a.shape[0], a.shape[1] * a.shape[2]) + a.shape[3:])


def fox_attention(q, k, v, log_f):
    B, S, H, d = q.shape
    nb = S // Q_BLOCK
    scale = d ** -0.5
    c = jnp.cumsum(log_f, axis=1)
    c_keys = c.transpose(0, 2, 1)[:, :, None, :]
    kpos = jnp.arange(S)

    def block(args):
        qi, ci, i = args
        qpos = i * Q_BLOCK + jnp.arange(Q_BLOCK)
        s = jnp.einsum('bqhd,bkhd->bhqk', qi, k, preferred_element_type=F32) * scale
        s = s + ci.transpose(0, 2, 1)[..., None] - c_keys
        s = jnp.where((kpos[None, :] <= qpos[:, None])[None, None], s, NEG_INF)
        p = jax.nn.softmax(s, axis=-1)
        return jnp.einsum('bhqk,bkhd->bqhd', p.astype(v.dtype), v)

    out = lax.map(block, (to_blocks(q), to_blocks(c), jnp.arange(nb)))
    return from_blocks(out)


def dsa_attention(q, k, v, q_idx, k_idx, w_idx):
    B, S, H, d = q.shape
    G = k.shape[2]
    R = H // G
    nb = S // Q_BLOCK
    n_sel = min(DSA_TOPK, S // 4)
    scale = d ** -0.5
    kpos = jnp.arange(S)

    def block(args):
        qi, qii, wi, i = args
        qpos = i * Q_BLOCK + jnp.arange(Q_BLOCK)
        logits = jnp.einsum('bqhd,bkd->bqhk', qii, k_idx, preferred_element_type=F32) * (IDX_DIM ** -0.5)
        score = jnp.einsum('bqh,bqhk->bqk', wi.astype(F32), jax.nn.relu(logits))
        score = jnp.where((kpos[None, :] <= qpos[:, None])[None], score, -jnp.inf)
        _, sel = lax.top_k(score, n_sel)
        kg = jax.vmap(lambda kk, ii: kk[ii])(k, sel)
        vg = jax.vmap(lambda vv, ii: vv[ii])(v, sel)
        qg = qi.reshape(B, Q_BLOCK, G, R, d)
        s = jnp.einsum('bqgrd,bqngd->bgrqn', qg, kg, preferred_element_type=F32) * scale
        valid = sel <= qpos[None, :, None]
        s = jnp.where(valid[:, None, None], s, NEG_INF)
        p = jax.nn.softmax(s, axis=-1)
        o = jnp.einsum('bgrqn,bqngd->bqgrd', p.astype(vg.dtype), vg)
        return o.reshape(B, Q_BLOCK, H, d)

    out = lax.map(block, (to_blocks(q), to_blocks(q_idx), to_blocks(w_idx), jnp.arange(nb)))
    return from_blocks(out)


def hgrn2_chunked(q, log_f, k, v):
    B, S, H, dk = q.shape
    dv = v.shape[-1]
    C = HGRN_CHUNK
    n = S // C

    def to_chunks(a):
        return a.reshape(B, n, C, H, a.shape[-1]).transpose(1, 0, 3, 2, 4)

    tri = jnp.tril(jnp.ones((C, C), dtype=bool))[None, None, :, :, None]

    def step(state, inp):
        qc, fc, kc, vc = inp
        b = jnp.cumsum(fc, axis=2)
        o_inter = jnp.einsum('bhtk,bhkv->bhtv', qc * jnp.exp(b), state)
        rel = jnp.exp(jnp.where(tri, b[:, :, :, None, :] - b[:, :, None, :, :], NEG_INF))
        scores = jnp.einsum('bhtk,bhtsk,bhsk->bhts', qc, rel, kc)
        o_intra = jnp.einsum('bhts,bhsv->bhtv', scores, vc)
        b_last = b[:, :, -1, :]
        new_state = state * jnp.exp(b_last)[..., None] + jnp.einsum(
            'bhsk,bhsv->bhkv', kc * jnp.exp(b_last[:, :, None, :] - b), vc)
        return new_state, o_inter + o_intra

    state0 = jnp.zeros((B, H, dk, dv), F32)
    _, out = lax.scan(step, state0, (to_chunks(q), to_chunks(log_f), to_chunks(k), to_chunks(v)))
    return out.transpose(1, 0, 3, 2, 4).reshape(B, S, H, dv)


def diff_attention(q, k, v, lam):
    B, S, H, _, d = q.shape
    nb = S // Q_BLOCK
    scale = d ** -0.5
    kpos = jnp.arange(S)

    def block(args):
        qi, i = args
        qpos = i * Q_BLOCK + jnp.arange(Q_BLOCK)
        s = jnp.einsum('bqhmd,bkhmd->bhmqk', qi, k, preferred_element_type=F32) * scale
        s = jnp.where((kpos[None, :] <= qpos[:, None])[None, None, None], s, NEG_INF)
        p = jax.nn.softmax(s, axis=-1)
        a = p[:, :, 0] - lam * p[:, :, 1]
        return jnp.einsum('bhqk,bkhv->bqhv', a.astype(v.dtype), v)

    out = lax.map(block, (to_blocks(q), jnp.arange(nb)))
    return from_blocks(out)


def even_mixer(h, w_in, b_f, w_out, pos):
    B, S, _ = h.shape
    fq, fk, fv, ff, dq, dk, dv, iq, ik, iw = split_cols(h @ w_in, EVEN_SPLITS)
    log_f = jax.nn.log_sigmoid((ff + b_f).astype(F32))
    fox_o = fox_attention(fq.reshape(B, S, FOX_HEADS, HEAD_DIM), fk.reshape(B, S, FOX_HEADS, HEAD_DIM),
                          fv.reshape(B, S, FOX_HEADS, HEAD_DIM), log_f)
    dsa_o = dsa_attention(
        rope(dq.reshape(B, S, DSA_HEADS, HEAD_DIM), pos),
        rope(dk.reshape(B, S, DSA_KV_HEADS, HEAD_DIM), pos),
        dv.reshape(B, S, DSA_KV_HEADS, HEAD_DIM),
        rope(iq.reshape(B, S, IDX_HEADS, IDX_DIM), pos),
        rope(ik[:, :, None, :], pos)[:, :, 0, :],
        iw * (IDX_HEADS ** -0.5))
    mixed = jnp.concatenate([fox_o.reshape(B, S, FOX_W), dsa_o.reshape(B, S, DSA_W)], axis=-1)
    return mixed @ w_out


def odd_mixer(h, w_in, lb, hgrn_g, lam_vecs, lam_init, diff_g, w_out, pos):
    B, S, _ = h.shape
    hq, hf, hi, hg, q, k, v = split_cols(h @ w_in, ODD_SPLITS)
    f = lb + (1.0 - lb) * jax.nn.sigmoid(hf.astype(F32))
    shp_k = (B, S, HGRN_HEADS, HGRN_KDIM)
    o_h = hgrn2_chunked(jax.nn.silu(hq.astype(F32)).reshape(shp_k), jnp.log(f).reshape(shp_k),
                        (1.0 - f).reshape(shp_k), hi.astype(F32).reshape(B, S, HGRN_HEADS, HGRN_VDIM))
    o_h = rms_norm(o_h, hgrn_g) * jax.nn.silu(hg.astype(F32)).reshape(B, S, HGRN_HEADS, HGRN_VDIM)
    o_h = o_h.astype(h.dtype).reshape(B, S, HGRN_VW)
    lv = lam_vecs.astype(F32)
    lam = jnp.exp(jnp.sum(lv[0] * lv[1])) - jnp.exp(jnp.sum(lv[2] * lv[3])) + lam_init
    qd = rope(q.reshape(B, S, 2 * DIFF_HEADS, DIFF_DIM), pos).reshape(B, S, DIFF_HEADS, 2, DIFF_DIM)
    kd = rope(k.reshape(B, S, 2 * DIFF_HEADS, DIFF_DIM), pos).reshape(B, S, DIFF_HEADS, 2, DIFF_DIM)
    vd = v.reshape(B, S, DIFF_HEADS, 2 * DIFF_DIM)
    o_d = rms_norm(diff_attention(qd, kd, vd, lam), diff_g) * (1.0 - lam_init)
    o_d = o_d.astype(h.dtype).reshape(B, S, DIFF_VW)
    mixed = jnp.concatenate([o_h, o_d], axis=-1)
    return mixed @ w_out


def swiglu(h, w_gate, w_up, w_down):
    return (jax.nn.silu(h @ w_gate) * (h @ w_up)) @ w_down


def moe_swiglu(h, w_router, w_gate, w_up, w_down):
    logits = jnp.einsum('bsd,de->bse', h, w_router, preferred_element_type=F32)
    top_val, top_idx = lax.top_k(logits, TOP_K)
    gates = jax.nn.softmax(top_val, axis=-1)
    dense_gate = jnp.sum(jax.nn.one_hot(top_idx, N_EXPERTS, dtype=F32) * gates[..., None], axis=-2)
    dense_gate = dense_gate.astype(h.dtype)
    y = jnp.zeros_like(h)
    for e in range(N_EXPERTS):
        y = y + dense_gate[..., e:e + 1] * swiglu(h, w_gate[e], w_up[e], w_down[e])
    return y


def hgrn_lower_bound(lb_logits, layer):
    p = jax.nn.softmax(lb_logits.astype(F32), axis=0)
    return jnp.cumsum(p, axis=0)[layer] - p[0]


def setup_inputs(seed: int = 0) -> dict:
    key = jax.random.key(seed)
    ks = jax.random.split(key, 24)

    def w(k, shape, fan_in):
        return jax.random.normal(k, shape, F32) * (fan_in ** -0.5)

    def gain(k, shape):
        return 1.0 + 0.02 * jax.random.normal(k, shape, F32)

    return {
        'x': jax.random.normal(ks[0], (BATCH, SEQ, D_MODEL), F32),
        'ln_mix_e': gain(ks[1], (N_EVEN, D_MODEL)),
        'w_in_e': w(ks[2], (N_EVEN, D_MODEL, IN_EVEN), D_MODEL),
        'b_fox_f': 0.1 * jax.random.normal(ks[3], (N_EVEN, FOX_HEADS), F32),
        'w_out_e': w(ks[4], (N_EVEN, MIX_EVEN, D_MODEL), MIX_EVEN),
        'ln_ffn_e': gain(ks[5], (N_EVEN, D_MODEL)),
        'w_ffn_gate': w(ks[6], (N_EVEN, D_MODEL, D_FF), D_MODEL),
        'w_ffn_up': w(ks[7], (N_EVEN, D_MODEL, D_FF), D_MODEL),
        'w_ffn_down': w(ks[8], (N_EVEN, D_FF, D_MODEL), D_FF),
        'ln_mix_o': gain(ks[9], (N_ODD, D_MODEL)),
        'w_in_o': w(ks[10], (N_ODD, D_MODEL, IN_ODD), D_MODEL),
        'hgrn_lb_logits': jax.random.normal(ks[11], (DEPTH, HGRN_KW), F32),
        'hgrn_norm_g': gain(ks[12], (N_ODD, HGRN_VDIM)),
        'diff_lambda': 0.1 * jax.random.normal(ks[13], (N_ODD, 4, DIFF_DIM), F32),
        'diff_norm_g': gain(ks[14], (N_ODD, 2 * DIFF_DIM)),
        'w_out_o': w(ks[15], (N_ODD, MIX_ODD, D_MODEL), MIX_ODD),
        'ln_ffn_o': gain(ks[16], (N_ODD, D_MODEL)),
        'w_router': w(ks[17], (N_ODD, D_MODEL, N_EXPERTS), D_MODEL),
        'w_exp_gate': w(ks[18], (N_ODD, N_EXPERTS, D_MODEL, D_FF_EXPERT), D_MODEL),
        'w_exp_up': w(ks[19], (N_ODD, N_EXPERTS, D_MODEL, D_FF_EXPERT), D_MODEL),
        'w_exp_down': w(ks[20], (N_ODD, N_EXPERTS, D_FF_EXPERT, D_MODEL), D_FF_EXPERT),
        'ln_final': gain(ks[21], (D_MODEL,)),
    }


def reference(x, ln_mix_e, w_in_e, b_fox_f, w_out_e, ln_ffn_e, w_ffn_gate, w_ffn_up, w_ffn_down,
              ln_mix_o, w_in_o, hgrn_lb_logits, hgrn_norm_g, diff_lambda, diff_norm_g, w_out_o,
              ln_ffn_o, w_router, w_exp_gate, w_exp_up, w_exp_down, ln_final):
    S = x.shape[1]
    pos = jnp.arange(S)
    for layer in range(DEPTH):
        j = layer // 2
        if layer % 2 == 0:
            x = x + even_mixer(rms_norm(x, ln_mix_e[j]), w_in_e[j], b_fox_f[j], w_out_e[j], pos)
            x = x + swiglu(rms_norm(x, ln_ffn_e[j]), w_ffn_gate[j], w_ffn_up[j], w_ffn_down[j])
        else:
            lb = hgrn_lower_bound(hgrn_lb_logits, layer)
            lam_init = 0.8 - 0.6 * math.exp(-0.3 * layer)
            x = x + odd_mixer(rms_norm(x, ln_mix_o[j]), w_in_o[j], lb, hgrn_norm_g[j], diff_lambda[j],
                              lam_init, diff_norm_g[j], w_out_o[j], pos)
            x = x + moe_swiglu(rms_norm(x, ln_ffn_o[j]), w_router[j], w_exp_gate[j], w_exp_up[j], w_exp_down[j])
    return rms_norm(x, ln_final)
```

```python
import functools
import math

import jax
import jax.numpy as jnp
from jax import lax
from jax.experimental import pallas as pl
from jax.experimental.pallas import tpu as pltpu

F32 = jnp.float32
BF16 = jnp.bfloat16

D_MODEL = 1024
HEAD_DIM = 64
FOX_HEADS = 8
DSA_HEADS = 8
DSA_KV_HEADS = 2
IDX_HEADS = 4
IDX_DIM = 64
DSA_TOPK = 256
HGRN_HEADS = 8
DIFF_HEADS = 4
DIFF_DIM = 64
ROPE_THETA = 10000.0
N_EXPERTS = 8
NORM_EPS = 1e-6
NEG_INF = -1e30

LANES = 128
SUBLANES = 8
VMEM_LIMIT_BYTES = 56 * 1024 * 1024

ROW_TILE = 512
ATTN_TILE = 256
HGRN_CHUNK = 256
SUB = 8

INT_MIN = -(2 ** 31)


def _params(*sem):
    return pltpu.CompilerParams(dimension_semantics=sem, vmem_limit_bytes=VMEM_LIMIT_BYTES)


def _dot(a, b):
    return jnp.dot(a, b, preferred_element_type=F32)


def _dot_nt(a, b):
    return lax.dot_general(a, b, (((1,), (1,)), ((), ())), preferred_element_type=F32)


def _dot_tn(a, b):
    return lax.dot_general(a, b, (((0,), (0,)), ((), ())), preferred_element_type=F32)


def _sigmoid(x):
    return 1.0 / (1.0 + jnp.exp(-x))


def _silu(x):
    return x * _sigmoid(x)


def _lane_half(shape):
    return lax.broadcasted_iota(jnp.int32, shape, len(shape) - 1) // HEAD_DIM


def _rms_proj_kernel(x_ref, g_ref, w_ref, cos_ref, sin_ref, o_ref, *, rope):
    x = x_ref[...]
    var = jnp.mean(x * x, axis=-1, keepdims=True)
    xn = (x * lax.rsqrt(var + NORM_EPS) * g_ref[...]).astype(BF16)
    y = _dot(xn, w_ref[...])
    if not rope:
        o_ref[...] = y.astype(o_ref.dtype)
        return
    cos = cos_ref[...]
    sin = sin_ref[...]
    first = (lax.broadcasted_iota(jnp.int32, cos.shape, 1) % HEAD_DIM) < (HEAD_DIM // 2)
    for c in range(y.shape[1] // LANES):
        yc = y[:, c * LANES:(c + 1) * LANES]
        partner = jnp.where(first, pltpu.roll(yc, LANES - HEAD_DIM // 2, 1), pltpu.roll(yc, HEAD_DIM // 2, 1))
        o_ref[:, c * LANES:(c + 1) * LANES] = (yc * cos + partner * sin).astype(o_ref.dtype)


def _rms_proj(x, g, w, cos, sin, *, seq, rope, out_dtype):
    t, d = x.shape
    n = w.shape[1]
    tm = ROW_TILE
    nseq = seq // tm
    return pl.pallas_call(
        functools.partial(_rms_proj_kernel, rope=rope),
        grid=(t // tm,),
        in_specs=[
            pl.BlockSpec((tm, d), lambda i: (i, 0)),
            pl.BlockSpec((1, d), lambda i: (0, 0)),
            pl.BlockSpec((d, n), lambda i: (0, 0)),
            pl.BlockSpec((tm, LANES), lambda i: (i % nseq, 0)),
            pl.BlockSpec((tm, LANES), lambda i: (i % nseq, 0)),
        ],
        out_specs=pl.BlockSpec((tm, n), lambda i: (i, 0)),
        out_shape=jax.ShapeDtypeStruct((t, n), out_dtype),
        compiler_params=_params("parallel"),
        name="rms_proj_rope" if rope else "rms_proj",
    )(x, g, w, cos, sin)


def _out_proj_kernel(a1_ref, a2_ref, w1_ref, w2_ref, r_ref, o_ref):
    o_ref[...] = r_ref[...] + _dot(a1_ref[...], w1_ref[...]) + _dot(a2_ref[...], w2_ref[...])


def _out_proj(a1, a2, w1, w2, res):
    t, d = res.shape
    k1, k2 = a1.shape[1], a2.shape[1]
    tm = ROW_TILE
    return pl.pallas_call(
        _out_proj_kernel,
        grid=(t // tm,),
        in_specs=[
            pl.BlockSpec((tm, k1), lambda i: (i, 0)),
            pl.BlockSpec((tm, k2), lambda i: (i, 0)),
            pl.BlockSpec((k1, d), lambda i: (0, 0)),
            pl.BlockSpec((k2, d), lambda i: (0, 0)),
            pl.BlockSpec((tm, d), lambda i: (i, 0)),
        ],
        out_specs=pl.BlockSpec((tm, d), lambda i: (i, 0)),
        out_shape=jax.ShapeDtypeStruct((t, d), F32),
        compiler_params=_params("parallel"),
        name="out_proj",
    )(a1, a2, w1, w2, res)


def _cumsum_rows(x):
    rows = x.shape[0]
    row = lax.broadcasted_iota(jnp.int32, x.shape, 0)
    k = 1
    while k < rows:
        x = x + jnp.where(row >= k, pltpu.roll(x, k, 0), 0.0)
        k *= 2
    return x


def _fox_gate_kernel(z_ref, b_ref, ccol_ref, crow_ref, *, tk):
    z = z_ref[...] + b_ref[...]
    logf = jnp.minimum(z, 0.0) - jnp.log(1.0 + jnp.exp(-jnp.abs(z)))
    c = _cumsum_rows(logf)
    ccol_ref[...] = c
    for j in range(c.shape[0] // tk):
        ct = c[j * tk:(j + 1) * tk, :].T
        crow_ref[0, j] = ct[:SUBLANES, :]


def _fox_gate(z, b_pad, *, batch, seq):
    tk = ATTN_TILE
    nk = seq // tk
    return pl.pallas_call(
        functools.partial(_fox_gate_kernel, tk=tk),
        grid=(batch,),
        in_specs=[
            pl.BlockSpec((seq, LANES), lambda b: (b, 0)),
            pl.BlockSpec((1, LANES), lambda b: (0, 0)),
        ],
        out_specs=[
            pl.BlockSpec((seq, LANES), lambda b: (b, 0)),
            pl.BlockSpec((1, nk, SUBLANES, tk), lambda b: (b, 0, 0, 0)),
        ],
        out_shape=[
            jax.ShapeDtypeStruct((batch * seq, LANES), F32),
            jax.ShapeDtypeStruct((batch, nk, SUBLANES, tk), F32),
        ],
        compiler_params=_params("parallel"),
        name="fox_gate",
    )(z, b_pad)


def _softmax_step(carry, s, vb):
    m, l, acc = carry
    m_new = jnp.maximum(m, jnp.max(s, axis=-1, keepdims=True))
    alpha = jnp.exp(m - m_new)
    p = jnp.exp(s - m_new)
    l = alpha * l + jnp.sum(p, axis=-1, keepdims=True)
    acc = alpha * acc + _dot(p.astype(BF16), vb)
    return m_new, l, acc


def _softmax_init(tq):
    return (jnp.full((tq, 1), NEG_INF, F32), jnp.zeros((tq, 1), F32), jnp.zeros((tq, LANES), F32))


def _causal_mask(tq, tk):
    return lax.broadcasted_iota(jnp.int32, (tq, tk), 1) <= lax.broadcasted_iota(jnp.int32, (tq, tk), 0)


def _fox_kernel(q_ref, k_ref, v_ref, cc_ref, cr_ref, o_ref, *, tq, scale):
    hp = pl.program_id(1)
    i = pl.program_id(2)
    q = q_ref[...]
    half = _lane_half((1, LANES))
    cc = cc_ref[...]
    lane = lax.broadcasted_iota(jnp.int32, cc.shape, 1)
    causal = _causal_mask(tq, tq)
    outs = []
    for e in range(2):
        h = 2 * hp + e
        qe = jnp.where(half == e, q, jnp.zeros_like(q))
        cq = jnp.sum(jnp.where(lane == h, cc, 0.0), axis=-1, keepdims=True)

        def logits(j, qe=qe, cq=cq, h=h):
            start = pl.multiple_of(j * tq, tq)
            kb = k_ref[pl.ds(start, tq), :]
            vb = v_ref[pl.ds(start, tq), :]
            ck = cr_ref[0, j, pl.ds(h, 1), :]
            return _dot_nt(qe, kb) * scale + cq - ck, vb

        def body(j, carry):
            s, vb = logits(j)
            return _softmax_step(carry, s, vb)

        carry = lax.fori_loop(0, i, body, _softmax_init(tq))
        s, vb = logits(i)
        m, l, acc = _softmax_step(carry, jnp.where(causal, s, NEG_INF), vb)
        outs.append(acc / l)
    o_ref[...] = jnp.where(half == 0, outs[0], outs[1]).astype(o_ref.dtype)


def _fox_attention(p1, ccol, crow, *, batch, seq):
    tq = ATTN_TILE
    nq = seq // tq
    ng = FOX_HEADS // 2
    return pl.pallas_call(
        functools.partial(_fox_kernel, tq=tq, scale=HEAD_DIM ** -0.5),
        grid=(batch, ng, nq),
        in_specs=[
            pl.BlockSpec((tq, LANES), lambda b, g, i: (b * nq + i, g)),
            pl.BlockSpec((seq, LANES), lambda b, g, i: (b, ng + g)),
            pl.BlockSpec((seq, LANES), lambda b, g, i: (b, 2 * ng + g)),
            pl.BlockSpec((tq, LANES), lambda b, g, i: (b * nq + i, 0)),
            pl.BlockSpec((1, nq, SUBLANES, tq), lambda b, g, i: (b, 0, 0, 0)),
        ],
        out_specs=pl.BlockSpec((tq, LANES), lambda b, g, i: (b * nq + i, g)),
        out_shape=jax.ShapeDtypeStruct((batch * seq, ng * LANES), BF16),
        compiler_params=_params("parallel", "parallel", "arbitrary"),
        name="fox_attention",
    )(p1, p1, p1, ccol, crow)


def _diff_kernel(q_ref, k_ref, v_ref, lam_ref, g_ref, o_ref, *, tq, scale, lam_init):
    i = pl.program_id(2)
    q = q_ref[...]
    half = _lane_half((1, LANES))
    causal = _causal_mask(tq, tq)
    lv = lam_ref[...]
    lam = (jnp.exp(jnp.sum(lv[0:1, :] * lv[1:2, :], axis=-1, keepdims=True))
           - jnp.exp(jnp.sum(lv[2:3, :] * lv[3:4, :], axis=-1, keepdims=True)) + lam_init)
    outs = []
    for e in range(2):
        qe = jnp.where(half == e, q, jnp.zeros_like(q))

        def logits(j, qe=qe):
            start = pl.multiple_of(j * tq, tq)
            kb = k_ref[pl.ds(start, tq), :]
            vb = v_ref[pl.ds(start, tq), :]
            return _dot_nt(qe, kb) * scale, vb

        def body(j, carry):
            s, vb = logits(j)
            return _softmax_step(carry, s, vb)

        carry = lax.fori_loop(0, i, body, _softmax_init(tq))
        s, vb = logits(i)
        m, l, acc = _softmax_step(carry, jnp.where(causal, s, NEG_INF), vb)
        outs.append(acc / l)
    o = outs[0] - lam * outs[1]
    var = jnp.mean(o * o, axis=-1, keepdims=True)
    o = o * lax.rsqrt(var + NORM_EPS) * g_ref[...] * (1.0 - lam_init)
    o_ref[...] = o.astype(o_ref.dtype)


def _diff_attention(pqk, pv, lam_vecs, g, *, batch, seq, lam_init):
    tq = ATTN_TILE
    nq = seq // tq
    ng = DIFF_HEADS
    return pl.pallas_call(
        functools.partial(_diff_kernel, tq=tq, scale=DIFF_DIM ** -0.5, lam_init=lam_init),
        grid=(batch, ng, nq),
        in_specs=[
            pl.BlockSpec((tq, LANES), lambda b, g, i: (b * nq + i, g)),
            pl.BlockSpec((seq, LANES), lambda b, g, i: (b, ng + g)),
            pl.BlockSpec((seq, LANES), lambda b, g, i: (b, g)),
            pl.BlockSpec(lam_vecs.shape, lambda b, g, i: (0, 0)),
            pl.BlockSpec((1, LANES), lambda b, g, i: (0, 0)),
        ],
        out_specs=pl.BlockSpec((tq, LANES), lambda b, g, i: (b * nq + i, g)),
        out_shape=jax.ShapeDtypeStruct((batch * seq, ng * LANES), BF16),
        compiler_params=_params("parallel", "parallel", "arbitrary"),
        name="diff_attention",
    )(pqk, pqk, pv, lam_vecs, g)


def _sortable_key(x):
    bits = pltpu.bitcast(x, jnp.int32)
    return jnp.where(bits < 0, bits ^ jnp.int32(0x7FFFFFFF), bits)


def _dsa_kernel(q_ref, kd_ref, iq_ref, ik_ref, vd_ref, z_ref, o_ref, key_ref, madd_ref,
                *, tq, n_sel, scale, idx_scale, w_scale):
    i = pl.program_id(1)
    nk = key_ref.shape[0]
    half = _lane_half((1, LANES))
    z = z_ref[...]
    row = lax.broadcasted_iota(jnp.int32, (tq, tq), 0)
    col = lax.broadcasted_iota(jnp.int32, (tq, tq), 1)
    row_g = row + i * tq

    iq = iq_ref[...]
    iq_h = []
    w_h = []
    for h in range(IDX_HEADS):
        grp = iq[:, (h // 2) * LANES:(h // 2 + 1) * LANES]
        iq_h.append(jnp.where(half == h % 2, grp, jnp.zeros_like(grp)))
        w_h.append(z[:, SUBLANES + h:SUBLANES + h + 1] * w_scale)
    neg_key = _sortable_key(jnp.full((tq, tq), -jnp.inf, F32))
    for j in range(nk):
        key_ref[j] = neg_key

    def score_body(j, _):
        start = pl.multiple_of(j * tq, tq)
        ikb = ik_ref[pl.ds(start, tq), :]
        sc = jnp.zeros((tq, tq), F32)
        for h in range(IDX_HEADS):
            sc = sc + w_h[h] * jnp.maximum(_dot_nt(iq_h[h], ikb) * idx_scale, 0.0)
        sc = jnp.where(col + j * tq <= row_g, sc, -jnp.inf)
        key_ref[j] = _sortable_key(sc)
        return 0

    lax.fori_loop(0, i + 1, score_body, 0)

    def count(pred):
        return jnp.sum(jnp.sum(jnp.where(pred, 1.0, 0.0), axis=0), axis=-1, keepdims=True)

    thr0 = jnp.where(count(key_ref[...] >= 0) >= n_sel, 0, INT_MIN).astype(jnp.int32)

    def bit_body(it, thr):
        cand = thr | jnp.left_shift(jnp.int32(1), 30 - it)
        return jnp.where(count(key_ref[...] >= cand[None]) >= n_sel, cand, thr)

    thr = lax.fori_loop(0, 31, bit_body, thr0)
    need = n_sel - count(key_ref[...] > thr[None])

    tri = jnp.where(row < col, 1.0, 0.0).astype(BF16)

    def mask_body(j, off):
        kj = key_ref[j]
        eq = kj == thr
        eqf = jnp.where(eq, 1.0, 0.0)
        rank = _dot(eqf.astype(BF16), tri) + off
        take = jnp.where(kj > thr, 1.0, jnp.where(eq, jnp.where(rank < need, 1.0, 0.0), 0.0))
        valid = col + j * tq <= row_g
        madd_ref[j] = jnp.where(valid, jnp.where(take > 0.0, 0.0, NEG_INF), NEG_INF)
        return off + jnp.sum(eqf, axis=-1, keepdims=True)

    lax.fori_loop(0, i + 1, mask_body, jnp.zeros((tq, 1), F32))

    q = q_ref[...]
    for p in range(DSA_HEADS // 2):
        qg = q[:, p * LANES:(p + 1) * LANES]
        g = p // (DSA_HEADS // DSA_KV_HEADS // 2)
        outs = []
        for e in range(2):
            qe = jnp.where(half == e, qg, jnp.zeros_like(qg))

            def body(j, carry, qe=qe, g=g):
                start = pl.multiple_of(j * tq, tq)
                kb = kd_ref[pl.ds(start, tq), g * LANES:(g + 1) * LANES]
                vb = vd_ref[pl.ds(start, tq), g * LANES:(g + 1) * LANES]
                s = _dot_nt(qe, kb) * scale + madd_ref[j]
                return _softmax_step(carry, s, vb)

            m, l, acc = lax.fori_loop(0, i + 1, body, _softmax_init(tq))
            outs.append(acc / l)
        o_ref[:, p * LANES:(p + 1) * LANES] = jnp.where(half == 0, outs[0], outs[1]).astype(o_ref.dtype)


def _dsa_attention(p2, p1, z, *, batch, seq):
    tq = ATTN_TILE
    nq = seq // tq
    n_sel = min(DSA_TOPK, seq // 4)
    qw = DSA_HEADS * HEAD_DIM
    return pl.pallas_call(
        functools.partial(_dsa_kernel, tq=tq, n_sel=float(n_sel), scale=HEAD_DIM ** -0.5,
                          idx_scale=IDX_DIM ** -0.5, w_scale=IDX_HEADS ** -0.5),
        grid=(batch, nq),
        in_specs=[
            pl.BlockSpec((tq, qw), lambda b, i: (b * nq + i, 0)),
            pl.BlockSpec((seq, 2 * LANES), lambda b, i: (b, 2)),
            pl.BlockSpec((tq, 2 * LANES), lambda b, i: (b * nq + i, 3)),
            pl.BlockSpec((seq, LANES), lambda b, i: (b, 8)),
            pl.BlockSpec((seq, 2 * LANES), lambda b, i: (b, 6)),
            pl.BlockSpec((tq, LANES), lambda b, i: (b * nq + i, 0)),
        ],
        out_specs=pl.BlockSpec((tq, qw), lambda b, i: (b * nq + i, 0)),
        out_shape=jax.ShapeDtypeStruct((batch * seq, qw), BF16),
        scratch_shapes=[pltpu.VMEM((nq, tq, tq), jnp.int32), pltpu.VMEM((nq, tq, tq), F32)],
        compiler_params=_params("parallel", "arbitrary"),
        name="dsa_attention",
    )(p2, p2, p2, p2, p1, z)


def _hgrn_kernel(q_ref, f_ref, i_ref, g_ref, lbl_ref, gn_ref, o_ref, b_scr, *, chunk, layer):
    seq = q_ref.shape[0]
    half = _lane_half((1, LANES))
    logits = lbl_ref[...]
    pe = jnp.exp(logits - jnp.max(logits, axis=0, keepdims=True))
    prob = pe / jnp.sum(pe, axis=0, keepdims=True)
    csum = prob[0:1, :]
    for d in range(1, layer + 1):
        csum = csum + prob[d:d + 1, :]
    lb = csum - prob[0:1, :]

    rowi = lax.broadcasted_iota(jnp.int32, (chunk, 1), 0)
    xor_rc = (lax.broadcasted_iota(jnp.int32, (chunk, chunk), 0)
              ^ lax.broadcasted_iota(jnp.int32, (chunk, chunk), 1))
    same_head = (lax.broadcasted_iota(jnp.int32, (LANES, LANES), 0) // HEAD_DIM
                 == lax.broadcasted_iota(jnp.int32, (LANES, LANES), 1) // HEAD_DIM)
    ones_bd = jnp.where(same_head, 1.0, 0.0).astype(BF16)
    gn = gn_ref[...]

    def chunk_body(c, state_t):
        r0 = pl.multiple_of(c * chunk, chunk)
        q = _silu(q_ref[pl.ds(r0, chunk), :])
        f = lb + (1.0 - lb) * _sigmoid(f_ref[pl.ds(r0, chunk), :])
        kc = 1.0 - f
        v = i_ref[pl.ds(r0, chunk), :]
        b = _cumsum_rows(jnp.log(f))
        b_scr[...] = b
        b_last = b_scr[chunk - 1:chunk, :]

        o = _dot_nt((q * jnp.exp(b)).astype(BF16), state_t.astype(BF16))
        k2 = (kc * jnp.exp(b_last - b)).astype(BF16)
        upd = _dot_tn(v.astype(BF16), k2)
        new_state = state_t * jnp.exp(b_last) + jnp.where(same_head, upd, 0.0)

        s_tot = [jnp.zeros((chunk, chunk), F32), jnp.zeros((chunk, chunk), F32)]
        m = chunk
        while m > SUB:
            hm = m // 2
            pieces = [jnp.broadcast_to(b_scr[blk * m + hm - 1:blk * m + hm, :], (m, LANES))
                      for blk in range(chunk // m)]
            ref = pieces[0] if len(pieces) == 1 else jnp.concatenate(pieces, axis=0)
            right = (rowi % m) >= hm
            qm = jnp.where(right, q * jnp.exp(jnp.where(right, b - ref, 0.0)), 0.0).astype(BF16)
            km = jnp.where(right, 0.0, kc * jnp.exp(jnp.where(right, 0.0, ref - b))).astype(BF16)
            for e in range(2):
                s = _dot_nt(jnp.where(half == e, qm, jnp.zeros_like(qm)), km)
                s_tot[e] = s_tot[e] + (s if m == chunk else jnp.where(xor_rc < m, s, 0.0))
            m = hm
        vb = v.astype(BF16)
        for e in range(2):
            o = o + _dot(s_tot[e].astype(BF16), jnp.where(half == e, vb, jnp.zeros_like(vb)))

        for d in range(SUB):
            ok = (rowi % SUB) >= d
            bs = b if d == 0 else pltpu.roll(b, d, 0)
            ks = kc if d == 0 else pltpu.roll(kc, d, 0)
            vs = v if d == 0 else pltpu.roll(v, d, 0)
            x = jnp.where(ok, q * ks * jnp.exp(jnp.where(ok, b - bs, 0.0)), 0.0)
            o = o + _dot(x.astype(BF16), ones_bd) * vs

        sq = o * o
        s0 = jnp.sum(jnp.where(half == 0, sq, 0.0), axis=-1, keepdims=True)
        s1 = jnp.sum(jnp.where(half == 1, sq, 0.0), axis=-1, keepdims=True)
        var = jnp.where(half == 0, s0, s1) * (1.0 / HEAD_DIM)
        y = o * lax.rsqrt(var + NORM_EPS) * gn * _silu(g_ref[pl.ds(r0, chunk), :])
        o_ref[pl.ds(r0, chunk), :] = y.astype(o_ref.dtype)
        return new_state

    lax.fori_loop(0, seq // chunk, chunk_body, jnp.zeros((LANES, LANES), F32))


def _hgrn(hproj, lb_logits, gn, *, batch, seq, layer):
    ng = HGRN_HEADS // 2
    chunk = min(HGRN_CHUNK, seq)
    spec = lambda off: pl.BlockSpec((seq, LANES), lambda b, g: (b, off * ng + g))
    return pl.pallas_call(
        functools.partial(_hgrn_kernel, chunk=chunk, layer=layer),
        grid=(batch, ng),
        in_specs=[spec(0), spec(1), spec(2), spec(3),
                  pl.BlockSpec((lb_logits.shape[0], LANES), lambda b, g: (0, g)),
                  pl.BlockSpec((1, LANES), lambda b, g: (0, 0))],
        out_specs=pl.BlockSpec((seq, LANES), lambda b, g: (b, g)),
        out_shape=jax.ShapeDtypeStruct((batch * seq, ng * LANES), BF16),
        scratch_shapes=[pltpu.VMEM((chunk, LANES), F32)],
        compiler_params=_params("parallel", "parallel"),
        name="hgrn2",
    )(hproj, hproj, hproj, hproj, lb_logits, gn)


def _router_kernel(x_ref, g_ref, w_ref, o_ref):
    x = x_ref[...]
    var = jnp.mean(x * x, axis=-1, keepdims=True)
    xn = x * lax.rsqrt(var + NORM_EPS) * g_ref[...]
    logits = jnp.dot(xn, w_ref[...], precision=lax.Precision.HIGHEST, preferred_element_type=F32)
    lane = lax.broadcasted_iota(jnp.int32, logits.shape, 1)
    lg = jnp.where(lane < N_EXPERTS, logits, -jnp.inf)
    m1 = jnp.max(lg, axis=-1, keepdims=True)
    i1 = jnp.min(jnp.where(lg == m1, lane, LANES), axis=-1, keepdims=True)
    lg2 = jnp.where(lane == i1, -jnp.inf, lg)
    m2 = jnp.max(lg2, axis=-1, keepdims=True)
    i2 = jnp.min(jnp.where(lg2 == m2, lane, LANES), axis=-1, keepdims=True)
    t = jnp.exp(m2 - m1)
    den = 1.0 + t
    o_ref[...] = jnp.where(lane == i1, 1.0 / den, 0.0) + jnp.where(lane == i2, t / den, 0.0)


def _router(x, g, w_pad):
    t, d = x.shape
    tm = ROW_TILE
    return pl.pallas_call(
        _router_kernel,
        grid=(t // tm,),
        in_specs=[pl.BlockSpec((tm, d), lambda i: (i, 0)),
                  pl.BlockSpec((1, d), lambda i: (0, 0)),
                  pl.BlockSpec((d, LANES), lambda i: (0, 0))],
        out_specs=pl.BlockSpec((tm, LANES), lambda i: (i, 0)),
        out_shape=jax.ShapeDtypeStruct((t, LANES), F32),
        compiler_params=_params("parallel"),
        name="router",
    )(x, g, w_pad)


def _ffn_kernel(x_ref, g_ref, gate_ref, wg_ref, wu_ref, wd_ref, gf_ref, o_ref, xn_ref, acc_ref,
                *, gated, final_norm):
    e = pl.program_id(1)
    f = pl.program_id(2)
    last = jnp.logical_and(e == pl.num_programs(1) - 1, f == pl.num_programs(2) - 1)

    @pl.when(jnp.logical_and(e == 0, f == 0))
    def _():
        x = x_ref[...]
        var = jnp.mean(x * x, axis=-1, keepdims=True)
        xn_ref[...] = (x * lax.rsqrt(var + NORM_EPS) * g_ref[...]).astype(BF16)
        acc_ref[...] = jnp.zeros_like(acc_ref)

    xn = xn_ref[...]
    mid = (_silu(_dot(xn, wg_ref[0])) * _dot(xn, wu_ref[0])).astype(BF16)
    y = _dot(mid, wd_ref[0])
    if gated:
        gates = gate_ref[...]
        lane = lax.broadcasted_iota(jnp.int32, gates.shape, 1)
        y = y * jnp.sum(jnp.where(lane == e, gates, 0.0), axis=-1, keepdims=True)
    acc_ref[...] += y

    @pl.when(last)
    def _():
        out = x_ref[...] + acc_ref[...]
        if final_norm:
            var = jnp.mean(out * out, axis=-1, keepdims=True)
            out = out * lax.rsqrt(var + NORM_EPS) * gf_ref[...]
        o_ref[...] = out


def _ffn(x, g, gates, wg, wu, wd, gf, *, tf, gated, final_norm):
    t, d = x.shape
    ne, _, ff = wg.shape
    tm = ROW_TILE
    return pl.pallas_call(
        functools.partial(_ffn_kernel, gated=gated, final_norm=final_norm),
        grid=(t // tm, ne, ff // tf),
        in_specs=[
            pl.BlockSpec((tm, d), lambda i, e, f: (i, 0)),
            pl.BlockSpec((1, d), lambda i, e, f: (0, 0)),
            pl.BlockSpec((tm, LANES), lambda i, e, f: (i, 0)),
            pl.BlockSpec((1, d, tf), lambda i, e, f: (e, 0, f)),
            pl.BlockSpec((1, d, tf), lambda i, e, f: (e, 0, f)),
            pl.BlockSpec((1, tf, d), lambda i, e, f: (e, f, 0)),
            pl.BlockSpec((1, d), lambda i, e, f: (0, 0)),
        ],
        out_specs=pl.BlockSpec((tm, d), lambda i, e, f: (i, 0)),
        out_shape=jax.ShapeDtypeStruct((t, d), F32),
        scratch_shapes=[pltpu.VMEM((tm, d), BF16), pltpu.VMEM((tm, d), F32)],
        compiler_params=_params("parallel", "arbitrary", "arbitrary"),
        name="moe_ffn" if gated else "ffn",
    )(x, g, gates, wg, wu, wd, gf)


def _rope_tables(seq):
    half = HEAD_DIM // 2
    inv = ROPE_THETA ** (-jnp.arange(half, dtype=F32) * 2.0 / HEAD_DIM)
    ang = jnp.arange(seq, dtype=F32)[:, None] * inv[None, :]
    cos = jnp.cos(ang)
    sin = jnp.sin(ang)
    cos_t = jnp.concatenate([cos, cos, cos, cos], axis=-1)
    sin_t = jnp.concatenate([-sin, sin, -sin, sin], axis=-1)
    return cos_t, sin_t


def _dup_heads(w):
    d, n = w.shape
    w = w.reshape(d, n // HEAD_DIM, 1, HEAD_DIM)
    return jnp.broadcast_to(w, (d, n // HEAD_DIM, 2, HEAD_DIM)).reshape(d, 2 * n)


def kernel(x, ln_mix_e, w_in_e, b_fox_f, w_out_e, ln_ffn_e, w_ffn_gate, w_ffn_up, w_ffn_down,
           ln_mix_o, w_in_o, hgrn_lb_logits, hgrn_norm_g, diff_lambda, diff_norm_g, w_out_o,
           ln_ffn_o, w_router, w_exp_gate, w_exp_up, w_exp_down, ln_final):
    batch, seq, d = x.shape
    t = batch * seq
    depth = ln_mix_e.shape[0] + ln_mix_o.shape[0]
    xs = x.reshape(t, d)
    cos_t, sin_t = _rope_tables(seq)
    row = lambda v: v.reshape(1, -1)
    fw = FOX_HEADS * HEAD_DIM
    dw = DSA_HEADS * HEAD_DIM
    kvw = DSA_KV_HEADS * HEAD_DIM
    hw = HGRN_HEADS * HEAD_DIM
    qw = DIFF_HEADS * 2 * DIFF_DIM

    for layer in range(depth):
        j = layer // 2
        if layer % 2 == 0:
            w = w_in_e[j]
            o = 0
            fq, fk, fv = w[:, o:o + fw], w[:, o + fw:o + 2 * fw], w[:, o + 2 * fw:o + 3 * fw]
            o += 3 * fw
            ffw = w[:, o:o + FOX_HEADS]
            o += FOX_HEADS
            dq = w[:, o:o + dw]
            o += dw
            dk = w[:, o:o + kvw]
            o += kvw
            dv = w[:, o:o + kvw]
            o += kvw
            iqw = w[:, o:o + IDX_HEADS * IDX_DIM]
            o += IDX_HEADS * IDX_DIM
            ikw = w[:, o:o + IDX_DIM]
            o += IDX_DIM
            iww = w[:, o:o + IDX_HEADS]
            w1 = jnp.concatenate([fq, fk, fv, _dup_heads(dv)], axis=1).astype(BF16)
            w2 = jnp.concatenate([dq, _dup_heads(dk), iqw, _dup_heads(ikw)], axis=1).astype(BF16)
            w3 = jnp.concatenate([ffw, iww, jnp.zeros((d, LANES - FOX_HEADS - IDX_HEADS), F32)], axis=1).astype(BF16)
            g = row(ln_mix_e[j])
            p1 = _rms_proj(xs, g, w1, cos_t, sin_t, seq=seq, rope=False, out_dtype=BF16)
            p2 = _rms_proj(xs, g, w2, cos_t, sin_t, seq=seq, rope=True, out_dtype=BF16)
            z = _rms_proj(xs, g, w3, cos_t, sin_t, seq=seq, rope=False, out_dtype=F32)
            b_pad = jnp.concatenate([b_fox_f[j], jnp.zeros((LANES - FOX_HEADS,), F32)]).reshape(1, LANES)
            ccol, crow = _fox_gate(z, b_pad, batch=batch, seq=seq)
            fox_o = _fox_attention(p1, ccol, crow, batch=batch, seq=seq)
            dsa_o = _dsa_attention(p2, p1, z, batch=batch, seq=seq)
            wo = w_out_e[j].astype(BF16)
            xs = _out_proj(fox_o, dsa_o, wo[:fw], wo[fw:], xs)
            xs = _ffn(xs, row(ln_ffn_e[j]), z, w_ffn_gate[j][None].astype(BF16), w_ffn_up[j][None].astype(BF16),
                      w_ffn_down[j][None].astype(BF16), row(ln_final), tf=1408, gated=False,
                      final_norm=(layer == depth - 1))
        else:
            w = w_in_o[j]
            lam_init = 0.8 - 0.6 * math.exp(-0.3 * layer)
            g = row(ln_mix_o[j])
            hp = _rms_proj(xs, g, w[:, :4 * hw].astype(BF16), cos_t, sin_t, seq=seq, rope=False, out_dtype=F32)
            pqk = _rms_proj(xs, g, w[:, 4 * hw:4 * hw + 2 * qw].astype(BF16), cos_t, sin_t, seq=seq, rope=True,
                            out_dtype=BF16)
            pv = _rms_proj(xs, g, w[:, 4 * hw + 2 * qw:].astype(BF16), cos_t, sin_t, seq=seq, rope=False,
                           out_dtype=BF16)
            gn = jnp.concatenate([hgrn_norm_g[j], hgrn_norm_g[j]]).reshape(1, LANES)
            o_h = _hgrn(hp, hgrn_lb_logits, gn, batch=batch, seq=seq, layer=layer)
            o_d = _diff_attention(pqk, pv, diff_lambda[j], row(diff_norm_g[j]), batch=batch, seq=seq,
                                  lam_init=lam_init)
            wo = w_out_o[j].astype(BF16)
            xs = _out_proj(o_h, o_d, wo[:hw], wo[hw:], xs)
            wr = jnp.concatenate([w_router[j], jnp.zeros((d, LANES - N_EXPERTS), F32)], axis=1)
            gates = _router(xs, row(ln_ffn_o[j]), wr)
            xs = _ffn(xs, row(ln_ffn_o[j]), gates, w_exp_gate[j].astype(BF16), w_exp_up[j].astype(BF16),
                      w_exp_down[j].astype(BF16), row(ln_final), tf=896, gated=True,
                      final_norm=(layer == depth - 1))
    if depth % 2 == 1:
        pass
    return xs.reshape(batch, seq, d)
```

```python
import functools
import math

import jax
import jax.numpy as jnp
from jax import lax
from jax.experimental import pallas as pl
from jax.experimental.pallas import tpu as pltpu

F32 = jnp.float32
BF16 = jnp.bfloat16

D_MODEL = 1024
HEAD_DIM = 64
FOX_HEADS = 8
DSA_HEADS = 8
DSA_KV_HEADS = 2
IDX_HEADS = 4
IDX_DIM = 64
DSA_TOPK = 256
HGRN_HEADS = 8
DIFF_HEADS = 4
DIFF_DIM = 64
ROPE_THETA = 10000.0
N_EXPERTS = 8
NORM_EPS = 1e-6
NEG_INF = -1e30

LANES = 128
SUBLANES = 8
VMEM_LIMIT_BYTES = 56 * 1024 * 1024

ROW_TILE = 512
ATTN_TILE = 256
HGRN_CHUNK = 256
SUB = 8
MOE_ROW_TILE = 512
MOE_SUB = 128
MOE_TOKEN_BLOCK = 512

INT_MIN = -(2 ** 31)


def _params(*sem):
    return pltpu.CompilerParams(dimension_semantics=sem, vmem_limit_bytes=VMEM_LIMIT_BYTES)


def _dot(a, b):
    return jnp.dot(a, b, preferred_element_type=F32)


def _dot_nt(a, b):
    return lax.dot_general(a, b, (((1,), (1,)), ((), ())), preferred_element_type=F32)


def _dot_tn(a, b):
    return lax.dot_general(a, b, (((0,), (0,)), ((), ())), preferred_element_type=F32)


def _sigmoid(x):
    return 1.0 / (1.0 + jnp.exp(-x))


def _silu(x):
    return x * _sigmoid(x)


def _lane_half(shape):
    return lax.broadcasted_iota(jnp.int32, shape, len(shape) - 1) // HEAD_DIM


def _rms_proj_kernel(x_ref, g_ref, w_ref, cos_ref, sin_ref, o_ref, *, rope):
    x = x_ref[...]
    var = jnp.mean(x * x, axis=-1, keepdims=True)
    xn = (x * lax.rsqrt(var + NORM_EPS) * g_ref[...]).astype(BF16)
    y = _dot(xn, w_ref[...])
    if not rope:
        o_ref[...] = y.astype(o_ref.dtype)
        return
    cos = cos_ref[...]
    sin = sin_ref[...]
    first = (lax.broadcasted_iota(jnp.int32, cos.shape, 1) % HEAD_DIM) < (HEAD_DIM // 2)
    for c in range(y.shape[1] // LANES):
        yc = y[:, c * LANES:(c + 1) * LANES]
        partner = jnp.where(first, pltpu.roll(yc, LANES - HEAD_DIM // 2, 1), pltpu.roll(yc, HEAD_DIM // 2, 1))
        o_ref[:, c * LANES:(c + 1) * LANES] = (yc * cos + partner * sin).astype(o_ref.dtype)


def _rms_proj(x, g, w, cos, sin, *, seq, rope, out_dtype):
    t, d = x.shape
    n = w.shape[1]
    tm = ROW_TILE
    nseq = seq // tm
    return pl.pallas_call(
        functools.partial(_rms_proj_kernel, rope=rope),
        grid=(t // tm,),
        in_specs=[
            pl.BlockSpec((tm, d), lambda i: (i, 0)),
            pl.BlockSpec((1, d), lambda i: (0, 0)),
            pl.BlockSpec((d, n), lambda i: (0, 0)),
            pl.BlockSpec((tm, LANES), lambda i: (i % nseq, 0)),
            pl.BlockSpec((tm, LANES), lambda i: (i % nseq, 0)),
        ],
        out_specs=pl.BlockSpec((tm, n), lambda i: (i, 0)),
        out_shape=jax.ShapeDtypeStruct((t, n), out_dtype),
        compiler_params=_params("parallel"),
        name="rms_proj_rope" if rope else "rms_proj",
    )(x, g, w, cos, sin)


def _out_proj_kernel(a1_ref, a2_ref, w1_ref, w2_ref, r_ref, o_ref):
    o_ref[...] = r_ref[...] + _dot(a1_ref[...], w1_ref[...]) + _dot(a2_ref[...], w2_ref[...])


def _out_proj(a1, a2, w1, w2, res):
    t, d = res.shape
    k1, k2 = a1.shape[1], a2.shape[1]
    tm = ROW_TILE
    return pl.pallas_call(
        _out_proj_kernel,
        grid=(t // tm,),
        in_specs=[
            pl.BlockSpec((tm, k1), lambda i: (i, 0)),
            pl.BlockSpec((tm, k2), lambda i: (i, 0)),
            pl.BlockSpec((k1, d), lambda i: (0, 0)),
            pl.BlockSpec((k2, d), lambda i: (0, 0)),
            pl.BlockSpec((tm, d), lambda i: (i, 0)),
        ],
        out_specs=pl.BlockSpec((tm, d), lambda i: (i, 0)),
        out_shape=jax.ShapeDtypeStruct((t, d), F32),
        compiler_params=_params("parallel"),
        name="out_proj",
    )(a1, a2, w1, w2, res)


def _cumsum_rows(x):
    rows = x.shape[0]
    row = lax.broadcasted_iota(jnp.int32, x.shape, 0)
    k = 1
    while k < rows:
        x = x + jnp.where(row >= k, pltpu.roll(x, k, 0), 0.0)
        k *= 2
    return x


def _fox_gate_kernel(z_ref, b_ref, ccol_ref, crow_ref, *, tk):
    z = z_ref[...] + b_ref[...]
    logf = jnp.minimum(z, 0.0) - jnp.log(1.0 + jnp.exp(-jnp.abs(z)))
    c = _cumsum_rows(logf)
    ccol_ref[...] = c
    for j in range(c.shape[0] // tk):
        ct = c[j * tk:(j + 1) * tk, :].T
        crow_ref[0, j] = ct[:SUBLANES, :]


def _fox_gate(z, b_pad, *, batch, seq):
    tk = ATTN_TILE
    nk = seq // tk
    return pl.pallas_call(
        functools.partial(_fox_gate_kernel, tk=tk),
        grid=(batch,),
        in_specs=[
            pl.BlockSpec((seq, LANES), lambda b: (b, 0)),
            pl.BlockSpec((1, LANES), lambda b: (0, 0)),
        ],
        out_specs=[
            pl.BlockSpec((seq, LANES), lambda b: (b, 0)),
            pl.BlockSpec((1, nk, SUBLANES, tk), lambda b: (b, 0, 0, 0)),
        ],
        out_shape=[
            jax.ShapeDtypeStruct((batch * seq, LANES), F32),
            jax.ShapeDtypeStruct((batch, nk, SUBLANES, tk), F32),
        ],
        compiler_params=_params("parallel"),
        name="fox_gate",
    )(z, b_pad)


def _softmax_step(carry, s, vb):
    m, l, acc = carry
    m_new = jnp.maximum(m, jnp.max(s, axis=-1, keepdims=True))
    alpha = jnp.exp(m - m_new)
    p = jnp.exp(s - m_new)
    l = alpha * l + jnp.sum(p, axis=-1, keepdims=True)
    acc = alpha * acc + _dot(p.astype(BF16), vb)
    return m_new, l, acc


def _softmax_init(tq):
    return (jnp.full((tq, 1), NEG_INF, F32), jnp.zeros((tq, 1), F32), jnp.zeros((tq, LANES), F32))


def _causal_mask(tq, tk):
    return lax.broadcasted_iota(jnp.int32, (tq, tk), 1) <= lax.broadcasted_iota(jnp.int32, (tq, tk), 0)


def _fox_kernel(q_ref, k_ref, v_ref, cc_ref, cr_ref, o_ref, *, tq, scale):
    hp = pl.program_id(1)
    i = pl.program_id(2)
    q = q_ref[...]
    half = _lane_half((1, LANES))
    cc = cc_ref[...]
    lane = lax.broadcasted_iota(jnp.int32, cc.shape, 1)
    causal = _causal_mask(tq, tq)
    outs = []
    for e in range(2):
        h = 2 * hp + e
        qe = jnp.where(half == e, q, jnp.zeros_like(q))
        cq = jnp.sum(jnp.where(lane == h, cc, 0.0), axis=-1, keepdims=True)

        def logits(j, qe=qe, cq=cq, h=h):
            start = pl.multiple_of(j * tq, tq)
            kb = k_ref[pl.ds(start, tq), :]
            vb = v_ref[pl.ds(start, tq), :]
            ck = cr_ref[0, j, pl.ds(h, 1), :]
            return _dot_nt(qe, kb) * scale + cq - ck, vb

        def body(j, carry):
            s, vb = logits(j)
            return _softmax_step(carry, s, vb)

        carry = lax.fori_loop(0, i, body, _softmax_init(tq))
        s, vb = logits(i)
        m, l, acc = _softmax_step(carry, jnp.where(causal, s, NEG_INF), vb)
        outs.append(acc / l)
    o_ref[...] = jnp.where(half == 0, outs[0], outs[1]).astype(o_ref.dtype)


def _fox_attention(p1, ccol, crow, *, batch, seq):
    tq = ATTN_TILE
    nq = seq // tq
    ng = FOX_HEADS // 2
    return pl.pallas_call(
        functools.partial(_fox_kernel, tq=tq, scale=HEAD_DIM ** -0.5),
        grid=(batch, ng, nq),
        in_specs=[
            pl.BlockSpec((tq, LANES), lambda b, g, i: (b * nq + i, g)),
            pl.BlockSpec((seq, LANES), lambda b, g, i: (b, ng + g)),
            pl.BlockSpec((seq, LANES), lambda b, g, i: (b, 2 * ng + g)),
            pl.BlockSpec((tq, LANES), lambda b, g, i: (b * nq + i, 0)),
            pl.BlockSpec((1, nq, SUBLANES, tq), lambda b, g, i: (b, 0, 0, 0)),
        ],
        out_specs=pl.BlockSpec((tq, LANES), lambda b, g, i: (b * nq + i, g)),
        out_shape=jax.ShapeDtypeStruct((batch * seq, ng * LANES), BF16),
        compiler_params=_params("parallel", "parallel", "arbitrary"),
        name="fox_attention",
    )(p1, p1, p1, ccol, crow)


def _diff_kernel(q_ref, k_ref, v_ref, lam_ref, g_ref, o_ref, *, tq, scale, lam_init):
    i = pl.program_id(2)
    q = q_ref[...]
    half = _lane_half((1, LANES))
    causal = _causal_mask(tq, tq)
    lv = lam_ref[...]
    lam = (jnp.exp(jnp.sum(lv[0:1, :] * lv[1:2, :], axis=-1, keepdims=True))
           - jnp.exp(jnp.sum(lv[2:3, :] * lv[3:4, :], axis=-1, keepdims=True)) + lam_init)
    outs = []
    for e in range(2):
        qe = jnp.where(half == e, q, jnp.zeros_like(q))

        def logits(j, qe=qe):
            start = pl.multiple_of(j * tq, tq)
            kb = k_ref[pl.ds(start, tq), :]
            vb = v_ref[pl.ds(start, tq), :]
            return _dot_nt(qe, kb) * scale, vb

        def body(j, carry):
            s, vb = logits(j)
            return _softmax_step(carry, s, vb)

        carry = lax.fori_loop(0, i, body, _softmax_init(tq))
        s, vb = logits(i)
        m, l, acc = _softmax_step(carry, jnp.where(causal, s, NEG_INF), vb)
        outs.append(acc / l)
    o = outs[0] - lam * outs[1]
    var = jnp.mean(o * o, axis=-1, keepdims=True)
    o = o * lax.rsqrt(var + NORM_EPS) * g_ref[...] * (1.0 - lam_init)
    o_ref[...] = o.astype(o_ref.dtype)


def _diff_attention(pqk, pv, lam_vecs, g, *, batch, seq, lam_init):
    tq = ATTN_TILE
    nq = seq // tq
    ng = DIFF_HEADS
    return pl.pallas_call(
        functools.partial(_diff_kernel, tq=tq, scale=DIFF_DIM ** -0.5, lam_init=lam_init),
        grid=(batch, ng, nq),
        in_specs=[
            pl.BlockSpec((tq, LANES), lambda b, g, i: (b * nq + i, g)),
            pl.BlockSpec((seq, LANES), lambda b, g, i: (b, ng + g)),
            pl.BlockSpec((seq, LANES), lambda b, g, i: (b, g)),
            pl.BlockSpec(lam_vecs.shape, lambda b, g, i: (0, 0)),
            pl.BlockSpec((1, LANES), lambda b, g, i: (0, 0)),
        ],
        out_specs=pl.BlockSpec((tq, LANES), lambda b, g, i: (b * nq + i, g)),
        out_shape=jax.ShapeDtypeStruct((batch * seq, ng * LANES), BF16),
        compiler_params=_params("parallel", "parallel", "arbitrary"),
        name="diff_attention",
    )(pqk, pqk, pv, lam_vecs, g)


def _sortable_key(x):
    bits = pltpu.bitcast(x, jnp.int32)
    return jnp.where(bits < 0, bits ^ jnp.int32(0x7FFFFFFF), bits)


def _dsa_kernel(q_ref, kd_ref, iq_ref, ik_ref, vd_ref, z_ref, o_ref, key_ref, madd_ref,
                *, tq, n_sel, scale, idx_scale, w_scale):
    i = pl.program_id(1)
    nk = key_ref.shape[0]
    half = _lane_half((1, LANES))
    z = z_ref[...]
    row = lax.broadcasted_iota(jnp.int32, (tq, tq), 0)
    col = lax.broadcasted_iota(jnp.int32, (tq, tq), 1)
    row_g = row + i * tq

    iq = iq_ref[...]
    iq_h = []
    w_h = []
    for h in range(IDX_HEADS):
        grp = iq[:, (h // 2) * LANES:(h // 2 + 1) * LANES]
        iq_h.append(jnp.where(half == h % 2, grp, jnp.zeros_like(grp)))
        w_h.append(z[:, SUBLANES + h:SUBLANES + h + 1] * w_scale)
    neg_key = _sortable_key(jnp.full((tq, tq), -jnp.inf, F32))
    for j in range(nk):
        key_ref[j] = neg_key

    def score_body(j, _):
        start = pl.multiple_of(j * tq, tq)
        ikb = ik_ref[pl.ds(start, tq), :]
        sc = jnp.zeros((tq, tq), F32)
        for h in range(IDX_HEADS):
            sc = sc + w_h[h] * jnp.maximum(_dot_nt(iq_h[h], ikb) * idx_scale, 0.0)
        sc = jnp.where(col + j * tq <= row_g, sc, -jnp.inf)
        key_ref[j] = _sortable_key(sc)
        return 0

    lax.fori_loop(0, i + 1, score_body, 0)

    def count(pred):
        return jnp.sum(jnp.sum(jnp.where(pred, 1.0, 0.0), axis=0), axis=-1, keepdims=True)

    thr0 = jnp.where(count(key_ref[...] >= 0) >= n_sel, 0, INT_MIN).astype(jnp.int32)

    def bit_body(it, thr):
        cand = thr | jnp.left_shift(jnp.int32(1), 30 - it)
        return jnp.where(count(key_ref[...] >= cand[None]) >= n_sel, cand, thr)

    thr = lax.fori_loop(0, 31, bit_body, thr0)
    need = n_sel - count(key_ref[...] > thr[None])

    tri = jnp.where(row < col, 1.0, 0.0).astype(BF16)

    def mask_body(j, off):
        kj = key_ref[j]
        eq = kj == thr
        eqf = jnp.where(eq, 1.0, 0.0)
        rank = _dot(eqf.astype(BF16), tri) + off
        take = jnp.where(kj > thr, 1.0, jnp.where(eq, jnp.where(rank < need, 1.0, 0.0), 0.0))
        valid = col + j * tq <= row_g
        madd_ref[j] = jnp.where(valid, jnp.where(take > 0.0, 0.0, NEG_INF), NEG_INF)
        return off + jnp.sum(eqf, axis=-1, keepdims=True)

    lax.fori_loop(0, i + 1, mask_body, jnp.zeros((tq, 1), F32))

    q = q_ref[...]
    for p in range(DSA_HEADS // 2):
        qg = q[:, p * LANES:(p + 1) * LANES]
        g = p // (DSA_HEADS // DSA_KV_HEADS // 2)
        outs = []
        for e in range(2):
            qe = jnp.where(half == e, qg, jnp.zeros_like(qg))

            def body(j, carry, qe=qe, g=g):
                start = pl.multiple_of(j * tq, tq)
                kb = kd_ref[pl.ds(start, tq), g * LANES:(g + 1) * LANES]
                vb = vd_ref[pl.ds(start, tq), g * LANES:(g + 1) * LANES]
                s = _dot_nt(qe, kb) * scale + madd_ref[j]
                return _softmax_step(carry, s, vb)

            m, l, acc = lax.fori_loop(0, i + 1, body, _softmax_init(tq))
            outs.append(acc / l)
        o_ref[:, p * LANES:(p + 1) * LANES] = jnp.where(half == 0, outs[0], outs[1]).astype(o_ref.dtype)


def _dsa_attention(p2, p1, z, *, batch, seq):
    tq = ATTN_TILE
    nq = seq // tq
    n_sel = min(DSA_TOPK, seq // 4)
    qw = DSA_HEADS * HEAD_DIM
    return pl.pallas_call(
        functools.partial(_dsa_kernel, tq=tq, n_sel=float(n_sel), scale=HEAD_DIM ** -0.5,
                          idx_scale=IDX_DIM ** -0.5, w_scale=IDX_HEADS ** -0.5),
        grid=(batch, nq),
        in_specs=[
            pl.BlockSpec((tq, qw), lambda b, i: (b * nq + i, 0)),
            pl.BlockSpec((seq, 2 * LANES), lambda b, i: (b, 2)),
            pl.BlockSpec((tq, 2 * LANES), lambda b, i: (b * nq + i, 3)),
            pl.BlockSpec((seq, LANES), lambda b, i: (b, 8)),
            pl.BlockSpec((seq, 2 * LANES), lambda b, i: (b, 6)),
            pl.BlockSpec((tq, LANES), lambda b, i: (b * nq + i, 0)),
        ],
        out_specs=pl.BlockSpec((tq, qw), lambda b, i: (b * nq + i, 0)),
        out_shape=jax.ShapeDtypeStruct((batch * seq, qw), BF16),
        scratch_shapes=[pltpu.VMEM((nq, tq, tq), jnp.int32), pltpu.VMEM((nq, tq, tq), F32)],
        compiler_params=_params("parallel", "arbitrary"),
        name="dsa_attention",
    )(p2, p2, p2, p2, p1, z)


def _hgrn_kernel(q_ref, f_ref, i_ref, g_ref, lbl_ref, gn_ref, o_ref, b_scr, *, chunk, layer):
    seq = q_ref.shape[0]
    half = _lane_half((1, LANES))
    logits = lbl_ref[...]
    pe = jnp.exp(logits - jnp.max(logits, axis=0, keepdims=True))
    prob = pe / jnp.sum(pe, axis=0, keepdims=True)
    csum = prob[0:1, :]
    for d in range(1, layer + 1):
        csum = csum + prob[d:d + 1, :]
    lb = csum - prob[0:1, :]

    rowi = lax.broadcasted_iota(jnp.int32, (chunk, 1), 0)
    xor_rc = (lax.broadcasted_iota(jnp.int32, (chunk, chunk), 0)
              ^ lax.broadcasted_iota(jnp.int32, (chunk, chunk), 1))
    same_head = (lax.broadcasted_iota(jnp.int32, (LANES, LANES), 0) // HEAD_DIM
                 == lax.broadcasted_iota(jnp.int32, (LANES, LANES), 1) // HEAD_DIM)
    ones_bd = jnp.where(same_head, 1.0, 0.0).astype(BF16)
    gn = gn_ref[...]

    def chunk_body(c, state_t):
        r0 = pl.multiple_of(c * chunk, chunk)
        q = _silu(q_ref[pl.ds(r0, chunk), :])
        f = lb + (1.0 - lb) * _sigmoid(f_ref[pl.ds(r0, chunk), :])
        kc = 1.0 - f
        v = i_ref[pl.ds(r0, chunk), :]
        b = _cumsum_rows(jnp.log(f))
        b_scr[...] = b
        b_last = b_scr[chunk - 1:chunk, :]

        o = _dot_nt((q * jnp.exp(b)).astype(BF16), state_t.astype(BF16))
        k2 = (kc * jnp.exp(b_last - b)).astype(BF16)
        upd = _dot_tn(v.astype(BF16), k2)
        new_state = state_t * jnp.exp(b_last) + jnp.where(same_head, upd, 0.0)

        s_tot = [jnp.zeros((chunk, chunk), F32), jnp.zeros((chunk, chunk), F32)]
        m = chunk
        while m > SUB:
            hm = m // 2
            pieces = [jnp.broadcast_to(b_scr[blk * m + hm - 1:blk * m + hm, :], (m, LANES))
                      for blk in range(chunk // m)]
            ref = pieces[0] if len(pieces) == 1 else jnp.concatenate(pieces, axis=0)
            right = (rowi % m) >= hm
            qm = jnp.where(right, q * jnp.exp(jnp.where(right, b - ref, 0.0)), 0.0).astype(BF16)
            km = jnp.where(right, 0.0, kc * jnp.exp(jnp.where(right, 0.0, ref - b))).astype(BF16)
            for e in range(2):
                s = _dot_nt(jnp.where(half == e, qm, jnp.zeros_like(qm)), km)
                s_tot[e] = s_tot[e] + (s if m == chunk else jnp.where(xor_rc < m, s, 0.0))
            m = hm
        vb = v.astype(BF16)
        for e in range(2):
            o = o + _dot(s_tot[e].astype(BF16), jnp.where(half == e, vb, jnp.zeros_like(vb)))

        for d in range(SUB):
            ok = (rowi % SUB) >= d
            bs = b if d == 0 else pltpu.roll(b, d, 0)
            ks = kc if d == 0 else pltpu.roll(kc, d, 0)
            vs = v if d == 0 else pltpu.roll(v, d, 0)
            x = jnp.where(ok, q * ks * jnp.exp(jnp.where(ok, b - bs, 0.0)), 0.0)
            o = o + _dot(x.astype(BF16), ones_bd) * vs

        sq = o * o
        s0 = jnp.sum(jnp.where(half == 0, sq, 0.0), axis=-1, keepdims=True)
        s1 = jnp.sum(jnp.where(half == 1, sq, 0.0), axis=-1, keepdims=True)
        var = jnp.where(half == 0, s0, s1) * (1.0 / HEAD_DIM)
        y = o * lax.rsqrt(var + NORM_EPS) * gn * _silu(g_ref[pl.ds(r0, chunk), :])
        o_ref[pl.ds(r0, chunk), :] = y.astype(o_ref.dtype)
        return new_state

    lax.fori_loop(0, seq // chunk, chunk_body, jnp.zeros((LANES, LANES), F32))


def _hgrn(hproj, lb_logits, gn, *, batch, seq, layer):
    ng = HGRN_HEADS // 2
    chunk = min(HGRN_CHUNK, seq)
    spec = lambda off: pl.BlockSpec((seq, LANES), lambda b, g: (b, off * ng + g))
    return pl.pallas_call(
        functools.partial(_hgrn_kernel, chunk=chunk, layer=layer),
        grid=(batch, ng),
        in_specs=[spec(0), spec(1), spec(2), spec(3),
                  pl.BlockSpec((lb_logits.shape[0], LANES), lambda b, g: (0, g)),
                  pl.BlockSpec((1, LANES), lambda b, g: (0, 0))],
        out_specs=pl.BlockSpec((seq, LANES), lambda b, g: (b, g)),
        out_shape=jax.ShapeDtypeStruct((batch * seq, ng * LANES), BF16),
        scratch_shapes=[pltpu.VMEM((chunk, LANES), F32)],
        compiler_params=_params("parallel", "parallel"),
        name="hgrn2",
    )(hproj, hproj, hproj, hproj, lb_logits, gn)


def _router_kernel(x_ref, g_ref, w_ref, o_ref, xn_ref):
    x = x_ref[...]
    var = jnp.mean(x * x, axis=-1, keepdims=True)
    xn = x * lax.rsqrt(var + NORM_EPS) * g_ref[...]
    xn_ref[...] = xn.astype(BF16)
    logits = jnp.dot(xn, w_ref[...], precision=lax.Precision.HIGHEST, preferred_element_type=F32)
    lane = lax.broadcasted_iota(jnp.int32, logits.shape, 1)
    lg = jnp.where(lane < N_EXPERTS, logits, -jnp.inf)
    m1 = jnp.max(lg, axis=-1, keepdims=True)
    i1 = jnp.min(jnp.where(lg == m1, lane, LANES), axis=-1, keepdims=True)
    lg2 = jnp.where(lane == i1, -jnp.inf, lg)
    m2 = jnp.max(lg2, axis=-1, keepdims=True)
    i2 = jnp.min(jnp.where(lg2 == m2, lane, LANES), axis=-1, keepdims=True)
    t = jnp.exp(m2 - m1)
    den = 1.0 + t
    o_ref[...] = (jnp.where(lane == 0, i1.astype(F32), 0.0) + jnp.where(lane == 1, i2.astype(F32), 0.0)
                  + jnp.where(lane == 2, 1.0 / den, 0.0) + jnp.where(lane == 3, t / den, 0.0))


def _router(x, g, w_pad):
    t, d = x.shape
    tm = ROW_TILE
    return pl.pallas_call(
        _router_kernel,
        grid=(t // tm,),
        in_specs=[pl.BlockSpec((tm, d), lambda i: (i, 0)),
                  pl.BlockSpec((1, d), lambda i: (0, 0)),
                  pl.BlockSpec((d, LANES), lambda i: (0, 0))],
        out_specs=[pl.BlockSpec((tm, LANES), lambda i: (i, 0)),
                   pl.BlockSpec((tm, d), lambda i: (i, 0))],
        out_shape=[jax.ShapeDtypeStruct((t, LANES), F32), jax.ShapeDtypeStruct((t, d), BF16)],
        compiler_params=_params("parallel"),
        name="router",
    )(x, g, w_pad)


def _moe_gather_kernel(osub_ref, iblk_ref, first_ref, valid_ref, ids_ref, x_ref, o_ref, *, tw):
    w = pl.program_id(0)

    @pl.when(first_ref[w] == 1)
    def _():
        o_ref[...] = jnp.zeros_like(o_ref)

    @pl.when(valid_ref[w] == 1)
    def _():
        tok = ids_ref[0, 0:1, :]
        cand = lax.broadcasted_iota(jnp.int32, (tw, MOE_SUB), 0) + iblk_ref[w] * tw
        onehot_t = jnp.where(cand == tok, 1.0, 0.0).astype(BF16)
        o_ref[...] += _dot_tn(onehot_t, x_ref[...]).astype(o_ref.dtype)


def _moe_gather(plan, ids3, xn):
    t, d = xn.shape
    n_sub = ids3.shape[0]
    osub, iblk, first, valid = plan
    tw = MOE_TOKEN_BLOCK
    return pl.pallas_call(
        functools.partial(_moe_gather_kernel, tw=tw),
        grid_spec=pltpu.PrefetchScalarGridSpec(
            num_scalar_prefetch=4,
            grid=(osub.shape[0],),
            in_specs=[pl.BlockSpec((1, SUBLANES, MOE_SUB), lambda w, o, i, f, v: (o[w], 0, 0)),
                      pl.BlockSpec((tw, d), lambda w, o, i, f, v: (i[w], 0))],
            out_specs=pl.BlockSpec((MOE_SUB, d), lambda w, o, i, f, v: (o[w], 0)),
        ),
        out_shape=jax.ShapeDtypeStruct((n_sub * MOE_SUB, d), BF16),
        compiler_params=_params("arbitrary"),
        name="moe_gather",
    )(osub, iblk, first, valid, ids3, xn)


def _moe_combine_kernel(otile_ref, isub_ref, first_ref, last_ref, valid_ref, ids_ref, y_ref, x_ref, gf_ref,
                        o_ref, acc_ref, *, tt, final_norm):
    w = pl.program_id(0)

    @pl.when(first_ref[w] == 1)
    def _():
        acc_ref[...] = x_ref[...]

    @pl.when(valid_ref[w] == 1)
    def _():
        tok = ids_ref[0, 0:1, :]
        cand = lax.broadcasted_iota(jnp.int32, (tt, MOE_SUB), 0) + otile_ref[w] * tt
        onehot = jnp.where(cand == tok, 1.0, 0.0).astype(BF16)
        acc_ref[...] += _dot(onehot, y_ref[...])

    @pl.when(last_ref[w] == 1)
    def _():
        out = acc_ref[...]
        if final_norm:
            var = jnp.mean(out * out, axis=-1, keepdims=True)
            out = out * lax.rsqrt(var + NORM_EPS) * gf_ref[...]
        o_ref[...] = out


def _moe_combine(plan, ids3, y, x, gf, *, final_norm):
    t, d = x.shape
    otile, isub, first, last, valid = plan
    tt = ROW_TILE
    return pl.pallas_call(
        functools.partial(_moe_combine_kernel, tt=tt, final_norm=final_norm),
        grid_spec=pltpu.PrefetchScalarGridSpec(
            num_scalar_prefetch=5,
            grid=(otile.shape[0],),
            in_specs=[pl.BlockSpec((1, SUBLANES, MOE_SUB), lambda w, o, i, f, l, v: (i[w], 0, 0)),
                      pl.BlockSpec((MOE_SUB, d), lambda w, o, i, f, l, v: (i[w], 0)),
                      pl.BlockSpec((tt, d), lambda w, o, i, f, l, v: (o[w], 0)),
                      pl.BlockSpec((1, d), lambda w, o, i, f, l, v: (0, 0))],
            out_specs=pl.BlockSpec((tt, d), lambda w, o, i, f, l, v: (o[w], 0)),
            scratch_shapes=[pltpu.VMEM((tt, d), F32)],
        ),
        out_shape=jax.ShapeDtypeStruct((t, d), F32),
        compiler_params=_params("arbitrary"),
        name="moe_combine",
    )(otile, isub, first, last, valid, ids3, y, x, gf)


def _moe_ffn_kernel(te_ref, act_ref, x_ref, gate_ref, wg_ref, wu_ref, wd_ref, o_ref, acc_ref, *, tm):
    i = pl.program_id(0)
    f = pl.program_id(1)

    @pl.when(act_ref[i] == 1)
    def _():
        @pl.when(f == 0)
        def _():
            acc_ref[...] = jnp.zeros_like(acc_ref)

        x = x_ref[...]
        mid = (_silu(_dot(x, wg_ref[0])) * _dot(x, wu_ref[0])).astype(BF16)
        acc_ref[...] += _dot(mid, wd_ref[0])

        @pl.when(f == pl.num_programs(1) - 1)
        def _():
            gates = gate_ref[0]
            for k in range(tm // LANES):
                col = jnp.broadcast_to(gates[k:k + 1, :], (LANES, LANES)).T
                rows = acc_ref[k * LANES:(k + 1) * LANES, :]
                for c in range(rows.shape[1] // LANES):
                    o_ref[k * LANES:(k + 1) * LANES, c * LANES:(c + 1) * LANES] = (
                        rows[:, c * LANES:(c + 1) * LANES] * col).astype(o_ref.dtype)

    @pl.when(jnp.logical_and(act_ref[i] == 0, f == 0))
    def _():
        o_ref[...] = jnp.zeros_like(o_ref)


def _moe_ffn(tile_e, tile_act, xs, gates3, wg, wu, wd, *, tf):
    p, d = xs.shape
    ff = wg.shape[2]
    tm = MOE_ROW_TILE
    return pl.pallas_call(
        functools.partial(_moe_ffn_kernel, tm=tm),
        grid_spec=pltpu.PrefetchScalarGridSpec(
            num_scalar_prefetch=2,
            grid=(p // tm, ff // tf),
            in_specs=[pl.BlockSpec((tm, d), lambda i, f, te, ta: (i, 0)),
                      pl.BlockSpec((1, tm // LANES, LANES), lambda i, f, te, ta: (i, 0, 0)),
                      pl.BlockSpec((1, d, tf), lambda i, f, te, ta: (te[i], 0, f * ta[i])),
                      pl.BlockSpec((1, d, tf), lambda i, f, te, ta: (te[i], 0, f * ta[i])),
                      pl.BlockSpec((1, tf, d), lambda i, f, te, ta: (te[i], f * ta[i], 0))],
            out_specs=pl.BlockSpec((tm, d), lambda i, f, te, ta: (i, 0)),
            scratch_shapes=[pltpu.VMEM((tm, d), F32)],
        ),
        out_shape=jax.ShapeDtypeStruct((p, d), BF16),
        compiler_params=_params("arbitrary", "arbitrary"),
        name="moe_ffn",
    )(tile_e, tile_act, xs, gates3, wg, wu, wd)


def _moe_plan(info, t):
    tm, sub, tw, tt = MOE_ROW_TILE, MOE_SUB, MOE_TOKEN_BLOCK, ROW_TILE
    n_e = N_EXPERTS
    a = 2 * t
    p = a + n_e * tm
    n_sub = p // sub
    e_flat = info[:, 0:2].astype(jnp.int32).reshape(a)
    g_flat = info[:, 2:4].reshape(a)
    onehot = (e_flat[:, None] == jnp.arange(n_e, dtype=jnp.int32)[None, :]).astype(jnp.int32)
    csum = jnp.cumsum(onehot, axis=0)
    rank = jnp.sum(csum * onehot, axis=1) - 1
    counts = csum[-1]
    padded = ((counts + tm - 1) // tm) * tm
    ends = jnp.cumsum(padded)
    starts = ends - padded
    pos = starts[e_flat] + rank
    tok_row = jnp.full((p,), -1, jnp.int32).at[pos].set(jnp.arange(a, dtype=jnp.int32) // 2)
    gate_row = jnp.zeros((p,), F32).at[pos].set(g_flat)
    ids3 = jnp.broadcast_to(tok_row.reshape(n_sub, 1, sub), (n_sub, SUBLANES, sub))
    gates3 = gate_row.reshape(p // tm, tm // LANES, LANES)

    tile_start = jnp.arange(p // tm, dtype=jnp.int32) * tm
    tile_e = jnp.minimum(jnp.searchsorted(ends, tile_start, side='right'), n_e - 1).astype(jnp.int32)
    tile_act = (tile_start < ends[-1]).astype(jnp.int32)

    tok2 = tok_row.reshape(n_sub, sub)
    has = jnp.any(tok2 >= 0, axis=1)
    tmin = jnp.min(jnp.where(tok2 >= 0, tok2, t), axis=1)
    tmax = jnp.max(tok2, axis=1)

    def worklist(lo, hi, cnt, length):
        cend = jnp.cumsum(cnt)
        total = cend[-1]
        w = jnp.arange(length, dtype=jnp.int32)
        s = jnp.minimum(jnp.searchsorted(cend, w, side='right'), n_sub - 1).astype(jnp.int32)
        k = w - (cend[s] - cnt[s])
        valid = w < total
        blk = jnp.where(valid, lo[s] + k, 0)
        return s, blk.astype(jnp.int32), k, valid

    lo = jnp.where(has, tmin // tw, 0)
    hi = jnp.where(has, tmax // tw, 0)
    cnt = (hi - lo + 1).astype(jnp.int32)
    len_g = n_sub + n_e * (t // tw - 1)
    s, blk, k, valid = worklist(lo, hi, cnt, len_g)
    last_valid = jnp.sum(valid.astype(jnp.int32)) - 1
    s = jnp.where(valid, s, s[last_valid])
    blk = jnp.where(valid, blk, blk[last_valid])
    gather_plan = (s, blk, jnp.logical_and(valid, k == 0).astype(jnp.int32), valid.astype(jnp.int32))

    lo = jnp.where(has, tmin // tt, 0)
    hi = jnp.where(has, tmax // tt, -1)
    cnt = (hi - lo + 1).astype(jnp.int32)
    len_c = n_sub + n_e * (t // tt - 1)
    s, tile, k, valid = worklist(lo, hi, cnt, len_c)
    n_tt = t // tt
    order = jnp.argsort(jnp.where(valid, tile, n_tt), stable=True)
    s, tile, valid = s[order], tile[order], valid[order]
    tile = jnp.where(valid, tile, n_tt - 1)
    last_valid = jnp.sum(valid.astype(jnp.int32)) - 1
    s = jnp.where(valid, s, s[last_valid])
    prev = jnp.concatenate([jnp.full((1,), -1, jnp.int32), tile[:-1]])
    nxt = jnp.concatenate([tile[1:], jnp.full((1,), -1, jnp.int32)])
    combine_plan = (tile, s, (tile != prev).astype(jnp.int32), (tile != nxt).astype(jnp.int32),
                    valid.astype(jnp.int32))
    return ids3, gates3, tile_e, tile_act, gather_plan, combine_plan


def _ffn_kernel(x_ref, g_ref, wg_ref, wu_ref, wd_ref, gf_ref, o_ref, xn_ref, acc_ref, *, final_norm):
    f = pl.program_id(1)

    @pl.when(f == 0)
    def _():
        x = x_ref[...]
        var = jnp.mean(x * x, axis=-1, keepdims=True)
        xn_ref[...] = (x * lax.rsqrt(var + NORM_EPS) * g_ref[...]).astype(BF16)
        acc_ref[...] = jnp.zeros_like(acc_ref)

    xn = xn_ref[...]
    mid = (_silu(_dot(xn, wg_ref[...])) * _dot(xn, wu_ref[...])).astype(BF16)
    acc_ref[...] += _dot(mid, wd_ref[...])

    @pl.when(f == pl.num_programs(1) - 1)
    def _():
        out = x_ref[...] + acc_ref[...]
        if final_norm:
            var = jnp.mean(out * out, axis=-1, keepdims=True)
            out = out * lax.rsqrt(var + NORM_EPS) * gf_ref[...]
        o_ref[...] = out


def _ffn(x, g, wg, wu, wd, gf, *, tf, final_norm):
    t, d = x.shape
    ff = wg.shape[1]
    tm = ROW_TILE
    return pl.pallas_call(
        functools.partial(_ffn_kernel, final_norm=final_norm),
        grid=(t // tm, ff // tf),
        in_specs=[
            pl.BlockSpec((tm, d), lambda i, f: (i, 0)),
            pl.BlockSpec((1, d), lambda i, f: (0, 0)),
            pl.BlockSpec((d, tf), lambda i, f: (0, f)),
            pl.BlockSpec((d, tf), lambda i, f: (0, f)),
            pl.BlockSpec((tf, d), lambda i, f: (f, 0)),
            pl.BlockSpec((1, d), lambda i, f: (0, 0)),
        ],
        out_specs=pl.BlockSpec((tm, d), lambda i, f: (i, 0)),
        out_shape=jax.ShapeDtypeStruct((t, d), F32),
        scratch_shapes=[pltpu.VMEM((tm, d), BF16), pltpu.VMEM((tm, d), F32)],
        compiler_params=_params("parallel", "arbitrary"),
        name="ffn",
    )(x, g, wg, wu, wd, gf)


def _rope_tables(seq):
    half = HEAD_DIM // 2
    inv = ROPE_THETA ** (-jnp.arange(half, dtype=F32) * 2.0 / HEAD_DIM)
    ang = jnp.arange(seq, dtype=F32)[:, None] * inv[None, :]
    cos = jnp.cos(ang)
    sin = jnp.sin(ang)
    cos_t = jnp.concatenate([cos, cos, cos, cos], axis=-1)
    sin_t = jnp.concatenate([-sin, sin, -sin, sin], axis=-1)
    return cos_t, sin_t


def _dup_heads(w):
    d, n = w.shape
    w = w.reshape(d, n // HEAD_DIM, 1, HEAD_DIM)
    return jnp.broadcast_to(w, (d, n // HEAD_DIM, 2, HEAD_DIM)).reshape(d, 2 * n)


def kernel(x, ln_mix_e, w_in_e, b_fox_f, w_out_e, ln_ffn_e, w_ffn_gate, w_ffn_up, w_ffn_down,
           ln_mix_o, w_in_o, hgrn_lb_logits, hgrn_norm_g, diff_lambda, diff_norm_g, w_out_o,
           ln_ffn_o, w_router, w_exp_gate, w_exp_up, w_exp_down, ln_final):
    batch, seq, d = x.shape
    t = batch * seq
    depth = ln_mix_e.shape[0] + ln_mix_o.shape[0]
    xs = x.reshape(t, d)
    cos_t, sin_t = _rope_tables(seq)
    row = lambda v: v.reshape(1, -1)
    fw = FOX_HEADS * HEAD_DIM
    dw = DSA_HEADS * HEAD_DIM
    kvw = DSA_KV_HEADS * HEAD_DIM
    hw = HGRN_HEADS * HEAD_DIM
    qw = DIFF_HEADS * 2 * DIFF_DIM

    for layer in range(depth):
        j = layer // 2
        if layer % 2 == 0:
            w = w_in_e[j]
            o = 0
            fq, fk, fv = w[:, o:o + fw], w[:, o + fw:o + 2 * fw], w[:, o + 2 * fw:o + 3 * fw]
            o += 3 * fw
            ffw = w[:, o:o + FOX_HEADS]
            o += FOX_HEADS
            dq = w[:, o:o + dw]
            o += dw
            dk = w[:, o:o + kvw]
            o += kvw
            dv = w[:, o:o + kvw]
            o += kvw
            iqw = w[:, o:o + IDX_HEADS * IDX_DIM]
            o += IDX_HEADS * IDX_DIM
            ikw = w[:, o:o + IDX_DIM]
            o += IDX_DIM
            iww = w[:, o:o + IDX_HEADS]
            w1 = jnp.concatenate([fq, fk, fv, _dup_heads(dv)], axis=1).astype(BF16)
            w2 = jnp.concatenate([dq, _dup_heads(dk), iqw, _dup_heads(ikw)], axis=1).astype(BF16)
            w3 = jnp.concatenate([ffw, iww, jnp.zeros((d, LANES - FOX_HEADS - IDX_HEADS), F32)], axis=1).astype(BF16)
            g = row(ln_mix_e[j])
            p1 = _rms_proj(xs, g, w1, cos_t, sin_t, seq=seq, rope=False, out_dtype=BF16)
            p2 = _rms_proj(xs, g, w2, cos_t, sin_t, seq=seq, rope=True, out_dtype=BF16)
            z = _rms_proj(xs, g, w3, cos_t, sin_t, seq=seq, rope=False, out_dtype=F32)
            b_pad = jnp.concatenate([b_fox_f[j], jnp.zeros((LANES - FOX_HEADS,), F32)]).reshape(1, LANES)
            ccol, crow = _fox_gate(z, b_pad, batch=batch, seq=seq)
            fox_o = _fox_attention(p1, ccol, crow, batch=batch, seq=seq)
            dsa_o = _dsa_attention(p2, p1, z, batch=batch, seq=seq)
            wo = w_out_e[j].astype(BF16)
            xs = _out_proj(fox_o, dsa_o, wo[:fw], wo[fw:], xs)
            xs = _ffn(xs, row(ln_ffn_e[j]), w_ffn_gate[j].astype(BF16), w_ffn_up[j].astype(BF16),
                      w_ffn_down[j].astype(BF16), row(ln_final), tf=1408, final_norm=(layer == depth - 1))
        else:
            w = w_in_o[j]
            lam_init = 0.8 - 0.6 * math.exp(-0.3 * layer)
            g = row(ln_mix_o[j])
            hp = _rms_proj(xs, g, w[:, :4 * hw].astype(BF16), cos_t, sin_t, seq=seq, rope=False, out_dtype=F32)
            pqk = _rms_proj(xs, g, w[:, 4 * hw:4 * hw + 2 * qw].astype(BF16), cos_t, sin_t, seq=seq, rope=True,
                            out_dtype=BF16)
            pv = _rms_proj(xs, g, w[:, 4 * hw + 2 * qw:].astype(BF16), cos_t, sin_t, seq=seq, rope=False,
                           out_dtype=BF16)
            gn = jnp.concatenate([hgrn_norm_g[j], hgrn_norm_g[j]]).reshape(1, LANES)
            o_h = _hgrn(hp, hgrn_lb_logits, gn, batch=batch, seq=seq, layer=layer)
            o_d = _diff_attention(pqk, pv, diff_lambda[j], row(diff_norm_g[j]), batch=batch, seq=seq,
                                  lam_init=lam_init)
            wo = w_out_o[j].astype(BF16)
            xs = _out_proj(o_h, o_d, wo[:hw], wo[hw:], xs)
            wr = jnp.concatenate([w_router[j], jnp.zeros((d, LANES - N_EXPERTS), F32)], axis=1)
            info, xn = _router(xs, row(ln_ffn_o[j]), wr)
            ids3, gates3, tile_e, tile_act, gather_plan, combine_plan = _moe_plan(info, t)
            rows = _moe_gather(gather_plan, ids3, xn)
            y = _moe_ffn(tile_e, tile_act, rows, gates3, w_exp_gate[j].astype(BF16), w_exp_up[j].astype(BF16),
                         w_exp_down[j].astype(BF16), tf=896)
            xs = _moe_combine(combine_plan, ids3, y, xs, row(ln_final), final_norm=(layer == depth - 1))
    if depth % 2 == 1:
        pass
    return xs.reshape(batch, seq, d)
```

```python
import functools
import math

import jax
import jax.numpy as jnp
from jax import lax
from jax.experimental import pallas as pl
from jax.experimental.pallas import tpu as pltpu

F32 = jnp.float32
BF16 = jnp.bfloat16

D_MODEL = 1024
HEAD_DIM = 64
FOX_HEADS = 8
DSA_HEADS = 8
DSA_KV_HEADS = 2
IDX_HEADS = 4
IDX_DIM = 64
DSA_TOPK = 256
HGRN_HEADS = 8
DIFF_HEADS = 4
DIFF_DIM = 64
ROPE_THETA = 10000.0
N_EXPERTS = 8
NORM_EPS = 1e-6
NEG_INF = -1e30

LANES = 128
SUBLANES = 8
VMEM_LIMIT_BYTES = 56 * 1024 * 1024

ROW_TILE = 512
ATTN_TILE = 256
HGRN_CHUNK = 256
SUB = 8
MOE_ROW_TILE = 512
MOE_SUB = 128
MOE_TOKEN_BLOCK = 512

INT_MIN = -(2 ** 31)


def _params(*sem):
    return pltpu.CompilerParams(dimension_semantics=sem, vmem_limit_bytes=VMEM_LIMIT_BYTES)


def _dot(a, b):
    return jnp.dot(a, b, preferred_element_type=F32)


def _dot_nt(a, b):
    return lax.dot_general(a, b, (((1,), (1,)), ((), ())), preferred_element_type=F32)


def _dot_tn(a, b):
    return lax.dot_general(a, b, (((0,), (0,)), ((), ())), preferred_element_type=F32)


def _sigmoid(x):
    return 1.0 / (1.0 + jnp.exp(-x))


def _silu(x):
    return x * _sigmoid(x)


def _lane_half(shape):
    return lax.broadcasted_iota(jnp.int32, shape, len(shape) - 1) // HEAD_DIM


def _rms_proj_kernel(x_ref, g_ref, w_ref, cos_ref, sin_ref, o_ref, *, rope):
    x = x_ref[...]
    var = jnp.mean(x * x, axis=-1, keepdims=True)
    xn = (x * lax.rsqrt(var + NORM_EPS) * g_ref[...]).astype(BF16)
    y = _dot(xn, w_ref[...])
    if not rope:
        o_ref[...] = y.astype(o_ref.dtype)
        return
    cos = cos_ref[...]
    sin = sin_ref[...]
    first = (lax.broadcasted_iota(jnp.int32, cos.shape, 1) % HEAD_DIM) < (HEAD_DIM // 2)
    for c in range(y.shape[1] // LANES):
        yc = y[:, c * LANES:(c + 1) * LANES]
        partner = jnp.where(first, pltpu.roll(yc, LANES - HEAD_DIM // 2, 1), pltpu.roll(yc, HEAD_DIM // 2, 1))
        o_ref[:, c * LANES:(c + 1) * LANES] = (yc * cos + partner * sin).astype(o_ref.dtype)


def _rms_proj(x, g, w, cos, sin, *, seq, rope, out_dtype):
    t, d = x.shape
    n = w.shape[1]
    tm = ROW_TILE
    nseq = seq // tm
    return pl.pallas_call(
        functools.partial(_rms_proj_kernel, rope=rope),
        grid=(t // tm,),
        in_specs=[
            pl.BlockSpec((tm, d), lambda i: (i, 0)),
            pl.BlockSpec((1, d), lambda i: (0, 0)),
            pl.BlockSpec((d, n), lambda i: (0, 0)),
            pl.BlockSpec((tm, LANES), lambda i: (i % nseq, 0)),
            pl.BlockSpec((tm, LANES), lambda i: (i % nseq, 0)),
        ],
        out_specs=pl.BlockSpec((tm, n), lambda i: (i, 0)),
        out_shape=jax.ShapeDtypeStruct((t, n), out_dtype),
        compiler_params=_params("parallel"),
        name="rms_proj_rope" if rope else "rms_proj",
    )(x, g, w, cos, sin)


def _out_proj_kernel(a1_ref, a2_ref, w1_ref, w2_ref, r_ref, o_ref):
    o_ref[...] = r_ref[...] + _dot(a1_ref[...], w1_ref[...]) + _dot(a2_ref[...], w2_ref[...])


def _out_proj(a1, a2, w1, w2, res):
    t, d = res.shape
    k1, k2 = a1.shape[1], a2.shape[1]
    tm = ROW_TILE
    return pl.pallas_call(
        _out_proj_kernel,
        grid=(t // tm,),
        in_specs=[
            pl.BlockSpec((tm, k1), lambda i: (i, 0)),
            pl.BlockSpec((tm, k2), lambda i: (i, 0)),
            pl.BlockSpec((k1, d), lambda i: (0, 0)),
            pl.BlockSpec((k2, d), lambda i: (0, 0)),
            pl.BlockSpec((tm, d), lambda i: (i, 0)),
        ],
        out_specs=pl.BlockSpec((tm, d), lambda i: (i, 0)),
        out_shape=jax.ShapeDtypeStruct((t, d), F32),
        compiler_params=_params("parallel"),
        name="out_proj",
    )(a1, a2, w1, w2, res)


def _cumsum_rows(x):
    rows = x.shape[0]
    row = lax.broadcasted_iota(jnp.int32, x.shape, 0)
    k = 1
    while k < rows:
        x = x + jnp.where(row >= k, pltpu.roll(x, k, 0), 0.0)
        k *= 2
    return x


def _fox_gate_kernel(z_ref, b_ref, ccol_ref, crow_ref, *, tk):
    z = z_ref[...] + b_ref[...]
    logf = jnp.minimum(z, 0.0) - jnp.log(1.0 + jnp.exp(-jnp.abs(z)))
    c = _cumsum_rows(logf)
    ccol_ref[...] = c
    for j in range(c.shape[0] // tk):
        ct = c[j * tk:(j + 1) * tk, :].T
        crow_ref[0, :, j * tk:(j + 1) * tk] = ct[:SUBLANES, :]


def _fox_gate(z, b_pad, *, batch, seq):
    tk = ATTN_TILE
    return pl.pallas_call(
        functools.partial(_fox_gate_kernel, tk=tk),
        grid=(batch,),
        in_specs=[
            pl.BlockSpec((seq, LANES), lambda b: (b, 0)),
            pl.BlockSpec((1, LANES), lambda b: (0, 0)),
        ],
        out_specs=[
            pl.BlockSpec((seq, LANES), lambda b: (b, 0)),
            pl.BlockSpec((1, SUBLANES, seq), lambda b: (b, 0, 0)),
        ],
        out_shape=[
            jax.ShapeDtypeStruct((batch * seq, LANES), F32),
            jax.ShapeDtypeStruct((batch, SUBLANES, seq), F32),
        ],
        compiler_params=_params("parallel"),
        name="fox_gate",
    )(z, b_pad)


def _softmax_pv(s, v):
    m = jnp.max(s, axis=-1, keepdims=True)
    p = jnp.exp(s - m)
    l = jnp.sum(p, axis=-1, keepdims=True)
    return _dot(p.astype(BF16), v) / l


def _mask_last_block(s, tq):
    n = s.shape[1]
    causal = lax.broadcasted_iota(jnp.int32, (tq, tq), 1) <= lax.broadcasted_iota(jnp.int32, (tq, tq), 0)
    tail = jnp.where(causal, s[:, n - tq:], NEG_INF)
    return tail if n == tq else jnp.concatenate([s[:, :n - tq], tail], axis=1)


def _fox_kernel(q_ref, k_ref, v_ref, cc_ref, cr_ref, o_ref, *, tq, scale):
    hp = pl.program_id(1)
    seq = q_ref.shape[0]
    half = _lane_half((1, LANES))
    lane = lax.broadcasted_iota(jnp.int32, (tq, LANES), 1)
    for i in range(seq // tq):
        n = (i + 1) * tq
        q = q_ref[i * tq:n, :] * scale
        cc = cc_ref[i * tq:n, :]
        k = k_ref[0:n, :]
        v = v_ref[0:n, :]
        outs = []
        for e in range(2):
            h = 2 * hp + e
            qe = jnp.where(half == e, q, jnp.zeros_like(q))
            cq = jnp.sum(jnp.where(lane == h, cc, 0.0), axis=-1, keepdims=True)
            ck = cr_ref[0, pl.ds(h, 1), 0:n]
            s = _mask_last_block(_dot_nt(qe, k) + cq - ck, tq)
            outs.append(_softmax_pv(s, v))
        o_ref[i * tq:n, :] = jnp.where(half == 0, outs[0], outs[1]).astype(o_ref.dtype)


def _fox_attention(p1, ccol, crow, *, batch, seq):
    tq = min(ATTN_TILE, seq)
    ng = FOX_HEADS // 2
    return pl.pallas_call(
        functools.partial(_fox_kernel, tq=tq, scale=HEAD_DIM ** -0.5),
        grid=(batch, ng),
        in_specs=[
            pl.BlockSpec((seq, LANES), lambda b, g: (b, g)),
            pl.BlockSpec((seq, LANES), lambda b, g: (b, ng + g)),
            pl.BlockSpec((seq, LANES), lambda b, g: (b, 2 * ng + g)),
            pl.BlockSpec((seq, LANES), lambda b, g: (b, 0)),
            pl.BlockSpec((1, SUBLANES, seq), lambda b, g: (b, 0, 0)),
        ],
        out_specs=pl.BlockSpec((seq, LANES), lambda b, g: (b, g)),
        out_shape=jax.ShapeDtypeStruct((batch * seq, ng * LANES), BF16),
        compiler_params=_params("parallel", "parallel"),
        name="fox_attention",
    )(p1, p1, p1, ccol, crow)


def _diff_kernel(q_ref, k_ref, v_ref, lam_ref, g_ref, o_ref, *, tq, scale, lam_init):
    seq = q_ref.shape[0]
    half = _lane_half((1, LANES))
    lv = lam_ref[...]
    lam = (jnp.exp(jnp.sum(lv[0:1, :] * lv[1:2, :], axis=-1, keepdims=True))
           - jnp.exp(jnp.sum(lv[2:3, :] * lv[3:4, :], axis=-1, keepdims=True)) + lam_init)
    gn = g_ref[...] * (1.0 - lam_init)
    for i in range(seq // tq):
        n = (i + 1) * tq
        q = q_ref[i * tq:n, :] * scale
        k = k_ref[0:n, :]
        v = v_ref[0:n, :]
        outs = []
        for e in range(2):
            qe = jnp.where(half == e, q, jnp.zeros_like(q))
            outs.append(_softmax_pv(_mask_last_block(_dot_nt(qe, k), tq), v))
        o = outs[0] - lam * outs[1]
        var = jnp.mean(o * o, axis=-1, keepdims=True)
        o_ref[i * tq:n, :] = (o * lax.rsqrt(var + NORM_EPS) * gn).astype(o_ref.dtype)


def _diff_attention(pqk, pv, lam_vecs, g, *, batch, seq, lam_init):
    tq = min(ATTN_TILE, seq)
    ng = DIFF_HEADS
    return pl.pallas_call(
        functools.partial(_diff_kernel, tq=tq, scale=DIFF_DIM ** -0.5, lam_init=lam_init),
        grid=(batch, ng),
        in_specs=[
            pl.BlockSpec((seq, LANES), lambda b, g: (b, g)),
            pl.BlockSpec((seq, LANES), lambda b, g: (b, ng + g)),
            pl.BlockSpec((seq, LANES), lambda b, g: (b, g)),
            pl.BlockSpec(lam_vecs.shape, lambda b, g: (0, 0)),
            pl.BlockSpec((1, LANES), lambda b, g: (0, 0)),
        ],
        out_specs=pl.BlockSpec((seq, LANES), lambda b, g: (b, g)),
        out_shape=jax.ShapeDtypeStruct((batch * seq, ng * LANES), BF16),
        compiler_params=_params("parallel", "parallel"),
        name="diff_attention",
    )(pqk, pqk, pv, lam_vecs, g)


def _sortable_key(x):
    bits = pltpu.bitcast(x, jnp.int32)
    return jnp.where(bits < 0, bits ^ jnp.int32(0x7FFFFFFF), bits)


def _dsa_kernel(q_ref, kd_ref, iq_ref, ik_ref, vd_ref, z_ref, o_ref, key_ref, madd_ref,
                *, tq, n_sel, scale, idx_scale, w_scale):
    i = pl.program_id(1)
    seq = key_ref.shape[1]
    nq = seq // tq
    span = 2 if nq % 2 == 0 else 1
    half = _lane_half((1, LANES))
    row = lax.broadcasted_iota(jnp.int32, (tq, tq), 0)
    col = lax.broadcasted_iota(jnp.int32, (tq, tq), 1)
    row_g = row + i * tq
    tri = jnp.where(row < col, 1.0, 0.0).astype(BF16)

    def block(n):
        z = z_ref[...]
        iq = iq_ref[...]
        ik = ik_ref[0:n, :]
        sc = jnp.zeros((tq, n), F32)
        for h in range(IDX_HEADS):
            grp = iq[:, (h // 2) * LANES:(h // 2 + 1) * LANES]
            iq_h = jnp.where(half == h % 2, grp, jnp.zeros_like(grp))
            w_h = z[:, SUBLANES + h:SUBLANES + h + 1] * w_scale
            sc = sc + w_h * jnp.maximum(_dot_nt(iq_h, ik) * idx_scale, 0.0)
        for j in range(n // tq):
            scj = jnp.where(col + j * tq <= row_g, sc[:, j * tq:(j + 1) * tq], -jnp.inf)
            key_ref[:, j * tq:(j + 1) * tq] = _sortable_key(scj)

        def count(pred):
            return jnp.sum(jnp.where(pred, 1.0, 0.0), axis=-1, keepdims=True)

        thr0 = jnp.where(count(key_ref[:, 0:n] >= 0) >= n_sel, 0, INT_MIN).astype(jnp.int32)

        def bit_body(it, thr):
            cand = thr | jnp.left_shift(jnp.int32(1), 30 - it)
            return jnp.where(count(key_ref[:, 0:n] >= cand) >= n_sel, cand, thr)

        thr = lax.fori_loop(0, 31, bit_body, thr0)
        need = n_sel - count(key_ref[:, 0:n] > thr)

        off = jnp.zeros((tq, 1), F32)
        for j in range(n // tq):
            kj = key_ref[:, j * tq:(j + 1) * tq]
            eq = kj == thr
            eqf = jnp.where(eq, 1.0, 0.0)
            rank = _dot(eqf.astype(BF16), tri) + off
            take = jnp.where(kj > thr, 1.0, jnp.where(eq, jnp.where(rank < need, 1.0, 0.0), 0.0))
            valid = col + j * tq <= row_g
            madd_ref[:, j * tq:(j + 1) * tq] = jnp.where(valid, jnp.where(take > 0.0, 0.0, NEG_INF), NEG_INF)
            off = off + jnp.sum(eqf, axis=-1, keepdims=True)

        q = q_ref[...] * scale
        for p in range(DSA_HEADS // 2):
            qg = q[:, p * LANES:(p + 1) * LANES]
            g = p // (DSA_HEADS // DSA_KV_HEADS // 2)
            k = kd_ref[0:n, g * LANES:(g + 1) * LANES]
            v = vd_ref[0:n, g * LANES:(g + 1) * LANES]
            outs = []
            for e in range(2):
                qe = jnp.where(half == e, qg, jnp.zeros_like(qg))
                outs.append(_softmax_pv(_dot_nt(qe, k) + madd_ref[:, 0:n], v))
            o_ref[:, p * LANES:(p + 1) * LANES] = jnp.where(half == 0, outs[0], outs[1]).astype(o_ref.dtype)

    for c in range(nq // span):
        pl.when(i // span == c)(functools.partial(block, (c + 1) * span * tq))


def _dsa_attention(p2, p1, z, *, batch, seq):
    tq = min(ATTN_TILE, seq)
    nq = seq // tq
    n_sel = min(DSA_TOPK, seq // 4)
    qw = DSA_HEADS * HEAD_DIM
    return pl.pallas_call(
        functools.partial(_dsa_kernel, tq=tq, n_sel=float(n_sel), scale=HEAD_DIM ** -0.5,
                          idx_scale=IDX_DIM ** -0.5, w_scale=IDX_HEADS ** -0.5),
        grid=(batch, nq),
        in_specs=[
            pl.BlockSpec((tq, qw), lambda b, i: (b * nq + i, 0)),
            pl.BlockSpec((seq, 2 * LANES), lambda b, i: (b, 2)),
            pl.BlockSpec((tq, 2 * LANES), lambda b, i: (b * nq + i, 3)),
            pl.BlockSpec((seq, LANES), lambda b, i: (b, 8)),
            pl.BlockSpec((seq, 2 * LANES), lambda b, i: (b, 6)),
            pl.BlockSpec((tq, LANES), lambda b, i: (b * nq + i, 0)),
        ],
        out_specs=pl.BlockSpec((tq, qw), lambda b, i: (b * nq + i, 0)),
        out_shape=jax.ShapeDtypeStruct((batch * seq, qw), BF16),
        scratch_shapes=[pltpu.VMEM((tq, seq), jnp.int32), pltpu.VMEM((tq, seq), F32)],
        compiler_params=_params("parallel", "arbitrary"),
        name="dsa_attention",
    )(p2, p2, p2, p2, p1, z)


def _hgrn_kernel(q_ref, f_ref, i_ref, g_ref, lbl_ref, gn_ref, o_ref, b_scr, *, chunk, layer):
    seq = q_ref.shape[0]
    half = _lane_half((1, LANES))
    logits = lbl_ref[...]
    pe = jnp.exp(logits - jnp.max(logits, axis=0, keepdims=True))
    prob = pe / jnp.sum(pe, axis=0, keepdims=True)
    csum = prob[0:1, :]
    for d in range(1, layer + 1):
        csum = csum + prob[d:d + 1, :]
    lb = csum - prob[0:1, :]

    rowi = lax.broadcasted_iota(jnp.int32, (chunk, 1), 0)
    xor_rc = (lax.broadcasted_iota(jnp.int32, (chunk, chunk), 0)
              ^ lax.broadcasted_iota(jnp.int32, (chunk, chunk), 1))
    same_head = (lax.broadcasted_iota(jnp.int32, (LANES, LANES), 0) // HEAD_DIM
                 == lax.broadcasted_iota(jnp.int32, (LANES, LANES), 1) // HEAD_DIM)
    ones_bd = jnp.where(same_head, 1.0, 0.0).astype(BF16)
    gn = gn_ref[...]

    def chunk_body(c, state_t):
        r0 = pl.multiple_of(c * chunk, chunk)
        q = _silu(q_ref[pl.ds(r0, chunk), :])
        f = lb + (1.0 - lb) * _sigmoid(f_ref[pl.ds(r0, chunk), :])
        kc = 1.0 - f
        v = i_ref[pl.ds(r0, chunk), :]
        b = _cumsum_rows(jnp.log(f))
        b_scr[...] = b
        b_last = b_scr[chunk - 1:chunk, :]

        o = _dot_nt((q * jnp.exp(b)).astype(BF16), state_t.astype(BF16))
        k2 = (kc * jnp.exp(b_last - b)).astype(BF16)
        upd = _dot_tn(v.astype(BF16), k2)
        new_state = state_t * jnp.exp(b_last) + jnp.where(same_head, upd, 0.0)

        s_tot = [jnp.zeros((chunk, chunk), F32), jnp.zeros((chunk, chunk), F32)]
        m = chunk
        while m > SUB:
            hm = m // 2
            pieces = [jnp.broadcast_to(b_scr[blk * m + hm - 1:blk * m + hm, :], (m, LANES))
                      for blk in range(chunk // m)]
            ref = pieces[0] if len(pieces) == 1 else jnp.concatenate(pieces, axis=0)
            right = (rowi % m) >= hm
            qm = jnp.where(right, q * jnp.exp(jnp.where(right, b - ref, 0.0)), 0.0).astype(BF16)
            km = jnp.where(right, 0.0, kc * jnp.exp(jnp.where(right, 0.0, ref - b))).astype(BF16)
            for e in range(2):
                s = _dot_nt(jnp.where(half == e, qm, jnp.zeros_like(qm)), km)
                s_tot[e] = s_tot[e] + (s if m == chunk else jnp.where(xor_rc < m, s, 0.0))
            m = hm
        vb = v.astype(BF16)
        for e in range(2):
            o = o + _dot(s_tot[e].astype(BF16), jnp.where(half == e, vb, jnp.zeros_like(vb)))

        for d in range(SUB):
            ok = (rowi % SUB) >= d
            bs = b if d == 0 else pltpu.roll(b, d, 0)
            ks = kc if d == 0 else pltpu.roll(kc, d, 0)
            vs = v if d == 0 else pltpu.roll(v, d, 0)
            x = jnp.where(ok, q * ks * jnp.exp(jnp.where(ok, b - bs, 0.0)), 0.0)
            o = o + _dot(x.astype(BF16), ones_bd) * vs

        sq = o * o
        s0 = jnp.sum(jnp.where(half == 0, sq, 0.0), axis=-1, keepdims=True)
        s1 = jnp.sum(jnp.where(half == 1, sq, 0.0), axis=-1, keepdims=True)
        var = jnp.where(half == 0, s0, s1) * (1.0 / HEAD_DIM)
        y = o * lax.rsqrt(var + NORM_EPS) * gn * _silu(g_ref[pl.ds(r0, chunk), :])
        o_ref[pl.ds(r0, chunk), :] = y.astype(o_ref.dtype)
        return new_state

    lax.fori_loop(0, seq // chunk, chunk_body, jnp.zeros((LANES, LANES), F32))


def _hgrn(hproj, lb_logits, gn, *, batch, seq, layer):
    ng = HGRN_HEADS // 2
    chunk = min(HGRN_CHUNK, seq)
    spec = lambda off: pl.BlockSpec((seq, LANES), lambda b, g: (b, off * ng + g))
    return pl.pallas_call(
        functools.partial(_hgrn_kernel, chunk=chunk, layer=layer),
        grid=(batch, ng),
        in_specs=[spec(0), spec(1), spec(2), spec(3),
                  pl.BlockSpec((lb_logits.shape[0], LANES), lambda b, g: (0, g)),
                  pl.BlockSpec((1, LANES), lambda b, g: (0, 0))],
        out_specs=pl.BlockSpec((seq, LANES), lambda b, g: (b, g)),
        out_shape=jax.ShapeDtypeStruct((batch * seq, ng * LANES), BF16),
        scratch_shapes=[pltpu.VMEM((chunk, LANES), F32)],
        compiler_params=_params("parallel", "parallel"),
        name="hgrn2",
    )(hproj, hproj, hproj, hproj, lb_logits, gn)


def _router_kernel(x_ref, g_ref, w_ref, o_ref, xn_ref):
    x = x_ref[...]
    var = jnp.mean(x * x, axis=-1, keepdims=True)
    xn = x * lax.rsqrt(var + NORM_EPS) * g_ref[...]
    xn_ref[...] = xn.astype(BF16)
    logits = jnp.dot(xn, w_ref[...], precision=lax.Precision.HIGHEST, preferred_element_type=F32)
    lane = lax.broadcasted_iota(jnp.int32, logits.shape, 1)
    lg = jnp.where(lane < N_EXPERTS, logits, -jnp.inf)
    m1 = jnp.max(lg, axis=-1, keepdims=True)
    i1 = jnp.min(jnp.where(lg == m1, lane, LANES), axis=-1, keepdims=True)
    lg2 = jnp.where(lane == i1, -jnp.inf, lg)
    m2 = jnp.max(lg2, axis=-1, keepdims=True)
    i2 = jnp.min(jnp.where(lg2 == m2, lane, LANES), axis=-1, keepdims=True)
    t = jnp.exp(m2 - m1)
    den = 1.0 + t
    o_ref[...] = (jnp.where(lane == 0, i1.astype(F32), 0.0) + jnp.where(lane == 1, i2.astype(F32), 0.0)
                  + jnp.where(lane == 2, 1.0 / den, 0.0) + jnp.where(lane == 3, t / den, 0.0))


def _router(x, g, w_pad):
    t, d = x.shape
    tm = ROW_TILE
    return pl.pallas_call(
        _router_kernel,
        grid=(t // tm,),
        in_specs=[pl.BlockSpec((tm, d), lambda i: (i, 0)),
                  pl.BlockSpec((1, d), lambda i: (0, 0)),
                  pl.BlockSpec((d, LANES), lambda i: (0, 0))],
        out_specs=[pl.BlockSpec((tm, LANES), lambda i: (i, 0)),
                   pl.BlockSpec((tm, d), lambda i: (i, 0))],
        out_shape=[jax.ShapeDtypeStruct((t, LANES), F32), jax.ShapeDtypeStruct((t, d), BF16)],
        compiler_params=_params("parallel"),
        name="router",
    )(x, g, w_pad)


def _moe_gather_kernel(osub_ref, iblk_ref, first_ref, valid_ref, ids_ref, x_ref, o_ref, *, tw):
    w = pl.program_id(0)

    @pl.when(first_ref[w] == 1)
    def _():
        o_ref[...] = jnp.zeros_like(o_ref)

    @pl.when(valid_ref[w] == 1)
    def _():
        tok = ids_ref[0, 0:1, :]
        cand = lax.broadcasted_iota(jnp.int32, (tw, MOE_SUB), 0) + iblk_ref[w] * tw
        onehot_t = jnp.where(cand == tok, 1.0, 0.0).astype(BF16)
        o_ref[...] += _dot_tn(onehot_t, x_ref[...]).astype(o_ref.dtype)


def _moe_gather(plan, ids3, xn):
    t, d = xn.shape
    n_sub = ids3.shape[0]
    osub, iblk, first, valid = plan
    tw = MOE_TOKEN_BLOCK
    return pl.pallas_call(
        functools.partial(_moe_gather_kernel, tw=tw),
        grid_spec=pltpu.PrefetchScalarGridSpec(
            num_scalar_prefetch=4,
            grid=(osub.shape[0],),
            in_specs=[pl.BlockSpec((1, SUBLANES, MOE_SUB), lambda w, o, i, f, v: (o[w], 0, 0)),
                      pl.BlockSpec((tw, d), lambda w, o, i, f, v: (i[w], 0))],
            out_specs=pl.BlockSpec((MOE_SUB, d), lambda w, o, i, f, v: (o[w], 0)),
        ),
        out_shape=jax.ShapeDtypeStruct((n_sub * MOE_SUB, d), BF16),
        compiler_params=_params("arbitrary"),
        name="moe_gather",
    )(osub, iblk, first, valid, ids3, xn)


def _moe_combine_kernel(otile_ref, isub_ref, first_ref, last_ref, valid_ref, ids_ref, y_ref, x_ref, gf_ref,
                        o_ref, acc_ref, *, tt, final_norm):
    w = pl.program_id(0)

    @pl.when(first_ref[w] == 1)
    def _():
        acc_ref[...] = x_ref[...]

    @pl.when(valid_ref[w] == 1)
    def _():
        tok = ids_ref[0, 0:1, :]
        cand = lax.broadcasted_iota(jnp.int32, (tt, MOE_SUB), 0) + otile_ref[w] * tt
        onehot = jnp.where(cand == tok, 1.0, 0.0).astype(BF16)
        acc_ref[...] += _dot(onehot, y_ref[...])

    @pl.when(last_ref[w] == 1)
    def _():
        out = acc_ref[...]
        if final_norm:
            var = jnp.mean(out * out, axis=-1, keepdims=True)
            out = out * lax.rsqrt(var + NORM_EPS) * gf_ref[...]
        o_ref[...] = out


def _moe_combine(plan, ids3, y, x, gf, *, final_norm):
    t, d = x.shape
    otile, isub, first, last, valid = plan
    tt = ROW_TILE
    return pl.pallas_call(
        functools.partial(_moe_combine_kernel, tt=tt, final_norm=final_norm),
        grid_spec=pltpu.PrefetchScalarGridSpec(
            num_scalar_prefetch=5,
            grid=(otile.shape[0],),
            in_specs=[pl.BlockSpec((1, SUBLANES, MOE_SUB), lambda w, o, i, f, l, v: (i[w], 0, 0)),
                      pl.BlockSpec((MOE_SUB, d), lambda w, o, i, f, l, v: (i[w], 0)),
                      pl.BlockSpec((tt, d), lambda w, o, i, f, l, v: (o[w], 0)),
                      pl.BlockSpec((1, d), lambda w, o, i, f, l, v: (0, 0))],
            out_specs=pl.BlockSpec((tt, d), lambda w, o, i, f, l, v: (o[w], 0)),
            scratch_shapes=[pltpu.VMEM((tt, d), F32)],
        ),
        out_shape=jax.ShapeDtypeStruct((t, d), F32),
        compiler_params=_params("arbitrary"),
        name="moe_combine",
    )(otile, isub, first, last, valid, ids3, y, x, gf)


def _moe_ffn_kernel(te_ref, act_ref, x_ref, gate_ref, wg_ref, wu_ref, wd_ref, o_ref, acc_ref, *, tm):
    i = pl.program_id(0)
    f = pl.program_id(1)

    @pl.when(act_ref[i] == 1)
    def _():
        @pl.when(f == 0)
        def _():
            acc_ref[...] = jnp.zeros_like(acc_ref)

        x = x_ref[...]
        mid = (_silu(_dot(x, wg_ref[0])) * _dot(x, wu_ref[0])).astype(BF16)
        acc_ref[...] += _dot(mid, wd_ref[0])

        @pl.when(f == pl.num_programs(1) - 1)
        def _():
            gates = gate_ref[0]
            for k in range(tm // LANES):
                col = jnp.broadcast_to(gates[k:k + 1, :], (LANES, LANES)).T
                rows = acc_ref[k * LANES:(k + 1) * LANES, :]
                for c in range(rows.shape[1] // LANES):
                    o_ref[k * LANES:(k + 1) * LANES, c * LANES:(c + 1) * LANES] = (
                        rows[:, c * LANES:(c + 1) * LANES] * col).astype(o_ref.dtype)

    @pl.when(jnp.logical_and(act_ref[i] == 0, f == 0))
    def _():
        o_ref[...] = jnp.zeros_like(o_ref)


def _moe_ffn(tile_e, tile_act, xs, gates3, wg, wu, wd, *, tf):
    p, d = xs.shape
    ff = wg.shape[2]
    tm = MOE_ROW_TILE
    return pl.pallas_call(
        functools.partial(_moe_ffn_kernel, tm=tm),
        grid_spec=pltpu.PrefetchScalarGridSpec(
            num_scalar_prefetch=2,
            grid=(p // tm, ff // tf),
            in_specs=[pl.BlockSpec((tm, d), lambda i, f, te, ta: (i, 0)),
                      pl.BlockSpec((1, tm // LANES, LANES), lambda i, f, te, ta: (i, 0, 0)),
                      pl.BlockSpec((1, d, tf), lambda i, f, te, ta: (te[i], 0, f * ta[i])),
                      pl.BlockSpec((1, d, tf), lambda i, f, te, ta: (te[i], 0, f * ta[i])),
                      pl.BlockSpec((1, tf, d), lambda i, f, te, ta: (te[i], f * ta[i], 0))],
            out_specs=pl.BlockSpec((tm, d), lambda i, f, te, ta: (i, 0)),
            scratch_shapes=[pltpu.VMEM((tm, d), F32)],
        ),
        out_shape=jax.ShapeDtypeStruct((p, d), BF16),
        compiler_params=_params("arbitrary", "arbitrary"),
        name="moe_ffn",
    )(tile_e, tile_act, xs, gates3, wg, wu, wd)


def _moe_plan(info, t):
    tm, sub, tw, tt = MOE_ROW_TILE, MOE_SUB, MOE_TOKEN_BLOCK, ROW_TILE
    n_e = N_EXPERTS
    a = 2 * t
    p = a + n_e * tm
    n_sub = p // sub
    e_flat = info[:, 0:2].astype(jnp.int32).reshape(a)
    g_flat = info[:, 2:4].reshape(a)
    onehot = (e_flat[:, None] == jnp.arange(n_e, dtype=jnp.int32)[None, :]).astype(jnp.int32)
    csum = jnp.cumsum(onehot, axis=0)
    rank = jnp.sum(csum * onehot, axis=1) - 1
    counts = csum[-1]
    padded = ((counts + tm - 1) // tm) * tm
    ends = jnp.cumsum(padded)
    starts = ends - padded
    pos = starts[e_flat] + rank
    tok_row = jnp.full((p,), -1, jnp.int32).at[pos].set(jnp.arange(a, dtype=jnp.int32) // 2)
    gate_row = jnp.zeros((p,), F32).at[pos].set(g_flat)
    ids3 = jnp.broadcast_to(tok_row.reshape(n_sub, 1, sub), (n_sub, SUBLANES, sub))
    gates3 = gate_row.reshape(p // tm, tm // LANES, LANES)

    tile_start = jnp.arange(p // tm, dtype=jnp.int32) * tm
    tile_e = jnp.minimum(jnp.searchsorted(ends, tile_start, side='right'), n_e - 1).astype(jnp.int32)
    tile_act = (tile_start < ends[-1]).astype(jnp.int32)

    tok2 = tok_row.reshape(n_sub, sub)
    has = jnp.any(tok2 >= 0, axis=1)
    tmin = jnp.min(jnp.where(tok2 >= 0, tok2, t), axis=1)
    tmax = jnp.max(tok2, axis=1)

    def worklist(lo, hi, cnt, length):
        cend = jnp.cumsum(cnt)
        total = cend[-1]
        w = jnp.arange(length, dtype=jnp.int32)
        s = jnp.minimum(jnp.searchsorted(cend, w, side='right'), n_sub - 1).astype(jnp.int32)
        k = w - (cend[s] - cnt[s])
        valid = w < total
        blk = jnp.where(valid, lo[s] + k, 0)
        return s, blk.astype(jnp.int32), k, valid

    lo = jnp.where(has, tmin // tw, 0)
    hi = jnp.where(has, tmax // tw, 0)
    cnt = (hi - lo + 1).astype(jnp.int32)
    len_g = n_sub + n_e * (t // tw - 1)
    s, blk, k, valid = worklist(lo, hi, cnt, len_g)
    last_valid = jnp.sum(valid.astype(jnp.int32)) - 1
    s = jnp.where(valid, s, s[last_valid])
    blk = jnp.where(valid, blk, blk[last_valid])
    gather_plan = (s, blk, jnp.logical_and(valid, k == 0).astype(jnp.int32), valid.astype(jnp.int32))

    lo = jnp.where(has, tmin // tt, 0)
    hi = jnp.where(has, tmax // tt, -1)
    cnt = (hi - lo + 1).astype(jnp.int32)
    len_c = n_sub + n_e * (t // tt - 1)
    s, tile, k, valid = worklist(lo, hi, cnt, len_c)
    n_tt = t // tt
    order = jnp.argsort(jnp.where(valid, tile, n_tt), stable=True)
    s, tile, valid = s[order], tile[order], valid[order]
    tile = jnp.where(valid, tile, n_tt - 1)
    last_valid = jnp.sum(valid.astype(jnp.int32)) - 1
    s = jnp.where(valid, s, s[last_valid])
    prev = jnp.concatenate([jnp.full((1,), -1, jnp.int32), tile[:-1]])
    nxt = jnp.concatenate([tile[1:], jnp.full((1,), -1, jnp.int32)])
    combine_plan = (tile, s, (tile != prev).astype(jnp.int32), (tile != nxt).astype(jnp.int32),
                    valid.astype(jnp.int32))
    return ids3, gates3, tile_e, tile_act, gather_plan, combine_plan


def _ffn_kernel(x_ref, g_ref, wg_ref, wu_ref, wd_ref, gf_ref, o_ref, xn_ref, acc_ref, *, final_norm):
    f = pl.program_id(1)

    @pl.when(f == 0)
    def _():
        x = x_ref[...]
        var = jnp.mean(x * x, axis=-1, keepdims=True)
        xn_ref[...] = (x * lax.rsqrt(var + NORM_EPS) * g_ref[...]).astype(BF16)
        acc_ref[...] = jnp.zeros_like(acc_ref)

    xn = xn_ref[...]
    mid = (_silu(_dot(xn, wg_ref[...])) * _dot(xn, wu_ref[...])).astype(BF16)
    acc_ref[...] += _dot(mid, wd_ref[...])

    @pl.when(f == pl.num_programs(1) - 1)
    def _():
        out = x_ref[...] + acc_ref[...]
        if final_norm:
            var = jnp.mean(out * out, axis=-1, keepdims=True)
            out = out * lax.rsqrt(var + NORM_EPS) * gf_ref[...]
        o_ref[...] = out


def _ffn(x, g, wg, wu, wd, gf, *, tf, final_norm):
    t, d = x.shape
    ff = wg.shape[1]
    tm = ROW_TILE
    return pl.pallas_call(
        functools.partial(_ffn_kernel, final_norm=final_norm),
        grid=(t // tm, ff // tf),
        in_specs=[
            pl.BlockSpec((tm, d), lambda i, f: (i, 0)),
            pl.BlockSpec((1, d), lambda i, f: (0, 0)),
            pl.BlockSpec((d, tf), lambda i, f: (0, f)),
            pl.BlockSpec((d, tf), lambda i, f: (0, f)),
            pl.BlockSpec((tf, d), lambda i, f: (f, 0)),
            pl.BlockSpec((1, d), lambda i, f: (0, 0)),
        ],
        out_specs=pl.BlockSpec((tm, d), lambda i, f: (i, 0)),
        out_shape=jax.ShapeDtypeStruct((t, d), F32),
        scratch_shapes=[pltpu.VMEM((tm, d), BF16), pltpu.VMEM((tm, d), F32)],
        compiler_params=_params("parallel", "arbitrary"),
        name="ffn",
    )(x, g, wg, wu, wd, gf)


def _rope_tables(seq):
    half = HEAD_DIM // 2
    inv = ROPE_THETA ** (-jnp.arange(half, dtype=F32) * 2.0 / HEAD_DIM)
    ang = jnp.arange(seq, dtype=F32)[:, None] * inv[None, :]
    cos = jnp.cos(ang)
    sin = jnp.sin(ang)
    cos_t = jnp.concatenate([cos, cos, cos, cos], axis=-1)
    sin_t = jnp.concatenate([-sin, sin, -sin, sin], axis=-1)
    return cos_t, sin_t


def _dup_heads(w):
    d, n = w.shape
    w = w.reshape(d, n // HEAD_DIM, 1, HEAD_DIM)
    return jnp.broadcast_to(w, (d, n // HEAD_DIM, 2, HEAD_DIM)).reshape(d, 2 * n)


def kernel(x, ln_mix_e, w_in_e, b_fox_f, w_out_e, ln_ffn_e, w_ffn_gate, w_ffn_up, w_ffn_down,
           ln_mix_o, w_in_o, hgrn_lb_logits, hgrn_norm_g, diff_lambda, diff_norm_g, w_out_o,
           ln_ffn_o, w_router, w_exp_gate, w_exp_up, w_exp_down, ln_final):
    batch, seq, d = x.shape
    t = batch * seq
    depth = ln_mix_e.shape[0] + ln_mix_o.shape[0]
    xs = x.reshape(t, d)
    cos_t, sin_t = _rope_tables(seq)
    row = lambda v: v.reshape(1, -1)
    fw = FOX_HEADS * HEAD_DIM
    dw = DSA_HEADS * HEAD_DIM
    kvw = DSA_KV_HEADS * HEAD_DIM
    hw = HGRN_HEADS * HEAD_DIM
    qw = DIFF_HEADS * 2 * DIFF_DIM

    for layer in range(depth):
        j = layer // 2
        if layer % 2 == 0:
            w = w_in_e[j]
            o = 0
            fq, fk, fv = w[:, o:o + fw], w[:, o + fw:o + 2 * fw], w[:, o + 2 * fw:o + 3 * fw]
            o += 3 * fw
            ffw = w[:, o:o + FOX_HEADS]
            o += FOX_HEADS
            dq = w[:, o:o + dw]
            o += dw
            dk = w[:, o:o + kvw]
            o += kvw
            dv = w[:, o:o + kvw]
            o += kvw
            iqw = w[:, o:o + IDX_HEADS * IDX_DIM]
            o += IDX_HEADS * IDX_DIM
            ikw = w[:, o:o + IDX_DIM]
            o += IDX_DIM
            iww = w[:, o:o + IDX_HEADS]
            w1 = jnp.concatenate([fq, fk, fv, _dup_heads(dv)], axis=1).astype(BF16)
            w2 = jnp.concatenate([dq, _dup_heads(dk), iqw, _dup_heads(ikw)], axis=1).astype(BF16)
            w3 = jnp.concatenate([ffw, iww, jnp.zeros((d, LANES - FOX_HEADS - IDX_HEADS), F32)], axis=1).astype(BF16)
            g = row(ln_mix_e[j])
            p1 = _rms_proj(xs, g, w1, cos_t, sin_t, seq=seq, rope=False, out_dtype=BF16)
            p2 = _rms_proj(xs, g, w2, cos_t, sin_t, seq=seq, rope=True, out_dtype=BF16)
            z = _rms_proj(xs, g, w3, cos_t, sin_t, seq=seq, rope=False, out_dtype=F32)
            b_pad = jnp.concatenate([b_fox_f[j], jnp.zeros((LANES - FOX_HEADS,), F32)]).reshape(1, LANES)
            ccol, crow = _fox_gate(z, b_pad, batch=batch, seq=seq)
            fox_o = _fox_attention(p1, ccol, crow, batch=batch, seq=seq)
            dsa_o = _dsa_attention(p2, p1, z, batch=batch, seq=seq)
            wo = w_out_e[j].astype(BF16)
            xs = _out_proj(fox_o, dsa_o, wo[:fw], wo[fw:], xs)
            xs = _ffn(xs, row(ln_ffn_e[j]), w_ffn_gate[j].astype(BF16), w_ffn_up[j].astype(BF16),
                      w_ffn_down[j].astype(BF16), row(ln_final), tf=1408, final_norm=(layer == depth - 1))
        else:
            w = w_in_o[j]
            lam_init = 0.8 - 0.6 * math.exp(-0.3 * layer)
            g = row(ln_mix_o[j])
            hp = _rms_proj(xs, g, w[:, :4 * hw].astype(BF16), cos_t, sin_t, seq=seq, rope=False, out_dtype=F32)
            pqk = _rms_proj(xs, g, w[:, 4 * hw:4 * hw + 2 * qw].astype(BF16), cos_t, sin_t, seq=seq, rope=True,
                            out_dtype=BF16)
            pv = _rms_proj(xs, g, w[:, 4 * hw + 2 * qw:].astype(BF16), cos_t, sin_t, seq=seq, rope=False,
                           out_dtype=BF16)
            gn = jnp.concatenate([hgrn_norm_g[j], hgrn_norm_g[j]]).reshape(1, LANES)
            o_h = _hgrn(hp, hgrn_lb_logits, gn, batch=batch, seq=seq, layer=layer)
            o_d = _diff_attention(pqk, pv, diff_lambda[j], row(diff_norm_g[j]), batch=batch, seq=seq,
                                  lam_init=lam_init)
            wo = w_out_o[j].astype(BF16)
            xs = _out_proj(o_h, o_d, wo[:hw], wo[hw:], xs)
            wr = jnp.concatenate([w_router[j], jnp.zeros((d, LANES - N_EXPERTS), F32)], axis=1)
            info, xn = _router(xs, row(ln_ffn_o[j]), wr)
            ids3, gates3, tile_e, tile_act, gather_plan, combine_plan = _moe_plan(info, t)
            rows = _moe_gather(gather_plan, ids3, xn)
            y = _moe_ffn(tile_e, tile_act, rows, gates3, w_exp_gate[j].astype(BF16), w_exp_up[j].astype(BF16),
                         w_exp_down[j].astype(BF16), tf=896)
            xs = _moe_combine(combine_plan, ids3, y, xs, row(ln_final), final_norm=(layer == depth - 1))
    if depth % 2 == 1:
        pass
    return xs.reshape(batch, seq, d)
```

```python
import functools
import math

import jax
import jax.numpy as jnp
from jax import lax
from jax.experimental import pallas as pl
from jax.experimental.pallas import tpu as pltpu
from jax.experimental.pallas import tpu_sc as plsc

F32 = jnp.float32
BF16 = jnp.bfloat16

D_MODEL = 1024
HEAD_DIM = 64
FOX_HEADS = 8
DSA_HEADS = 8
DSA_KV_HEADS = 2
IDX_HEADS = 4
IDX_DIM = 64
DSA_TOPK = 256
HGRN_HEADS = 8
DIFF_HEADS = 4
DIFF_DIM = 64
ROPE_THETA = 10000.0
N_EXPERTS = 8
NORM_EPS = 1e-6
NEG_INF = -1e30

LANES = 128
SUBLANES = 8
VMEM_LIMIT_BYTES = 56 * 1024 * 1024

ROW_TILE = 512
ATTN_TILE = 256
HGRN_CHUNK = 256
SUB = 8
MOE_ROW_TILE = 512
MOE_SUB = 128
MOE_TOKEN_BLOCK = 512
SC_WINDOW = 128

INT_MIN = -(2 ** 31)


def _params(*sem):
    return pltpu.CompilerParams(dimension_semantics=sem, vmem_limit_bytes=VMEM_LIMIT_BYTES)


def _dot(a, b):
    return jnp.dot(a, b, preferred_element_type=F32)


def _dot_nt(a, b):
    return lax.dot_general(a, b, (((1,), (1,)), ((), ())), preferred_element_type=F32)


def _dot_tn(a, b):
    return lax.dot_general(a, b, (((0,), (0,)), ((), ())), preferred_element_type=F32)


def _sigmoid(x):
    return 1.0 / (1.0 + jnp.exp(-x))


def _silu(x):
    return x * _sigmoid(x)


def _lane_half(shape):
    return lax.broadcasted_iota(jnp.int32, shape, len(shape) - 1) // HEAD_DIM


def _rms_proj_kernel(x_ref, g_ref, w_ref, cos_ref, sin_ref, o_ref, *, rope):
    x = x_ref[...]
    var = jnp.mean(x * x, axis=-1, keepdims=True)
    xn = (x * lax.rsqrt(var + NORM_EPS) * g_ref[...]).astype(BF16)
    y = _dot(xn, w_ref[...])
    if not rope:
        o_ref[...] = y.astype(o_ref.dtype)
        return
    cos = cos_ref[...]
    sin = sin_ref[...]
    first = (lax.broadcasted_iota(jnp.int32, cos.shape, 1) % HEAD_DIM) < (HEAD_DIM // 2)
    for c in range(y.shape[1] // LANES):
        yc = y[:, c * LANES:(c + 1) * LANES]
        partner = jnp.where(first, pltpu.roll(yc, LANES - HEAD_DIM // 2, 1), pltpu.roll(yc, HEAD_DIM // 2, 1))
        o_ref[:, c * LANES:(c + 1) * LANES] = (yc * cos + partner * sin).astype(o_ref.dtype)


def _rms_proj(x, g, w, cos, sin, *, seq, rope, out_dtype):
    t, d = x.shape
    n = w.shape[1]
    tm = ROW_TILE
    nseq = seq // tm
    return pl.pallas_call(
        functools.partial(_rms_proj_kernel, rope=rope),
        grid=(t // tm,),
        in_specs=[
            pl.BlockSpec((tm, d), lambda i: (i, 0)),
            pl.BlockSpec((1, d), lambda i: (0, 0)),
            pl.BlockSpec((d, n), lambda i: (0, 0)),
            pl.BlockSpec((tm, LANES), lambda i: (i % nseq, 0)),
            pl.BlockSpec((tm, LANES), lambda i: (i % nseq, 0)),
        ],
        out_specs=pl.BlockSpec((tm, n), lambda i: (i, 0)),
        out_shape=jax.ShapeDtypeStruct((t, n), out_dtype),
        compiler_params=_params("parallel"),
        name="rms_proj_rope" if rope else "rms_proj",
    )(x, g, w, cos, sin)


def _out_proj_kernel(a1_ref, a2_ref, w1_ref, w2_ref, r_ref, o_ref):
    o_ref[...] = r_ref[...] + _dot(a1_ref[...], w1_ref[...]) + _dot(a2_ref[...], w2_ref[...])


def _out_proj(a1, a2, w1, w2, res):
    t, d = res.shape
    k1, k2 = a1.shape[1], a2.shape[1]
    tm = ROW_TILE
    return pl.pallas_call(
        _out_proj_kernel,
        grid=(t // tm,),
        in_specs=[
            pl.BlockSpec((tm, k1), lambda i: (i, 0)),
            pl.BlockSpec((tm, k2), lambda i: (i, 0)),
            pl.BlockSpec((k1, d), lambda i: (0, 0)),
            pl.BlockSpec((k2, d), lambda i: (0, 0)),
            pl.BlockSpec((tm, d), lambda i: (i, 0)),
        ],
        out_specs=pl.BlockSpec((tm, d), lambda i: (i, 0)),
        out_shape=jax.ShapeDtypeStruct((t, d), F32),
        compiler_params=_params("parallel"),
        name="out_proj",
    )(a1, a2, w1, w2, res)


def _cumsum_rows(x):
    rows = x.shape[0]
    row = lax.broadcasted_iota(jnp.int32, x.shape, 0)
    k = 1
    while k < rows:
        x = x + jnp.where(row >= k, pltpu.roll(x, k, 0), 0.0)
        k *= 2
    return x


def _fox_gate_kernel(z_ref, b_ref, ccol_ref, crow_ref, *, tk):
    z = z_ref[...] + b_ref[...]
    logf = jnp.minimum(z, 0.0) - jnp.log(1.0 + jnp.exp(-jnp.abs(z)))
    c = _cumsum_rows(logf)
    ccol_ref[...] = c
    for j in range(c.shape[0] // tk):
        ct = c[j * tk:(j + 1) * tk, :].T
        crow_ref[0, :, j * tk:(j + 1) * tk] = ct[:SUBLANES, :]


def _fox_gate(z, b_pad, *, batch, seq):
    tk = ATTN_TILE
    return pl.pallas_call(
        functools.partial(_fox_gate_kernel, tk=tk),
        grid=(batch,),
        in_specs=[
            pl.BlockSpec((seq, LANES), lambda b: (b, 0)),
            pl.BlockSpec((1, LANES), lambda b: (0, 0)),
        ],
        out_specs=[
            pl.BlockSpec((seq, LANES), lambda b: (b, 0)),
            pl.BlockSpec((1, SUBLANES, seq), lambda b: (b, 0, 0)),
        ],
        out_shape=[
            jax.ShapeDtypeStruct((batch * seq, LANES), F32),
            jax.ShapeDtypeStruct((batch, SUBLANES, seq), F32),
        ],
        compiler_params=_params("parallel"),
        name="fox_gate",
    )(z, b_pad)


def _softmax_pv(s, v):
    m = jnp.max(s, axis=-1, keepdims=True)
    p = jnp.exp(s - m)
    l = jnp.sum(p, axis=-1, keepdims=True)
    return _dot(p.astype(BF16), v) / l


def _mask_last_block(s, tq):
    n = s.shape[1]
    causal = lax.broadcasted_iota(jnp.int32, (tq, tq), 1) <= lax.broadcasted_iota(jnp.int32, (tq, tq), 0)
    tail = jnp.where(causal, s[:, n - tq:], NEG_INF)
    return tail if n == tq else jnp.concatenate([s[:, :n - tq], tail], axis=1)


def _fox_kernel(q_ref, k_ref, v_ref, cc_ref, cr_ref, o_ref, *, tq, scale):
    hp = pl.program_id(1)
    seq = q_ref.shape[0]
    half = _lane_half((1, LANES))
    lane = lax.broadcasted_iota(jnp.int32, (tq, LANES), 1)
    for i in range(seq // tq):
        n = (i + 1) * tq
        q = q_ref[i * tq:n, :] * scale
        cc = cc_ref[i * tq:n, :]
        k = k_ref[0:n, :]
        v = v_ref[0:n, :]
        outs = []
        for e in range(2):
            h = 2 * hp + e
            qe = jnp.where(half == e, q, jnp.zeros_like(q))
            cq = jnp.sum(jnp.where(lane == h, cc, 0.0), axis=-1, keepdims=True)
            ck = cr_ref[0, pl.ds(h, 1), 0:n]
            s = _mask_last_block(_dot_nt(qe, k) + cq - ck, tq)
            outs.append(_softmax_pv(s, v))
        o_ref[i * tq:n, :] = jnp.where(half == 0, outs[0], outs[1]).astype(o_ref.dtype)


def _fox_attention(p1, ccol, crow, *, batch, seq):
    tq = min(ATTN_TILE, seq)
    ng = FOX_HEADS // 2
    return pl.pallas_call(
        functools.partial(_fox_kernel, tq=tq, scale=HEAD_DIM ** -0.5),
        grid=(batch, ng),
        in_specs=[
            pl.BlockSpec((seq, LANES), lambda b, g: (b, g)),
            pl.BlockSpec((seq, LANES), lambda b, g: (b, ng + g)),
            pl.BlockSpec((seq, LANES), lambda b, g: (b, 2 * ng + g)),
            pl.BlockSpec((seq, LANES), lambda b, g: (b, 0)),
            pl.BlockSpec((1, SUBLANES, seq), lambda b, g: (b, 0, 0)),
        ],
        out_specs=pl.BlockSpec((seq, LANES), lambda b, g: (b, g)),
        out_shape=jax.ShapeDtypeStruct((batch * seq, ng * LANES), BF16),
        compiler_params=_params("parallel", "parallel"),
        name="fox_attention",
    )(p1, p1, p1, ccol, crow)


def _diff_kernel(q_ref, k_ref, v_ref, lam_ref, g_ref, o_ref, *, tq, scale, lam_init):
    seq = q_ref.shape[0]
    half = _lane_half((1, LANES))
    lv = lam_ref[...]
    lam = (jnp.exp(jnp.sum(lv[0:1, :] * lv[1:2, :], axis=-1, keepdims=True))
           - jnp.exp(jnp.sum(lv[2:3, :] * lv[3:4, :], axis=-1, keepdims=True)) + lam_init)
    gn = g_ref[...] * (1.0 - lam_init)
    for i in range(seq // tq):
        n = (i + 1) * tq
        q = q_ref[i * tq:n, :] * scale
        k = k_ref[0:n, :]
        v = v_ref[0:n, :]
        outs = []
        for e in range(2):
            qe = jnp.where(half == e, q, jnp.zeros_like(q))
            outs.append(_softmax_pv(_mask_last_block(_dot_nt(qe, k), tq), v))
        o = outs[0] - lam * outs[1]
        var = jnp.mean(o * o, axis=-1, keepdims=True)
        o_ref[i * tq:n, :] = (o * lax.rsqrt(var + NORM_EPS) * gn).astype(o_ref.dtype)


def _diff_attention(pqk, pv, lam_vecs, g, *, batch, seq, lam_init):
    tq = min(ATTN_TILE, seq)
    ng = DIFF_HEADS
    return pl.pallas_call(
        functools.partial(_diff_kernel, tq=tq, scale=DIFF_DIM ** -0.5, lam_init=lam_init),
        grid=(batch, ng),
        in_specs=[
            pl.BlockSpec((seq, LANES), lambda b, g: (b, g)),
            pl.BlockSpec((seq, LANES), lambda b, g: (b, ng + g)),
            pl.BlockSpec((seq, LANES), lambda b, g: (b, g)),
            pl.BlockSpec(lam_vecs.shape, lambda b, g: (0, 0)),
            pl.BlockSpec((1, LANES), lambda b, g: (0, 0)),
        ],
        out_specs=pl.BlockSpec((seq, LANES), lambda b, g: (b, g)),
        out_shape=jax.ShapeDtypeStruct((batch * seq, ng * LANES), BF16),
        compiler_params=_params("parallel", "parallel"),
        name="diff_attention",
    )(pqk, pqk, pv, lam_vecs, g)


def _sortable_key(x):
    bits = pltpu.bitcast(x, jnp.int32)
    return jnp.where(bits < 0, bits ^ jnp.int32(0x7FFFFFFF), bits)


def _dsa_kernel(q_ref, kd_ref, iq_ref, ik_ref, vd_ref, z_ref, o_ref, key_ref, madd_ref,
                *, tq, n_sel, scale, idx_scale, w_scale):
    i = pl.program_id(1)
    seq = key_ref.shape[1]
    nq = seq // tq
    span = 2 if nq % 2 == 0 else 1
    half = _lane_half((1, LANES))
    row = lax.broadcasted_iota(jnp.int32, (tq, tq), 0)
    col = lax.broadcasted_iota(jnp.int32, (tq, tq), 1)
    row_g = row + i * tq
    tri = jnp.where(row < col, 1.0, 0.0).astype(BF16)

    def block(n):
        z = z_ref[...]
        iq = iq_ref[...]
        ik = ik_ref[0:n, :]
        sc = jnp.zeros((tq, n), F32)
        for h in range(IDX_HEADS):
            grp = iq[:, (h // 2) * LANES:(h // 2 + 1) * LANES]
            iq_h = jnp.where(half == h % 2, grp, jnp.zeros_like(grp))
            w_h = z[:, SUBLANES + h:SUBLANES + h + 1] * w_scale
            sc = sc + w_h * jnp.maximum(_dot_nt(iq_h, ik) * idx_scale, 0.0)
        for j in range(n // tq):
            scj = jnp.where(col + j * tq <= row_g, sc[:, j * tq:(j + 1) * tq], -jnp.inf)
            key_ref[:, j * tq:(j + 1) * tq] = _sortable_key(scj)

        def count(pred):
            return jnp.sum(jnp.where(pred, 1.0, 0.0), axis=-1, keepdims=True)

        thr0 = jnp.where(count(key_ref[:, 0:n] >= 0) >= n_sel, 0, INT_MIN).astype(jnp.int32)

        def bit_body(it, thr):
            cand = thr | jnp.left_shift(jnp.int32(1), 30 - it)
            return jnp.where(count(key_ref[:, 0:n] >= cand) >= n_sel, cand, thr)

        thr = lax.fori_loop(0, 31, bit_body, thr0)
        need = n_sel - count(key_ref[:, 0:n] > thr)

        off = jnp.zeros((tq, 1), F32)
        for j in range(n // tq):
            kj = key_ref[:, j * tq:(j + 1) * tq]
            eq = kj == thr
            eqf = jnp.where(eq, 1.0, 0.0)
            rank = _dot(eqf.astype(BF16), tri) + off
            take = jnp.where(kj > thr, 1.0, jnp.where(eq, jnp.where(rank < need, 1.0, 0.0), 0.0))
            valid = col + j * tq <= row_g
            madd_ref[:, j * tq:(j + 1) * tq] = jnp.where(valid, jnp.where(take > 0.0, 0.0, NEG_INF), NEG_INF)
            off = off + jnp.sum(eqf, axis=-1, keepdims=True)

        q = q_ref[...] * scale
        for p in range(DSA_HEADS // 2):
            qg = q[:, p * LANES:(p + 1) * LANES]
            g = p // (DSA_HEADS // DSA_KV_HEADS // 2)
            k = kd_ref[0:n, g * LANES:(g + 1) * LANES]
            v = vd_ref[0:n, g * LANES:(g + 1) * LANES]
            outs = []
            for e in range(2):
                qe = jnp.where(half == e, qg, jnp.zeros_like(qg))
                outs.append(_softmax_pv(_dot_nt(qe, k) + madd_ref[:, 0:n], v))
            o_ref[:, p * LANES:(p + 1) * LANES] = jnp.where(half == 0, outs[0], outs[1]).astype(o_ref.dtype)

    for c in range(nq // span):
        pl.when(i // span == c)(functools.partial(block, (c + 1) * span * tq))


def _dsa_attention(p2, p1, z, *, batch, seq):
    tq = min(ATTN_TILE, seq)
    nq = seq // tq
    n_sel = min(DSA_TOPK, seq // 4)
    qw = DSA_HEADS * HEAD_DIM
    return pl.pallas_call(
        functools.partial(_dsa_kernel, tq=tq, n_sel=float(n_sel), scale=HEAD_DIM ** -0.5,
                          idx_scale=IDX_DIM ** -0.5, w_scale=IDX_HEADS ** -0.5),
        grid=(batch, nq),
        in_specs=[
            pl.BlockSpec((tq, qw), lambda b, i: (b * nq + i, 0)),
            pl.BlockSpec((seq, 2 * LANES), lambda b, i: (b, 2)),
            pl.BlockSpec((tq, 2 * LANES), lambda b, i: (b * nq + i, 3)),
            pl.BlockSpec((seq, LANES), lambda b, i: (b, 8)),
            pl.BlockSpec((seq, 2 * LANES), lambda b, i: (b, 6)),
            pl.BlockSpec((tq, LANES), lambda b, i: (b * nq + i, 0)),
        ],
        out_specs=pl.BlockSpec((tq, qw), lambda b, i: (b * nq + i, 0)),
        out_shape=jax.ShapeDtypeStruct((batch * seq, qw), BF16),
        scratch_shapes=[pltpu.VMEM((tq, seq), jnp.int32), pltpu.VMEM((tq, seq), F32)],
        compiler_params=_params("parallel", "arbitrary"),
        name="dsa_attention",
    )(p2, p2, p2, p2, p1, z)


def _hgrn_kernel(q_ref, f_ref, i_ref, g_ref, lbl_ref, gn_ref, o_ref, b_scr, *, chunk, layer):
    seq = q_ref.shape[0]
    half = _lane_half((1, LANES))
    logits = lbl_ref[...]
    pe = jnp.exp(logits - jnp.max(logits, axis=0, keepdims=True))
    prob = pe / jnp.sum(pe, axis=0, keepdims=True)
    csum = prob[0:1, :]
    for d in range(1, layer + 1):
        csum = csum + prob[d:d + 1, :]
    lb = csum - prob[0:1, :]

    rowi = lax.broadcasted_iota(jnp.int32, (chunk, 1), 0)
    xor_rc = (lax.broadcasted_iota(jnp.int32, (chunk, chunk), 0)
              ^ lax.broadcasted_iota(jnp.int32, (chunk, chunk), 1))
    same_head = (lax.broadcasted_iota(jnp.int32, (LANES, LANES), 0) // HEAD_DIM
                 == lax.broadcasted_iota(jnp.int32, (LANES, LANES), 1) // HEAD_DIM)
    ones_bd = jnp.where(same_head, 1.0, 0.0).astype(BF16)
    gn = gn_ref[...]

    def chunk_body(c, state_t):
        r0 = pl.multiple_of(c * chunk, chunk)
        q = _silu(q_ref[pl.ds(r0, chunk), :])
        f = lb + (1.0 - lb) * _sigmoid(f_ref[pl.ds(r0, chunk), :])
        kc = 1.0 - f
        v = i_ref[pl.ds(r0, chunk), :]
        b = _cumsum_rows(jnp.log(f))
        b_scr[...] = b
        b_last = b_scr[chunk - 1:chunk, :]

        o = _dot_nt((q * jnp.exp(b)).astype(BF16), state_t.astype(BF16))
        k2 = (kc * jnp.exp(b_last - b)).astype(BF16)
        upd = _dot_tn(v.astype(BF16), k2)
        new_state = state_t * jnp.exp(b_last) + jnp.where(same_head, upd, 0.0)

        s_tot = [jnp.zeros((chunk, chunk), F32), jnp.zeros((chunk, chunk), F32)]
        m = chunk
        while m > SUB:
            hm = m // 2
            pieces = [jnp.broadcast_to(b_scr[blk * m + hm - 1:blk * m + hm, :], (m, LANES))
                      for blk in range(chunk // m)]
            ref = pieces[0] if len(pieces) == 1 else jnp.concatenate(pieces, axis=0)
            right = (rowi % m) >= hm
            qm = jnp.where(right, q * jnp.exp(jnp.where(right, b - ref, 0.0)), 0.0).astype(BF16)
            km = jnp.where(right, 0.0, kc * jnp.exp(jnp.where(right, 0.0, ref - b))).astype(BF16)
            for e in range(2):
                s = _dot_nt(jnp.where(half == e, qm, jnp.zeros_like(qm)), km)
                s_tot[e] = s_tot[e] + (s if m == chunk else jnp.where(xor_rc < m, s, 0.0))
            m = hm
        vb = v.astype(BF16)
        for e in range(2):
            o = o + _dot(s_tot[e].astype(BF16), jnp.where(half == e, vb, jnp.zeros_like(vb)))

        for d in range(SUB):
            ok = (rowi % SUB) >= d
            bs = b if d == 0 else pltpu.roll(b, d, 0)
            ks = kc if d == 0 else pltpu.roll(kc, d, 0)
            vs = v if d == 0 else pltpu.roll(v, d, 0)
            x = jnp.where(ok, q * ks * jnp.exp(jnp.where(ok, b - bs, 0.0)), 0.0)
            o = o + _dot(x.astype(BF16), ones_bd) * vs

        sq = o * o
        s0 = jnp.sum(jnp.where(half == 0, sq, 0.0), axis=-1, keepdims=True)
        s1 = jnp.sum(jnp.where(half == 1, sq, 0.0), axis=-1, keepdims=True)
        var = jnp.where(half == 0, s0, s1) * (1.0 / HEAD_DIM)
        y = o * lax.rsqrt(var + NORM_EPS) * gn * _silu(g_ref[pl.ds(r0, chunk), :])
        o_ref[pl.ds(r0, chunk), :] = y.astype(o_ref.dtype)
        return new_state

    lax.fori_loop(0, seq // chunk, chunk_body, jnp.zeros((LANES, LANES), F32))


def _hgrn(hproj, lb_logits, gn, *, batch, seq, layer):
    ng = HGRN_HEADS // 2
    chunk = min(HGRN_CHUNK, seq)
    spec = lambda off: pl.BlockSpec((seq, LANES), lambda b, g: (b, off * ng + g))
    return pl.pallas_call(
        functools.partial(_hgrn_kernel, chunk=chunk, layer=layer),
        grid=(batch, ng),
        in_specs=[spec(0), spec(1), spec(2), spec(3),
                  pl.BlockSpec((lb_logits.shape[0], LANES), lambda b, g: (0, g)),
                  pl.BlockSpec((1, LANES), lambda b, g: (0, 0))],
        out_specs=pl.BlockSpec((seq, LANES), lambda b, g: (b, g)),
        out_shape=jax.ShapeDtypeStruct((batch * seq, ng * LANES), BF16),
        scratch_shapes=[pltpu.VMEM((chunk, LANES), F32)],
        compiler_params=_params("parallel", "parallel"),
        name="hgrn2",
    )(hproj, hproj, hproj, hproj, lb_logits, gn)


def _pack_pairs(x):
    hw = x.shape[1] // 2
    hi = pltpu.bitcast(x[:, :hw].astype(BF16).astype(F32), jnp.int32)
    lo = pltpu.bitcast(x[:, hw:].astype(BF16).astype(F32), jnp.int32)
    return hi | lax.shift_right_logical(lo, 16)


def _unpack_pairs(w):
    hi = pltpu.bitcast(w & jnp.int32(-65536), F32)
    lo = pltpu.bitcast(lax.shift_left(w, 16), F32)
    return hi, lo


def _router_kernel(x_ref, g_ref, w_ref, o_ref, xn_ref):
    x = x_ref[...]
    var = jnp.mean(x * x, axis=-1, keepdims=True)
    xn = x * lax.rsqrt(var + NORM_EPS) * g_ref[...]
    xn_ref[...] = _pack_pairs(xn)
    logits = jnp.dot(xn, w_ref[...], precision=lax.Precision.HIGHEST, preferred_element_type=F32)
    lane = lax.broadcasted_iota(jnp.int32, logits.shape, 1)
    lg = jnp.where(lane < N_EXPERTS, logits, -jnp.inf)
    m1 = jnp.max(lg, axis=-1, keepdims=True)
    i1 = jnp.min(jnp.where(lg == m1, lane, LANES), axis=-1, keepdims=True)
    lg2 = jnp.where(lane == i1, -jnp.inf, lg)
    m2 = jnp.max(lg2, axis=-1, keepdims=True)
    i2 = jnp.min(jnp.where(lg2 == m2, lane, LANES), axis=-1, keepdims=True)
    t = jnp.exp(m2 - m1)
    den = 1.0 + t
    o_ref[...] = (jnp.where(lane == 0, i1.astype(F32), 0.0) + jnp.where(lane == 1, i2.astype(F32), 0.0)
                  + jnp.where(lane == 2, 1.0 / den, 0.0) + jnp.where(lane == 3, t / den, 0.0))


def _router(x, g, w_pad):
    t, d = x.shape
    tm = ROW_TILE
    return pl.pallas_call(
        _router_kernel,
        grid=(t // tm,),
        in_specs=[pl.BlockSpec((tm, d), lambda i: (i, 0)),
                  pl.BlockSpec((1, d), lambda i: (0, 0)),
                  pl.BlockSpec((d, LANES), lambda i: (0, 0))],
        out_specs=[pl.BlockSpec((tm, LANES), lambda i: (i, 0)),
                   pl.BlockSpec((tm, d // 2), lambda i: (i, 0))],
        out_shape=[jax.ShapeDtypeStruct((t, LANES), F32), jax.ShapeDtypeStruct((t, d // 2), jnp.int32)],
        compiler_params=_params("parallel"),
        name="router",
    )(x, g, w_pad)


def _moe_gather_kernel(osub_ref, iblk_ref, first_ref, valid_ref, ids_ref, x_ref, o_ref, *, tw):
    w = pl.program_id(0)

    @pl.when(first_ref[w] == 1)
    def _():
        o_ref[...] = jnp.zeros_like(o_ref)

    @pl.when(valid_ref[w] == 1)
    def _():
        tok = ids_ref[0, 0:1, :]
        cand = lax.broadcasted_iota(jnp.int32, (tw, MOE_SUB), 0) + iblk_ref[w] * tw
        onehot_t = jnp.where(cand == tok, 1.0, 0.0).astype(BF16)
        o_ref[...] += _dot_tn(onehot_t, x_ref[...]).astype(o_ref.dtype)


def _moe_gather(plan, ids3, xn):
    t, d = xn.shape
    n_sub = ids3.shape[0]
    osub, iblk, first, valid = plan
    tw = MOE_TOKEN_BLOCK
    return pl.pallas_call(
        functools.partial(_moe_gather_kernel, tw=tw),
        grid_spec=pltpu.PrefetchScalarGridSpec(
            num_scalar_prefetch=4,
            grid=(osub.shape[0],),
            in_specs=[pl.BlockSpec((1, SUBLANES, MOE_SUB), lambda w, o, i, f, v: (o[w], 0, 0)),
                      pl.BlockSpec((tw, d), lambda w, o, i, f, v: (i[w], 0))],
            out_specs=pl.BlockSpec((MOE_SUB, d), lambda w, o, i, f, v: (o[w], 0)),
        ),
        out_shape=jax.ShapeDtypeStruct((n_sub * MOE_SUB, d), BF16),
        compiler_params=_params("arbitrary"),
        name="moe_gather",
    )(osub, iblk, first, valid, ids3, xn)


def _moe_combine_kernel(otile_ref, isub_ref, first_ref, last_ref, valid_ref, ids_ref, y_ref, x_ref, gf_ref,
                        o_ref, acc_ref, *, tt, final_norm):
    w = pl.program_id(0)

    @pl.when(first_ref[w] == 1)
    def _():
        acc_ref[...] = x_ref[...]

    @pl.when(valid_ref[w] == 1)
    def _():
        tok = ids_ref[0, 0:1, :]
        cand = lax.broadcasted_iota(jnp.int32, (tt, MOE_SUB), 0) + otile_ref[w] * tt
        onehot = jnp.where(cand == tok, 1.0, 0.0).astype(BF16)
        acc_ref[...] += _dot(onehot, y_ref[...])

    @pl.when(last_ref[w] == 1)
    def _():
        out = acc_ref[...]
        if final_norm:
            var = jnp.mean(out * out, axis=-1, keepdims=True)
            out = out * lax.rsqrt(var + NORM_EPS) * gf_ref[...]
        o_ref[...] = out


def _moe_combine(plan, ids3, y, x, gf, *, final_norm):
    t, d = x.shape
    otile, isub, first, last, valid = plan
    tt = ROW_TILE
    return pl.pallas_call(
        functools.partial(_moe_combine_kernel, tt=tt, final_norm=final_norm),
        grid_spec=pltpu.PrefetchScalarGridSpec(
            num_scalar_prefetch=5,
            grid=(otile.shape[0],),
            in_specs=[pl.BlockSpec((1, SUBLANES, MOE_SUB), lambda w, o, i, f, l, v: (i[w], 0, 0)),
                      pl.BlockSpec((MOE_SUB, d), lambda w, o, i, f, l, v: (i[w], 0)),
                      pl.BlockSpec((tt, d), lambda w, o, i, f, l, v: (o[w], 0)),
                      pl.BlockSpec((1, d), lambda w, o, i, f, l, v: (0, 0))],
            out_specs=pl.BlockSpec((tt, d), lambda w, o, i, f, l, v: (o[w], 0)),
            scratch_shapes=[pltpu.VMEM((tt, d), F32)],
        ),
        out_shape=jax.ShapeDtypeStruct((t, d), F32),
        compiler_params=_params("arbitrary"),
        name="moe_combine",
    )(otile, isub, first, last, valid, ids3, y, x, gf)


def _moe_ffn_kernel(te_ref, act_ref, x_ref, gate_ref, wg_ref, wu_ref, wd_ref, o_ref, acc_ref, *, tm):
    i = pl.program_id(0)
    f = pl.program_id(1)

    @pl.when(act_ref[i] == 1)
    def _():
        @pl.when(f == 0)
        def _():
            acc_ref[...] = jnp.zeros_like(acc_ref)

        x = x_ref[...]
        mid = (_silu(_dot(x, wg_ref[0])) * _dot(x, wu_ref[0])).astype(BF16)
        acc_ref[...] += _dot(mid, wd_ref[0])

        @pl.when(f == pl.num_programs(1) - 1)
        def _():
            gates = gate_ref[0]
            for k in range(tm // LANES):
                col = jnp.broadcast_to(gates[k:k + 1, :], (LANES, LANES)).T
                rows = acc_ref[k * LANES:(k + 1) * LANES, :]
                for c in range(rows.shape[1] // LANES):
                    o_ref[k * LANES:(k + 1) * LANES, c * LANES:(c + 1) * LANES] = (
                        rows[:, c * LANES:(c + 1) * LANES] * col).astype(o_ref.dtype)

    @pl.when(jnp.logical_and(act_ref[i] == 0, f == 0))
    def _():
        o_ref[...] = jnp.zeros_like(o_ref)


def _moe_ffn(tile_e, tile_act, xs, gates3, wg, wu, wd, *, tf):
    p, d = xs.shape
    ff = wg.shape[2]
    tm = MOE_ROW_TILE
    return pl.pallas_call(
        functools.partial(_moe_ffn_kernel, tm=tm),
        grid_spec=pltpu.PrefetchScalarGridSpec(
            num_scalar_prefetch=2,
            grid=(p // tm, ff // tf),
            in_specs=[pl.BlockSpec((tm, d), lambda i, f, te, ta: (i, 0)),
                      pl.BlockSpec((1, tm // LANES, LANES), lambda i, f, te, ta: (i, 0, 0)),
                      pl.BlockSpec((1, d, tf), lambda i, f, te, ta: (te[i], 0, f * ta[i])),
                      pl.BlockSpec((1, d, tf), lambda i, f, te, ta: (te[i], 0, f * ta[i])),
                      pl.BlockSpec((1, tf, d), lambda i, f, te, ta: (te[i], f * ta[i], 0))],
            out_specs=pl.BlockSpec((tm, d), lambda i, f, te, ta: (i, 0)),
            scratch_shapes=[pltpu.VMEM((tm, d), F32)],
        ),
        out_shape=jax.ShapeDtypeStruct((p, d), BF16),
        compiler_params=_params("arbitrary", "arbitrary"),
        name="moe_ffn",
    )(tile_e, tile_act, xs, gates3, wg, wu, wd)


def _moe_plan(info, t):
    tm, sub, tw, tt = MOE_ROW_TILE, MOE_SUB, MOE_TOKEN_BLOCK, ROW_TILE
    n_e = N_EXPERTS
    a = 2 * t
    p = a + n_e * tm
    n_sub = p // sub
    e_flat = info[:, 0:2].astype(jnp.int32).reshape(a)
    g_flat = info[:, 2:4].reshape(a)
    onehot = (e_flat[:, None] == jnp.arange(n_e, dtype=jnp.int32)[None, :]).astype(jnp.int32)
    csum = jnp.cumsum(onehot, axis=0)
    rank = jnp.sum(csum * onehot, axis=1) - 1
    counts = csum[-1]
    padded = ((counts + tm - 1) // tm) * tm
    ends = jnp.cumsum(padded)
    starts = ends - padded
    pos = starts[e_flat] + rank
    tok_row = jnp.full((p,), -1, jnp.int32).at[pos].set(jnp.arange(a, dtype=jnp.int32) // 2)
    gate_row = jnp.zeros((p,), F32).at[pos].set(g_flat)
    ids3 = jnp.broadcast_to(tok_row.reshape(n_sub, 1, sub), (n_sub, SUBLANES, sub))
    gates3 = gate_row.reshape(p // tm, tm // LANES, LANES)

    tile_start = jnp.arange(p // tm, dtype=jnp.int32) * tm
    tile_e = jnp.minimum(jnp.searchsorted(ends, tile_start, side='right'), n_e - 1).astype(jnp.int32)
    tile_act = (tile_start < ends[-1]).astype(jnp.int32)

    tok2 = tok_row.reshape(n_sub, sub)
    has = jnp.any(tok2 >= 0, axis=1)
    tmin = jnp.min(jnp.where(tok2 >= 0, tok2, t), axis=1)
    tmax = jnp.max(tok2, axis=1)

    def worklist(lo, hi, cnt, length):
        cend = jnp.cumsum(cnt)
        total = cend[-1]
        w = jnp.arange(length, dtype=jnp.int32)
        s = jnp.minimum(jnp.searchsorted(cend, w, side='right'), n_sub - 1).astype(jnp.int32)
        k = w - (cend[s] - cnt[s])
        valid = w < total
        blk = jnp.where(valid, lo[s] + k, 0)
        return s, blk.astype(jnp.int32), k, valid

    lo = jnp.where(has, tmin // tw, 0)
    hi = jnp.where(has, tmax // tw, 0)
    cnt = (hi - lo + 1).astype(jnp.int32)
    len_g = n_sub + n_e * (t // tw - 1)
    s, blk, k, valid = worklist(lo, hi, cnt, len_g)
    last_valid = jnp.sum(valid.astype(jnp.int32)) - 1
    s = jnp.where(valid, s, s[last_valid])
    blk = jnp.where(valid, blk, blk[last_valid])
    gather_plan = (s, blk, jnp.logical_and(valid, k == 0).astype(jnp.int32), valid.astype(jnp.int32))

    lo = jnp.where(has, tmin // tt, 0)
    hi = jnp.where(has, tmax // tt, -1)
    cnt = (hi - lo + 1).astype(jnp.int32)
    len_c = n_sub + n_e * (t // tt - 1)
    s, tile, k, valid = worklist(lo, hi, cnt, len_c)
    n_tt = t // tt
    order = jnp.argsort(jnp.where(valid, tile, n_tt), stable=True)
    s, tile, valid = s[order], tile[order], valid[order]
    tile = jnp.where(valid, tile, n_tt - 1)
    last_valid = jnp.sum(valid.astype(jnp.int32)) - 1
    s = jnp.where(valid, s, s[last_valid])
    prev = jnp.concatenate([jnp.full((1,), -1, jnp.int32), tile[:-1]])
    nxt = jnp.concatenate([tile[1:], jnp.full((1,), -1, jnp.int32)])
    combine_plan = (tile, s, (tile != prev).astype(jnp.int32), (tile != nxt).astype(jnp.int32),
                    valid.astype(jnp.int32))
    return ids3, gates3, tile_e, tile_act, gather_plan, combine_plan


def _sc_mesh():
    return plsc.VectorSubcoreMesh(core_axis_name="core", subcore_axis_name="subcore")


def _sc_scatter_rows(x, idx0, idx1, n_out):
    n, d = x.shape

    @functools.partial(pl.kernel, out_type=jax.ShapeDtypeStruct((n_out, d), x.dtype), mesh=_sc_mesh(),
                       scratch_types=[])
    def scatter(x_hbm, i0_hbm, i1_hbm, o_hbm):
        def body(x_vmem, i0_vmem, i1_vmem):
            pltpu.sync_copy(x_vmem, o_hbm.at[i0_vmem.at[0]])
            pltpu.sync_copy(x_vmem, o_hbm.at[i1_vmem.at[0]])

        pltpu.emit_pipeline(
            body,
            grid=(n // SC_WINDOW,),
            in_specs=[pl.BlockSpec((SC_WINDOW, d), index_map=lambda i: (i, 0)),
                      pl.BlockSpec((1, SC_WINDOW), index_map=lambda i: (0, i)),
                      pl.BlockSpec((1, SC_WINDOW), index_map=lambda i: (0, i))],
            out_specs=[],
            core_axis_name=("core", "subcore"),
            dimension_semantics=(pltpu.PARALLEL,),
        )(x_hbm, i0_hbm, i1_hbm)

    return scatter(x, idx0.reshape(1, n), idx1.reshape(1, n))


def _sc_gather_rows(x, idx):
    n = idx.shape[0]
    d = x.shape[1]

    @functools.partial(pl.kernel, out_type=jax.ShapeDtypeStruct((n, d), x.dtype), mesh=_sc_mesh(),
                       scratch_types=[])
    def gather(x_hbm, i_hbm, o_hbm):
        def body(i_vmem, o_vmem):
            pltpu.sync_copy(x_hbm.at[i_vmem.at[0]], o_vmem)

        pltpu.emit_pipeline(
            body,
            grid=(n // SC_WINDOW,),
            in_specs=[pl.BlockSpec((1, SC_WINDOW), index_map=lambda i: (0, i))],
            out_specs=[pl.BlockSpec((SC_WINDOW, d), index_map=lambda i: (i, 0))],
            core_axis_name=("core", "subcore"),
            dimension_semantics=(pltpu.PARALLEL,),
        )(i_hbm, o_hbm)

    return gather(x, idx.reshape(1, n))


def _moe_ffn_packed_kernel(te_ref, act_ref, x_ref, wg_ref, wu_ref, wd_ref, o_ref, xn_ref, acc_ref):
    i = pl.program_id(0)
    f = pl.program_id(1)
    hw = x_ref.shape[1]

    @pl.when(act_ref[i] == 1)
    def _():
        @pl.when(f == 0)
        def _():
            hi, lo = _unpack_pairs(x_ref[...])
            xn_ref[:, :hw] = hi.astype(BF16)
            xn_ref[:, hw:] = lo.astype(BF16)
            acc_ref[...] = jnp.zeros_like(acc_ref)

        x = xn_ref[...]
        mid = (_silu(_dot(x, wg_ref[0])) * _dot(x, wu_ref[0])).astype(BF16)
        acc_ref[...] += _dot(mid, wd_ref[0])

        @pl.when(f == pl.num_programs(1) - 1)
        def _():
            o_ref[...] = _pack_pairs(acc_ref[...])


def _moe_ffn_packed(tile_e, tile_act, xs, wg, wu, wd, *, tf):
    p, hw = xs.shape
    d = 2 * hw
    ff = wg.shape[2]
    tm = MOE_ROW_TILE
    return pl.pallas_call(
        _moe_ffn_packed_kernel,
        grid_spec=pltpu.PrefetchScalarGridSpec(
            num_scalar_prefetch=2,
            grid=(p // tm, ff // tf),
            in_specs=[pl.BlockSpec((tm, hw), lambda i, f, te, ta: (i, 0)),
                      pl.BlockSpec((1, d, tf), lambda i, f, te, ta: (te[i], 0, f * ta[i])),
                      pl.BlockSpec((1, d, tf), lambda i, f, te, ta: (te[i], 0, f * ta[i])),
                      pl.BlockSpec((1, tf, d), lambda i, f, te, ta: (te[i], f * ta[i], 0))],
            out_specs=pl.BlockSpec((tm, hw), lambda i, f, te, ta: (i, 0)),
            scratch_shapes=[pltpu.VMEM((tm, d), BF16), pltpu.VMEM((tm, d), F32)],
        ),
        out_shape=jax.ShapeDtypeStruct((p, hw), jnp.int32),
        compiler_params=_params("arbitrary", "arbitrary"),
        name="moe_ffn",
    )(tile_e, tile_act, xs, wg, wu, wd)


def _moe_sum_kernel(x_ref, info_ref, y_ref, gf_ref, o_ref, *, final_norm):
    hw = y_ref.shape[1] // 2
    info = info_ref[...]
    g1 = info[:, 2:3]
    g2 = info[:, 3:4]
    y = y_ref[...]
    hi1, lo1 = _unpack_pairs(y[:, :hw])
    hi2, lo2 = _unpack_pairs(y[:, hw:])
    x = x_ref[...]
    out_a = x[:, :hw] + g1 * hi1 + g2 * hi2
    out_b = x[:, hw:] + g1 * lo1 + g2 * lo2
    if final_norm:
        ss = jnp.sum(out_a * out_a, axis=-1, keepdims=True) + jnp.sum(out_b * out_b, axis=-1, keepdims=True)
        r = lax.rsqrt(ss * (1.0 / (2 * hw)) + NORM_EPS)
        gf = gf_ref[...]
        out_a = out_a * r * gf[:, :hw]
        out_b = out_b * r * gf[:, hw:]
    o_ref[:, :hw] = out_a
    o_ref[:, hw:] = out_b


def _moe_sum(x, info, y_pairs, gf, *, final_norm):
    t, d = x.shape
    tm = ROW_TILE
    return pl.pallas_call(
        functools.partial(_moe_sum_kernel, final_norm=final_norm),
        grid=(t // tm,),
        in_specs=[pl.BlockSpec((tm, d), lambda i: (i, 0)),
                  pl.BlockSpec((tm, LANES), lambda i: (i, 0)),
                  pl.BlockSpec((tm, d), lambda i: (i, 0)),
                  pl.BlockSpec((1, d), lambda i: (0, 0))],
        out_specs=pl.BlockSpec((tm, d), lambda i: (i, 0)),
        out_shape=jax.ShapeDtypeStruct((t, d), F32),
        compiler_params=_params("parallel"),
        name="moe_sum",
    )(x, info, y_pairs, gf)


def _moe_positions(info, t):
    tm = MOE_ROW_TILE
    n_e = N_EXPERTS
    a = 2 * t
    p = a + n_e * tm
    e_flat = info[:, 0:2].astype(jnp.int32).reshape(a)
    onehot = (e_flat[:, None] == jnp.arange(n_e, dtype=jnp.int32)[None, :]).astype(jnp.int32)
    csum = jnp.cumsum(onehot, axis=0)
    rank = jnp.sum(csum * onehot, axis=1) - 1
    counts = csum[-1]
    padded = ((counts + tm - 1) // tm) * tm
    ends = jnp.cumsum(padded)
    starts = ends - padded
    pos = jnp.sum(onehot * starts[None, :], axis=1) + rank
    tile_start = jnp.arange(p // tm, dtype=jnp.int32) * tm
    tile_e = jnp.minimum(jnp.sum((tile_start[:, None] >= ends[None, :]).astype(jnp.int32), axis=1), n_e - 1)
    tile_act = (tile_start < ends[-1]).astype(jnp.int32)
    return pos.astype(jnp.int32).reshape(t, 2), tile_e.astype(jnp.int32), tile_act, p


def _ffn_kernel(x_ref, g_ref, wg_ref, wu_ref, wd_ref, gf_ref, o_ref, xn_ref, acc_ref, *, final_norm):
    f = pl.program_id(1)

    @pl.when(f == 0)
    def _():
        x = x_ref[...]
        var = jnp.mean(x * x, axis=-1, keepdims=True)
        xn_ref[...] = (x * lax.rsqrt(var + NORM_EPS) * g_ref[...]).astype(BF16)
        acc_ref[...] = jnp.zeros_like(acc_ref)

    xn = xn_ref[...]
    mid = (_silu(_dot(xn, wg_ref[...])) * _dot(xn, wu_ref[...])).astype(BF16)
    acc_ref[...] += _dot(mid, wd_ref[...])

    @pl.when(f == pl.num_programs(1) - 1)
    def _():
        out = x_ref[...] + acc_ref[...]
        if final_norm:
            var = jnp.mean(out * out, axis=-1, keepdims=True)
            out = out * lax.rsqrt(var + NORM_EPS) * gf_ref[...]
        o_ref[...] = out


def _ffn(x, g, wg, wu, wd, gf, *, tf, final_norm):
    t, d = x.shape
    ff = wg.shape[1]
    tm = ROW_TILE
    return pl.pallas_call(
        functools.partial(_ffn_kernel, final_norm=final_norm),
        grid=(t // tm, ff // tf),
        in_specs=[
            pl.BlockSpec((tm, d), lambda i, f: (i, 0)),
            pl.BlockSpec((1, d), lambda i, f: (0, 0)),
            pl.BlockSpec((d, tf), lambda i, f: (0, f)),
            pl.BlockSpec((d, tf), lambda i, f: (0, f)),
            pl.BlockSpec((tf, d), lambda i, f: (f, 0)),
            pl.BlockSpec((1, d), lambda i, f: (0, 0)),
        ],
        out_specs=pl.BlockSpec((tm, d), lambda i, f: (i, 0)),
        out_shape=jax.ShapeDtypeStruct((t, d), F32),
        scratch_shapes=[pltpu.VMEM((tm, d), BF16), pltpu.VMEM((tm, d), F32)],
        compiler_params=_params("parallel", "arbitrary"),
        name="ffn",
    )(x, g, wg, wu, wd, gf)


def _rope_tables(seq):
    half = HEAD_DIM // 2
    inv = ROPE_THETA ** (-jnp.arange(half, dtype=F32) * 2.0 / HEAD_DIM)
    ang = jnp.arange(seq, dtype=F32)[:, None] * inv[None, :]
    cos = jnp.cos(ang)
    sin = jnp.sin(ang)
    cos_t = jnp.concatenate([cos, cos, cos, cos], axis=-1)
    sin_t = jnp.concatenate([-sin, sin, -sin, sin], axis=-1)
    return cos_t, sin_t


def _dup_heads(w):
    d, n = w.shape
    w = w.reshape(d, n // HEAD_DIM, 1, HEAD_DIM)
    return jnp.broadcast_to(w, (d, n // HEAD_DIM, 2, HEAD_DIM)).reshape(d, 2 * n)


def kernel(x, ln_mix_e, w_in_e, b_fox_f, w_out_e, ln_ffn_e, w_ffn_gate, w_ffn_up, w_ffn_down,
           ln_mix_o, w_in_o, hgrn_lb_logits, hgrn_norm_g, diff_lambda, diff_norm_g, w_out_o,
           ln_ffn_o, w_router, w_exp_gate, w_exp_up, w_exp_down, ln_final):
    batch, seq, d = x.shape
    t = batch * seq
    depth = ln_mix_e.shape[0] + ln_mix_o.shape[0]
    xs = x.reshape(t, d)
    cos_t, sin_t = _rope_tables(seq)
    row = lambda v: v.reshape(1, -1)
    fw = FOX_HEADS * HEAD_DIM
    dw = DSA_HEADS * HEAD_DIM
    kvw = DSA_KV_HEADS * HEAD_DIM
    hw = HGRN_HEADS * HEAD_DIM
    qw = DIFF_HEADS * 2 * DIFF_DIM

    for layer in range(depth):
        j = layer // 2
        if layer % 2 == 0:
            w = w_in_e[j]
            o = 0
            fq, fk, fv = w[:, o:o + fw], w[:, o + fw:o + 2 * fw], w[:, o + 2 * fw:o + 3 * fw]
            o += 3 * fw
            ffw = w[:, o:o + FOX_HEADS]
            o += FOX_HEADS
            dq = w[:, o:o + dw]
            o += dw
            dk = w[:, o:o + kvw]
            o += kvw
            dv = w[:, o:o + kvw]
            o += kvw
            iqw = w[:, o:o + IDX_HEADS * IDX_DIM]
            o += IDX_HEADS * IDX_DIM
            ikw = w[:, o:o + IDX_DIM]
            o += IDX_DIM
            iww = w[:, o:o + IDX_HEADS]
            w1 = jnp.concatenate([fq, fk, fv, _dup_heads(dv)], axis=1).astype(BF16)
            w2 = jnp.concatenate([dq, _dup_heads(dk), iqw, _dup_heads(ikw)], axis=1).astype(BF16)
            w3 = jnp.concatenate([ffw, iww, jnp.zeros((d, LANES - FOX_HEADS - IDX_HEADS), F32)], axis=1).astype(BF16)
            g = row(ln_mix_e[j])
            p1 = _rms_proj(xs, g, w1, cos_t, sin_t, seq=seq, rope=False, out_dtype=BF16)
            p2 = _rms_proj(xs, g, w2, cos_t, sin_t, seq=seq, rope=True, out_dtype=BF16)
            z = _rms_proj(xs, g, w3, cos_t, sin_t, seq=seq, rope=False, out_dtype=F32)
            b_pad = jnp.concatenate([b_fox_f[j], jnp.zeros((LANES - FOX_HEADS,), F32)]).reshape(1, LANES)
            ccol, crow = _fox_gate(z, b_pad, batch=batch, seq=seq)
            fox_o = _fox_attention(p1, ccol, crow, batch=batch, seq=seq)
            dsa_o = _dsa_attention(p2, p1, z, batch=batch, seq=seq)
            wo = w_out_e[j].astype(BF16)
            xs = _out_proj(fox_o, dsa_o, wo[:fw], wo[fw:], xs)
            xs = _ffn(xs, row(ln_ffn_e[j]), w_ffn_gate[j].astype(BF16), w_ffn_up[j].astype(BF16),
                      w_ffn_down[j].astype(BF16), row(ln_final), tf=1408, final_norm=(layer == depth - 1))
        else:
            w = w_in_o[j]
            lam_init = 0.8 - 0.6 * math.exp(-0.3 * layer)
            g = row(ln_mix_o[j])
            hp = _rms_proj(xs, g, w[:, :4 * hw].astype(BF16), cos_t, sin_t, seq=seq, rope=False, out_dtype=F32)
            pqk = _rms_proj(xs, g, w[:, 4 * hw:4 * hw + 2 * qw].astype(BF16), cos_t, sin_t, seq=seq, rope=True,
                            out_dtype=BF16)
            pv = _rms_proj(xs, g, w[:, 4 * hw + 2 * qw:].astype(BF16), cos_t, sin_t, seq=seq, rope=False,
                           out_dtype=BF16)
            gn = jnp.concatenate([hgrn_norm_g[j], hgrn_norm_g[j]]).reshape(1, LANES)
            o_h = _hgrn(hp, hgrn_lb_logits, gn, batch=batch, seq=seq, layer=layer)
            o_d = _diff_attention(pqk, pv, diff_lambda[j], row(diff_norm_g[j]), batch=batch, seq=seq,
                                  lam_init=lam_init)
            wo = w_out_o[j].astype(BF16)
            xs = _out_proj(o_h, o_d, wo[:hw], wo[hw:], xs)
            wr = jnp.concatenate([w_router[j], jnp.zeros((d, LANES - N_EXPERTS), F32)], axis=1)
            info, xn = _router(xs, row(ln_ffn_o[j]), wr)
            pos, tile_e, tile_act, n_rows = _moe_positions(info, t)
            halves = jnp.arange(2, dtype=jnp.int32)
            dst = (2 * pos[:, None, :] + halves[None, :, None]).reshape(2 * t, 2)
            rows = _sc_scatter_rows(xn.reshape(2 * t, d // 4), dst[:, 0], dst[:, 1], 2 * n_rows)
            y = _moe_ffn_packed(tile_e, tile_act, rows.reshape(n_rows, d // 2), w_exp_gate[j].astype(BF16),
                                w_exp_up[j].astype(BF16), w_exp_down[j].astype(BF16), tf=896)
            src = (2 * pos.reshape(2 * t, 1) + halves[None, :]).reshape(4 * t)
            y_pairs = _sc_gather_rows(y.reshape(2 * n_rows, d // 4), src).reshape(t, d)
            xs = _moe_sum(xs, info, y_pairs, row(ln_final), final_norm=(layer == depth - 1))
    if depth % 2 == 1:
        pass
    return xs.reshape(batch, seq, d)
```

```python
import functools
import math

import jax
import jax.numpy as jnp
from jax import lax
from jax.experimental import pallas as pl
from jax.experimental.pallas import tpu as pltpu
from jax.experimental.pallas import tpu_sc as plsc

F32 = jnp.float32
BF16 = jnp.bfloat16

D_MODEL = 1024
HEAD_DIM = 64
FOX_HEADS = 8
DSA_HEADS = 8
DSA_KV_HEADS = 2
IDX_HEADS = 4
IDX_DIM = 64
DSA_TOPK = 256
HGRN_HEADS = 8
DIFF_HEADS = 4
DIFF_DIM = 64
ROPE_THETA = 10000.0
N_EXPERTS = 8
NORM_EPS = 1e-6
NEG_INF = -1e30

LANES = 128
SUBLANES = 8
VMEM_LIMIT_BYTES = 56 * 1024 * 1024

ROW_TILE = 512
ATTN_TILE = 256
HGRN_CHUNK = 256
SUB = 8
MOE_ROW_TILE = 512
MOE_SUB = 128
MOE_TOKEN_BLOCK = 512
SC_WINDOW = 128
SC_PLANES = 2

INT_MIN = -(2 ** 31)


def _params(*sem):
    return pltpu.CompilerParams(dimension_semantics=sem, vmem_limit_bytes=VMEM_LIMIT_BYTES)


def _dot(a, b):
    return jnp.dot(a, b, preferred_element_type=F32)


def _dot_nt(a, b):
    return lax.dot_general(a, b, (((1,), (1,)), ((), ())), preferred_element_type=F32)


def _dot_tn(a, b):
    return lax.dot_general(a, b, (((0,), (0,)), ((), ())), preferred_element_type=F32)


def _sigmoid(x):
    return 1.0 / (1.0 + jnp.exp(-x))


def _silu(x):
    return x * _sigmoid(x)


def _lane_half(shape):
    return lax.broadcasted_iota(jnp.int32, shape, len(shape) - 1) // HEAD_DIM


def _rms_proj_kernel(x_ref, g_ref, w_ref, cos_ref, sin_ref, o_ref, *, rope):
    x = x_ref[...]
    var = jnp.mean(x * x, axis=-1, keepdims=True)
    xn = (x * lax.rsqrt(var + NORM_EPS) * g_ref[...]).astype(BF16)
    y = _dot(xn, w_ref[...])
    if not rope:
        o_ref[...] = y.astype(o_ref.dtype)
        return
    cos = cos_ref[...]
    sin = sin_ref[...]
    first = (lax.broadcasted_iota(jnp.int32, cos.shape, 1) % HEAD_DIM) < (HEAD_DIM // 2)
    for c in range(y.shape[1] // LANES):
        yc = y[:, c * LANES:(c + 1) * LANES]
        partner = jnp.where(first, pltpu.roll(yc, LANES - HEAD_DIM // 2, 1), pltpu.roll(yc, HEAD_DIM // 2, 1))
        o_ref[:, c * LANES:(c + 1) * LANES] = (yc * cos + partner * sin).astype(o_ref.dtype)


def _rms_proj(x, g, w, cos, sin, *, seq, rope, out_dtype):
    t, d = x.shape
    n = w.shape[1]
    tm = ROW_TILE
    nseq = seq // tm
    return pl.pallas_call(
        functools.partial(_rms_proj_kernel, rope=rope),
        grid=(t // tm,),
        in_specs=[
            pl.BlockSpec((tm, d), lambda i: (i, 0)),
            pl.BlockSpec((1, d), lambda i: (0, 0)),
            pl.BlockSpec((d, n), lambda i: (0, 0)),
            pl.BlockSpec((tm, LANES), lambda i: (i % nseq, 0)),
            pl.BlockSpec((tm, LANES), lambda i: (i % nseq, 0)),
        ],
        out_specs=pl.BlockSpec((tm, n), lambda i: (i, 0)),
        out_shape=jax.ShapeDtypeStruct((t, n), out_dtype),
        compiler_params=_params("parallel"),
        name="rms_proj_rope" if rope else "rms_proj",
    )(x, g, w, cos, sin)


def _out_proj_kernel(a1_ref, a2_ref, w1_ref, w2_ref, r_ref, o_ref):
    o_ref[...] = r_ref[...] + _dot(a1_ref[...], w1_ref[...]) + _dot(a2_ref[...], w2_ref[...])


def _out_proj(a1, a2, w1, w2, res):
    t, d = res.shape
    k1, k2 = a1.shape[1], a2.shape[1]
    tm = ROW_TILE
    return pl.pallas_call(
        _out_proj_kernel,
        grid=(t // tm,),
        in_specs=[
            pl.BlockSpec((tm, k1), lambda i: (i, 0)),
            pl.BlockSpec((tm, k2), lambda i: (i, 0)),
            pl.BlockSpec((k1, d), lambda i: (0, 0)),
            pl.BlockSpec((k2, d), lambda i: (0, 0)),
            pl.BlockSpec((tm, d), lambda i: (i, 0)),
        ],
        out_specs=pl.BlockSpec((tm, d), lambda i: (i, 0)),
        out_shape=jax.ShapeDtypeStruct((t, d), F32),
        compiler_params=_params("parallel"),
        name="out_proj",
    )(a1, a2, w1, w2, res)


def _cumsum_rows(x):
    rows = x.shape[0]
    row = lax.broadcasted_iota(jnp.int32, x.shape, 0)
    k = 1
    while k < rows:
        x = x + jnp.where(row >= k, pltpu.roll(x, k, 0), 0.0)
        k *= 2
    return x


def _fox_gate_kernel(z_ref, b_ref, ccol_ref, crow_ref, *, tk):
    z = z_ref[...] + b_ref[...]
    logf = jnp.minimum(z, 0.0) - jnp.log(1.0 + jnp.exp(-jnp.abs(z)))
    c = _cumsum_rows(logf)
    ccol_ref[...] = c
    for j in range(c.shape[0] // tk):
        ct = c[j * tk:(j + 1) * tk, :].T
        crow_ref[0, :, j * tk:(j + 1) * tk] = ct[:SUBLANES, :]


def _fox_gate(z, b_pad, *, batch, seq):
    tk = ATTN_TILE
    return pl.pallas_call(
        functools.partial(_fox_gate_kernel, tk=tk),
        grid=(batch,),
        in_specs=[
            pl.BlockSpec((seq, LANES), lambda b: (b, 0)),
            pl.BlockSpec((1, LANES), lambda b: (0, 0)),
        ],
        out_specs=[
            pl.BlockSpec((seq, LANES), lambda b: (b, 0)),
            pl.BlockSpec((1, SUBLANES, seq), lambda b: (b, 0, 0)),
        ],
        out_shape=[
            jax.ShapeDtypeStruct((batch * seq, LANES), F32),
            jax.ShapeDtypeStruct((batch, SUBLANES, seq), F32),
        ],
        compiler_params=_params("parallel"),
        name="fox_gate",
    )(z, b_pad)


def _softmax_pv(s, v):
    m = jnp.max(s, axis=-1, keepdims=True)
    p = jnp.exp(s - m)
    l = jnp.sum(p, axis=-1, keepdims=True)
    return _dot(p.astype(BF16), v) / l


def _mask_last_block(s, tq):
    n = s.shape[1]
    causal = lax.broadcasted_iota(jnp.int32, (tq, tq), 1) <= lax.broadcasted_iota(jnp.int32, (tq, tq), 0)
    tail = jnp.where(causal, s[:, n - tq:], NEG_INF)
    return tail if n == tq else jnp.concatenate([s[:, :n - tq], tail], axis=1)


def _fox_kernel(q_ref, k_ref, v_ref, cc_ref, cr_ref, o_ref, *, tq, scale):
    hp = pl.program_id(1)
    seq = q_ref.shape[0]
    half = _lane_half((1, LANES))
    lane = lax.broadcasted_iota(jnp.int32, (tq, LANES), 1)
    for i in range(seq // tq):
        n = (i + 1) * tq
        q = q_ref[i * tq:n, :] * scale
        cc = cc_ref[i * tq:n, :]
        k = k_ref[0:n, :]
        v = v_ref[0:n, :]
        outs = []
        for e in range(2):
            h = 2 * hp + e
            qe = jnp.where(half == e, q, jnp.zeros_like(q))
            cq = jnp.sum(jnp.where(lane == h, cc, 0.0), axis=-1, keepdims=True)
            ck = cr_ref[0, pl.ds(h, 1), 0:n]
            s = _mask_last_block(_dot_nt(qe, k) + cq - ck, tq)
            outs.append(_softmax_pv(s, v))
        o_ref[i * tq:n, :] = jnp.where(half == 0, outs[0], outs[1]).astype(o_ref.dtype)


def _fox_attention(p1, ccol, crow, *, batch, seq):
    tq = min(ATTN_TILE, seq)
    ng = FOX_HEADS // 2
    return pl.pallas_call(
        functools.partial(_fox_kernel, tq=tq, scale=HEAD_DIM ** -0.5),
        grid=(batch, ng),
        in_specs=[
            pl.BlockSpec((seq, LANES), lambda b, g: (b, g)),
            pl.BlockSpec((seq, LANES), lambda b, g: (b, ng + g)),
            pl.BlockSpec((seq, LANES), lambda b, g: (b, 2 * ng + g)),
            pl.BlockSpec((seq, LANES), lambda b, g: (b, 0)),
            pl.BlockSpec((1, SUBLANES, seq), lambda b, g: (b, 0, 0)),
        ],
        out_specs=pl.BlockSpec((seq, LANES), lambda b, g: (b, g)),
        out_shape=jax.ShapeDtypeStruct((batch * seq, ng * LANES), BF16),
        compiler_params=_params("parallel", "parallel"),
        name="fox_attention",
    )(p1, p1, p1, ccol, crow)


def _diff_kernel(q_ref, k_ref, v_ref, lam_ref, g_ref, o_ref, *, tq, scale, lam_init):
    seq = q_ref.shape[0]
    half = _lane_half((1, LANES))
    lv = lam_ref[...]
    lam = (jnp.exp(jnp.sum(lv[0:1, :] * lv[1:2, :], axis=-1, keepdims=True))
           - jnp.exp(jnp.sum(lv[2:3, :] * lv[3:4, :], axis=-1, keepdims=True)) + lam_init)
    gn = g_ref[...] * (1.0 - lam_init)
    for i in range(seq // tq):
        n = (i + 1) * tq
        q = q_ref[i * tq:n, :] * scale
        k = k_ref[0:n, :]
        v = v_ref[0:n, :]
        outs = []
        for e in range(2):
            qe = jnp.where(half == e, q, jnp.zeros_like(q))
            outs.append(_softmax_pv(_mask_last_block(_dot_nt(qe, k), tq), v))
        o = outs[0] - lam * outs[1]
        var = jnp.mean(o * o, axis=-1, keepdims=True)
        o_ref[i * tq:n, :] = (o * lax.rsqrt(var + NORM_EPS) * gn).astype(o_ref.dtype)


def _diff_attention(pqk, pv, lam_vecs, g, *, batch, seq, lam_init):
    tq = min(ATTN_TILE, seq)
    ng = DIFF_HEADS
    return pl.pallas_call(
        functools.partial(_diff_kernel, tq=tq, scale=DIFF_DIM ** -0.5, lam_init=lam_init),
        grid=(batch, ng),
        in_specs=[
            pl.BlockSpec((seq, LANES), lambda b, g: (b, g)),
            pl.BlockSpec((seq, LANES), lambda b, g: (b, ng + g)),
            pl.BlockSpec((seq, LANES), lambda b, g: (b, g)),
            pl.BlockSpec(lam_vecs.shape, lambda b, g: (0, 0)),
            pl.BlockSpec((1, LANES), lambda b, g: (0, 0)),
        ],
        out_specs=pl.BlockSpec((seq, LANES), lambda b, g: (b, g)),
        out_shape=jax.ShapeDtypeStruct((batch * seq, ng * LANES), BF16),
        compiler_params=_params("parallel", "parallel"),
        name="diff_attention",
    )(pqk, pqk, pv, lam_vecs, g)


def _sortable_key(x):
    bits = pltpu.bitcast(x, jnp.int32)
    return jnp.where(bits < 0, bits ^ jnp.int32(0x7FFFFFFF), bits)


def _dsa_kernel(q_ref, kd_ref, iq_ref, ik_ref, vd_ref, z_ref, o_ref, key_ref, madd_ref,
                *, tq, n_sel, scale, idx_scale, w_scale):
    i = pl.program_id(1)
    seq = key_ref.shape[1]
    nq = seq // tq
    span = 2 if nq % 2 == 0 else 1
    half = _lane_half((1, LANES))
    row = lax.broadcasted_iota(jnp.int32, (tq, tq), 0)
    col = lax.broadcasted_iota(jnp.int32, (tq, tq), 1)
    row_g = row + i * tq
    tri = jnp.where(row < col, 1.0, 0.0).astype(BF16)

    def block(n):
        z = z_ref[...]
        iq = iq_ref[...]
        ik = ik_ref[0:n, :]
        sc = jnp.zeros((tq, n), F32)
        for h in range(IDX_HEADS):
            grp = iq[:, (h // 2) * LANES:(h // 2 + 1) * LANES]
            iq_h = jnp.where(half == h % 2, grp, jnp.zeros_like(grp))
            w_h = z[:, SUBLANES + h:SUBLANES + h + 1] * w_scale
            sc = sc + w_h * jnp.maximum(_dot_nt(iq_h, ik) * idx_scale, 0.0)
        for j in range(n // tq):
            scj = jnp.where(col + j * tq <= row_g, sc[:, j * tq:(j + 1) * tq], -jnp.inf)
            key_ref[:, j * tq:(j + 1) * tq] = _sortable_key(scj)

        def count(pred):
            return jnp.sum(jnp.where(pred, 1.0, 0.0), axis=-1, keepdims=True)

        thr0 = jnp.where(count(key_ref[:, 0:n] >= 0) >= n_sel, 0, INT_MIN).astype(jnp.int32)

        def bit_body(it, thr):
            cand = thr | jnp.left_shift(jnp.int32(1), 30 - it)
            return jnp.where(count(key_ref[:, 0:n] >= cand) >= n_sel, cand, thr)

        thr = lax.fori_loop(0, 31, bit_body, thr0)
        need = n_sel - count(key_ref[:, 0:n] > thr)

        off = jnp.zeros((tq, 1), F32)
        for j in range(n // tq):
            kj = key_ref[:, j * tq:(j + 1) * tq]
            eq = kj == thr
            eqf = jnp.where(eq, 1.0, 0.0)
            rank = _dot(eqf.astype(BF16), tri) + off
            take = jnp.where(kj > thr, 1.0, jnp.where(eq, jnp.where(rank < need, 1.0, 0.0), 0.0))
            valid = col + j * tq <= row_g
            madd_ref[:, j * tq:(j + 1) * tq] = jnp.where(valid, jnp.where(take > 0.0, 0.0, NEG_INF), NEG_INF)
            off = off + jnp.sum(eqf, axis=-1, keepdims=True)

        q = q_ref[...] * scale
        for p in range(DSA_HEADS // 2):
            qg = q[:, p * LANES:(p + 1) * LANES]
            g = p // (DSA_HEADS // DSA_KV_HEADS // 2)
            k = kd_ref[0:n, g * LANES:(g + 1) * LANES]
            v = vd_ref[0:n, g * LANES:(g + 1) * LANES]
            outs = []
            for e in range(2):
                qe = jnp.where(half == e, qg, jnp.zeros_like(qg))
                outs.append(_softmax_pv(_dot_nt(qe, k) + madd_ref[:, 0:n], v))
            o_ref[:, p * LANES:(p + 1) * LANES] = jnp.where(half == 0, outs[0], outs[1]).astype(o_ref.dtype)

    for c in range(nq // span):
        pl.when(i // span == c)(functools.partial(block, (c + 1) * span * tq))


def _dsa_attention(p2, p1, z, *, batch, seq):
    tq = min(ATTN_TILE, seq)
    nq = seq // tq
    n_sel = min(DSA_TOPK, seq // 4)
    qw = DSA_HEADS * HEAD_DIM
    return pl.pallas_call(
        functools.partial(_dsa_kernel, tq=tq, n_sel=float(n_sel), scale=HEAD_DIM ** -0.5,
                          idx_scale=IDX_DIM ** -0.5, w_scale=IDX_HEADS ** -0.5),
        grid=(batch, nq),
        in_specs=[
            pl.BlockSpec((tq, qw), lambda b, i: (b * nq + i, 0)),
            pl.BlockSpec((seq, 2 * LANES), lambda b, i: (b, 2)),
            pl.BlockSpec((tq, 2 * LANES), lambda b, i: (b * nq + i, 3)),
            pl.BlockSpec((seq, LANES), lambda b, i: (b, 8)),
            pl.BlockSpec((seq, 2 * LANES), lambda b, i: (b, 6)),
            pl.BlockSpec((tq, LANES), lambda b, i: (b * nq + i, 0)),
        ],
        out_specs=pl.BlockSpec((tq, qw), lambda b, i: (b * nq + i, 0)),
        out_shape=jax.ShapeDtypeStruct((batch * seq, qw), BF16),
        scratch_shapes=[pltpu.VMEM((tq, seq), jnp.int32), pltpu.VMEM((tq, seq), F32)],
        compiler_params=_params("parallel", "arbitrary"),
        name="dsa_attention",
    )(p2, p2, p2, p2, p1, z)


def _hgrn_kernel(q_ref, f_ref, i_ref, g_ref, lbl_ref, gn_ref, o_ref, b_scr, *, chunk, layer):
    seq = q_ref.shape[0]
    half = _lane_half((1, LANES))
    logits = lbl_ref[...]
    pe = jnp.exp(logits - jnp.max(logits, axis=0, keepdims=True))
    prob = pe / jnp.sum(pe, axis=0, keepdims=True)
    csum = prob[0:1, :]
    for d in range(1, layer + 1):
        csum = csum + prob[d:d + 1, :]
    lb = csum - prob[0:1, :]

    rowi = lax.broadcasted_iota(jnp.int32, (chunk, 1), 0)
    xor_rc = (lax.broadcasted_iota(jnp.int32, (chunk, chunk), 0)
              ^ lax.broadcasted_iota(jnp.int32, (chunk, chunk), 1))
    same_head = (lax.broadcasted_iota(jnp.int32, (LANES, LANES), 0) // HEAD_DIM
                 == lax.broadcasted_iota(jnp.int32, (LANES, LANES), 1) // HEAD_DIM)
    ones_bd = jnp.where(same_head, 1.0, 0.0).astype(BF16)
    gn = gn_ref[...]

    def chunk_body(c, state_t):
        r0 = pl.multiple_of(c * chunk, chunk)
        q = _silu(q_ref[pl.ds(r0, chunk), :])
        f = lb + (1.0 - lb) * _sigmoid(f_ref[pl.ds(r0, chunk), :])
        kc = 1.0 - f
        v = i_ref[pl.ds(r0, chunk), :]
        b = _cumsum_rows(jnp.log(f))
        b_scr[...] = b
        b_last = b_scr[chunk - 1:chunk, :]

        o = _dot_nt((q * jnp.exp(b)).astype(BF16), state_t.astype(BF16))
        k2 = (kc * jnp.exp(b_last - b)).astype(BF16)
        upd = _dot_tn(v.astype(BF16), k2)
        new_state = state_t * jnp.exp(b_last) + jnp.where(same_head, upd, 0.0)

        s_tot = [jnp.zeros((chunk, chunk), F32), jnp.zeros((chunk, chunk), F32)]
        m = chunk
        while m > SUB:
            hm = m // 2
            pieces = [jnp.broadcast_to(b_scr[blk * m + hm - 1:blk * m + hm, :], (m, LANES))
                      for blk in range(chunk // m)]
            ref = pieces[0] if len(pieces) == 1 else jnp.concatenate(pieces, axis=0)
            right = (rowi % m) >= hm
            qm = jnp.where(right, q * jnp.exp(jnp.where(right, b - ref, 0.0)), 0.0).astype(BF16)
            km = jnp.where(right, 0.0, kc * jnp.exp(jnp.where(right, 0.0, ref - b))).astype(BF16)
            for e in range(2):
                s = _dot_nt(jnp.where(half == e, qm, jnp.zeros_like(qm)), km)
                s_tot[e] = s_tot[e] + (s if m == chunk else jnp.where(xor_rc < m, s, 0.0))
            m = hm
        vb = v.astype(BF16)
        for e in range(2):
            o = o + _dot(s_tot[e].astype(BF16), jnp.where(half == e, vb, jnp.zeros_like(vb)))

        for d in range(SUB):
            ok = (rowi % SUB) >= d
            bs = b if d == 0 else pltpu.roll(b, d, 0)
            ks = kc if d == 0 else pltpu.roll(kc, d, 0)
            vs = v if d == 0 else pltpu.roll(v, d, 0)
            x = jnp.where(ok, q * ks * jnp.exp(jnp.where(ok, b - bs, 0.0)), 0.0)
            o = o + _dot(x.astype(BF16), ones_bd) * vs

        sq = o * o
        s0 = jnp.sum(jnp.where(half == 0, sq, 0.0), axis=-1, keepdims=True)
        s1 = jnp.sum(jnp.where(half == 1, sq, 0.0), axis=-1, keepdims=True)
        var = jnp.where(half == 0, s0, s1) * (1.0 / HEAD_DIM)
        y = o * lax.rsqrt(var + NORM_EPS) * gn * _silu(g_ref[pl.ds(r0, chunk), :])
        o_ref[pl.ds(r0, chunk), :] = y.astype(o_ref.dtype)
        return new_state

    lax.fori_loop(0, seq // chunk, chunk_body, jnp.zeros((LANES, LANES), F32))


def _hgrn(hproj, lb_logits, gn, *, batch, seq, layer):
    ng = HGRN_HEADS // 2
    chunk = min(HGRN_CHUNK, seq)
    spec = lambda off: pl.BlockSpec((seq, LANES), lambda b, g: (b, off * ng + g))
    return pl.pallas_call(
        functools.partial(_hgrn_kernel, chunk=chunk, layer=layer),
        grid=(batch, ng),
        in_specs=[spec(0), spec(1), spec(2), spec(3),
                  pl.BlockSpec((lb_logits.shape[0], LANES), lambda b, g: (0, g)),
                  pl.BlockSpec((1, LANES), lambda b, g: (0, 0))],
        out_specs=pl.BlockSpec((seq, LANES), lambda b, g: (b, g)),
        out_shape=jax.ShapeDtypeStruct((batch * seq, ng * LANES), BF16),
        scratch_shapes=[pltpu.VMEM((chunk, LANES), F32)],
        compiler_params=_params("parallel", "parallel"),
        name="hgrn2",
    )(hproj, hproj, hproj, hproj, lb_logits, gn)


def _pack_pairs(x):
    hw = x.shape[1] // 2
    hi = pltpu.bitcast(x[:, :hw].astype(BF16).astype(F32), jnp.int32)
    lo = pltpu.bitcast(x[:, hw:].astype(BF16).astype(F32), jnp.int32)
    return hi | lax.shift_right_logical(lo, 16)


def _unpack_pairs(w):
    hi = pltpu.bitcast(w & jnp.int32(-65536), F32)
    lo = pltpu.bitcast(lax.shift_left(w, 16), F32)
    return hi, lo


def _store_planes(ref, words):
    pw = words.shape[1] // SC_PLANES
    for h in range(SC_PLANES):
        ref[h] = words[:, h * pw:(h + 1) * pw]


def _load_planes(ref):
    return jnp.concatenate([ref[h] for h in range(SC_PLANES)], axis=1)


def _router_kernel(x_ref, g_ref, w_ref, o_ref, xn_ref):
    x = x_ref[...]
    var = jnp.mean(x * x, axis=-1, keepdims=True)
    xn = x * lax.rsqrt(var + NORM_EPS) * g_ref[...]
    _store_planes(xn_ref, _pack_pairs(xn))
    logits = jnp.dot(xn, w_ref[...], precision=lax.Precision.HIGHEST, preferred_element_type=F32)
    lane = lax.broadcasted_iota(jnp.int32, logits.shape, 1)
    lg = jnp.where(lane < N_EXPERTS, logits, -jnp.inf)
    m1 = jnp.max(lg, axis=-1, keepdims=True)
    i1 = jnp.min(jnp.where(lg == m1, lane, LANES), axis=-1, keepdims=True)
    lg2 = jnp.where(lane == i1, -jnp.inf, lg)
    m2 = jnp.max(lg2, axis=-1, keepdims=True)
    i2 = jnp.min(jnp.where(lg2 == m2, lane, LANES), axis=-1, keepdims=True)
    t = jnp.exp(m2 - m1)
    den = 1.0 + t
    o_ref[...] = (jnp.where(lane == 0, i1.astype(F32), 0.0) + jnp.where(lane == 1, i2.astype(F32), 0.0)
                  + jnp.where(lane == 2, 1.0 / den, 0.0) + jnp.where(lane == 3, t / den, 0.0))


def _router(x, g, w_pad):
    t, d = x.shape
    tm = ROW_TILE
    return pl.pallas_call(
        _router_kernel,
        grid=(t // tm,),
        in_specs=[pl.BlockSpec((tm, d), lambda i: (i, 0)),
                  pl.BlockSpec((1, d), lambda i: (0, 0)),
                  pl.BlockSpec((d, LANES), lambda i: (0, 0))],
        out_specs=[pl.BlockSpec((tm, LANES), lambda i: (i, 0)),
                   pl.BlockSpec((SC_PLANES, tm, d // 2 // SC_PLANES), lambda i: (0, i, 0))],
        out_shape=[jax.ShapeDtypeStruct((t, LANES), F32),
                   jax.ShapeDtypeStruct((SC_PLANES, t, d // 2 // SC_PLANES), jnp.int32)],
        compiler_params=_params("parallel"),
        name="router",
    )(x, g, w_pad)


def _moe_gather_kernel(osub_ref, iblk_ref, first_ref, valid_ref, ids_ref, x_ref, o_ref, *, tw):
    w = pl.program_id(0)

    @pl.when(first_ref[w] == 1)
    def _():
        o_ref[...] = jnp.zeros_like(o_ref)

    @pl.when(valid_ref[w] == 1)
    def _():
        tok = ids_ref[0, 0:1, :]
        cand = lax.broadcasted_iota(jnp.int32, (tw, MOE_SUB), 0) + iblk_ref[w] * tw
        onehot_t = jnp.where(cand == tok, 1.0, 0.0).astype(BF16)
        o_ref[...] += _dot_tn(onehot_t, x_ref[...]).astype(o_ref.dtype)


def _moe_gather(plan, ids3, xn):
    t, d = xn.shape
    n_sub = ids3.shape[0]
    osub, iblk, first, valid = plan
    tw = MOE_TOKEN_BLOCK
    return pl.pallas_call(
        functools.partial(_moe_gather_kernel, tw=tw),
        grid_spec=pltpu.PrefetchScalarGridSpec(
            num_scalar_prefetch=4,
            grid=(osub.shape[0],),
            in_specs=[pl.BlockSpec((1, SUBLANES, MOE_SUB), lambda w, o, i, f, v: (o[w], 0, 0)),
                      pl.BlockSpec((tw, d), lambda w, o, i, f, v: (i[w], 0))],
            out_specs=pl.BlockSpec((MOE_SUB, d), lambda w, o, i, f, v: (o[w], 0)),
        ),
        out_shape=jax.ShapeDtypeStruct((n_sub * MOE_SUB, d), BF16),
        compiler_params=_params("arbitrary"),
        name="moe_gather",
    )(osub, iblk, first, valid, ids3, xn)


def _moe_combine_kernel(otile_ref, isub_ref, first_ref, last_ref, valid_ref, ids_ref, y_ref, x_ref, gf_ref,
                        o_ref, acc_ref, *, tt, final_norm):
    w = pl.program_id(0)

    @pl.when(first_ref[w] == 1)
    def _():
        acc_ref[...] = x_ref[...]

    @pl.when(valid_ref[w] == 1)
    def _():
        tok = ids_ref[0, 0:1, :]
        cand = lax.broadcasted_iota(jnp.int32, (tt, MOE_SUB), 0) + otile_ref[w] * tt
        onehot = jnp.where(cand == tok, 1.0, 0.0).astype(BF16)
        acc_ref[...] += _dot(onehot, y_ref[...])

    @pl.when(last_ref[w] == 1)
    def _():
        out = acc_ref[...]
        if final_norm:
            var = jnp.mean(out * out, axis=-1, keepdims=True)
            out = out * lax.rsqrt(var + NORM_EPS) * gf_ref[...]
        o_ref[...] = out


def _moe_combine(plan, ids3, y, x, gf, *, final_norm):
    t, d = x.shape
    otile, isub, first, last, valid = plan
    tt = ROW_TILE
    return pl.pallas_call(
        functools.partial(_moe_combine_kernel, tt=tt, final_norm=final_norm),
        grid_spec=pltpu.PrefetchScalarGridSpec(
            num_scalar_prefetch=5,
            grid=(otile.shape[0],),
            in_specs=[pl.BlockSpec((1, SUBLANES, MOE_SUB), lambda w, o, i, f, l, v: (i[w], 0, 0)),
                      pl.BlockSpec((MOE_SUB, d), lambda w, o, i, f, l, v: (i[w], 0)),
                      pl.BlockSpec((tt, d), lambda w, o, i, f, l, v: (o[w], 0)),
                      pl.BlockSpec((1, d), lambda w, o, i, f, l, v: (0, 0))],
            out_specs=pl.BlockSpec((tt, d), lambda w, o, i, f, l, v: (o[w], 0)),
            scratch_shapes=[pltpu.VMEM((tt, d), F32)],
        ),
        out_shape=jax.ShapeDtypeStruct((t, d), F32),
        compiler_params=_params("arbitrary"),
        name="moe_combine",
    )(otile, isub, first, last, valid, ids3, y, x, gf)


def _moe_ffn_kernel(te_ref, act_ref, x_ref, gate_ref, wg_ref, wu_ref, wd_ref, o_ref, acc_ref, *, tm):
    i = pl.program_id(0)
    f = pl.program_id(1)

    @pl.when(act_ref[i] == 1)
    def _():
        @pl.when(f == 0)
        def _():
            acc_ref[...] = jnp.zeros_like(acc_ref)

        x = x_ref[...]
        mid = (_silu(_dot(x, wg_ref[0])) * _dot(x, wu_ref[0])).astype(BF16)
        acc_ref[...] += _dot(mid, wd_ref[0])

        @pl.when(f == pl.num_programs(1) - 1)
        def _():
            gates = gate_ref[0]
            for k in range(tm // LANES):
                col = jnp.broadcast_to(gates[k:k + 1, :], (LANES, LANES)).T
                rows = acc_ref[k * LANES:(k + 1) * LANES, :]
                for c in range(rows.shape[1] // LANES):
                    o_ref[k * LANES:(k + 1) * LANES, c * LANES:(c + 1) * LANES] = (
                        rows[:, c * LANES:(c + 1) * LANES] * col).astype(o_ref.dtype)

    @pl.when(jnp.logical_and(act_ref[i] == 0, f == 0))
    def _():
        o_ref[...] = jnp.zeros_like(o_ref)


def _moe_ffn(tile_e, tile_act, xs, gates3, wg, wu, wd, *, tf):
    p, d = xs.shape
    ff = wg.shape[2]
    tm = MOE_ROW_TILE
    return pl.pallas_call(
        functools.partial(_moe_ffn_kernel, tm=tm),
        grid_spec=pltpu.PrefetchScalarGridSpec(
            num_scalar_prefetch=2,
            grid=(p // tm, ff // tf),
            in_specs=[pl.BlockSpec((tm, d), lambda i, f, te, ta: (i, 0)),
                      pl.BlockSpec((1, tm // LANES, LANES), lambda i, f, te, ta: (i, 0, 0)),
                      pl.BlockSpec((1, d, tf), lambda i, f, te, ta: (te[i], 0, f * ta[i])),
                      pl.BlockSpec((1, d, tf), lambda i, f, te, ta: (te[i], 0, f * ta[i])),
                      pl.BlockSpec((1, tf, d), lambda i, f, te, ta: (te[i], f * ta[i], 0))],
            out_specs=pl.BlockSpec((tm, d), lambda i, f, te, ta: (i, 0)),
            scratch_shapes=[pltpu.VMEM((tm, d), F32)],
        ),
        out_shape=jax.ShapeDtypeStruct((p, d), BF16),
        compiler_params=_params("arbitrary", "arbitrary"),
        name="moe_ffn",
    )(tile_e, tile_act, xs, gates3, wg, wu, wd)


def _moe_plan(info, t):
    tm, sub, tw, tt = MOE_ROW_TILE, MOE_SUB, MOE_TOKEN_BLOCK, ROW_TILE
    n_e = N_EXPERTS
    a = 2 * t
    p = a + n_e * tm
    n_sub = p // sub
    e_flat = info[:, 0:2].astype(jnp.int32).reshape(a)
    g_flat = info[:, 2:4].reshape(a)
    onehot = (e_flat[:, None] == jnp.arange(n_e, dtype=jnp.int32)[None, :]).astype(jnp.int32)
    csum = jnp.cumsum(onehot, axis=0)
    rank = jnp.sum(csum * onehot, axis=1) - 1
    counts = csum[-1]
    padded = ((counts + tm - 1) // tm) * tm
    ends = jnp.cumsum(padded)
    starts = ends - padded
    pos = starts[e_flat] + rank
    tok_row = jnp.full((p,), -1, jnp.int32).at[pos].set(jnp.arange(a, dtype=jnp.int32) // 2)
    gate_row = jnp.zeros((p,), F32).at[pos].set(g_flat)
    ids3 = jnp.broadcast_to(tok_row.reshape(n_sub, 1, sub), (n_sub, SUBLANES, sub))
    gates3 = gate_row.reshape(p // tm, tm // LANES, LANES)

    tile_start = jnp.arange(p // tm, dtype=jnp.int32) * tm
    tile_e = jnp.minimum(jnp.searchsorted(ends, tile_start, side='right'), n_e - 1).astype(jnp.int32)
    tile_act = (tile_start < ends[-1]).astype(jnp.int32)

    tok2 = tok_row.reshape(n_sub, sub)
    has = jnp.any(tok2 >= 0, axis=1)
    tmin = jnp.min(jnp.where(tok2 >= 0, tok2, t), axis=1)
    tmax = jnp.max(tok2, axis=1)

    def worklist(lo, hi, cnt, length):
        cend = jnp.cumsum(cnt)
        total = cend[-1]
        w = jnp.arange(length, dtype=jnp.int32)
        s = jnp.minimum(jnp.searchsorted(cend, w, side='right'), n_sub - 1).astype(jnp.int32)
        k = w - (cend[s] - cnt[s])
        valid = w < total
        blk = jnp.where(valid, lo[s] + k, 0)
        return s, blk.astype(jnp.int32), k, valid

    lo = jnp.where(has, tmin // tw, 0)
    hi = jnp.where(has, tmax // tw, 0)
    cnt = (hi - lo + 1).astype(jnp.int32)
    len_g = n_sub + n_e * (t // tw - 1)
    s, blk, k, valid = worklist(lo, hi, cnt, len_g)
    last_valid = jnp.sum(valid.astype(jnp.int32)) - 1
    s = jnp.where(valid, s, s[last_valid])
    blk = jnp.where(valid, blk, blk[last_valid])
    gather_plan = (s, blk, jnp.logical_and(valid, k == 0).astype(jnp.int32), valid.astype(jnp.int32))

    lo = jnp.where(has, tmin // tt, 0)
    hi = jnp.where(has, tmax // tt, -1)
    cnt = (hi - lo + 1).astype(jnp.int32)
    len_c = n_sub + n_e * (t // tt - 1)
    s, tile, k, valid = worklist(lo, hi, cnt, len_c)
    n_tt = t // tt
    order = jnp.argsort(jnp.where(valid, tile, n_tt), stable=True)
    s, tile, valid = s[order], tile[order], valid[order]
    tile = jnp.where(valid, tile, n_tt - 1)
    last_valid = jnp.sum(valid.astype(jnp.int32)) - 1
    s = jnp.where(valid, s, s[last_valid])
    prev = jnp.concatenate([jnp.full((1,), -1, jnp.int32), tile[:-1]])
    nxt = jnp.concatenate([tile[1:], jnp.full((1,), -1, jnp.int32)])
    combine_plan = (tile, s, (tile != prev).astype(jnp.int32), (tile != nxt).astype(jnp.int32),
                    valid.astype(jnp.int32))
    return ids3, gates3, tile_e, tile_act, gather_plan, combine_plan


def _sc_mesh():
    return plsc.VectorSubcoreMesh(core_axis_name="core", subcore_axis_name="subcore")


def _sc_scatter_rows(x, idx0, idx1, n_out):
    n, d = x.shape

    @functools.partial(pl.kernel, out_type=jax.ShapeDtypeStruct((n_out, d), x.dtype), mesh=_sc_mesh(),
                       scratch_types=[])
    def scatter(x_hbm, i0_hbm, i1_hbm, o_hbm):
        def body(x_vmem, i0_vmem, i1_vmem):
            pltpu.sync_copy(x_vmem, o_hbm.at[i0_vmem.at[0]])
            pltpu.sync_copy(x_vmem, o_hbm.at[i1_vmem.at[0]])

        pltpu.emit_pipeline(
            body,
            grid=(n // SC_WINDOW,),
            in_specs=[pl.BlockSpec((SC_WINDOW, d), index_map=lambda i: (i, 0)),
                      pl.BlockSpec((1, SC_WINDOW), index_map=lambda i: (0, i)),
                      pl.BlockSpec((1, SC_WINDOW), index_map=lambda i: (0, i))],
            out_specs=[],
            core_axis_name=("core", "subcore"),
            dimension_semantics=(pltpu.PARALLEL,),
        )(x_hbm, i0_hbm, i1_hbm)

    return scatter(x, idx0.reshape(1, n), idx1.reshape(1, n))


def _sc_gather_rows(x, idx):
    n = idx.shape[0]
    d = x.shape[1]

    @functools.partial(pl.kernel, out_type=jax.ShapeDtypeStruct((n, d), x.dtype), mesh=_sc_mesh(),
                       scratch_types=[])
    def gather(x_hbm, i_hbm, o_hbm):
        def body(i_vmem, o_vmem):
            pltpu.sync_copy(x_hbm.at[i_vmem.at[0]], o_vmem)

        pltpu.emit_pipeline(
            body,
            grid=(n // SC_WINDOW,),
            in_specs=[pl.BlockSpec((1, SC_WINDOW), index_map=lambda i: (0, i))],
            out_specs=[pl.BlockSpec((SC_WINDOW, d), index_map=lambda i: (i, 0))],
            core_axis_name=("core", "subcore"),
            dimension_semantics=(pltpu.PARALLEL,),
        )(i_hbm, o_hbm)

    return gather(x, idx.reshape(1, n))


def _moe_ffn_packed_kernel(te_ref, act_ref, x_ref, wg_ref, wu_ref, wd_ref, o_ref, xn_ref, acc_ref):
    i = pl.program_id(0)
    f = pl.program_id(1)
    hw = xn_ref.shape[1] // 2

    @pl.when(act_ref[i] == 1)
    def _():
        @pl.when(f == 0)
        def _():
            hi, lo = _unpack_pairs(_load_planes(x_ref))
            xn_ref[:, :hw] = hi.astype(BF16)
            xn_ref[:, hw:] = lo.astype(BF16)
            acc_ref[...] = jnp.zeros_like(acc_ref)

        x = xn_ref[...]
        mid = (_silu(_dot(x, wg_ref[0])) * _dot(x, wu_ref[0])).astype(BF16)
        acc_ref[...] += _dot(mid, wd_ref[0])

        @pl.when(f == pl.num_programs(1) - 1)
        def _():
            _store_planes(o_ref, _pack_pairs(acc_ref[...]))


def _moe_ffn_packed(tile_e, tile_act, xs, wg, wu, wd, *, tf):
    _, p, pw = xs.shape
    d = 2 * pw * SC_PLANES
    ff = wg.shape[2]
    tm = MOE_ROW_TILE
    return pl.pallas_call(
        _moe_ffn_packed_kernel,
        grid_spec=pltpu.PrefetchScalarGridSpec(
            num_scalar_prefetch=2,
            grid=(p // tm, ff // tf),
            in_specs=[pl.BlockSpec((SC_PLANES, tm, pw), lambda i, f, te, ta: (0, i, 0)),
                      pl.BlockSpec((1, d, tf), lambda i, f, te, ta: (te[i], 0, f * ta[i])),
                      pl.BlockSpec((1, d, tf), lambda i, f, te, ta: (te[i], 0, f * ta[i])),
                      pl.BlockSpec((1, tf, d), lambda i, f, te, ta: (te[i], f * ta[i], 0))],
            out_specs=pl.BlockSpec((SC_PLANES, tm, pw), lambda i, f, te, ta: (0, i, 0)),
            scratch_shapes=[pltpu.VMEM((tm, d), BF16), pltpu.VMEM((tm, d), F32)],
        ),
        out_shape=jax.ShapeDtypeStruct((SC_PLANES, p, pw), jnp.int32),
        compiler_params=_params("arbitrary", "arbitrary"),
        name="moe_ffn",
    )(tile_e, tile_act, xs, wg, wu, wd)


def _moe_sum_kernel(x_ref, info_ref, y_ref, gf_ref, o_ref, *, final_norm):
    hw = x_ref.shape[1] // 2
    info = info_ref[...]
    g1 = info[:, 2:3]
    g2 = info[:, 3:4]
    hi1, lo1 = _unpack_pairs(jnp.concatenate([y_ref[h] for h in range(SC_PLANES)], axis=1))
    hi2, lo2 = _unpack_pairs(jnp.concatenate([y_ref[SC_PLANES + h] for h in range(SC_PLANES)], axis=1))
    x = x_ref[...]
    out_a = x[:, :hw] + g1 * hi1 + g2 * hi2
    out_b = x[:, hw:] + g1 * lo1 + g2 * lo2
    if final_norm:
        ss = jnp.sum(out_a * out_a, axis=-1, keepdims=True) + jnp.sum(out_b * out_b, axis=-1, keepdims=True)
        r = lax.rsqrt(ss * (1.0 / (2 * hw)) + NORM_EPS)
        gf = gf_ref[...]
        out_a = out_a * r * gf[:, :hw]
        out_b = out_b * r * gf[:, hw:]
    o_ref[:, :hw] = out_a
    o_ref[:, hw:] = out_b


def _moe_sum(x, info, y_pairs, gf, *, final_norm):
    t, d = x.shape
    tm = ROW_TILE
    return pl.pallas_call(
        functools.partial(_moe_sum_kernel, final_norm=final_norm),
        grid=(t // tm,),
        in_specs=[pl.BlockSpec((tm, d), lambda i: (i, 0)),
                  pl.BlockSpec((tm, LANES), lambda i: (i, 0)),
                  pl.BlockSpec((2 * SC_PLANES, tm, y_pairs.shape[2]), lambda i: (0, i, 0)),
                  pl.BlockSpec((1, d), lambda i: (0, 0))],
        out_specs=pl.BlockSpec((tm, d), lambda i: (i, 0)),
        out_shape=jax.ShapeDtypeStruct((t, d), F32),
        compiler_params=_params("parallel"),
        name="moe_sum",
    )(x, info, y_pairs, gf)


def _moe_positions(info, t):
    tm = MOE_ROW_TILE
    n_e = N_EXPERTS
    a = 2 * t
    p = a + n_e * tm
    e_flat = info[:, 0:2].astype(jnp.int32).reshape(a)
    onehot = (e_flat[:, None] == jnp.arange(n_e, dtype=jnp.int32)[None, :]).astype(jnp.int32)
    csum = jnp.cumsum(onehot, axis=0)
    rank = jnp.sum(csum * onehot, axis=1) - 1
    counts = csum[-1]
    padded = ((counts + tm - 1) // tm) * tm
    ends = jnp.cumsum(padded)
    starts = ends - padded
    pos = jnp.sum(onehot * starts[None, :], axis=1) + rank
    tile_start = jnp.arange(p // tm, dtype=jnp.int32) * tm
    tile_e = jnp.minimum(jnp.sum((tile_start[:, None] >= ends[None, :]).astype(jnp.int32), axis=1), n_e - 1)
    tile_act = (tile_start < ends[-1]).astype(jnp.int32)
    return pos.astype(jnp.int32).reshape(t, 2), tile_e.astype(jnp.int32), tile_act, p


def _ffn_kernel(x_ref, g_ref, wg_ref, wu_ref, wd_ref, gf_ref, o_ref, xn_ref, acc_ref, *, final_norm):
    f = pl.program_id(1)

    @pl.when(f == 0)
    def _():
        x = x_ref[...]
        var = jnp.mean(x * x, axis=-1, keepdims=True)
        xn_ref[...] = (x * lax.rsqrt(var + NORM_EPS) * g_ref[...]).astype(BF16)
        acc_ref[...] = jnp.zeros_like(acc_ref)

    xn = xn_ref[...]
    mid = (_silu(_dot(xn, wg_ref[...])) * _dot(xn, wu_ref[...])).astype(BF16)
    acc_ref[...] += _dot(mid, wd_ref[...])

    @pl.when(f == pl.num_programs(1) - 1)
    def _():
        out = x_ref[...] + acc_ref[...]
        if final_norm:
            var = jnp.mean(out * out, axis=-1, keepdims=True)
            out = out * lax.rsqrt(var + NORM_EPS) * gf_ref[...]
        o_ref[...] = out


def _ffn(x, g, wg, wu, wd, gf, *, tf, final_norm):
    t, d = x.shape
    ff = wg.shape[1]
    tm = ROW_TILE
    return pl.pallas_call(
        functools.partial(_ffn_kernel, final_norm=final_norm),
        grid=(t // tm, ff // tf),
        in_specs=[
            pl.BlockSpec((tm, d), lambda i, f: (i, 0)),
            pl.BlockSpec((1, d), lambda i, f: (0, 0)),
            pl.BlockSpec((d, tf), lambda i, f: (0, f)),
            pl.BlockSpec((d, tf), lambda i, f: (0, f)),
            pl.BlockSpec((tf, d), lambda i, f: (f, 0)),
            pl.BlockSpec((1, d), lambda i, f: (0, 0)),
        ],
        out_specs=pl.BlockSpec((tm, d), lambda i, f: (i, 0)),
        out_shape=jax.ShapeDtypeStruct((t, d), F32),
        scratch_shapes=[pltpu.VMEM((tm, d), BF16), pltpu.VMEM((tm, d), F32)],
        compiler_params=_params("parallel", "arbitrary"),
        name="ffn",
    )(x, g, wg, wu, wd, gf)


def _rope_tables(seq):
    half = HEAD_DIM // 2
    inv = ROPE_THETA ** (-jnp.arange(half, dtype=F32) * 2.0 / HEAD_DIM)
    ang = jnp.arange(seq, dtype=F32)[:, None] * inv[None, :]
    cos = jnp.cos(ang)
    sin = jnp.sin(ang)
    cos_t = jnp.concatenate([cos, cos, cos, cos], axis=-1)
    sin_t = jnp.concatenate([-sin, sin, -sin, sin], axis=-1)
    return cos_t, sin_t


def _dup_heads(w):
    d, n = w.shape
    w = w.reshape(d, n // HEAD_DIM, 1, HEAD_DIM)
    return jnp.broadcast_to(w, (d, n // HEAD_DIM, 2, HEAD_DIM)).reshape(d, 2 * n)


def kernel(x, ln_mix_e, w_in_e, b_fox_f, w_out_e, ln_ffn_e, w_ffn_gate, w_ffn_up, w_ffn_down,
           ln_mix_o, w_in_o, hgrn_lb_logits, hgrn_norm_g, diff_lambda, diff_norm_g, w_out_o,
           ln_ffn_o, w_router, w_exp_gate, w_exp_up, w_exp_down, ln_final):
    batch, seq, d = x.shape
    t = batch * seq
    depth = ln_mix_e.shape[0] + ln_mix_o.shape[0]
    xs = x.reshape(t, d)
    cos_t, sin_t = _rope_tables(seq)
    row = lambda v: v.reshape(1, -1)
    fw = FOX_HEADS * HEAD_DIM
    dw = DSA_HEADS * HEAD_DIM
    kvw = DSA_KV_HEADS * HEAD_DIM
    hw = HGRN_HEADS * HEAD_DIM
    qw = DIFF_HEADS * 2 * DIFF_DIM

    for layer in range(depth):
        j = layer // 2
        if layer % 2 == 0:
            w = w_in_e[j]
            o = 0
            fq, fk, fv = w[:, o:o + fw], w[:, o + fw:o + 2 * fw], w[:, o + 2 * fw:o + 3 * fw]
            o += 3 * fw
            ffw = w[:, o:o + FOX_HEADS]
            o += FOX_HEADS
            dq = w[:, o:o + dw]
            o += dw
            dk = w[:, o:o + kvw]
            o += kvw
            dv = w[:, o:o + kvw]
            o += kvw
            iqw = w[:, o:o + IDX_HEADS * IDX_DIM]
            o += IDX_HEADS * IDX_DIM
            ikw = w[:, o:o + IDX_DIM]
            o += IDX_DIM
            iww = w[:, o:o + IDX_HEADS]
            w1 = jnp.concatenate([fq, fk, fv, _dup_heads(dv)], axis=1).astype(BF16)
            w2 = jnp.concatenate([dq, _dup_heads(dk), iqw, _dup_heads(ikw)], axis=1).astype(BF16)
            w3 = jnp.concatenate([ffw, iww, jnp.zeros((d, LANES - FOX_HEADS - IDX_HEADS), F32)], axis=1).astype(BF16)
            g = row(ln_mix_e[j])
            p1 = _rms_proj(xs, g, w1, cos_t, sin_t, seq=seq, rope=False, out_dtype=BF16)
            p2 = _rms_proj(xs, g, w2, cos_t, sin_t, seq=seq, rope=True, out_dtype=BF16)
            z = _rms_proj(xs, g, w3, cos_t, sin_t, seq=seq, rope=False, out_dtype=F32)
            b_pad = jnp.concatenate([b_fox_f[j], jnp.zeros((LANES - FOX_HEADS,), F32)]).reshape(1, LANES)
            ccol, crow = _fox_gate(z, b_pad, batch=batch, seq=seq)
            fox_o = _fox_attention(p1, ccol, crow, batch=batch, seq=seq)
            dsa_o = _dsa_attention(p2, p1, z, batch=batch, seq=seq)
            wo = w_out_e[j].astype(BF16)
            xs = _out_proj(fox_o, dsa_o, wo[:fw], wo[fw:], xs)
            xs = _ffn(xs, row(ln_ffn_e[j]), w_ffn_gate[j].astype(BF16), w_ffn_up[j].astype(BF16),
                      w_ffn_down[j].astype(BF16), row(ln_final), tf=1408, final_norm=(layer == depth - 1))
        else:
            w = w_in_o[j]
            lam_init = 0.8 - 0.6 * math.exp(-0.3 * layer)
            g = row(ln_mix_o[j])
            hp = _rms_proj(xs, g, w[:, :4 * hw].astype(BF16), cos_t, sin_t, seq=seq, rope=False, out_dtype=F32)
            pqk = _rms_proj(xs, g, w[:, 4 * hw:4 * hw + 2 * qw].astype(BF16), cos_t, sin_t, seq=seq, rope=True,
                            out_dtype=BF16)
            pv = _rms_proj(xs, g, w[:, 4 * hw + 2 * qw:].astype(BF16), cos_t, sin_t, seq=seq, rope=False,
                           out_dtype=BF16)
            gn = jnp.concatenate([hgrn_norm_g[j], hgrn_norm_g[j]]).reshape(1, LANES)
            o_h = _hgrn(hp, hgrn_lb_logits, gn, batch=batch, seq=seq, layer=layer)
            o_d = _diff_attention(pqk, pv, diff_lambda[j], row(diff_norm_g[j]), batch=batch, seq=seq,
                                  lam_init=lam_init)
            wo = w_out_o[j].astype(BF16)
            xs = _out_proj(o_h, o_d, wo[:hw], wo[hw:], xs)
            wr = jnp.concatenate([w_router[j], jnp.zeros((d, LANES - N_EXPERTS), F32)], axis=1)
            info, xn = _router(xs, row(ln_ffn_o[j]), wr)
            pos, tile_e, tile_act, n_rows = _moe_positions(info, t)
            pw = xn.shape[2]
            plane = jnp.arange(SC_PLANES, dtype=jnp.int32)[:, None] * n_rows
            dst0 = (plane + pos[None, :, 0]).reshape(SC_PLANES * t)
            dst1 = (plane + pos[None, :, 1]).reshape(SC_PLANES * t)
            rows = _sc_scatter_rows(xn.reshape(SC_PLANES * t, pw), dst0, dst1, SC_PLANES * n_rows)
            y = _moe_ffn_packed(tile_e, tile_act, rows.reshape(SC_PLANES, n_rows, pw), w_exp_gate[j].astype(BF16),
                                w_exp_up[j].astype(BF16), w_exp_down[j].astype(BF16), tf=896)
            src = jnp.concatenate([(plane + pos[None, :, s]).reshape(SC_PLANES * t) for s in range(2)])
            y_pairs = _sc_gather_rows(y.reshape(SC_PLANES * n_rows, pw), src).reshape(2 * SC_PLANES, t, pw)
            xs = _moe_sum(xs, info, y_pairs, row(ln_final), final_norm=(layer == depth - 1))
    if depth % 2 == 1:
        pass
    return xs.reshape(batch, seq, d)
```

```python
import functools
import math

import jax
import jax.numpy as jnp
from jax import lax
from jax.experimental import pallas as pl
from jax.experimental.pallas import tpu as pltpu
from jax.experimental.pallas import tpu_sc as plsc

F32 = jnp.float32
BF16 = jnp.bfloat16

D_MODEL = 1024
HEAD_DIM = 64
FOX_HEADS = 8
DSA_HEADS = 8
DSA_KV_HEADS = 2
IDX_HEADS = 4
IDX_DIM = 64
DSA_TOPK = 256
HGRN_HEADS = 8
DIFF_HEADS = 4
DIFF_DIM = 64
ROPE_THETA = 10000.0
N_EXPERTS = 8
NORM_EPS = 1e-6
NEG_INF = -1e30

LANES = 128
SUBLANES = 8
VMEM_LIMIT_BYTES = 56 * 1024 * 1024

ROW_TILE = 512
ATTN_TILE = 256
HGRN_CHUNK = 256
SUB = 8
MOE_ROW_TILE = 512
MOE_SUB = 128
MOE_TOKEN_BLOCK = 512
SC_WINDOW = 128
SC_PLANES = 2

INT_MIN = -(2 ** 31)


def _params(*sem):
    return pltpu.CompilerParams(dimension_semantics=sem, vmem_limit_bytes=VMEM_LIMIT_BYTES)


def _dot(a, b):
    return jnp.dot(a, b, preferred_element_type=F32)


def _dot_nt(a, b):
    return lax.dot_general(a, b, (((1,), (1,)), ((), ())), preferred_element_type=F32)


def _dot_tn(a, b):
    return lax.dot_general(a, b, (((0,), (0,)), ((), ())), preferred_element_type=F32)


def _sigmoid(x):
    return 1.0 / (1.0 + jnp.exp(-x))


def _silu(x):
    return x * _sigmoid(x)


def _lane_half(shape):
    return lax.broadcasted_iota(jnp.int32, shape, len(shape) - 1) // HEAD_DIM


def _rms_proj_kernel(x_ref, g_ref, w_ref, cos_ref, sin_ref, o_ref, *, rope):
    x = x_ref[...]
    var = jnp.mean(x * x, axis=-1, keepdims=True)
    xn = (x * lax.rsqrt(var + NORM_EPS) * g_ref[...]).astype(BF16)
    y = _dot(xn, w_ref[...])
    if not rope:
        o_ref[...] = y.astype(o_ref.dtype)
        return
    cos = cos_ref[...]
    sin = sin_ref[...]
    first = (lax.broadcasted_iota(jnp.int32, cos.shape, 1) % HEAD_DIM) < (HEAD_DIM // 2)
    for c in range(y.shape[1] // LANES):
        yc = y[:, c * LANES:(c + 1) * LANES]
        partner = jnp.where(first, pltpu.roll(yc, LANES - HEAD_DIM // 2, 1), pltpu.roll(yc, HEAD_DIM // 2, 1))
        o_ref[:, c * LANES:(c + 1) * LANES] = (yc * cos + partner * sin).astype(o_ref.dtype)


def _rms_proj(x, g, w, cos, sin, *, seq, rope, out_dtype):
    t, d = x.shape
    n = w.shape[1]
    tm = ROW_TILE
    nseq = seq // tm
    return pl.pallas_call(
        functools.partial(_rms_proj_kernel, rope=rope),
        grid=(t // tm,),
        in_specs=[
            pl.BlockSpec((tm, d), lambda i: (i, 0)),
            pl.BlockSpec((1, d), lambda i: (0, 0)),
            pl.BlockSpec((d, n), lambda i: (0, 0)),
            pl.BlockSpec((tm, LANES), lambda i: (i % nseq, 0)),
            pl.BlockSpec((tm, LANES), lambda i: (i % nseq, 0)),
        ],
        out_specs=pl.BlockSpec((tm, n), lambda i: (i, 0)),
        out_shape=jax.ShapeDtypeStruct((t, n), out_dtype),
        compiler_params=_params("parallel"),
        name="rms_proj_rope" if rope else "rms_proj",
    )(x, g, w, cos, sin)


def _out_proj_kernel(a1_ref, a2_ref, w1_ref, w2_ref, r_ref, o_ref):
    o_ref[...] = r_ref[...] + _dot(a1_ref[...], w1_ref[...]) + _dot(a2_ref[...], w2_ref[...])


def _out_proj(a1, a2, w1, w2, res):
    t, d = res.shape
    k1, k2 = a1.shape[1], a2.shape[1]
    tm = ROW_TILE
    return pl.pallas_call(
        _out_proj_kernel,
        grid=(t // tm,),
        in_specs=[
            pl.BlockSpec((tm, k1), lambda i: (i, 0)),
            pl.BlockSpec((tm, k2), lambda i: (i, 0)),
            pl.BlockSpec((k1, d), lambda i: (0, 0)),
            pl.BlockSpec((k2, d), lambda i: (0, 0)),
            pl.BlockSpec((tm, d), lambda i: (i, 0)),
        ],
        out_specs=pl.BlockSpec((tm, d), lambda i: (i, 0)),
        out_shape=jax.ShapeDtypeStruct((t, d), F32),
        compiler_params=_params("parallel"),
        name="out_proj",
    )(a1, a2, w1, w2, res)


def _cumsum_rows(x):
    rows = x.shape[0]
    row = lax.broadcasted_iota(jnp.int32, x.shape, 0)
    k = 1
    while k < rows:
        x = x + jnp.where(row >= k, pltpu.roll(x, k, 0), 0.0)
        k *= 2
    return x


def _fox_gate_kernel(z_ref, b_ref, ccol_ref, crow_ref, *, tk):
    z = z_ref[...] + b_ref[...]
    logf = jnp.minimum(z, 0.0) - jnp.log(1.0 + jnp.exp(-jnp.abs(z)))
    c = _cumsum_rows(logf)
    ccol_ref[...] = c
    for j in range(c.shape[0] // tk):
        ct = c[j * tk:(j + 1) * tk, :].T
        crow_ref[0, :, j * tk:(j + 1) * tk] = ct[:SUBLANES, :]


def _fox_gate(z, b_pad, *, batch, seq):
    tk = ATTN_TILE
    return pl.pallas_call(
        functools.partial(_fox_gate_kernel, tk=tk),
        grid=(batch,),
        in_specs=[
            pl.BlockSpec((seq, LANES), lambda b: (b, 0)),
            pl.BlockSpec((1, LANES), lambda b: (0, 0)),
        ],
        out_specs=[
            pl.BlockSpec((seq, LANES), lambda b: (b, 0)),
            pl.BlockSpec((1, SUBLANES, seq), lambda b: (b, 0, 0)),
        ],
        out_shape=[
            jax.ShapeDtypeStruct((batch * seq, LANES), F32),
            jax.ShapeDtypeStruct((batch, SUBLANES, seq), F32),
        ],
        compiler_params=_params("parallel"),
        name="fox_gate",
    )(z, b_pad)


def _softmax_pv(s, v):
    m = jnp.max(s, axis=-1, keepdims=True)
    p = jnp.exp(s - m)
    l = jnp.sum(p, axis=-1, keepdims=True)
    return _dot(p.astype(BF16), v) / l


def _mask_last_block(s, tq):
    n = s.shape[1]
    causal = lax.broadcasted_iota(jnp.int32, (tq, tq), 1) <= lax.broadcasted_iota(jnp.int32, (tq, tq), 0)
    tail = jnp.where(causal, s[:, n - tq:], NEG_INF)
    return tail if n == tq else jnp.concatenate([s[:, :n - tq], tail], axis=1)


def _fox_kernel(q_ref, k_ref, v_ref, cc_ref, cr_ref, o_ref, *, tq, scale):
    hp = pl.program_id(1)
    seq = q_ref.shape[0]
    half = _lane_half((1, LANES))
    lane = lax.broadcasted_iota(jnp.int32, (tq, LANES), 1)
    for i in range(seq // tq):
        n = (i + 1) * tq
        q = q_ref[i * tq:n, :] * scale
        cc = cc_ref[i * tq:n, :]
        k = k_ref[0:n, :]
        v = v_ref[0:n, :]
        outs = []
        for e in range(2):
            h = 2 * hp + e
            qe = jnp.where(half == e, q, jnp.zeros_like(q))
            cq = jnp.sum(jnp.where(lane == h, cc, 0.0), axis=-1, keepdims=True)
            ck = cr_ref[0, pl.ds(h, 1), 0:n]
            s = _mask_last_block(_dot_nt(qe, k) + cq - ck, tq)
            outs.append(_softmax_pv(s, v))
        o_ref[i * tq:n, :] = jnp.where(half == 0, outs[0], outs[1]).astype(o_ref.dtype)


def _fox_attention(p1, ccol, crow, *, batch, seq):
    tq = min(ATTN_TILE, seq)
    ng = FOX_HEADS // 2
    return pl.pallas_call(
        functools.partial(_fox_kernel, tq=tq, scale=HEAD_DIM ** -0.5),
        grid=(batch, ng),
        in_specs=[
            pl.BlockSpec((seq, LANES), lambda b, g: (b, g)),
            pl.BlockSpec((seq, LANES), lambda b, g: (b, ng + g)),
            pl.BlockSpec((seq, LANES), lambda b, g: (b, 2 * ng + g)),
            pl.BlockSpec((seq, LANES), lambda b, g: (b, 0)),
            pl.BlockSpec((1, SUBLANES, seq), lambda b, g: (b, 0, 0)),
        ],
        out_specs=pl.BlockSpec((seq, LANES), lambda b, g: (b, g)),
        out_shape=jax.ShapeDtypeStruct((batch * seq, ng * LANES), BF16),
        compiler_params=_params("parallel", "parallel"),
        name="fox_attention",
    )(p1, p1, p1, ccol, crow)


def _diff_kernel(q_ref, k_ref, v_ref, lam_ref, g_ref, o_ref, *, tq, scale, lam_init):
    seq = q_ref.shape[0]
    half = _lane_half((1, LANES))
    lv = lam_ref[...]
    lam = (jnp.exp(jnp.sum(lv[0:1, :] * lv[1:2, :], axis=-1, keepdims=True))
           - jnp.exp(jnp.sum(lv[2:3, :] * lv[3:4, :], axis=-1, keepdims=True)) + lam_init)
    gn = g_ref[...] * (1.0 - lam_init)
    for i in range(seq // tq):
        n = (i + 1) * tq
        q = q_ref[i * tq:n, :] * scale
        k = k_ref[0:n, :]
        v = v_ref[0:n, :]
        outs = []
        for e in range(2):
            qe = jnp.where(half == e, q, jnp.zeros_like(q))
            outs.append(_softmax_pv(_mask_last_block(_dot_nt(qe, k), tq), v))
        o = outs[0] - lam * outs[1]
        var = jnp.mean(o * o, axis=-1, keepdims=True)
        o_ref[i * tq:n, :] = (o * lax.rsqrt(var + NORM_EPS) * gn).astype(o_ref.dtype)


def _diff_attention(pqk, pv, lam_vecs, g, *, batch, seq, lam_init):
    tq = min(ATTN_TILE, seq)
    ng = DIFF_HEADS
    return pl.pallas_call(
        functools.partial(_diff_kernel, tq=tq, scale=DIFF_DIM ** -0.5, lam_init=lam_init),
        grid=(batch, ng),
        in_specs=[
            pl.BlockSpec((seq, LANES), lambda b, g: (b, g)),
            pl.BlockSpec((seq, LANES), lambda b, g: (b, ng + g)),
            pl.BlockSpec((seq, LANES), lambda b, g: (b, g)),
            pl.BlockSpec(lam_vecs.shape, lambda b, g: (0, 0)),
            pl.BlockSpec((1, LANES), lambda b, g: (0, 0)),
        ],
        out_specs=pl.BlockSpec((seq, LANES), lambda b, g: (b, g)),
        out_shape=jax.ShapeDtypeStruct((batch * seq, ng * LANES), BF16),
        compiler_params=_params("parallel", "parallel"),
        name="diff_attention",
    )(pqk, pqk, pv, lam_vecs, g)


def _sortable_key(x):
    bits = pltpu.bitcast(x, jnp.int32)
    return jnp.where(bits < 0, bits ^ jnp.int32(0x7FFFFFFF), bits)


def _dsa_kernel(q_ref, kd_ref, iq_ref, ik_ref, vd_ref, z_ref, o_ref, key_ref, madd_ref, plane_ref,
                *, tq, n_sel, scale, idx_scale, w_scale):
    i = pl.program_id(1)
    seq = key_ref.shape[1]
    nq = seq // tq
    span = 2 if nq % 2 == 0 else 1
    half = _lane_half((1, LANES))
    row = lax.broadcasted_iota(jnp.int32, (tq, tq), 0)
    col = lax.broadcasted_iota(jnp.int32, (tq, tq), 1)
    row_g = row + i * tq
    tri = jnp.where(row < col, 1.0, 0.0).astype(BF16)

    def block(n):
        z = z_ref[...]
        iq = iq_ref[...]
        ik = ik_ref[0:n, :]
        sc = jnp.zeros((tq, n), F32)
        for h in range(IDX_HEADS):
            grp = iq[:, (h // 2) * LANES:(h // 2 + 1) * LANES]
            iq_h = jnp.where(half == h % 2, grp, jnp.zeros_like(grp))
            w_h = z[:, SUBLANES + h:SUBLANES + h + 1] * w_scale
            sc = sc + w_h * jnp.maximum(_dot_nt(iq_h, ik) * idx_scale, 0.0)
        for j in range(n // tq):
            scj = jnp.where(col + j * tq <= row_g, sc[:, j * tq:(j + 1) * tq], -jnp.inf)
            kj = _sortable_key(scj)
            key_ref[:, j * tq:(j + 1) * tq] = kj
            for b in range(4):
                byte = (lax.shift_right_arithmetic(kj, 24) + 128) if b == 3 else (
                    lax.shift_right_arithmetic(kj, 8 * b) & 255)
                plane_ref[b, :, j * tq:(j + 1) * tq] = byte.astype(F32).astype(BF16)

        one = jnp.ones((tq, LANES), BF16)
        zero = jnp.zeros((tq, LANES), BF16)

        def count(b, pred):
            acc = zero
            for c in range(n // LANES):
                acc = acc + jnp.where(pred(plane_ref[b, :, c * LANES:(c + 1) * LANES]), one, zero)
            return jnp.sum(acc.astype(F32), axis=-1, keepdims=True)

        above = jnp.zeros((tq, 1), F32)
        thr = jnp.zeros((tq, 1), jnp.int32)
        for b in (3, 2, 1, 0):
            def bit_body(it, tb, b=b, above=above):
                cand = tb + lax.shift_left(jnp.int32(1), 7 - it).astype(F32)
                cand_b = jnp.broadcast_to(cand, (tq, LANES)).astype(BF16)
                return jnp.where(above + count(b, lambda m: m >= cand_b) >= n_sel, cand, tb)

            tb = lax.fori_loop(0, 8, bit_body, jnp.zeros((tq, 1), F32))
            tb_b = jnp.broadcast_to(tb, (tq, LANES)).astype(BF16)
            above = above + count(b, lambda m: m > tb_b)
            if b > 0:
                for c in range(n // LANES):
                    cs = slice(c * LANES, (c + 1) * LANES)
                    plane_ref[b - 1, :, cs] = jnp.where(plane_ref[b, :, cs] == tb_b, plane_ref[b - 1, :, cs], -one)
            tb_i = tb.astype(jnp.int32) - (128 if b == 3 else 0)
            thr = thr | lax.shift_left(tb_i, 8 * b)
        need = n_sel - above

        off = jnp.zeros((tq, 1), F32)
        for j in range(n // tq):
            kj = key_ref[:, j * tq:(j + 1) * tq]
            eq = kj == thr
            eqf = jnp.where(eq, 1.0, 0.0)
            rank = _dot(eqf.astype(BF16), tri) + off
            take = jnp.where(kj > thr, 1.0, jnp.where(eq, jnp.where(rank < need, 1.0, 0.0), 0.0))
            valid = col + j * tq <= row_g
            madd_ref[:, j * tq:(j + 1) * tq] = jnp.where(valid, jnp.where(take > 0.0, 0.0, NEG_INF), NEG_INF)
            off = off + jnp.sum(eqf, axis=-1, keepdims=True)

        q = q_ref[...] * scale
        for p in range(DSA_HEADS // 2):
            qg = q[:, p * LANES:(p + 1) * LANES]
            g = p // (DSA_HEADS // DSA_KV_HEADS // 2)
            k = kd_ref[0:n, g * LANES:(g + 1) * LANES]
            v = vd_ref[0:n, g * LANES:(g + 1) * LANES]
            outs = []
            for e in range(2):
                qe = jnp.where(half == e, qg, jnp.zeros_like(qg))
                outs.append(_softmax_pv(_dot_nt(qe, k) + madd_ref[:, 0:n], v))
            o_ref[:, p * LANES:(p + 1) * LANES] = jnp.where(half == 0, outs[0], outs[1]).astype(o_ref.dtype)

    for c in range(nq // span):
        pl.when(i // span == c)(functools.partial(block, (c + 1) * span * tq))


def _dsa_attention(p2, p1, z, *, batch, seq):
    tq = min(ATTN_TILE, seq)
    nq = seq // tq
    n_sel = min(DSA_TOPK, seq // 4)
    qw = DSA_HEADS * HEAD_DIM
    return pl.pallas_call(
        functools.partial(_dsa_kernel, tq=tq, n_sel=float(n_sel), scale=HEAD_DIM ** -0.5,
                          idx_scale=IDX_DIM ** -0.5, w_scale=IDX_HEADS ** -0.5),
        grid=(batch, nq),
        in_specs=[
            pl.BlockSpec((tq, qw), lambda b, i: (b * nq + i, 0)),
            pl.BlockSpec((seq, 2 * LANES), lambda b, i: (b, 2)),
            pl.BlockSpec((tq, 2 * LANES), lambda b, i: (b * nq + i, 3)),
            pl.BlockSpec((seq, LANES), lambda b, i: (b, 8)),
            pl.BlockSpec((seq, 2 * LANES), lambda b, i: (b, 6)),
            pl.BlockSpec((tq, LANES), lambda b, i: (b * nq + i, 0)),
        ],
        out_specs=pl.BlockSpec((tq, qw), lambda b, i: (b * nq + i, 0)),
        out_shape=jax.ShapeDtypeStruct((batch * seq, qw), BF16),
        scratch_shapes=[pltpu.VMEM((tq, seq), jnp.int32), pltpu.VMEM((tq, seq), F32),
                        pltpu.VMEM((4, tq, seq), BF16)],
        compiler_params=_params("parallel", "arbitrary"),
        name="dsa_attention",
    )(p2, p2, p2, p2, p1, z)


def _hgrn_kernel(q_ref, f_ref, i_ref, g_ref, lbl_ref, gn_ref, o_ref, b_scr, *, chunk, layer):
    seq = q_ref.shape[0]
    half = _lane_half((1, LANES))
    logits = lbl_ref[...]
    pe = jnp.exp(logits - jnp.max(logits, axis=0, keepdims=True))
    prob = pe / jnp.sum(pe, axis=0, keepdims=True)
    csum = prob[0:1, :]
    for d in range(1, layer + 1):
        csum = csum + prob[d:d + 1, :]
    lb = csum - prob[0:1, :]

    rowi = lax.broadcasted_iota(jnp.int32, (chunk, 1), 0)
    xor_rc = (lax.broadcasted_iota(jnp.int32, (chunk, chunk), 0)
              ^ lax.broadcasted_iota(jnp.int32, (chunk, chunk), 1))
    same_head = (lax.broadcasted_iota(jnp.int32, (LANES, LANES), 0) // HEAD_DIM
                 == lax.broadcasted_iota(jnp.int32, (LANES, LANES), 1) // HEAD_DIM)
    ones_bd = jnp.where(same_head, 1.0, 0.0).astype(BF16)
    gn = gn_ref[...]

    def chunk_body(c, state_t):
        r0 = pl.multiple_of(c * chunk, chunk)
        q = _silu(q_ref[pl.ds(r0, chunk), :])
        f = lb + (1.0 - lb) * _sigmoid(f_ref[pl.ds(r0, chunk), :])
        kc = 1.0 - f
        v = i_ref[pl.ds(r0, chunk), :]
        b = _cumsum_rows(jnp.log(f))
        b_scr[...] = b
        b_last = b_scr[chunk - 1:chunk, :]

        o = _dot_nt((q * jnp.exp(b)).astype(BF16), state_t.astype(BF16))
        k2 = (kc * jnp.exp(b_last - b)).astype(BF16)
        upd = _dot_tn(v.astype(BF16), k2)
        new_state = state_t * jnp.exp(b_last) + jnp.where(same_head, upd, 0.0)

        s_tot = [jnp.zeros((chunk, chunk), F32), jnp.zeros((chunk, chunk), F32)]
        m = chunk
        while m > SUB:
            hm = m // 2
            pieces = [jnp.broadcast_to(b_scr[blk * m + hm - 1:blk * m + hm, :], (m, LANES))
                      for blk in range(chunk // m)]
            ref = pieces[0] if len(pieces) == 1 else jnp.concatenate(pieces, axis=0)
            right = (rowi % m) >= hm
            qm = jnp.where(right, q * jnp.exp(jnp.where(right, b - ref, 0.0)), 0.0).astype(BF16)
            km = jnp.where(right, 0.0, kc * jnp.exp(jnp.where(right, 0.0, ref - b))).astype(BF16)
            for e in range(2):
                s = _dot_nt(jnp.where(half == e, qm, jnp.zeros_like(qm)), km)
                s_tot[e] = s_tot[e] + (s if m == chunk else jnp.where(xor_rc < m, s, 0.0))
            m = hm
        vb = v.astype(BF16)
        for e in range(2):
            o = o + _dot(s_tot[e].astype(BF16), jnp.where(half == e, vb, jnp.zeros_like(vb)))

        for d in range(SUB):
            ok = (rowi % SUB) >= d
            bs = b if d == 0 else pltpu.roll(b, d, 0)
            ks = kc if d == 0 else pltpu.roll(kc, d, 0)
            vs = v if d == 0 else pltpu.roll(v, d, 0)
            x = jnp.where(ok, q * ks * jnp.exp(jnp.where(ok, b - bs, 0.0)), 0.0)
            o = o + _dot(x.astype(BF16), ones_bd) * vs

        sq = o * o
        s0 = jnp.sum(jnp.where(half == 0, sq, 0.0), axis=-1, keepdims=True)
        s1 = jnp.sum(jnp.where(half == 1, sq, 0.0), axis=-1, keepdims=True)
        var = jnp.where(half == 0, s0, s1) * (1.0 / HEAD_DIM)
        y = o * lax.rsqrt(var + NORM_EPS) * gn * _silu(g_ref[pl.ds(r0, chunk), :])
        o_ref[pl.ds(r0, chunk), :] = y.astype(o_ref.dtype)
        return new_state

    lax.fori_loop(0, seq // chunk, chunk_body, jnp.zeros((LANES, LANES), F32))


def _hgrn(hproj, lb_logits, gn, *, batch, seq, layer):
    ng = HGRN_HEADS // 2
    chunk = min(HGRN_CHUNK, seq)
    spec = lambda off: pl.BlockSpec((seq, LANES), lambda b, g: (b, off * ng + g))
    return pl.pallas_call(
        functools.partial(_hgrn_kernel, chunk=chunk, layer=layer),
        grid=(batch, ng),
        in_specs=[spec(0), spec(1), spec(2), spec(3),
                  pl.BlockSpec((lb_logits.shape[0], LANES), lambda b, g: (0, g)),
                  pl.BlockSpec((1, LANES), lambda b, g: (0, 0))],
        out_specs=pl.BlockSpec((seq, LANES), lambda b, g: (b, g)),
        out_shape=jax.ShapeDtypeStruct((batch * seq, ng * LANES), BF16),
        scratch_shapes=[pltpu.VMEM((chunk, LANES), F32)],
        compiler_params=_params("parallel", "parallel"),
        name="hgrn2",
    )(hproj, hproj, hproj, hproj, lb_logits, gn)


def _pack_pairs(x):
    hw = x.shape[1] // 2
    hi = pltpu.bitcast(x[:, :hw].astype(BF16).astype(F32), jnp.int32)
    lo = pltpu.bitcast(x[:, hw:].astype(BF16).astype(F32), jnp.int32)
    return hi | lax.shift_right_logical(lo, 16)


def _unpack_pairs(w):
    hi = pltpu.bitcast(w & jnp.int32(-65536), F32)
    lo = pltpu.bitcast(lax.shift_left(w, 16), F32)
    return hi, lo


def _store_planes(ref, words):
    pw = words.shape[1] // SC_PLANES
    for h in range(SC_PLANES):
        ref[h] = words[:, h * pw:(h + 1) * pw]


def _load_planes(ref):
    return jnp.concatenate([ref[h] for h in range(SC_PLANES)], axis=1)


def _router_kernel(x_ref, g_ref, w_ref, o_ref, xn_ref):
    x = x_ref[...]
    var = jnp.mean(x * x, axis=-1, keepdims=True)
    xn = x * lax.rsqrt(var + NORM_EPS) * g_ref[...]
    _store_planes(xn_ref, _pack_pairs(xn))
    logits = jnp.dot(xn, w_ref[...], precision=lax.Precision.HIGHEST, preferred_element_type=F32)
    lane = lax.broadcasted_iota(jnp.int32, logits.shape, 1)
    lg = jnp.where(lane < N_EXPERTS, logits, -jnp.inf)
    m1 = jnp.max(lg, axis=-1, keepdims=True)
    i1 = jnp.min(jnp.where(lg == m1, lane, LANES), axis=-1, keepdims=True)
    lg2 = jnp.where(lane == i1, -jnp.inf, lg)
    m2 = jnp.max(lg2, axis=-1, keepdims=True)
    i2 = jnp.min(jnp.where(lg2 == m2, lane, LANES), axis=-1, keepdims=True)
    t = jnp.exp(m2 - m1)
    den = 1.0 + t
    o_ref[...] = (jnp.where(lane == 0, i1.astype(F32), 0.0) + jnp.where(lane == 1, i2.astype(F32), 0.0)
                  + jnp.where(lane == 2, 1.0 / den, 0.0) + jnp.where(lane == 3, t / den, 0.0))


def _router(x, g, w_pad):
    t, d = x.shape
    tm = ROW_TILE
    return pl.pallas_call(
        _router_kernel,
        grid=(t // tm,),
        in_specs=[pl.BlockSpec((tm, d), lambda i: (i, 0)),
                  pl.BlockSpec((1, d), lambda i: (0, 0)),
                  pl.BlockSpec((d, LANES), lambda i: (0, 0))],
        out_specs=[pl.BlockSpec((tm, LANES), lambda i: (i, 0)),
                   pl.BlockSpec((SC_PLANES, tm, d // 2 // SC_PLANES), lambda i: (0, i, 0))],
        out_shape=[jax.ShapeDtypeStruct((t, LANES), F32),
                   jax.ShapeDtypeStruct((SC_PLANES, t, d // 2 // SC_PLANES), jnp.int32)],
        compiler_params=_params("parallel"),
        name="router",
    )(x, g, w_pad)


def _moe_gather_kernel(osub_ref, iblk_ref, first_ref, valid_ref, ids_ref, x_ref, o_ref, *, tw):
    w = pl.program_id(0)

    @pl.when(first_ref[w] == 1)
    def _():
        o_ref[...] = jnp.zeros_like(o_ref)

    @pl.when(valid_ref[w] == 1)
    def _():
        tok = ids_ref[0, 0:1, :]
        cand = lax.broadcasted_iota(jnp.int32, (tw, MOE_SUB), 0) + iblk_ref[w] * tw
        onehot_t = jnp.where(cand == tok, 1.0, 0.0).astype(BF16)
        o_ref[...] += _dot_tn(onehot_t, x_ref[...]).astype(o_ref.dtype)


def _moe_gather(plan, ids3, xn):
    t, d = xn.shape
    n_sub = ids3.shape[0]
    osub, iblk, first, valid = plan
    tw = MOE_TOKEN_BLOCK
    return pl.pallas_call(
        functools.partial(_moe_gather_kernel, tw=tw),
        grid_spec=pltpu.PrefetchScalarGridSpec(
            num_scalar_prefetch=4,
            grid=(osub.shape[0],),
            in_specs=[pl.BlockSpec((1, SUBLANES, MOE_SUB), lambda w, o, i, f, v: (o[w], 0, 0)),
                      pl.BlockSpec((tw, d), lambda w, o, i, f, v: (i[w], 0))],
            out_specs=pl.BlockSpec((MOE_SUB, d), lambda w, o, i, f, v: (o[w], 0)),
        ),
        out_shape=jax.ShapeDtypeStruct((n_sub * MOE_SUB, d), BF16),
        compiler_params=_params("arbitrary"),
        name="moe_gather",
    )(osub, iblk, first, valid, ids3, xn)


def _moe_combine_kernel(otile_ref, isub_ref, first_ref, last_ref, valid_ref, ids_ref, y_ref, x_ref, gf_ref,
                        o_ref, acc_ref, *, tt, final_norm):
    w = pl.program_id(0)

    @pl.when(first_ref[w] == 1)
    def _():
        acc_ref[...] = x_ref[...]

    @pl.when(valid_ref[w] == 1)
    def _():
        tok = ids_ref[0, 0:1, :]
        cand = lax.broadcasted_iota(jnp.int32, (tt, MOE_SUB), 0) + otile_ref[w] * tt
        onehot = jnp.where(cand == tok, 1.0, 0.0).astype(BF16)
        acc_ref[...] += _dot(onehot, y_ref[...])

    @pl.when(last_ref[w] == 1)
    def _():
        out = acc_ref[...]
        if final_norm:
            var = jnp.mean(out * out, axis=-1, keepdims=True)
            out = out * lax.rsqrt(var + NORM_EPS) * gf_ref[...]
        o_ref[...] = out


def _moe_combine(plan, ids3, y, x, gf, *, final_norm):
    t, d = x.shape
    otile, isub, first, last, valid = plan
    tt = ROW_TILE
    return pl.pallas_call(
        functools.partial(_moe_combine_kernel, tt=tt, final_norm=final_norm),
        grid_spec=pltpu.PrefetchScalarGridSpec(
            num_scalar_prefetch=5,
            grid=(otile.shape[0],),
            in_specs=[pl.BlockSpec((1, SUBLANES, MOE_SUB), lambda w, o, i, f, l, v: (i[w], 0, 0)),
                      pl.BlockSpec((MOE_SUB, d), lambda w, o, i, f, l, v: (i[w], 0)),
                      pl.BlockSpec((tt, d), lambda w, o, i, f, l, v: (o[w], 0)),
                      pl.BlockSpec((1, d), lambda w, o, i, f, l, v: (0, 0))],
            out_specs=pl.BlockSpec((tt, d), lambda w, o, i, f, l, v: (o[w], 0)),
            scratch_shapes=[pltpu.VMEM((tt, d), F32)],
        ),
        out_shape=jax.ShapeDtypeStruct((t, d), F32),
        compiler_params=_params("arbitrary"),
        name="moe_combine",
    )(otile, isub, first, last, valid, ids3, y, x, gf)


def _moe_ffn_kernel(te_ref, act_ref, x_ref, gate_ref, wg_ref, wu_ref, wd_ref, o_ref, acc_ref, *, tm):
    i = pl.program_id(0)
    f = pl.program_id(1)

    @pl.when(act_ref[i] == 1)
    def _():
        @pl.when(f == 0)
        def _():
            acc_ref[...] = jnp.zeros_like(acc_ref)

        x = x_ref[...]
        mid = (_silu(_dot(x, wg_ref[0])) * _dot(x, wu_ref[0])).astype(BF16)
        acc_ref[...] += _dot(mid, wd_ref[0])

        @pl.when(f == pl.num_programs(1) - 1)
        def _():
            gates = gate_ref[0]
            for k in range(tm // LANES):
                col = jnp.broadcast_to(gates[k:k + 1, :], (LANES, LANES)).T
                rows = acc_ref[k * LANES:(k + 1) * LANES, :]
                for c in range(rows.shape[1] // LANES):
                    o_ref[k * LANES:(k + 1) * LANES, c * LANES:(c + 1) * LANES] = (
                        rows[:, c * LANES:(c + 1) * LANES] * col).astype(o_ref.dtype)

    @pl.when(jnp.logical_and(act_ref[i] == 0, f == 0))
    def _():
        o_ref[...] = jnp.zeros_like(o_ref)


def _moe_ffn(tile_e, tile_act, xs, gates3, wg, wu, wd, *, tf):
    p, d = xs.shape
    ff = wg.shape[2]
    tm = MOE_ROW_TILE
    return pl.pallas_call(
        functools.partial(_moe_ffn_kernel, tm=tm),
        grid_spec=pltpu.PrefetchScalarGridSpec(
            num_scalar_prefetch=2,
            grid=(p // tm, ff // tf),
            in_specs=[pl.BlockSpec((tm, d), lambda i, f, te, ta: (i, 0)),
                      pl.BlockSpec((1, tm // LANES, LANES), lambda i, f, te, ta: (i, 0, 0)),
                      pl.BlockSpec((1, d, tf), lambda i, f, te, ta: (te[i], 0, f * ta[i])),
                      pl.BlockSpec((1, d, tf), lambda i, f, te, ta: (te[i], 0, f * ta[i])),
                      pl.BlockSpec((1, tf, d), lambda i, f, te, ta: (te[i], f * ta[i], 0))],
            out_specs=pl.BlockSpec((tm, d), lambda i, f, te, ta: (i, 0)),
            scratch_shapes=[pltpu.VMEM((tm, d), F32)],
        ),
        out_shape=jax.ShapeDtypeStruct((p, d), BF16),
        compiler_params=_params("arbitrary", "arbitrary"),
        name="moe_ffn",
    )(tile_e, tile_act, xs, gates3, wg, wu, wd)


def _moe_plan(info, t):
    tm, sub, tw, tt = MOE_ROW_TILE, MOE_SUB, MOE_TOKEN_BLOCK, ROW_TILE
    n_e = N_EXPERTS
    a = 2 * t
    p = a + n_e * tm
    n_sub = p // sub
    e_flat = info[:, 0:2].astype(jnp.int32).reshape(a)
    g_flat = info[:, 2:4].reshape(a)
    onehot = (e_flat[:, None] == jnp.arange(n_e, dtype=jnp.int32)[None, :]).astype(jnp.int32)
    csum = jnp.cumsum(onehot, axis=0)
    rank = jnp.sum(csum * onehot, axis=1) - 1
    counts = csum[-1]
    padded = ((counts + tm - 1) // tm) * tm
    ends = jnp.cumsum(padded)
    starts = ends - padded
    pos = starts[e_flat] + rank
    tok_row = jnp.full((p,), -1, jnp.int32).at[pos].set(jnp.arange(a, dtype=jnp.int32) // 2)
    gate_row = jnp.zeros((p,), F32).at[pos].set(g_flat)
    ids3 = jnp.broadcast_to(tok_row.reshape(n_sub, 1, sub), (n_sub, SUBLANES, sub))
    gates3 = gate_row.reshape(p // tm, tm // LANES, LANES)

    tile_start = jnp.arange(p // tm, dtype=jnp.int32) * tm
    tile_e = jnp.minimum(jnp.searchsorted(ends, tile_start, side='right'), n_e - 1).astype(jnp.int32)
    tile_act = (tile_start < ends[-1]).astype(jnp.int32)

    tok2 = tok_row.reshape(n_sub, sub)
    has = jnp.any(tok2 >= 0, axis=1)
    tmin = jnp.min(jnp.where(tok2 >= 0, tok2, t), axis=1)
    tmax = jnp.max(tok2, axis=1)

    def worklist(lo, hi, cnt, length):
        cend = jnp.cumsum(cnt)
        total = cend[-1]
        w = jnp.arange(length, dtype=jnp.int32)
        s = jnp.minimum(jnp.searchsorted(cend, w, side='right'), n_sub - 1).astype(jnp.int32)
        k = w - (cend[s] - cnt[s])
        valid = w < total
        blk = jnp.where(valid, lo[s] + k, 0)
        return s, blk.astype(jnp.int32), k, valid

    lo = jnp.where(has, tmin // tw, 0)
    hi = jnp.where(has, tmax // tw, 0)
    cnt = (hi - lo + 1).astype(jnp.int32)
    len_g = n_sub + n_e * (t // tw - 1)
    s, blk, k, valid = worklist(lo, hi, cnt, len_g)
    last_valid = jnp.sum(valid.astype(jnp.int32)) - 1
    s = jnp.where(valid, s, s[last_valid])
    blk = jnp.where(valid, blk, blk[last_valid])
    gather_plan = (s, blk, jnp.logical_and(valid, k == 0).astype(jnp.int32), valid.astype(jnp.int32))

    lo = jnp.where(has, tmin // tt, 0)
    hi = jnp.where(has, tmax // tt, -1)
    cnt = (hi - lo + 1).astype(jnp.int32)
    len_c = n_sub + n_e * (t // tt - 1)
    s, tile, k, valid = worklist(lo, hi, cnt, len_c)
    n_tt = t // tt
    order = jnp.argsort(jnp.where(valid, tile, n_tt), stable=True)
    s, tile, valid = s[order], tile[order], valid[order]
    tile = jnp.where(valid, tile, n_tt - 1)
    last_valid = jnp.sum(valid.astype(jnp.int32)) - 1
    s = jnp.where(valid, s, s[last_valid])
    prev = jnp.concatenate([jnp.full((1,), -1, jnp.int32), tile[:-1]])
    nxt = jnp.concatenate([tile[1:], jnp.full((1,), -1, jnp.int32)])
    combine_plan = (tile, s, (tile != prev).astype(jnp.int32), (tile != nxt).astype(jnp.int32),
                    valid.astype(jnp.int32))
    return ids3, gates3, tile_e, tile_act, gather_plan, combine_plan


def _sc_mesh():
    return plsc.VectorSubcoreMesh(core_axis_name="core", subcore_axis_name="subcore")


def _sc_scatter_rows(x, idx0, idx1, n_out):
    n, d = x.shape

    @functools.partial(pl.kernel, out_type=jax.ShapeDtypeStruct((n_out, d), x.dtype), mesh=_sc_mesh(),
                       scratch_types=[])
    def scatter(x_hbm, i0_hbm, i1_hbm, o_hbm):
        def body(x_vmem, i0_vmem, i1_vmem):
            pltpu.sync_copy(x_vmem, o_hbm.at[i0_vmem.at[0]])
            pltpu.sync_copy(x_vmem, o_hbm.at[i1_vmem.at[0]])

        pltpu.emit_pipeline(
            body,
            grid=(n // SC_WINDOW,),
            in_specs=[pl.BlockSpec((SC_WINDOW, d), index_map=lambda i: (i, 0)),
                      pl.BlockSpec((1, SC_WINDOW), index_map=lambda i: (0, i)),
                      pl.BlockSpec((1, SC_WINDOW), index_map=lambda i: (0, i))],
            out_specs=[],
            core_axis_name=("core", "subcore"),
            dimension_semantics=(pltpu.PARALLEL,),
        )(x_hbm, i0_hbm, i1_hbm)

    return scatter(x, idx0.reshape(1, n), idx1.reshape(1, n))


def _sc_gather_rows(x, idx):
    n = idx.shape[0]
    d = x.shape[1]

    @functools.partial(pl.kernel, out_type=jax.ShapeDtypeStruct((n, d), x.dtype), mesh=_sc_mesh(),
                       scratch_types=[])
    def gather(x_hbm, i_hbm, o_hbm):
        def body(i_vmem, o_vmem):
            pltpu.sync_copy(x_hbm.at[i_vmem.at[0]], o_vmem)

        pltpu.emit_pipeline(
            body,
            grid=(n // SC_WINDOW,),
            in_specs=[pl.BlockSpec((1, SC_WINDOW), index_map=lambda i: (0, i))],
            out_specs=[pl.BlockSpec((SC_WINDOW, d), index_map=lambda i: (i, 0))],
            core_axis_name=("core", "subcore"),
            dimension_semantics=(pltpu.PARALLEL,),
        )(i_hbm, o_hbm)

    return gather(x, idx.reshape(1, n))


def _moe_ffn_packed_kernel(te_ref, act_ref, x_ref, wg_ref, wu_ref, wd_ref, o_ref, xn_ref, acc_ref):
    i = pl.program_id(0)
    f = pl.program_id(1)
    hw = xn_ref.shape[1] // 2

    @pl.when(act_ref[i] == 1)
    def _():
        @pl.when(f == 0)
        def _():
            hi, lo = _unpack_pairs(_load_planes(x_ref))
            xn_ref[:, :hw] = hi.astype(BF16)
            xn_ref[:, hw:] = lo.astype(BF16)
            acc_ref[...] = jnp.zeros_like(acc_ref)

        x = xn_ref[...]
        mid = (_silu(_dot(x, wg_ref[0])) * _dot(x, wu_ref[0])).astype(BF16)
        acc_ref[...] += _dot(mid, wd_ref[0])

        @pl.when(f == pl.num_programs(1) - 1)
        def _():
            _store_planes(o_ref, _pack_pairs(acc_ref[...]))


def _moe_ffn_packed(tile_e, tile_act, xs, wg, wu, wd, *, tf):
    _, p, pw = xs.shape
    d = 2 * pw * SC_PLANES
    ff = wg.shape[2]
    tm = MOE_ROW_TILE
    return pl.pallas_call(
        _moe_ffn_packed_kernel,
        grid_spec=pltpu.PrefetchScalarGridSpec(
            num_scalar_prefetch=2,
            grid=(p // tm, ff // tf),
            in_specs=[pl.BlockSpec((SC_PLANES, tm, pw), lambda i, f, te, ta: (0, i, 0)),
                      pl.BlockSpec((1, d, tf), lambda i, f, te, ta: (te[i], 0, f * ta[i])),
                      pl.BlockSpec((1, d, tf), lambda i, f, te, ta: (te[i], 0, f * ta[i])),
                      pl.BlockSpec((1, tf, d), lambda i, f, te, ta: (te[i], f * ta[i], 0))],
            out_specs=pl.BlockSpec((SC_PLANES, tm, pw), lambda i, f, te, ta: (0, i, 0)),
            scratch_shapes=[pltpu.VMEM((tm, d), BF16), pltpu.VMEM((tm, d), F32)],
        ),
        out_shape=jax.ShapeDtypeStruct((SC_PLANES, p, pw), jnp.int32),
        compiler_params=_params("arbitrary", "arbitrary"),
        name="moe_ffn",
    )(tile_e, tile_act, xs, wg, wu, wd)


def _moe_sum_kernel(x_ref, info_ref, y_ref, gf_ref, o_ref, *, final_norm):
    hw = x_ref.shape[1] // 2
    info = info_ref[...]
    g1 = info[:, 2:3]
    g2 = info[:, 3:4]
    hi1, lo1 = _unpack_pairs(jnp.concatenate([y_ref[h] for h in range(SC_PLANES)], axis=1))
    hi2, lo2 = _unpack_pairs(jnp.concatenate([y_ref[SC_PLANES + h] for h in range(SC_PLANES)], axis=1))
    x = x_ref[...]
    out_a = x[:, :hw] + g1 * hi1 + g2 * hi2
    out_b = x[:, hw:] + g1 * lo1 + g2 * lo2
    if final_norm:
        ss = jnp.sum(out_a * out_a, axis=-1, keepdims=True) + jnp.sum(out_b * out_b, axis=-1, keepdims=True)
        r = lax.rsqrt(ss * (1.0 / (2 * hw)) + NORM_EPS)
        gf = gf_ref[...]
        out_a = out_a * r * gf[:, :hw]
        out_b = out_b * r * gf[:, hw:]
    o_ref[:, :hw] = out_a
    o_ref[:, hw:] = out_b


def _moe_sum(x, info, y_pairs, gf, *, final_norm):
    t, d = x.shape
    tm = ROW_TILE
    return pl.pallas_call(
        functools.partial(_moe_sum_kernel, final_norm=final_norm),
        grid=(t // tm,),
        in_specs=[pl.BlockSpec((tm, d), lambda i: (i, 0)),
                  pl.BlockSpec((tm, LANES), lambda i: (i, 0)),
                  pl.BlockSpec((2 * SC_PLANES, tm, y_pairs.shape[2]), lambda i: (0, i, 0)),
                  pl.BlockSpec((1, d), lambda i: (0, 0))],
        out_specs=pl.BlockSpec((tm, d), lambda i: (i, 0)),
        out_shape=jax.ShapeDtypeStruct((t, d), F32),
        compiler_params=_params("parallel"),
        name="moe_sum",
    )(x, info, y_pairs, gf)


def _moe_positions(info, t):
    tm = MOE_ROW_TILE
    n_e = N_EXPERTS
    a = 2 * t
    p = a + n_e * tm
    e_flat = info[:, 0:2].astype(jnp.int32).reshape(a)
    onehot = (e_flat[:, None] == jnp.arange(n_e, dtype=jnp.int32)[None, :]).astype(jnp.int32)
    csum = jnp.cumsum(onehot, axis=0)
    rank = jnp.sum(csum * onehot, axis=1) - 1
    counts = csum[-1]
    padded = ((counts + tm - 1) // tm) * tm
    ends = jnp.cumsum(padded)
    starts = ends - padded
    pos = jnp.sum(onehot * starts[None, :], axis=1) + rank
    tile_start = jnp.arange(p // tm, dtype=jnp.int32) * tm
    tile_e = jnp.minimum(jnp.sum((tile_start[:, None] >= ends[None, :]).astype(jnp.int32), axis=1), n_e - 1)
    tile_act = (tile_start < ends[-1]).astype(jnp.int32)
    return pos.astype(jnp.int32).reshape(t, 2), tile_e.astype(jnp.int32), tile_act, p


def _ffn_kernel(x_ref, g_ref, wg_ref, wu_ref, wd_ref, gf_ref, o_ref, xn_ref, acc_ref, *, final_norm):
    f = pl.program_id(1)

    @pl.when(f == 0)
    def _():
        x = x_ref[...]
        var = jnp.mean(x * x, axis=-1, keepdims=True)
        xn_ref[...] = (x * lax.rsqrt(var + NORM_EPS) * g_ref[...]).astype(BF16)
        acc_ref[...] = jnp.zeros_like(acc_ref)

    xn = xn_ref[...]
    mid = (_silu(_dot(xn, wg_ref[...])) * _dot(xn, wu_ref[...])).astype(BF16)
    acc_ref[...] += _dot(mid, wd_ref[...])

    @pl.when(f == pl.num_programs(1) - 1)
    def _():
        out = x_ref[...] + acc_ref[...]
        if final_norm:
            var = jnp.mean(out * out, axis=-1, keepdims=True)
            out = out * lax.rsqrt(var + NORM_EPS) * gf_ref[...]
        o_ref[...] = out


def _ffn(x, g, wg, wu, wd, gf, *, tf, final_norm):
    t, d = x.shape
    ff = wg.shape[1]
    tm = ROW_TILE
    return pl.pallas_call(
        functools.partial(_ffn_kernel, final_norm=final_norm),
        grid=(t // tm, ff // tf),
        in_specs=[
            pl.BlockSpec((tm, d), lambda i, f: (i, 0)),
            pl.BlockSpec((1, d), lambda i, f: (0, 0)),
            pl.BlockSpec((d, tf), lambda i, f: (0, f)),
            pl.BlockSpec((d, tf), lambda i, f: (0, f)),
            pl.BlockSpec((tf, d), lambda i, f: (f, 0)),
            pl.BlockSpec((1, d), lambda i, f: (0, 0)),
        ],
        out_specs=pl.BlockSpec((tm, d), lambda i, f: (i, 0)),
        out_shape=jax.ShapeDtypeStruct((t, d), F32),
        scratch_shapes=[pltpu.VMEM((tm, d), BF16), pltpu.VMEM((tm, d), F32)],
        compiler_params=_params("parallel", "arbitrary"),
        name="ffn",
    )(x, g, wg, wu, wd, gf)


def _rope_tables(seq):
    half = HEAD_DIM // 2
    inv = ROPE_THETA ** (-jnp.arange(half, dtype=F32) * 2.0 / HEAD_DIM)
    ang = jnp.arange(seq, dtype=F32)[:, None] * inv[None, :]
    cos = jnp.cos(ang)
    sin = jnp.sin(ang)
    cos_t = jnp.concatenate([cos, cos, cos, cos], axis=-1)
    sin_t = jnp.concatenate([-sin, sin, -sin, sin], axis=-1)
    return cos_t, sin_t


def _dup_heads(w):
    d, n = w.shape
    w = w.reshape(d, n // HEAD_DIM, 1, HEAD_DIM)
    return jnp.broadcast_to(w, (d, n // HEAD_DIM, 2, HEAD_DIM)).reshape(d, 2 * n)


def kernel(x, ln_mix_e, w_in_e, b_fox_f, w_out_e, ln_ffn_e, w_ffn_gate, w_ffn_up, w_ffn_down,
           ln_mix_o, w_in_o, hgrn_lb_logits, hgrn_norm_g, diff_lambda, diff_norm_g, w_out_o,
           ln_ffn_o, w_router, w_exp_gate, w_exp_up, w_exp_down, ln_final):
    batch, seq, d = x.shape
    t = batch * seq
    depth = ln_mix_e.shape[0] + ln_mix_o.shape[0]
    xs = x.reshape(t, d)
    cos_t, sin_t = _rope_tables(seq)
    row = lambda v: v.reshape(1, -1)
    fw = FOX_HEADS * HEAD_DIM
    dw = DSA_HEADS * HEAD_DIM
    kvw = DSA_KV_HEADS * HEAD_DIM
    hw = HGRN_HEADS * HEAD_DIM
    qw = DIFF_HEADS * 2 * DIFF_DIM

    for layer in range(depth):
        j = layer // 2
        if layer % 2 == 0:
            w = w_in_e[j]
            o = 0
            fq, fk, fv = w[:, o:o + fw], w[:, o + fw:o + 2 * fw], w[:, o + 2 * fw:o + 3 * fw]
            o += 3 * fw
            ffw = w[:, o:o + FOX_HEADS]
            o += FOX_HEADS
            dq = w[:, o:o + dw]
            o += dw
            dk = w[:, o:o + kvw]
            o += kvw
            dv = w[:, o:o + kvw]
            o += kvw
            iqw = w[:, o:o + IDX_HEADS * IDX_DIM]
            o += IDX_HEADS * IDX_DIM
            ikw = w[:, o:o + IDX_DIM]
            o += IDX_DIM
            iww = w[:, o:o + IDX_HEADS]
            w1 = jnp.concatenate([fq, fk, fv, _dup_heads(dv)], axis=1).astype(BF16)
            w2 = jnp.concatenate([dq, _dup_heads(dk), iqw, _dup_heads(ikw)], axis=1).astype(BF16)
            w3 = jnp.concatenate([ffw, iww, jnp.zeros((d, LANES - FOX_HEADS - IDX_HEADS), F32)], axis=1).astype(BF16)
            g = row(ln_mix_e[j])
            p1 = _rms_proj(xs, g, w1, cos_t, sin_t, seq=seq, rope=False, out_dtype=BF16)
            p2 = _rms_proj(xs, g, w2, cos_t, sin_t, seq=seq, rope=True, out_dtype=BF16)
            z = _rms_proj(xs, g, w3, cos_t, sin_t, seq=seq, rope=False, out_dtype=F32)
            b_pad = jnp.concatenate([b_fox_f[j], jnp.zeros((LANES - FOX_HEADS,), F32)]).reshape(1, LANES)
            ccol, crow = _fox_gate(z, b_pad, batch=batch, seq=seq)
            fox_o = _fox_attention(p1, ccol, crow, batch=batch, seq=seq)
            dsa_o = _dsa_attention(p2, p1, z, batch=batch, seq=seq)
            wo = w_out_e[j].astype(BF16)
            xs = _out_proj(fox_o, dsa_o, wo[:fw], wo[fw:], xs)
            xs = _ffn(xs, row(ln_ffn_e[j]), w_ffn_gate[j].astype(BF16), w_ffn_up[j].astype(BF16),
                      w_ffn_down[j].astype(BF16), row(ln_final), tf=1408, final_norm=(layer == depth - 1))
        else:
            w = w_in_o[j]
            lam_init = 0.8 - 0.6 * math.exp(-0.3 * layer)
            g = row(ln_mix_o[j])
            hp = _rms_proj(xs, g, w[:, :4 * hw].astype(BF16), cos_t, sin_t, seq=seq, rope=False, out_dtype=F32)
            pqk = _rms_proj(xs, g, w[:, 4 * hw:4 * hw + 2 * qw].astype(BF16), cos_t, sin_t, seq=seq, rope=True,
                            out_dtype=BF16)
            pv = _rms_proj(xs, g, w[:, 4 * hw + 2 * qw:].astype(BF16), cos_t, sin_t, seq=seq, rope=False,
                           out_dtype=BF16)
            gn = jnp.concatenate([hgrn_norm_g[j], hgrn_norm_g[j]]).reshape(1, LANES)
            o_h = _hgrn(hp, hgrn_lb_logits, gn, batch=batch, seq=seq, layer=layer)
            o_d = _diff_attention(pqk, pv, diff_lambda[j], row(diff_norm_g[j]), batch=batch, seq=seq,
                                  lam_init=lam_init)
            wo = w_out_o[j].astype(BF16)
            xs = _out_proj(o_h, o_d, wo[:hw], wo[hw:], xs)
            wr = jnp.concatenate([w_router[j], jnp.zeros((d, LANES - N_EXPERTS), F32)], axis=1)
            info, xn = _router(xs, row(ln_ffn_o[j]), wr)
            pos, tile_e, tile_act, n_rows = _moe_positions(info, t)
            pw = xn.shape[2]
            plane = jnp.arange(SC_PLANES, dtype=jnp.int32)[:, None] * n_rows
            dst0 = (plane + pos[None, :, 0]).reshape(SC_PLANES * t)
            dst1 = (plane + pos[None, :, 1]).reshape(SC_PLANES * t)
            rows = _sc_scatter_rows(xn.reshape(SC_PLANES * t, pw), dst0, dst1, SC_PLANES * n_rows)
            y = _moe_ffn_packed(tile_e, tile_act, rows.reshape(SC_PLANES, n_rows, pw), w_exp_gate[j].astype(BF16),
                                w_exp_up[j].astype(BF16), w_exp_down[j].astype(BF16), tf=1792)
            src = jnp.concatenate([(plane + pos[None, :, s]).reshape(SC_PLANES * t) for s in range(2)])
            y_pairs = _sc_gather_rows(y.reshape(SC_PLANES * n_rows, pw), src).reshape(2 * SC_PLANES, t, pw)
            xs = _moe_sum(xs, info, y_pairs, row(ln_final), final_norm=(layer == depth - 1))
    if depth % 2 == 1:
        pass
    return xs.reshape(batch, seq, d)
```

```python
import functools
import math

import jax
import jax.numpy as jnp
from jax import lax
from jax.experimental import pallas as pl
from jax.experimental.pallas import tpu as pltpu
from jax.experimental.pallas import tpu_sc as plsc

F32 = jnp.float32
BF16 = jnp.bfloat16

D_MODEL = 1024
HEAD_DIM = 64
FOX_HEADS = 8
DSA_HEADS = 8
DSA_KV_HEADS = 2
IDX_HEADS = 4
IDX_DIM = 64
DSA_TOPK = 256
HGRN_HEADS = 8
DIFF_HEADS = 4
DIFF_DIM = 64
ROPE_THETA = 10000.0
N_EXPERTS = 8
NORM_EPS = 1e-6
NEG_INF = -1e30

LANES = 128
SUBLANES = 8
VMEM_LIMIT_BYTES = 56 * 1024 * 1024

ROW_TILE = 512
ATTN_TILE = 256
HGRN_CHUNK = 256
SUB = 8
MOE_ROW_TILE = 512
MOE_SUB = 128
MOE_TOKEN_BLOCK = 512
SC_WINDOW = 128
SC_PLANES = 2

INT_MIN = -(2 ** 31)


def _params(*sem):
    return pltpu.CompilerParams(dimension_semantics=sem, vmem_limit_bytes=VMEM_LIMIT_BYTES)


def _dot(a, b):
    return jnp.dot(a, b, preferred_element_type=F32)


def _dot_nt(a, b):
    return lax.dot_general(a, b, (((1,), (1,)), ((), ())), preferred_element_type=F32)


def _dot_tn(a, b):
    return lax.dot_general(a, b, (((0,), (0,)), ((), ())), preferred_element_type=F32)


def _sigmoid(x):
    return 1.0 / (1.0 + jnp.exp(-x))


def _silu(x):
    return x * _sigmoid(x)


def _lane_half(shape):
    return lax.broadcasted_iota(jnp.int32, shape, len(shape) - 1) // HEAD_DIM


def _rms_proj_kernel(x_ref, g_ref, cos_ref, sin_ref, *refs, ropes):
    n = len(ropes)
    w_refs, o_refs = refs[:n], refs[n:]
    x = x_ref[...]
    var = jnp.mean(x * x, axis=-1, keepdims=True)
    xn = (x * lax.rsqrt(var + NORM_EPS) * g_ref[...]).astype(BF16)
    cos = cos_ref[...]
    sin = sin_ref[...]
    first = (lax.broadcasted_iota(jnp.int32, cos.shape, 1) % HEAD_DIM) < (HEAD_DIM // 2)
    for w_ref, o_ref, rope in zip(w_refs, o_refs, ropes):
        y = _dot(xn, w_ref[...])
        if not rope:
            o_ref[...] = y.astype(o_ref.dtype)
            continue
        for c in range(y.shape[1] // LANES):
            yc = y[:, c * LANES:(c + 1) * LANES]
            partner = jnp.where(first, pltpu.roll(yc, LANES - HEAD_DIM // 2, 1), pltpu.roll(yc, HEAD_DIM // 2, 1))
            o_ref[:, c * LANES:(c + 1) * LANES] = (yc * cos + partner * sin).astype(o_ref.dtype)


def _rms_proj(x, g, cos, sin, ws, ropes, out_dtypes, *, seq):
    t, d = x.shape
    tm = ROW_TILE
    nseq = seq // tm
    return pl.pallas_call(
        functools.partial(_rms_proj_kernel, ropes=tuple(ropes)),
        grid=(t // tm,),
        in_specs=[
            pl.BlockSpec((tm, d), lambda i: (i, 0)),
            pl.BlockSpec((1, d), lambda i: (0, 0)),
            pl.BlockSpec((tm, LANES), lambda i: (i % nseq, 0)),
            pl.BlockSpec((tm, LANES), lambda i: (i % nseq, 0)),
        ] + [pl.BlockSpec((d, w.shape[1]), lambda i: (0, 0)) for w in ws],
        out_specs=[pl.BlockSpec((tm, w.shape[1]), lambda i: (i, 0)) for w in ws],
        out_shape=[jax.ShapeDtypeStruct((t, w.shape[1]), dt) for w, dt in zip(ws, out_dtypes)],
        compiler_params=_params("parallel"),
        name="rms_proj",
    )(x, g, cos, sin, *ws)


def _out_proj_kernel(a1_ref, a2_ref, w1_ref, w2_ref, r_ref, o_ref):
    o_ref[...] = r_ref[...] + _dot(a1_ref[...], w1_ref[...]) + _dot(a2_ref[...], w2_ref[...])


def _out_proj(a1, a2, w1, w2, res):
    t, d = res.shape
    k1, k2 = a1.shape[1], a2.shape[1]
    tm = ROW_TILE
    return pl.pallas_call(
        _out_proj_kernel,
        grid=(t // tm,),
        in_specs=[
            pl.BlockSpec((tm, k1), lambda i: (i, 0)),
            pl.BlockSpec((tm, k2), lambda i: (i, 0)),
            pl.BlockSpec((k1, d), lambda i: (0, 0)),
            pl.BlockSpec((k2, d), lambda i: (0, 0)),
            pl.BlockSpec((tm, d), lambda i: (i, 0)),
        ],
        out_specs=pl.BlockSpec((tm, d), lambda i: (i, 0)),
        out_shape=jax.ShapeDtypeStruct((t, d), F32),
        compiler_params=_params("parallel"),
        name="out_proj",
    )(a1, a2, w1, w2, res)


def _cumsum_rows(x):
    rows = x.shape[0]
    row = lax.broadcasted_iota(jnp.int32, x.shape, 0)
    k = 1
    while k < rows:
        x = x + jnp.where(row >= k, pltpu.roll(x, k, 0), 0.0)
        k *= 2
    return x


def _fox_gate_kernel(z_ref, b_ref, ccol_ref, crow_ref, *, tk):
    z = z_ref[...] + b_ref[...]
    logf = jnp.minimum(z, 0.0) - jnp.log(1.0 + jnp.exp(-jnp.abs(z)))
    c = _cumsum_rows(logf)
    ccol_ref[...] = c
    for j in range(c.shape[0] // tk):
        ct = c[j * tk:(j + 1) * tk, :].T
        crow_ref[0, :, j * tk:(j + 1) * tk] = ct[:SUBLANES, :]


def _fox_gate(z, b_pad, *, batch, seq):
    tk = ATTN_TILE
    return pl.pallas_call(
        functools.partial(_fox_gate_kernel, tk=tk),
        grid=(batch,),
        in_specs=[
            pl.BlockSpec((seq, LANES), lambda b: (b, 0)),
            pl.BlockSpec((1, LANES), lambda b: (0, 0)),
        ],
        out_specs=[
            pl.BlockSpec((seq, LANES), lambda b: (b, 0)),
            pl.BlockSpec((1, SUBLANES, seq), lambda b: (b, 0, 0)),
        ],
        out_shape=[
            jax.ShapeDtypeStruct((batch * seq, LANES), F32),
            jax.ShapeDtypeStruct((batch, SUBLANES, seq), F32),
        ],
        compiler_params=_params("parallel"),
        name="fox_gate",
    )(z, b_pad)


def _softmax_pv(s, v):
    m = jnp.max(s, axis=-1, keepdims=True)
    p = jnp.exp(s - m)
    l = jnp.sum(p, axis=-1, keepdims=True)
    return _dot(p.astype(BF16), v) / l


def _mask_last_block(s, tq):
    n = s.shape[1]
    causal = lax.broadcasted_iota(jnp.int32, (tq, tq), 1) <= lax.broadcasted_iota(jnp.int32, (tq, tq), 0)
    tail = jnp.where(causal, s[:, n - tq:], NEG_INF)
    return tail if n == tq else jnp.concatenate([s[:, :n - tq], tail], axis=1)


def _fox_kernel(q_ref, k_ref, v_ref, cc_ref, cr_ref, o_ref, *, tq, scale):
    hp = pl.program_id(1)
    seq = q_ref.shape[0]
    half = _lane_half((1, LANES))
    lane = lax.broadcasted_iota(jnp.int32, (tq, LANES), 1)
    for i in range(seq // tq):
        n = (i + 1) * tq
        q = q_ref[i * tq:n, :] * scale
        cc = cc_ref[i * tq:n, :]
        k = k_ref[0:n, :]
        v = v_ref[0:n, :]
        outs = []
        for e in range(2):
            h = 2 * hp + e
            qe = jnp.where(half == e, q, jnp.zeros_like(q))
            cq = jnp.sum(jnp.where(lane == h, cc, 0.0), axis=-1, keepdims=True)
            ck = cr_ref[0, pl.ds(h, 1), 0:n]
            s = _mask_last_block(_dot_nt(qe, k) + cq - ck, tq)
            outs.append(_softmax_pv(s, v))
        o_ref[i * tq:n, :] = jnp.where(half == 0, outs[0], outs[1]).astype(o_ref.dtype)


def _fox_attention(p1, ccol, crow, *, batch, seq):
    tq = min(ATTN_TILE, seq)
    ng = FOX_HEADS // 2
    return pl.pallas_call(
        functools.partial(_fox_kernel, tq=tq, scale=HEAD_DIM ** -0.5),
        grid=(batch, ng),
        in_specs=[
            pl.BlockSpec((seq, LANES), lambda b, g: (b, g)),
            pl.BlockSpec((seq, LANES), lambda b, g: (b, ng + g)),
            pl.BlockSpec((seq, LANES), lambda b, g: (b, 2 * ng + g)),
            pl.BlockSpec((seq, LANES), lambda b, g: (b, 0)),
            pl.BlockSpec((1, SUBLANES, seq), lambda b, g: (b, 0, 0)),
        ],
        out_specs=pl.BlockSpec((seq, LANES), lambda b, g: (b, g)),
        out_shape=jax.ShapeDtypeStruct((batch * seq, ng * LANES), BF16),
        compiler_params=_params("parallel", "parallel"),
        name="fox_attention",
    )(p1, p1, p1, ccol, crow)


def _diff_kernel(q_ref, k_ref, v_ref, lam_ref, g_ref, o_ref, *, tq, scale, lam_init):
    seq = q_ref.shape[0]
    half = _lane_half((1, LANES))
    lv = lam_ref[...]
    lam = (jnp.exp(jnp.sum(lv[0:1, :] * lv[1:2, :], axis=-1, keepdims=True))
           - jnp.exp(jnp.sum(lv[2:3, :] * lv[3:4, :], axis=-1, keepdims=True)) + lam_init)
    gn = g_ref[...] * (1.0 - lam_init)
    for i in range(seq // tq):
        n = (i + 1) * tq
        q = q_ref[i * tq:n, :] * scale
        k = k_ref[0:n, :]
        v = v_ref[0:n, :]
        outs = []
        for e in range(2):
            qe = jnp.where(half == e, q, jnp.zeros_like(q))
            outs.append(_softmax_pv(_mask_last_block(_dot_nt(qe, k), tq), v))
        o = outs[0] - lam * outs[1]
        var = jnp.mean(o * o, axis=-1, keepdims=True)
        o_ref[i * tq:n, :] = (o * lax.rsqrt(var + NORM_EPS) * gn).astype(o_ref.dtype)


def _diff_attention(pqk, pv, lam_vecs, g, *, batch, seq, lam_init):
    tq = min(ATTN_TILE, seq)
    ng = DIFF_HEADS
    return pl.pallas_call(
        functools.partial(_diff_kernel, tq=tq, scale=DIFF_DIM ** -0.5, lam_init=lam_init),
        grid=(batch, ng),
        in_specs=[
            pl.BlockSpec((seq, LANES), lambda b, g: (b, g)),
            pl.BlockSpec((seq, LANES), lambda b, g: (b, ng + g)),
            pl.BlockSpec((seq, LANES), lambda b, g: (b, g)),
            pl.BlockSpec(lam_vecs.shape, lambda b, g: (0, 0)),
            pl.BlockSpec((1, LANES), lambda b, g: (0, 0)),
        ],
        out_specs=pl.BlockSpec((seq, LANES), lambda b, g: (b, g)),
        out_shape=jax.ShapeDtypeStruct((batch * seq, ng * LANES), BF16),
        compiler_params=_params("parallel", "parallel"),
        name="diff_attention",
    )(pqk, pqk, pv, lam_vecs, g)


def _sortable_key(x):
    bits = pltpu.bitcast(x, jnp.int32)
    return jnp.where(bits < 0, bits ^ jnp.int32(0x7FFFFFFF), bits)


def _dsa_kernel(q_ref, kd_ref, iq_ref, ik_ref, vd_ref, z_ref, o_ref, key_ref, madd_ref,
                *, tq, n_sel, scale, idx_scale, w_scale):
    i = pl.program_id(1)
    seq = key_ref.shape[1]
    nq = seq // tq
    span = 2 if nq % 2 == 0 else 1
    half = _lane_half((1, LANES))
    row = lax.broadcasted_iota(jnp.int32, (tq, tq), 0)
    col = lax.broadcasted_iota(jnp.int32, (tq, tq), 1)
    row_g = row + i * tq
    tri = jnp.where(row < col, 1.0, 0.0).astype(BF16)

    def block(n):
        z = z_ref[...]
        iq = iq_ref[...]
        ik = ik_ref[0:n, :]
        sc = jnp.zeros((tq, n), F32)
        for h in range(IDX_HEADS):
            grp = iq[:, (h // 2) * LANES:(h // 2 + 1) * LANES]
            iq_h = jnp.where(half == h % 2, grp, jnp.zeros_like(grp))
            w_h = z[:, SUBLANES + h:SUBLANES + h + 1] * w_scale
            sc = sc + w_h * jnp.maximum(_dot_nt(iq_h, ik) * idx_scale, 0.0)
        for j in range(n // tq):
            scj = jnp.where(col + j * tq <= row_g, sc[:, j * tq:(j + 1) * tq], -jnp.inf)
            key_ref[:, j * tq:(j + 1) * tq] = _sortable_key(scj)

        def count(pred):
            return jnp.sum(jnp.where(pred, 1.0, 0.0), axis=-1, keepdims=True)

        thr0 = jnp.where(count(key_ref[:, 0:n] >= 0) >= n_sel, 0, INT_MIN).astype(jnp.int32)

        def bit_body(it, thr):
            cand = thr | jnp.left_shift(jnp.int32(1), 30 - it)
            return jnp.where(count(key_ref[:, 0:n] >= cand) >= n_sel, cand, thr)

        thr = lax.fori_loop(0, 31, bit_body, thr0)
        need = n_sel - count(key_ref[:, 0:n] > thr)

        off = jnp.zeros((tq, 1), F32)
        for j in range(n // tq):
            kj = key_ref[:, j * tq:(j + 1) * tq]
            eq = kj == thr
            eqf = jnp.where(eq, 1.0, 0.0)
            rank = _dot(eqf.astype(BF16), tri) + off
            take = jnp.where(kj > thr, 1.0, jnp.where(eq, jnp.where(rank < need, 1.0, 0.0), 0.0))
            valid = col + j * tq <= row_g
            madd_ref[:, j * tq:(j + 1) * tq] = jnp.where(valid, jnp.where(take > 0.0, 0.0, NEG_INF), NEG_INF)
            off = off + jnp.sum(eqf, axis=-1, keepdims=True)

        q = q_ref[...] * scale
        for p in range(DSA_HEADS // 2):
            qg = q[:, p * LANES:(p + 1) * LANES]
            g = p // (DSA_HEADS // DSA_KV_HEADS // 2)
            k = kd_ref[0:n, g * LANES:(g + 1) * LANES]
            v = vd_ref[0:n, g * LANES:(g + 1) * LANES]
            outs = []
            for e in range(2):
                qe = jnp.where(half == e, qg, jnp.zeros_like(qg))
                outs.append(_softmax_pv(_dot_nt(qe, k) + madd_ref[:, 0:n], v))
            o_ref[:, p * LANES:(p + 1) * LANES] = jnp.where(half == 0, outs[0], outs[1]).astype(o_ref.dtype)

    for c in range(nq // span):
        pl.when(i // span == c)(functools.partial(block, (c + 1) * span * tq))


def _dsa_attention(p2, p1, z, *, batch, seq):
    tq = min(ATTN_TILE, seq)
    nq = seq // tq
    n_sel = min(DSA_TOPK, seq // 4)
    qw = DSA_HEADS * HEAD_DIM
    return pl.pallas_call(
        functools.partial(_dsa_kernel, tq=tq, n_sel=float(n_sel), scale=HEAD_DIM ** -0.5,
                          idx_scale=IDX_DIM ** -0.5, w_scale=IDX_HEADS ** -0.5),
        grid=(batch, nq),
        in_specs=[
            pl.BlockSpec((tq, qw), lambda b, i: (b * nq + i, 0)),
            pl.BlockSpec((seq, 2 * LANES), lambda b, i: (b, 2)),
            pl.BlockSpec((tq, 2 * LANES), lambda b, i: (b * nq + i, 3)),
            pl.BlockSpec((seq, LANES), lambda b, i: (b, 8)),
            pl.BlockSpec((seq, 2 * LANES), lambda b, i: (b, 6)),
            pl.BlockSpec((tq, LANES), lambda b, i: (b * nq + i, 0)),
        ],
        out_specs=pl.BlockSpec((tq, qw), lambda b, i: (b * nq + i, 0)),
        out_shape=jax.ShapeDtypeStruct((batch * seq, qw), BF16),
        scratch_shapes=[pltpu.VMEM((tq, seq), jnp.int32), pltpu.VMEM((tq, seq), F32)],
        compiler_params=_params("parallel", "arbitrary"),
        name="dsa_attention",
    )(p2, p2, p2, p2, p1, z)


def _hgrn_kernel(q_ref, f_ref, i_ref, g_ref, lbl_ref, gn_ref, o_ref, b_scr, *, chunk, layer):
    seq = q_ref.shape[0]
    half = _lane_half((1, LANES))
    logits = lbl_ref[...]
    pe = jnp.exp(logits - jnp.max(logits, axis=0, keepdims=True))
    prob = pe / jnp.sum(pe, axis=0, keepdims=True)
    csum = prob[0:1, :]
    for d in range(1, layer + 1):
        csum = csum + prob[d:d + 1, :]
    lb = csum - prob[0:1, :]

    rowi = lax.broadcasted_iota(jnp.int32, (chunk, 1), 0)
    xor_rc = (lax.broadcasted_iota(jnp.int32, (chunk, chunk), 0)
              ^ lax.broadcasted_iota(jnp.int32, (chunk, chunk), 1))
    same_head = (lax.broadcasted_iota(jnp.int32, (LANES, LANES), 0) // HEAD_DIM
                 == lax.broadcasted_iota(jnp.int32, (LANES, LANES), 1) // HEAD_DIM)
    ones_bd = jnp.where(same_head, 1.0, 0.0).astype(BF16)
    gn = gn_ref[...]

    def chunk_body(c, state_t):
        r0 = pl.multiple_of(c * chunk, chunk)
        q = _silu(q_ref[pl.ds(r0, chunk), :].astype(F32))
        f = lb + (1.0 - lb) * _sigmoid(f_ref[pl.ds(r0, chunk), :])
        kc = 1.0 - f
        v = i_ref[pl.ds(r0, chunk), :].astype(F32)
        b = _cumsum_rows(jnp.log(f))
        b_scr[...] = b
        b_last = b_scr[chunk - 1:chunk, :]

        o = _dot_nt((q * jnp.exp(b)).astype(BF16), state_t.astype(BF16))
        k2 = (kc * jnp.exp(b_last - b)).astype(BF16)
        upd = _dot_tn(v.astype(BF16), k2)
        new_state = state_t * jnp.exp(b_last) + jnp.where(same_head, upd, 0.0)

        s_tot = [jnp.zeros((chunk, chunk), F32), jnp.zeros((chunk, chunk), F32)]
        m = chunk
        while m > SUB:
            hm = m // 2
            pieces = [jnp.broadcast_to(b_scr[blk * m + hm - 1:blk * m + hm, :], (m, LANES))
                      for blk in range(chunk // m)]
            ref = pieces[0] if len(pieces) == 1 else jnp.concatenate(pieces, axis=0)
            right = (rowi % m) >= hm
            qm = jnp.where(right, q * jnp.exp(jnp.where(right, b - ref, 0.0)), 0.0).astype(BF16)
            km = jnp.where(right, 0.0, kc * jnp.exp(jnp.where(right, 0.0, ref - b))).astype(BF16)
            for e in range(2):
                s = _dot_nt(jnp.where(half == e, qm, jnp.zeros_like(qm)), km)
                s_tot[e] = s_tot[e] + (s if m == chunk else jnp.where(xor_rc < m, s, 0.0))
            m = hm
        vb = v.astype(BF16)
        for e in range(2):
            o = o + _dot(s_tot[e].astype(BF16), jnp.where(half == e, vb, jnp.zeros_like(vb)))

        for d in range(SUB):
            ok = (rowi % SUB) >= d
            bs = b if d == 0 else pltpu.roll(b, d, 0)
            ks = kc if d == 0 else pltpu.roll(kc, d, 0)
            vs = v if d == 0 else pltpu.roll(v, d, 0)
            x = jnp.where(ok, q * ks * jnp.exp(jnp.where(ok, b - bs, 0.0)), 0.0)
            o = o + _dot(x.astype(BF16), ones_bd) * vs

        sq = o * o
        s0 = jnp.sum(jnp.where(half == 0, sq, 0.0), axis=-1, keepdims=True)
        s1 = jnp.sum(jnp.where(half == 1, sq, 0.0), axis=-1, keepdims=True)
        var = jnp.where(half == 0, s0, s1) * (1.0 / HEAD_DIM)
        y = o * lax.rsqrt(var + NORM_EPS) * gn * _silu(g_ref[pl.ds(r0, chunk), :].astype(F32))
        o_ref[pl.ds(r0, chunk), :] = y.astype(o_ref.dtype)
        return new_state

    lax.fori_loop(0, seq // chunk, chunk_body, jnp.zeros((LANES, LANES), F32))


def _hgrn(h_qig, h_f, lb_logits, gn, *, batch, seq, layer):
    ng = HGRN_HEADS // 2
    chunk = min(HGRN_CHUNK, seq)
    spec = lambda off: pl.BlockSpec((seq, LANES), lambda b, g: (b, off * ng + g))
    return pl.pallas_call(
        functools.partial(_hgrn_kernel, chunk=chunk, layer=layer),
        grid=(batch, ng),
        in_specs=[spec(0), spec(0), spec(1), spec(2),
                  pl.BlockSpec((lb_logits.shape[0], LANES), lambda b, g: (0, g)),
                  pl.BlockSpec((1, LANES), lambda b, g: (0, 0))],
        out_specs=pl.BlockSpec((seq, LANES), lambda b, g: (b, g)),
        out_shape=jax.ShapeDtypeStruct((batch * seq, ng * LANES), BF16),
        scratch_shapes=[pltpu.VMEM((chunk, LANES), F32)],
        compiler_params=_params("parallel", "parallel"),
        name="hgrn2",
    )(h_qig, h_f, h_qig, h_qig, lb_logits, gn)


def _pack_pairs(x):
    hw = x.shape[1] // 2
    hi = pltpu.bitcast(x[:, :hw].astype(BF16).astype(F32), jnp.int32)
    lo = pltpu.bitcast(x[:, hw:].astype(BF16).astype(F32), jnp.int32)
    return hi | lax.shift_right_logical(lo, 16)


def _unpack_pairs(w):
    hi = pltpu.bitcast(w & jnp.int32(-65536), F32)
    lo = pltpu.bitcast(lax.shift_left(w, 16), F32)
    return hi, lo


def _store_planes(ref, words):
    pw = words.shape[1] // SC_PLANES
    for h in range(SC_PLANES):
        ref[h] = words[:, h * pw:(h + 1) * pw]


def _load_planes(ref):
    return jnp.concatenate([ref[h] for h in range(SC_PLANES)], axis=1)


def _router_kernel(x_ref, g_ref, w_ref, o_ref, xn_ref):
    x = x_ref[...]
    var = jnp.mean(x * x, axis=-1, keepdims=True)
    xn = x * lax.rsqrt(var + NORM_EPS) * g_ref[...]
    _store_planes(xn_ref, _pack_pairs(xn))
    logits = jnp.dot(xn, w_ref[...], precision=lax.Precision.HIGHEST, preferred_element_type=F32)
    lane = lax.broadcasted_iota(jnp.int32, logits.shape, 1)
    lg = jnp.where(lane < N_EXPERTS, logits, -jnp.inf)
    m1 = jnp.max(lg, axis=-1, keepdims=True)
    i1 = jnp.min(jnp.where(lg == m1, lane, LANES), axis=-1, keepdims=True)
    lg2 = jnp.where(lane == i1, -jnp.inf, lg)
    m2 = jnp.max(lg2, axis=-1, keepdims=True)
    i2 = jnp.min(jnp.where(lg2 == m2, lane, LANES), axis=-1, keepdims=True)
    t = jnp.exp(m2 - m1)
    den = 1.0 + t
    o_ref[...] = (jnp.where(lane == 0, i1.astype(F32), 0.0) + jnp.where(lane == 1, i2.astype(F32), 0.0)
                  + jnp.where(lane == 2, 1.0 / den, 0.0) + jnp.where(lane == 3, t / den, 0.0))


def _router(x, g, w_pad):
    t, d = x.shape
    tm = ROW_TILE
    return pl.pallas_call(
        _router_kernel,
        grid=(t // tm,),
        in_specs=[pl.BlockSpec((tm, d), lambda i: (i, 0)),
                  pl.BlockSpec((1, d), lambda i: (0, 0)),
                  pl.BlockSpec((d, LANES), lambda i: (0, 0))],
        out_specs=[pl.BlockSpec((tm, LANES), lambda i: (i, 0)),
                   pl.BlockSpec((SC_PLANES, tm, d // 2 // SC_PLANES), lambda i: (0, i, 0))],
        out_shape=[jax.ShapeDtypeStruct((t, LANES), F32),
                   jax.ShapeDtypeStruct((SC_PLANES, t, d // 2 // SC_PLANES), jnp.int32)],
        compiler_params=_params("parallel"),
        name="router",
    )(x, g, w_pad)


def _moe_gather_kernel(osub_ref, iblk_ref, first_ref, valid_ref, ids_ref, x_ref, o_ref, *, tw):
    w = pl.program_id(0)

    @pl.when(first_ref[w] == 1)
    def _():
        o_ref[...] = jnp.zeros_like(o_ref)

    @pl.when(valid_ref[w] == 1)
    def _():
        tok = ids_ref[0, 0:1, :]
        cand = lax.broadcasted_iota(jnp.int32, (tw, MOE_SUB), 0) + iblk_ref[w] * tw
        onehot_t = jnp.where(cand == tok, 1.0, 0.0).astype(BF16)
        o_ref[...] += _dot_tn(onehot_t, x_ref[...]).astype(o_ref.dtype)


def _moe_gather(plan, ids3, xn):
    t, d = xn.shape
    n_sub = ids3.shape[0]
    osub, iblk, first, valid = plan
    tw = MOE_TOKEN_BLOCK
    return pl.pallas_call(
        functools.partial(_moe_gather_kernel, tw=tw),
        grid_spec=pltpu.PrefetchScalarGridSpec(
            num_scalar_prefetch=4,
            grid=(osub.shape[0],),
            in_specs=[pl.BlockSpec((1, SUBLANES, MOE_SUB), lambda w, o, i, f, v: (o[w], 0, 0)),
                      pl.BlockSpec((tw, d), lambda w, o, i, f, v: (i[w], 0))],
            out_specs=pl.BlockSpec((MOE_SUB, d), lambda w, o, i, f, v: (o[w], 0)),
        ),
        out_shape=jax.ShapeDtypeStruct((n_sub * MOE_SUB, d), BF16),
        compiler_params=_params("arbitrary"),
        name="moe_gather",
    )(osub, iblk, first, valid, ids3, xn)


def _moe_combine_kernel(otile_ref, isub_ref, first_ref, last_ref, valid_ref, ids_ref, y_ref, x_ref, gf_ref,
                        o_ref, acc_ref, *, tt, final_norm):
    w = pl.program_id(0)

    @pl.when(first_ref[w] == 1)
    def _():
        acc_ref[...] = x_ref[...]

    @pl.when(valid_ref[w] == 1)
    def _():
        tok = ids_ref[0, 0:1, :]
        cand = lax.broadcasted_iota(jnp.int32, (tt, MOE_SUB), 0) + otile_ref[w] * tt
        onehot = jnp.where(cand == tok, 1.0, 0.0).astype(BF16)
        acc_ref[...] += _dot(onehot, y_ref[...])

    @pl.when(last_ref[w] == 1)
    def _():
        out = acc_ref[...]
        if final_norm:
            var = jnp.mean(out * out, axis=-1, keepdims=True)
            out = out * lax.rsqrt(var + NORM_EPS) * gf_ref[...]
        o_ref[...] = out


def _moe_combine(plan, ids3, y, x, gf, *, final_norm):
    t, d = x.shape
    otile, isub, first, last, valid = plan
    tt = ROW_TILE
    return pl.pallas_call(
        functools.partial(_moe_combine_kernel, tt=tt, final_norm=final_norm),
        grid_spec=pltpu.PrefetchScalarGridSpec(
            num_scalar_prefetch=5,
            grid=(otile.shape[0],),
            in_specs=[pl.BlockSpec((1, SUBLANES, MOE_SUB), lambda w, o, i, f, l, v: (i[w], 0, 0)),
                      pl.BlockSpec((MOE_SUB, d), lambda w, o, i, f, l, v: (i[w], 0)),
                      pl.BlockSpec((tt, d), lambda w, o, i, f, l, v: (o[w], 0)),
                      pl.BlockSpec((1, d), lambda w, o, i, f, l, v: (0, 0))],
            out_specs=pl.BlockSpec((tt, d), lambda w, o, i, f, l, v: (o[w], 0)),
            scratch_shapes=[pltpu.VMEM((tt, d), F32)],
        ),
        out_shape=jax.ShapeDtypeStruct((t, d), F32),
        compiler_params=_params("arbitrary"),
        name="moe_combine",
    )(otile, isub, first, last, valid, ids3, y, x, gf)


def _moe_ffn_kernel(te_ref, act_ref, x_ref, gate_ref, wg_ref, wu_ref, wd_ref, o_ref, acc_ref, *, tm):
    i = pl.program_id(0)
    f = pl.program_id(1)

    @pl.when(act_ref[i] == 1)
    def _():
        @pl.when(f == 0)
        def _():
            acc_ref[...] = jnp.zeros_like(acc_ref)

        x = x_ref[...]
        mid = (_silu(_dot(x, wg_ref[0])) * _dot(x, wu_ref[0])).astype(BF16)
        acc_ref[...] += _dot(mid, wd_ref[0])

        @pl.when(f == pl.num_programs(1) - 1)
        def _():
            gates = gate_ref[0]
            for k in range(tm // LANES):
                col = jnp.broadcast_to(gates[k:k + 1, :], (LANES, LANES)).T
                rows = acc_ref[k * LANES:(k + 1) * LANES, :]
                for c in range(rows.shape[1] // LANES):
                    o_ref[k * LANES:(k + 1) * LANES, c * LANES:(c + 1) * LANES] = (
                        rows[:, c * LANES:(c + 1) * LANES] * col).astype(o_ref.dtype)

    @pl.when(jnp.logical_and(act_ref[i] == 0, f == 0))
    def _():
        o_ref[...] = jnp.zeros_like(o_ref)


def _moe_ffn(tile_e, tile_act, xs, gates3, wg, wu, wd, *, tf):
    p, d = xs.shape
    ff = wg.shape[2]
    tm = MOE_ROW_TILE
    return pl.pallas_call(
        functools.partial(_moe_ffn_kernel, tm=tm),
        grid_spec=pltpu.PrefetchScalarGridSpec(
            num_scalar_prefetch=2,
            grid=(p // tm, ff // tf),
            in_specs=[pl.BlockSpec((tm, d), lambda i, f, te, ta: (i, 0)),
                      pl.BlockSpec((1, tm // LANES, LANES), lambda i, f, te, ta: (i, 0, 0)),
                      pl.BlockSpec((1, d, tf), lambda i, f, te, ta: (te[i], 0, f * ta[i])),
                      pl.BlockSpec((1, d, tf), lambda i, f, te, ta: (te[i], 0, f * ta[i])),
                      pl.BlockSpec((1, tf, d), lambda i, f, te, ta: (te[i], f * ta[i], 0))],
            out_specs=pl.BlockSpec((tm, d), lambda i, f, te, ta: (i, 0)),
            scratch_shapes=[pltpu.VMEM((tm, d), F32)],
        ),
        out_shape=jax.ShapeDtypeStruct((p, d), BF16),
        compiler_params=_params("arbitrary", "arbitrary"),
        name="moe_ffn",
    )(tile_e, tile_act, xs, gates3, wg, wu, wd)


def _moe_plan(info, t):
    tm, sub, tw, tt = MOE_ROW_TILE, MOE_SUB, MOE_TOKEN_BLOCK, ROW_TILE
    n_e = N_EXPERTS
    a = 2 * t
    p = a + n_e * tm
    n_sub = p // sub
    e_flat = info[:, 0:2].astype(jnp.int32).reshape(a)
    g_flat = info[:, 2:4].reshape(a)
    onehot = (e_flat[:, None] == jnp.arange(n_e, dtype=jnp.int32)[None, :]).astype(jnp.int32)
    csum = jnp.cumsum(onehot, axis=0)
    rank = jnp.sum(csum * onehot, axis=1) - 1
    counts = csum[-1]
    padded = ((counts + tm - 1) // tm) * tm
    ends = jnp.cumsum(padded)
    starts = ends - padded
    pos = starts[e_flat] + rank
    tok_row = jnp.full((p,), -1, jnp.int32).at[pos].set(jnp.arange(a, dtype=jnp.int32) // 2)
    gate_row = jnp.zeros((p,), F32).at[pos].set(g_flat)
    ids3 = jnp.broadcast_to(tok_row.reshape(n_sub, 1, sub), (n_sub, SUBLANES, sub))
    gates3 = gate_row.reshape(p // tm, tm // LANES, LANES)

    tile_start = jnp.arange(p // tm, dtype=jnp.int32) * tm
    tile_e = jnp.minimum(jnp.searchsorted(ends, tile_start, side='right'), n_e - 1).astype(jnp.int32)
    tile_act = (tile_start < ends[-1]).astype(jnp.int32)

    tok2 = tok_row.reshape(n_sub, sub)
    has = jnp.any(tok2 >= 0, axis=1)
    tmin = jnp.min(jnp.where(tok2 >= 0, tok2, t), axis=1)
    tmax = jnp.max(tok2, axis=1)

    def worklist(lo, hi, cnt, length):
        cend = jnp.cumsum(cnt)
        total = cend[-1]
        w = jnp.arange(length, dtype=jnp.int32)
        s = jnp.minimum(jnp.searchsorted(cend, w, side='right'), n_sub - 1).astype(jnp.int32)
        k = w - (cend[s] - cnt[s])
        valid = w < total
        blk = jnp.where(valid, lo[s] + k, 0)
        return s, blk.astype(jnp.int32), k, valid

    lo = jnp.where(has, tmin // tw, 0)
    hi = jnp.where(has, tmax // tw, 0)
    cnt = (hi - lo + 1).astype(jnp.int32)
    len_g = n_sub + n_e * (t // tw - 1)
    s, blk, k, valid = worklist(lo, hi, cnt, len_g)
    last_valid = jnp.sum(valid.astype(jnp.int32)) - 1
    s = jnp.where(valid, s, s[last_valid])
    blk = jnp.where(valid, blk, blk[last_valid])
    gather_plan = (s, blk, jnp.logical_and(valid, k == 0).astype(jnp.int32), valid.astype(jnp.int32))

    lo = jnp.where(has, tmin // tt, 0)
    hi = jnp.where(has, tmax // tt, -1)
    cnt = (hi - lo + 1).astype(jnp.int32)
    len_c = n_sub + n_e * (t // tt - 1)
    s, tile, k, valid = worklist(lo, hi, cnt, len_c)
    n_tt = t // tt
    order = jnp.argsort(jnp.where(valid, tile, n_tt), stable=True)
    s, tile, valid = s[order], tile[order], valid[order]
    tile = jnp.where(valid, tile, n_tt - 1)
    last_valid = jnp.sum(valid.astype(jnp.int32)) - 1
    s = jnp.where(valid, s, s[last_valid])
    prev = jnp.concatenate([jnp.full((1,), -1, jnp.int32), tile[:-1]])
    nxt = jnp.concatenate([tile[1:], jnp.full((1,), -1, jnp.int32)])
    combine_plan = (tile, s, (tile != prev).astype(jnp.int32), (tile != nxt).astype(jnp.int32),
                    valid.astype(jnp.int32))
    return ids3, gates3, tile_e, tile_act, gather_plan, combine_plan


def _sc_mesh():
    return plsc.VectorSubcoreMesh(core_axis_name="core", subcore_axis_name="subcore")


def _sc_scatter_rows(x, idx0, idx1, n_out):
    n, d = x.shape

    @functools.partial(pl.kernel, out_type=jax.ShapeDtypeStruct((n_out, d), x.dtype), mesh=_sc_mesh(),
                       scratch_types=[])
    def scatter(x_hbm, i0_hbm, i1_hbm, o_hbm):
        def body(x_vmem, i0_vmem, i1_vmem):
            pltpu.sync_copy(x_vmem, o_hbm.at[i0_vmem.at[0]])
            pltpu.sync_copy(x_vmem, o_hbm.at[i1_vmem.at[0]])

        pltpu.emit_pipeline(
            body,
            grid=(n // SC_WINDOW,),
            in_specs=[pl.BlockSpec((SC_WINDOW, d), index_map=lambda i: (i, 0)),
                      pl.BlockSpec((1, SC_WINDOW), index_map=lambda i: (0, i)),
                      pl.BlockSpec((1, SC_WINDOW), index_map=lambda i: (0, i))],
            out_specs=[],
            core_axis_name=("core", "subcore"),
            dimension_semantics=(pltpu.PARALLEL,),
        )(x_hbm, i0_hbm, i1_hbm)

    return scatter(x, idx0.reshape(1, n), idx1.reshape(1, n))


def _sc_gather_rows(x, idx):
    n = idx.shape[0]
    d = x.shape[1]

    @functools.partial(pl.kernel, out_type=jax.ShapeDtypeStruct((n, d), x.dtype), mesh=_sc_mesh(),
                       scratch_types=[])
    def gather(x_hbm, i_hbm, o_hbm):
        def body(i_vmem, o_vmem):
            pltpu.sync_copy(x_hbm.at[i_vmem.at[0]], o_vmem)

        pltpu.emit_pipeline(
            body,
            grid=(n // SC_WINDOW,),
            in_specs=[pl.BlockSpec((1, SC_WINDOW), index_map=lambda i: (0, i))],
            out_specs=[pl.BlockSpec((SC_WINDOW, d), index_map=lambda i: (i, 0))],
            core_axis_name=("core", "subcore"),
            dimension_semantics=(pltpu.PARALLEL,),
        )(i_hbm, o_hbm)

    return gather(x, idx.reshape(1, n))


def _moe_ffn_packed_kernel(te_ref, act_ref, x_ref, wg_ref, wu_ref, wd_ref, o_ref, xn_ref, acc_ref):
    i = pl.program_id(0)
    f = pl.program_id(1)
    hw = xn_ref.shape[1] // 2

    @pl.when(act_ref[i] == 1)
    def _():
        @pl.when(f == 0)
        def _():
            hi, lo = _unpack_pairs(_load_planes(x_ref))
            xn_ref[:, :hw] = hi.astype(BF16)
            xn_ref[:, hw:] = lo.astype(BF16)
            acc_ref[...] = jnp.zeros_like(acc_ref)

        x = xn_ref[...]
        mid = (_silu(_dot(x, wg_ref[0])) * _dot(x, wu_ref[0])).astype(BF16)
        acc_ref[...] += _dot(mid, wd_ref[0])

        @pl.when(f == pl.num_programs(1) - 1)
        def _():
            _store_planes(o_ref, _pack_pairs(acc_ref[...]))


def _moe_ffn_packed(tile_e, tile_act, xs, wg, wu, wd, *, tf):
    _, p, pw = xs.shape
    d = 2 * pw * SC_PLANES
    ff = wg.shape[2]
    tm = MOE_ROW_TILE
    return pl.pallas_call(
        _moe_ffn_packed_kernel,
        grid_spec=pltpu.PrefetchScalarGridSpec(
            num_scalar_prefetch=2,
            grid=(p // tm, ff // tf),
            in_specs=[pl.BlockSpec((SC_PLANES, tm, pw), lambda i, f, te, ta: (0, i, 0)),
                      pl.BlockSpec((1, d, tf), lambda i, f, te, ta: (te[i], 0, f * ta[i])),
                      pl.BlockSpec((1, d, tf), lambda i, f, te, ta: (te[i], 0, f * ta[i])),
                      pl.BlockSpec((1, tf, d), lambda i, f, te, ta: (te[i], f * ta[i], 0))],
            out_specs=pl.BlockSpec((SC_PLANES, tm, pw), lambda i, f, te, ta: (0, i, 0)),
            scratch_shapes=[pltpu.VMEM((tm, d), BF16), pltpu.VMEM((tm, d), F32)],
        ),
        out_shape=jax.ShapeDtypeStruct((SC_PLANES, p, pw), jnp.int32),
        compiler_params=_params("arbitrary", "arbitrary"),
        name="moe_ffn",
    )(tile_e, tile_act, xs, wg, wu, wd)


def _moe_sum_kernel(x_ref, info_ref, y_ref, gf_ref, o_ref, *, final_norm):
    hw = x_ref.shape[1] // 2
    info = info_ref[...]
    g1 = info[:, 2:3]
    g2 = info[:, 3:4]
    hi1, lo1 = _unpack_pairs(jnp.concatenate([y_ref[h] for h in range(SC_PLANES)], axis=1))
    hi2, lo2 = _unpack_pairs(jnp.concatenate([y_ref[SC_PLANES + h] for h in range(SC_PLANES)], axis=1))
    x = x_ref[...]
    out_a = x[:, :hw] + g1 * hi1 + g2 * hi2
    out_b = x[:, hw:] + g1 * lo1 + g2 * lo2
    if final_norm:
        ss = jnp.sum(out_a * out_a, axis=-1, keepdims=True) + jnp.sum(out_b * out_b, axis=-1, keepdims=True)
        r = lax.rsqrt(ss * (1.0 / (2 * hw)) + NORM_EPS)
        gf = gf_ref[...]
        out_a = out_a * r * gf[:, :hw]
        out_b = out_b * r * gf[:, hw:]
    o_ref[:, :hw] = out_a
    o_ref[:, hw:] = out_b


def _moe_sum(x, info, y_pairs, gf, *, final_norm):
    t, d = x.shape
    tm = ROW_TILE
    return pl.pallas_call(
        functools.partial(_moe_sum_kernel, final_norm=final_norm),
        grid=(t // tm,),
        in_specs=[pl.BlockSpec((tm, d), lambda i: (i, 0)),
                  pl.BlockSpec((tm, LANES), lambda i: (i, 0)),
                  pl.BlockSpec((2 * SC_PLANES, tm, y_pairs.shape[2]), lambda i: (0, i, 0)),
                  pl.BlockSpec((1, d), lambda i: (0, 0))],
        out_specs=pl.BlockSpec((tm, d), lambda i: (i, 0)),
        out_shape=jax.ShapeDtypeStruct((t, d), F32),
        compiler_params=_params("parallel"),
        name="moe_sum",
    )(x, info, y_pairs, gf)


def _moe_positions(info, t):
    tm = MOE_ROW_TILE
    n_e = N_EXPERTS
    a = 2 * t
    p = a + n_e * tm
    e_flat = info[:, 0:2].astype(jnp.int32).reshape(a)
    onehot = (e_flat[:, None] == jnp.arange(n_e, dtype=jnp.int32)[None, :]).astype(jnp.int32)
    csum = jnp.cumsum(onehot, axis=0)
    rank = jnp.sum(csum * onehot, axis=1) - 1
    counts = csum[-1]
    padded = ((counts + tm - 1) // tm) * tm
    ends = jnp.cumsum(padded)
    starts = ends - padded
    pos = jnp.sum(onehot * starts[None, :], axis=1) + rank
    tile_start = jnp.arange(p // tm, dtype=jnp.int32) * tm
    tile_e = jnp.minimum(jnp.sum((tile_start[:, None] >= ends[None, :]).astype(jnp.int32), axis=1), n_e - 1)
    tile_act = (tile_start < ends[-1]).astype(jnp.int32)
    return pos.astype(jnp.int32).reshape(t, 2), tile_e.astype(jnp.int32), tile_act, p


def _ffn_kernel(x_ref, g_ref, wg_ref, wu_ref, wd_ref, gf_ref, o_ref, xn_ref, acc_ref, *, final_norm):
    f = pl.program_id(1)

    @pl.when(f == 0)
    def _():
        x = x_ref[...]
        var = jnp.mean(x * x, axis=-1, keepdims=True)
        xn_ref[...] = (x * lax.rsqrt(var + NORM_EPS) * g_ref[...]).astype(BF16)
        acc_ref[...] = jnp.zeros_like(acc_ref)

    xn = xn_ref[...]
    mid = (_silu(_dot(xn, wg_ref[...])) * _dot(xn, wu_ref[...])).astype(BF16)
    acc_ref[...] += _dot(mid, wd_ref[...])

    @pl.when(f == pl.num_programs(1) - 1)
    def _():
        out = x_ref[...] + acc_ref[...]
        if final_norm:
            var = jnp.mean(out * out, axis=-1, keepdims=True)
            out = out * lax.rsqrt(var + NORM_EPS) * gf_ref[...]
        o_ref[...] = out


def _ffn(x, g, wg, wu, wd, gf, *, tf, final_norm):
    t, d = x.shape
    ff = wg.shape[1]
    tm = ROW_TILE
    return pl.pallas_call(
        functools.partial(_ffn_kernel, final_norm=final_norm),
        grid=(t // tm, ff // tf),
        in_specs=[
            pl.BlockSpec((tm, d), lambda i, f: (i, 0)),
            pl.BlockSpec((1, d), lambda i, f: (0, 0)),
            pl.BlockSpec((d, tf), lambda i, f: (0, f)),
            pl.BlockSpec((d, tf), lambda i, f: (0, f)),
            pl.BlockSpec((tf, d), lambda i, f: (f, 0)),
            pl.BlockSpec((1, d), lambda i, f: (0, 0)),
        ],
        out_specs=pl.BlockSpec((tm, d), lambda i, f: (i, 0)),
        out_shape=jax.ShapeDtypeStruct((t, d), F32),
        scratch_shapes=[pltpu.VMEM((tm, d), BF16), pltpu.VMEM((tm, d), F32)],
        compiler_params=_params("parallel", "arbitrary"),
        name="ffn",
    )(x, g, wg, wu, wd, gf)


def _rope_tables(seq):
    half = HEAD_DIM // 2
    inv = ROPE_THETA ** (-jnp.arange(half, dtype=F32) * 2.0 / HEAD_DIM)
    ang = jnp.arange(seq, dtype=F32)[:, None] * inv[None, :]
    cos = jnp.cos(ang)
    sin = jnp.sin(ang)
    cos_t = jnp.concatenate([cos, cos, cos, cos], axis=-1)
    sin_t = jnp.concatenate([-sin, sin, -sin, sin], axis=-1)
    return cos_t, sin_t


def _dup_heads(w):
    d, n = w.shape
    w = w.reshape(d, n // HEAD_DIM, 1, HEAD_DIM)
    return jnp.broadcast_to(w, (d, n // HEAD_DIM, 2, HEAD_DIM)).reshape(d, 2 * n)


def kernel(x, ln_mix_e, w_in_e, b_fox_f, w_out_e, ln_ffn_e, w_ffn_gate, w_ffn_up, w_ffn_down,
           ln_mix_o, w_in_o, hgrn_lb_logits, hgrn_norm_g, diff_lambda, diff_norm_g, w_out_o,
           ln_ffn_o, w_router, w_exp_gate, w_exp_up, w_exp_down, ln_final):
    batch, seq, d = x.shape
    t = batch * seq
    depth = ln_mix_e.shape[0] + ln_mix_o.shape[0]
    xs = x.reshape(t, d)
    cos_t, sin_t = _rope_tables(seq)
    row = lambda v: v.reshape(1, -1)
    fw = FOX_HEADS * HEAD_DIM
    dw = DSA_HEADS * HEAD_DIM
    kvw = DSA_KV_HEADS * HEAD_DIM
    hw = HGRN_HEADS * HEAD_DIM
    qw = DIFF_HEADS * 2 * DIFF_DIM

    for layer in range(depth):
        j = layer // 2
        if layer % 2 == 0:
            w = w_in_e[j]
            o = 0
            fq, fk, fv = w[:, o:o + fw], w[:, o + fw:o + 2 * fw], w[:, o + 2 * fw:o + 3 * fw]
            o += 3 * fw
            ffw = w[:, o:o + FOX_HEADS]
            o += FOX_HEADS
            dq = w[:, o:o + dw]
            o += dw
            dk = w[:, o:o + kvw]
            o += kvw
            dv = w[:, o:o + kvw]
            o += kvw
            iqw = w[:, o:o + IDX_HEADS * IDX_DIM]
            o += IDX_HEADS * IDX_DIM
            ikw = w[:, o:o + IDX_DIM]
            o += IDX_DIM
            iww = w[:, o:o + IDX_HEADS]
            w1 = jnp.concatenate([fq, fk, fv, _dup_heads(dv)], axis=1).astype(BF16)
            w2 = jnp.concatenate([dq, _dup_heads(dk), iqw, _dup_heads(ikw)], axis=1).astype(BF16)
            w3 = jnp.concatenate([ffw, iww, jnp.zeros((d, LANES - FOX_HEADS - IDX_HEADS), F32)], axis=1).astype(BF16)
            g = row(ln_mix_e[j])
            p1, p2, z = _rms_proj(xs, g, cos_t, sin_t, (w1, w2, w3), (False, True, False), (BF16, BF16, F32), seq=seq)
            b_pad = jnp.concatenate([b_fox_f[j], jnp.zeros((LANES - FOX_HEADS,), F32)]).reshape(1, LANES)
            ccol, crow = _fox_gate(z, b_pad, batch=batch, seq=seq)
            fox_o = _fox_attention(p1, ccol, crow, batch=batch, seq=seq)
            dsa_o = _dsa_attention(p2, p1, z, batch=batch, seq=seq)
            wo = w_out_e[j].astype(BF16)
            xs = _out_proj(fox_o, dsa_o, wo[:fw], wo[fw:], xs)
            xs = _ffn(xs, row(ln_ffn_e[j]), w_ffn_gate[j].astype(BF16), w_ffn_up[j].astype(BF16),
                      w_ffn_down[j].astype(BF16), row(ln_final), tf=1408, final_norm=(layer == depth - 1))
        else:
            w = w_in_o[j]
            lam_init = 0.8 - 0.6 * math.exp(-0.3 * layer)
            g = row(ln_mix_o[j])
            w_qig = jnp.concatenate([w[:, :hw], w[:, 2 * hw:4 * hw]], axis=1).astype(BF16)
            w_f = w[:, hw:2 * hw].astype(BF16)
            w_qk = w[:, 4 * hw:4 * hw + 2 * qw].astype(BF16)
            w_v = w[:, 4 * hw + 2 * qw:].astype(BF16)
            h_qig, h_f, pqk, pv = _rms_proj(xs, g, cos_t, sin_t, (w_qig, w_f, w_qk, w_v),
                                            (False, False, True, False), (BF16, F32, BF16, BF16), seq=seq)
            gn = jnp.concatenate([hgrn_norm_g[j], hgrn_norm_g[j]]).reshape(1, LANES)
            o_h = _hgrn(h_qig, h_f, hgrn_lb_logits, gn, batch=batch, seq=seq, layer=layer)
            o_d = _diff_attention(pqk, pv, diff_lambda[j], row(diff_norm_g[j]), batch=batch, seq=seq,
                                  lam_init=lam_init)
            wo = w_out_o[j].astype(BF16)
            xs = _out_proj(o_h, o_d, wo[:hw], wo[hw:], xs)
            wr = jnp.concatenate([w_router[j], jnp.zeros((d, LANES - N_EXPERTS), F32)], axis=1)
            info, xn = _router(xs, row(ln_ffn_o[j]), wr)
            pos, tile_e, tile_act, n_rows = _moe_positions(info, t)
            pw = xn.shape[2]
            plane = jnp.arange(SC_PLANES, dtype=jnp.int32)[:, None] * n_rows
            dst0 = (plane + pos[None, :, 0]).reshape(SC_PLANES * t)
            dst1 = (plane + pos[None, :, 1]).reshape(SC_PLANES * t)
            rows = _sc_scatter_rows(xn.reshape(SC_PLANES * t, pw), dst0, dst1, SC_PLANES * n_rows)
            y = _moe_ffn_packed(tile_e, tile_act, rows.reshape(SC_PLANES, n_rows, pw), w_exp_gate[j].astype(BF16),
                                w_exp_up[j].astype(BF16), w_exp_down[j].astype(BF16), tf=1792)
            src = jnp.concatenate([(plane + pos[None, :, s]).reshape(SC_PLANES * t) for s in range(2)])
            y_pairs = _sc_gather_rows(y.reshape(SC_PLANES * n_rows, pw), src).reshape(2 * SC_PLANES, t, pw)
            xs = _moe_sum(xs, info, y_pairs, row(ln_final), final_norm=(layer == depth - 1))
    if depth % 2 == 1:
        pass
    return xs.reshape(batch, seq, d)
```

```python
import functools
import math

import jax
import jax.numpy as jnp
from jax import lax
from jax.experimental import pallas as pl
from jax.experimental.pallas import tpu as pltpu
from jax.experimental.pallas import tpu_sc as plsc

F32 = jnp.float32
BF16 = jnp.bfloat16

D_MODEL = 1024
HEAD_DIM = 64
FOX_HEADS = 8
DSA_HEADS = 8
DSA_KV_HEADS = 2
IDX_HEADS = 4
IDX_DIM = 64
DSA_TOPK = 256
HGRN_HEADS = 8
DIFF_HEADS = 4
DIFF_DIM = 64
ROPE_THETA = 10000.0
N_EXPERTS = 8
NORM_EPS = 1e-6
NEG_INF = -1e30

LANES = 128
SUBLANES = 8
VMEM_LIMIT_BYTES = 56 * 1024 * 1024

ROW_TILE = 512
ATTN_TILE = 256
HGRN_CHUNK = 256
SUB = 8
MOE_ROW_TILE = 512
SC_WINDOW = 128
SC_PLANES = 2

INT_MIN = -(2 ** 31)


def _params(*sem):
    return pltpu.CompilerParams(dimension_semantics=sem, vmem_limit_bytes=VMEM_LIMIT_BYTES)


def _dot(a, b):
    return jnp.dot(a, b, preferred_element_type=F32)


def _dot_nt(a, b):
    return lax.dot_general(a, b, (((1,), (1,)), ((), ())), preferred_element_type=F32)


def _dot_tn(a, b):
    return lax.dot_general(a, b, (((0,), (0,)), ((), ())), preferred_element_type=F32)


def _sigmoid(x):
    return 1.0 / (1.0 + jnp.exp(-x))


def _silu(x):
    return x * _sigmoid(x)


def _lane_half(shape):
    return lax.broadcasted_iota(jnp.int32, shape, len(shape) - 1) // HEAD_DIM


def _rms_proj_kernel(x_ref, g_ref, cos_ref, sin_ref, *refs, ropes):
    n = len(ropes)
    w_refs, o_refs = refs[:n], refs[n:]
    x = x_ref[...]
    var = jnp.mean(x * x, axis=-1, keepdims=True)
    xn = (x * lax.rsqrt(var + NORM_EPS) * g_ref[...]).astype(BF16)
    cos = cos_ref[...]
    sin = sin_ref[...]
    first = (lax.broadcasted_iota(jnp.int32, cos.shape, 1) % HEAD_DIM) < (HEAD_DIM // 2)
    for w_ref, o_ref, rope in zip(w_refs, o_refs, ropes):
        y = _dot(xn, w_ref[...])
        if not rope:
            o_ref[...] = y.astype(o_ref.dtype)
            continue
        for c in range(y.shape[1] // LANES):
            yc = y[:, c * LANES:(c + 1) * LANES]
            partner = jnp.where(first, pltpu.roll(yc, LANES - HEAD_DIM // 2, 1), pltpu.roll(yc, HEAD_DIM // 2, 1))
            o_ref[:, c * LANES:(c + 1) * LANES] = (yc * cos + partner * sin).astype(o_ref.dtype)


def _rms_proj(x, g, cos, sin, ws, ropes, out_dtypes, *, seq):
    t, d = x.shape
    tm = ROW_TILE
    nseq = seq // tm
    return pl.pallas_call(
        functools.partial(_rms_proj_kernel, ropes=tuple(ropes)),
        grid=(t // tm,),
        in_specs=[
            pl.BlockSpec((tm, d), lambda i: (i, 0)),
            pl.BlockSpec((1, d), lambda i: (0, 0)),
            pl.BlockSpec((tm, LANES), lambda i: (i % nseq, 0)),
            pl.BlockSpec((tm, LANES), lambda i: (i % nseq, 0)),
        ] + [pl.BlockSpec((d, w.shape[1]), lambda i: (0, 0)) for w in ws],
        out_specs=[pl.BlockSpec((tm, w.shape[1]), lambda i: (i, 0)) for w in ws],
        out_shape=[jax.ShapeDtypeStruct((t, w.shape[1]), dt) for w, dt in zip(ws, out_dtypes)],
        compiler_params=_params("parallel"),
        name="rms_proj",
    )(x, g, cos, sin, *ws)


def _cumsum_rows(x):
    rows = x.shape[0]
    row = lax.broadcasted_iota(jnp.int32, x.shape, 0)
    k = 1
    while k < rows:
        x = x + jnp.where(row >= k, pltpu.roll(x, k, 0), 0.0)
        k *= 2
    return x


def _fox_gate_kernel(z_ref, b_ref, ccol_ref, crow_ref, *, tk):
    z = z_ref[...] + b_ref[...]
    logf = jnp.minimum(z, 0.0) - jnp.log(1.0 + jnp.exp(-jnp.abs(z)))
    c = _cumsum_rows(logf)
    ccol_ref[...] = c
    for j in range(c.shape[0] // tk):
        ct = c[j * tk:(j + 1) * tk, :].T
        crow_ref[0, :, j * tk:(j + 1) * tk] = ct[:SUBLANES, :]


def _fox_gate(z, b_pad, *, batch, seq):
    tk = ATTN_TILE
    return pl.pallas_call(
        functools.partial(_fox_gate_kernel, tk=tk),
        grid=(batch,),
        in_specs=[
            pl.BlockSpec((seq, LANES), lambda b: (b, 0)),
            pl.BlockSpec((1, LANES), lambda b: (0, 0)),
        ],
        out_specs=[
            pl.BlockSpec((seq, LANES), lambda b: (b, 0)),
            pl.BlockSpec((1, SUBLANES, seq), lambda b: (b, 0, 0)),
        ],
        out_shape=[
            jax.ShapeDtypeStruct((batch * seq, LANES), F32),
            jax.ShapeDtypeStruct((batch, SUBLANES, seq), F32),
        ],
        compiler_params=_params("parallel"),
        name="fox_gate",
    )(z, b_pad)


def _softmax_pv(s, v):
    m = jnp.max(s, axis=-1, keepdims=True)
    p = jnp.exp(s - m)
    l = jnp.sum(p, axis=-1, keepdims=True)
    return _dot(p.astype(BF16), v) / l


def _mask_last_block(s, tq):
    n = s.shape[1]
    causal = lax.broadcasted_iota(jnp.int32, (tq, tq), 1) <= lax.broadcasted_iota(jnp.int32, (tq, tq), 0)
    tail = jnp.where(causal, s[:, n - tq:], NEG_INF)
    return tail if n == tq else jnp.concatenate([s[:, :n - tq], tail], axis=1)


def _fox_kernel(q_ref, k_ref, v_ref, cc_ref, cr_ref, o_ref, *, tq, scale):
    hp = pl.program_id(1)
    seq = q_ref.shape[0]
    half = _lane_half((1, LANES))
    lane = lax.broadcasted_iota(jnp.int32, (tq, LANES), 1)
    for i in range(seq // tq):
        n = (i + 1) * tq
        q = q_ref[i * tq:n, :] * scale
        cc = cc_ref[i * tq:n, :]
        k = k_ref[0:n, :]
        v = v_ref[0:n, :]
        outs = []
        for e in range(2):
            h = 2 * hp + e
            qe = jnp.where(half == e, q, jnp.zeros_like(q))
            cq = jnp.sum(jnp.where(lane == h, cc, 0.0), axis=-1, keepdims=True)
            ck = cr_ref[0, pl.ds(h, 1), 0:n]
            s = _mask_last_block(_dot_nt(qe, k) + cq - ck, tq)
            outs.append(_softmax_pv(s, v))
        o_ref[i * tq:n, :] = jnp.where(half == 0, outs[0], outs[1]).astype(o_ref.dtype)


def _fox_attention(p1, ccol, crow, *, batch, seq):
    tq = min(ATTN_TILE, seq)
    ng = FOX_HEADS // 2
    return pl.pallas_call(
        functools.partial(_fox_kernel, tq=tq, scale=HEAD_DIM ** -0.5),
        grid=(batch, ng),
        in_specs=[
            pl.BlockSpec((seq, LANES), lambda b, g: (b, g)),
            pl.BlockSpec((seq, LANES), lambda b, g: (b, ng + g)),
            pl.BlockSpec((seq, LANES), lambda b, g: (b, 2 * ng + g)),
            pl.BlockSpec((seq, LANES), lambda b, g: (b, 0)),
            pl.BlockSpec((1, SUBLANES, seq), lambda b, g: (b, 0, 0)),
        ],
        out_specs=pl.BlockSpec((seq, LANES), lambda b, g: (b, g)),
        out_shape=jax.ShapeDtypeStruct((batch * seq, ng * LANES), BF16),
        compiler_params=_params("parallel", "parallel"),
        name="fox_attention",
    )(p1, p1, p1, ccol, crow)


def _diff_kernel(q_ref, k_ref, v_ref, lam_ref, g_ref, o_ref, *, tq, scale, lam_init):
    seq = q_ref.shape[0]
    half = _lane_half((1, LANES))
    lv = lam_ref[...]
    lam = (jnp.exp(jnp.sum(lv[0:1, :] * lv[1:2, :], axis=-1, keepdims=True))
           - jnp.exp(jnp.sum(lv[2:3, :] * lv[3:4, :], axis=-1, keepdims=True)) + lam_init)
    gn = g_ref[...] * (1.0 - lam_init)
    for i in range(seq // tq):
        n = (i + 1) * tq
        q = q_ref[i * tq:n, :] * scale
        k = k_ref[0:n, :]
        v = v_ref[0:n, :]
        outs = []
        for e in range(2):
            qe = jnp.where(half == e, q, jnp.zeros_like(q))
            outs.append(_softmax_pv(_mask_last_block(_dot_nt(qe, k), tq), v))
        o = outs[0] - lam * outs[1]
        var = jnp.mean(o * o, axis=-1, keepdims=True)
        o_ref[i * tq:n, :] = (o * lax.rsqrt(var + NORM_EPS) * gn).astype(o_ref.dtype)


def _diff_attention(pqk, pv, lam_vecs, g, *, batch, seq, lam_init):
    tq = min(ATTN_TILE, seq)
    ng = DIFF_HEADS
    return pl.pallas_call(
        functools.partial(_diff_kernel, tq=tq, scale=DIFF_DIM ** -0.5, lam_init=lam_init),
        grid=(batch, ng),
        in_specs=[
            pl.BlockSpec((seq, LANES), lambda b, g: (b, g)),
            pl.BlockSpec((seq, LANES), lambda b, g: (b, ng + g)),
            pl.BlockSpec((seq, LANES), lambda b, g: (b, g)),
            pl.BlockSpec(lam_vecs.shape, lambda b, g: (0, 0)),
            pl.BlockSpec((1, LANES), lambda b, g: (0, 0)),
        ],
        out_specs=pl.BlockSpec((seq, LANES), lambda b, g: (b, g)),
        out_shape=jax.ShapeDtypeStruct((batch * seq, ng * LANES), BF16),
        compiler_params=_params("parallel", "parallel"),
        name="diff_attention",
    )(pqk, pqk, pv, lam_vecs, g)


def _sortable_key(x):
    bits = pltpu.bitcast(x, jnp.int32)
    return jnp.where(bits < 0, bits ^ jnp.int32(0x7FFFFFFF), bits)


def _dsa_kernel(q_ref, kd_ref, iq_ref, ik_ref, vd_ref, z_ref, o_ref, key_ref, madd_ref,
                *, tq, n_sel, scale, idx_scale, w_scale):
    i = pl.program_id(1)
    seq = key_ref.shape[1]
    nq = seq // tq
    span = 2 if nq % 2 == 0 else 1
    half = _lane_half((1, LANES))
    row = lax.broadcasted_iota(jnp.int32, (tq, tq), 0)
    col = lax.broadcasted_iota(jnp.int32, (tq, tq), 1)
    row_g = row + i * tq
    tri = jnp.where(row < col, 1.0, 0.0).astype(BF16)

    def block(n):
        z = z_ref[...]
        iq = iq_ref[...]
        ik = ik_ref[0:n, :]
        sc = jnp.zeros((tq, n), F32)
        for h in range(IDX_HEADS):
            grp = iq[:, (h // 2) * LANES:(h // 2 + 1) * LANES]
            iq_h = jnp.where(half == h % 2, grp, jnp.zeros_like(grp))
            w_h = z[:, SUBLANES + h:SUBLANES + h + 1] * w_scale
            sc = sc + w_h * jnp.maximum(_dot_nt(iq_h, ik) * idx_scale, 0.0)
        for j in range(n // tq):
            scj = jnp.where(col + j * tq <= row_g, sc[:, j * tq:(j + 1) * tq], -jnp.inf)
            key_ref[:, j * tq:(j + 1) * tq] = _sortable_key(scj)

        def count(pred):
            return jnp.sum(jnp.where(pred, 1.0, 0.0), axis=-1, keepdims=True)

        thr0 = jnp.where(count(key_ref[:, 0:n] >= 0) >= n_sel, 0, INT_MIN).astype(jnp.int32)

        def bit_body(it, thr):
            cand = thr | jnp.left_shift(jnp.int32(1), 30 - it)
            return jnp.where(count(key_ref[:, 0:n] >= cand) >= n_sel, cand, thr)

        thr = lax.fori_loop(0, 31, bit_body, thr0)
        need = n_sel - count(key_ref[:, 0:n] > thr)

        off = jnp.zeros((tq, 1), F32)
        for j in range(n // tq):
            kj = key_ref[:, j * tq:(j + 1) * tq]
            eq = kj == thr
            eqf = jnp.where(eq, 1.0, 0.0)
            rank = _dot(eqf.astype(BF16), tri) + off
            take = jnp.where(kj > thr, 1.0, jnp.where(eq, jnp.where(rank < need, 1.0, 0.0), 0.0))
            valid = col + j * tq <= row_g
            madd_ref[:, j * tq:(j + 1) * tq] = jnp.where(valid, jnp.where(take > 0.0, 0.0, NEG_INF), NEG_INF)
            off = off + jnp.sum(eqf, axis=-1, keepdims=True)

        q = q_ref[...] * scale
        for p in range(DSA_HEADS // 2):
            qg = q[:, p * LANES:(p + 1) * LANES]
            g = p // (DSA_HEADS // DSA_KV_HEADS // 2)
            k = kd_ref[0:n, g * LANES:(g + 1) * LANES]
            v = vd_ref[0:n, g * LANES:(g + 1) * LANES]
            outs = []
            for e in range(2):
                qe = jnp.where(half == e, qg, jnp.zeros_like(qg))
                outs.append(_softmax_pv(_dot_nt(qe, k) + madd_ref[:, 0:n], v))
            o_ref[:, p * LANES:(p + 1) * LANES] = jnp.where(half == 0, outs[0], outs[1]).astype(o_ref.dtype)

    for c in range(nq // span):
        pl.when(i // span == c)(functools.partial(block, (c + 1) * span * tq))


def _dsa_attention(p2, p1, z, *, batch, seq):
    tq = min(ATTN_TILE, seq)
    nq = seq // tq
    n_sel = min(DSA_TOPK, seq // 4)
    qw = DSA_HEADS * HEAD_DIM
    return pl.pallas_call(
        functools.partial(_dsa_kernel, tq=tq, n_sel=float(n_sel), scale=HEAD_DIM ** -0.5,
                          idx_scale=IDX_DIM ** -0.5, w_scale=IDX_HEADS ** -0.5),
        grid=(batch, nq),
        in_specs=[
            pl.BlockSpec((tq, qw), lambda b, i: (b * nq + i, 0)),
            pl.BlockSpec((seq, 2 * LANES), lambda b, i: (b, 2)),
            pl.BlockSpec((tq, 2 * LANES), lambda b, i: (b * nq + i, 3)),
            pl.BlockSpec((seq, LANES), lambda b, i: (b, 8)),
            pl.BlockSpec((seq, 2 * LANES), lambda b, i: (b, 6)),
            pl.BlockSpec((tq, LANES), lambda b, i: (b * nq + i, 0)),
        ],
        out_specs=pl.BlockSpec((tq, qw), lambda b, i: (b * nq + i, 0)),
        out_shape=jax.ShapeDtypeStruct((batch * seq, qw), BF16),
        scratch_shapes=[pltpu.VMEM((tq, seq), jnp.int32), pltpu.VMEM((tq, seq), F32)],
        compiler_params=_params("parallel", "arbitrary"),
        name="dsa_attention",
    )(p2, p2, p2, p2, p1, z)


def _hgrn_kernel(q_ref, f_ref, i_ref, g_ref, lbl_ref, gn_ref, o_ref, b_scr, *, chunk, layer):
    seq = q_ref.shape[0]
    half = _lane_half((1, LANES))
    logits = lbl_ref[...]
    pe = jnp.exp(logits - jnp.max(logits, axis=0, keepdims=True))
    prob = pe / jnp.sum(pe, axis=0, keepdims=True)
    csum = prob[0:1, :]
    for d in range(1, layer + 1):
        csum = csum + prob[d:d + 1, :]
    lb = csum - prob[0:1, :]

    rowi = lax.broadcasted_iota(jnp.int32, (chunk, 1), 0)
    xor_rc = (lax.broadcasted_iota(jnp.int32, (chunk, chunk), 0)
              ^ lax.broadcasted_iota(jnp.int32, (chunk, chunk), 1))
    same_head = (lax.broadcasted_iota(jnp.int32, (LANES, LANES), 0) // HEAD_DIM
                 == lax.broadcasted_iota(jnp.int32, (LANES, LANES), 1) // HEAD_DIM)
    ones_bd = jnp.where(same_head, 1.0, 0.0).astype(BF16)
    gn = gn_ref[...]

    def chunk_body(c, state_t):
        r0 = pl.multiple_of(c * chunk, chunk)
        q = _silu(q_ref[pl.ds(r0, chunk), :].astype(F32))
        f = lb + (1.0 - lb) * _sigmoid(f_ref[pl.ds(r0, chunk), :])
        kc = 1.0 - f
        v = i_ref[pl.ds(r0, chunk), :].astype(F32)
        b = _cumsum_rows(jnp.log(f))
        b_scr[...] = b
        b_last = b_scr[chunk - 1:chunk, :]

        o = _dot_nt((q * jnp.exp(b)).astype(BF16), state_t.astype(BF16))
        k2 = (kc * jnp.exp(b_last - b)).astype(BF16)
        upd = _dot_tn(v.astype(BF16), k2)
        new_state = state_t * jnp.exp(b_last) + jnp.where(same_head, upd, 0.0)

        s_tot = [jnp.zeros((chunk, chunk), F32), jnp.zeros((chunk, chunk), F32)]
        m = chunk
        while m > SUB:
            hm = m // 2
            pieces = [jnp.broadcast_to(b_scr[blk * m + hm - 1:blk * m + hm, :], (m, LANES))
                      for blk in range(chunk // m)]
            ref = pieces[0] if len(pieces) == 1 else jnp.concatenate(pieces, axis=0)
            right = (rowi % m) >= hm
            qm = jnp.where(right, q * jnp.exp(jnp.where(right, b - ref, 0.0)), 0.0).astype(BF16)
            km = jnp.where(right, 0.0, kc * jnp.exp(jnp.where(right, 0.0, ref - b))).astype(BF16)
            for e in range(2):
                s = _dot_nt(jnp.where(half == e, qm, jnp.zeros_like(qm)), km)
                s_tot[e] = s_tot[e] + (s if m == chunk else jnp.where(xor_rc < m, s, 0.0))
            m = hm
        vb = v.astype(BF16)
        for e in range(2):
            o = o + _dot(s_tot[e].astype(BF16), jnp.where(half == e, vb, jnp.zeros_like(vb)))

        for d in range(SUB):
            ok = (rowi % SUB) >= d
            bs = b if d == 0 else pltpu.roll(b, d, 0)
            ks = kc if d == 0 else pltpu.roll(kc, d, 0)
            vs = v if d == 0 else pltpu.roll(v, d, 0)
            x = jnp.where(ok, q * ks * jnp.exp(jnp.where(ok, b - bs, 0.0)), 0.0)
            o = o + _dot(x.astype(BF16), ones_bd) * vs

        sq = o * o
        s0 = jnp.sum(jnp.where(half == 0, sq, 0.0), axis=-1, keepdims=True)
        s1 = jnp.sum(jnp.where(half == 1, sq, 0.0), axis=-1, keepdims=True)
        var = jnp.where(half == 0, s0, s1) * (1.0 / HEAD_DIM)
        y = o * lax.rsqrt(var + NORM_EPS) * gn * _silu(g_ref[pl.ds(r0, chunk), :].astype(F32))
        o_ref[pl.ds(r0, chunk), :] = y.astype(o_ref.dtype)
        return new_state

    lax.fori_loop(0, seq // chunk, chunk_body, jnp.zeros((LANES, LANES), F32))


def _hgrn(h_qig, h_f, lb_logits, gn, *, batch, seq, layer):
    ng = HGRN_HEADS // 2
    chunk = min(HGRN_CHUNK, seq)
    spec = lambda off: pl.BlockSpec((seq, LANES), lambda b, g: (b, off * ng + g))
    return pl.pallas_call(
        functools.partial(_hgrn_kernel, chunk=chunk, layer=layer),
        grid=(batch, ng),
        in_specs=[spec(0), spec(0), spec(1), spec(2),
                  pl.BlockSpec((lb_logits.shape[0], LANES), lambda b, g: (0, g)),
                  pl.BlockSpec((1, LANES), lambda b, g: (0, 0))],
        out_specs=pl.BlockSpec((seq, LANES), lambda b, g: (b, g)),
        out_shape=jax.ShapeDtypeStruct((batch * seq, ng * LANES), BF16),
        scratch_shapes=[pltpu.VMEM((chunk, LANES), F32)],
        compiler_params=_params("parallel", "parallel"),
        name="hgrn2",
    )(h_qig, h_f, h_qig, h_qig, lb_logits, gn)


def _pack_pairs(x):
    hw = x.shape[1] // 2
    hi = pltpu.bitcast(x[:, :hw].astype(BF16).astype(F32), jnp.int32)
    lo = pltpu.bitcast(x[:, hw:].astype(BF16).astype(F32), jnp.int32)
    return hi | lax.shift_right_logical(lo, 16)


def _unpack_pairs(w):
    hi = pltpu.bitcast(w & jnp.int32(-65536), F32)
    lo = pltpu.bitcast(lax.shift_left(w, 16), F32)
    return hi, lo


def _store_planes(ref, words):
    pw = words.shape[1] // SC_PLANES
    for h in range(SC_PLANES):
        ref[h] = words[:, h * pw:(h + 1) * pw]


def _load_planes(ref):
    return jnp.concatenate([ref[h] for h in range(SC_PLANES)], axis=1)


def _router_kernel(a1_ref, a2_ref, w1_ref, w2_ref, x_ref, g_ref, w_ref, xo_ref, o_ref, xn_ref):
    x = x_ref[...] + _dot(a1_ref[...], w1_ref[...]) + _dot(a2_ref[...], w2_ref[...])
    xo_ref[...] = x
    var = jnp.mean(x * x, axis=-1, keepdims=True)
    xn = x * lax.rsqrt(var + NORM_EPS) * g_ref[...]
    _store_planes(xn_ref, _pack_pairs(xn))
    logits = jnp.dot(xn, w_ref[...], precision=lax.Precision.HIGHEST, preferred_element_type=F32)
    lane = lax.broadcasted_iota(jnp.int32, logits.shape, 1)
    lg = jnp.where(lane < N_EXPERTS, logits, -jnp.inf)
    m1 = jnp.max(lg, axis=-1, keepdims=True)
    i1 = jnp.min(jnp.where(lg == m1, lane, LANES), axis=-1, keepdims=True)
    lg2 = jnp.where(lane == i1, -jnp.inf, lg)
    m2 = jnp.max(lg2, axis=-1, keepdims=True)
    i2 = jnp.min(jnp.where(lg2 == m2, lane, LANES), axis=-1, keepdims=True)
    t = jnp.exp(m2 - m1)
    den = 1.0 + t
    o_ref[...] = (jnp.where(lane == 0, i1.astype(F32), 0.0) + jnp.where(lane == 1, i2.astype(F32), 0.0)
                  + jnp.where(lane == 2, 1.0 / den, 0.0) + jnp.where(lane == 3, t / den, 0.0))


def _mix_router(a1, a2, w1, w2, x, g, w_pad):
    t, d = x.shape
    k1, k2 = a1.shape[1], a2.shape[1]
    tm = ROW_TILE
    return pl.pallas_call(
        _router_kernel,
        grid=(t // tm,),
        in_specs=[pl.BlockSpec((tm, k1), lambda i: (i, 0)),
                  pl.BlockSpec((tm, k2), lambda i: (i, 0)),
                  pl.BlockSpec((k1, d), lambda i: (0, 0)),
                  pl.BlockSpec((k2, d), lambda i: (0, 0)),
                  pl.BlockSpec((tm, d), lambda i: (i, 0)),
                  pl.BlockSpec((1, d), lambda i: (0, 0)),
                  pl.BlockSpec((d, LANES), lambda i: (0, 0))],
        out_specs=[pl.BlockSpec((tm, d), lambda i: (i, 0)),
                   pl.BlockSpec((tm, LANES), lambda i: (i, 0)),
                   pl.BlockSpec((SC_PLANES, tm, d // 2 // SC_PLANES), lambda i: (0, i, 0))],
        out_shape=[jax.ShapeDtypeStruct((t, d), F32),
                   jax.ShapeDtypeStruct((t, LANES), F32),
                   jax.ShapeDtypeStruct((SC_PLANES, t, d // 2 // SC_PLANES), jnp.int32)],
        compiler_params=_params("parallel"),
        name="mix_router",
    )(a1, a2, w1, w2, x, g, w_pad)


def _sc_mesh():
    return plsc.VectorSubcoreMesh(core_axis_name="core", subcore_axis_name="subcore")


def _sc_scatter_rows(x, idx0, idx1, n_out):
    n, d = x.shape

    @functools.partial(pl.kernel, out_type=jax.ShapeDtypeStruct((n_out, d), x.dtype), mesh=_sc_mesh(),
                       scratch_types=[])
    def scatter(x_hbm, i0_hbm, i1_hbm, o_hbm):
        def body(x_vmem, i0_vmem, i1_vmem):
            pltpu.sync_copy(x_vmem, o_hbm.at[i0_vmem.at[0]])
            pltpu.sync_copy(x_vmem, o_hbm.at[i1_vmem.at[0]])

        pltpu.emit_pipeline(
            body,
            grid=(n // SC_WINDOW,),
            in_specs=[pl.BlockSpec((SC_WINDOW, d), index_map=lambda i: (i, 0)),
                      pl.BlockSpec((1, SC_WINDOW), index_map=lambda i: (0, i)),
                      pl.BlockSpec((1, SC_WINDOW), index_map=lambda i: (0, i))],
            out_specs=[],
            core_axis_name=("core", "subcore"),
            dimension_semantics=(pltpu.PARALLEL,),
        )(x_hbm, i0_hbm, i1_hbm)

    return scatter(x, idx0.reshape(1, n), idx1.reshape(1, n))


def _sc_gather_rows(x, idx):
    n = idx.shape[0]
    d = x.shape[1]

    @functools.partial(pl.kernel, out_type=jax.ShapeDtypeStruct((n, d), x.dtype), mesh=_sc_mesh(),
                       scratch_types=[])
    def gather(x_hbm, i_hbm, o_hbm):
        def body(i_vmem, o_vmem):
            pltpu.sync_copy(x_hbm.at[i_vmem.at[0]], o_vmem)

        pltpu.emit_pipeline(
            body,
            grid=(n // SC_WINDOW,),
            in_specs=[pl.BlockSpec((1, SC_WINDOW), index_map=lambda i: (0, i))],
            out_specs=[pl.BlockSpec((SC_WINDOW, d), index_map=lambda i: (i, 0))],
            core_axis_name=("core", "subcore"),
            dimension_semantics=(pltpu.PARALLEL,),
        )(i_hbm, o_hbm)

    return gather(x, idx.reshape(1, n))


def _moe_ffn_packed_kernel(te_ref, act_ref, x_ref, wg_ref, wu_ref, wd_ref, o_ref, xn_ref, acc_ref):
    i = pl.program_id(0)
    f = pl.program_id(1)
    hw = xn_ref.shape[1] // 2

    @pl.when(act_ref[i] == 1)
    def _():
        @pl.when(f == 0)
        def _():
            hi, lo = _unpack_pairs(_load_planes(x_ref))
            xn_ref[:, :hw] = hi.astype(BF16)
            xn_ref[:, hw:] = lo.astype(BF16)
            acc_ref[...] = jnp.zeros_like(acc_ref)

        x = xn_ref[...]
        mid = (_silu(_dot(x, wg_ref[0])) * _dot(x, wu_ref[0])).astype(BF16)
        acc_ref[...] += _dot(mid, wd_ref[0])

        @pl.when(f == pl.num_programs(1) - 1)
        def _():
            _store_planes(o_ref, _pack_pairs(acc_ref[...]))


def _moe_ffn_packed(tile_e, tile_act, xs, wg, wu, wd, *, tf):
    _, p, pw = xs.shape
    d = 2 * pw * SC_PLANES
    ff = wg.shape[2]
    tm = MOE_ROW_TILE
    return pl.pallas_call(
        _moe_ffn_packed_kernel,
        grid_spec=pltpu.PrefetchScalarGridSpec(
            num_scalar_prefetch=2,
            grid=(p // tm, ff // tf),
            in_specs=[pl.BlockSpec((SC_PLANES, tm, pw), lambda i, f, te, ta: (0, i, 0)),
                      pl.BlockSpec((1, d, tf), lambda i, f, te, ta: (te[i], 0, f * ta[i])),
                      pl.BlockSpec((1, d, tf), lambda i, f, te, ta: (te[i], 0, f * ta[i])),
                      pl.BlockSpec((1, tf, d), lambda i, f, te, ta: (te[i], f * ta[i], 0))],
            out_specs=pl.BlockSpec((SC_PLANES, tm, pw), lambda i, f, te, ta: (0, i, 0)),
            scratch_shapes=[pltpu.VMEM((tm, d), BF16), pltpu.VMEM((tm, d), F32)],
        ),
        out_shape=jax.ShapeDtypeStruct((SC_PLANES, p, pw), jnp.int32),
        compiler_params=_params("arbitrary", "arbitrary"),
        name="moe_ffn",
    )(tile_e, tile_act, xs, wg, wu, wd)


def _moe_sum_kernel(x_ref, info_ref, y_ref, gf_ref, o_ref, *, final_norm):
    hw = x_ref.shape[1] // 2
    info = info_ref[...]
    g1 = info[:, 2:3]
    g2 = info[:, 3:4]
    hi1, lo1 = _unpack_pairs(jnp.concatenate([y_ref[h] for h in range(SC_PLANES)], axis=1))
    hi2, lo2 = _unpack_pairs(jnp.concatenate([y_ref[SC_PLANES + h] for h in range(SC_PLANES)], axis=1))
    x = x_ref[...]
    out_a = x[:, :hw] + g1 * hi1 + g2 * hi2
    out_b = x[:, hw:] + g1 * lo1 + g2 * lo2
    if final_norm:
        ss = jnp.sum(out_a * out_a, axis=-1, keepdims=True) + jnp.sum(out_b * out_b, axis=-1, keepdims=True)
        r = lax.rsqrt(ss * (1.0 / (2 * hw)) + NORM_EPS)
        gf = gf_ref[...]
        out_a = out_a * r * gf[:, :hw]
        out_b = out_b * r * gf[:, hw:]
    o_ref[:, :hw] = out_a
    o_ref[:, hw:] = out_b


def _moe_sum(x, info, y_pairs, gf, *, final_norm):
    t, d = x.shape
    tm = ROW_TILE
    return pl.pallas_call(
        functools.partial(_moe_sum_kernel, final_norm=final_norm),
        grid=(t // tm,),
        in_specs=[pl.BlockSpec((tm, d), lambda i: (i, 0)),
                  pl.BlockSpec((tm, LANES), lambda i: (i, 0)),
                  pl.BlockSpec((2 * SC_PLANES, tm, y_pairs.shape[2]), lambda i: (0, i, 0)),
                  pl.BlockSpec((1, d), lambda i: (0, 0))],
        out_specs=pl.BlockSpec((tm, d), lambda i: (i, 0)),
        out_shape=jax.ShapeDtypeStruct((t, d), F32),
        compiler_params=_params("parallel"),
        name="moe_sum",
    )(x, info, y_pairs, gf)


def _moe_positions(info, t):
    tm = MOE_ROW_TILE
    n_e = N_EXPERTS
    a = 2 * t
    p = a + n_e * tm
    e_flat = info[:, 0:2].astype(jnp.int32).reshape(a)
    onehot = (e_flat[:, None] == jnp.arange(n_e, dtype=jnp.int32)[None, :]).astype(jnp.int32)
    csum = jnp.cumsum(onehot, axis=0)
    rank = jnp.sum(csum * onehot, axis=1) - 1
    counts = csum[-1]
    padded = ((counts + tm - 1) // tm) * tm
    ends = jnp.cumsum(padded)
    starts = ends - padded
    pos = jnp.sum(onehot * starts[None, :], axis=1) + rank
    tile_start = jnp.arange(p // tm, dtype=jnp.int32) * tm
    tile_e = jnp.minimum(jnp.sum((tile_start[:, None] >= ends[None, :]).astype(jnp.int32), axis=1), n_e - 1)
    tile_act = (tile_start < ends[-1]).astype(jnp.int32)
    return pos.astype(jnp.int32).reshape(t, 2), tile_e.astype(jnp.int32), tile_act, p


def _mix_ffn_kernel(a1_ref, a2_ref, w1_ref, w2_ref, x_ref, g_ref, wg_ref, wu_ref, wd_ref, gf_ref, o_ref,
                    *, final_norm):
    x = x_ref[...] + _dot(a1_ref[...], w1_ref[...]) + _dot(a2_ref[...], w2_ref[...])
    var = jnp.mean(x * x, axis=-1, keepdims=True)
    xn = (x * lax.rsqrt(var + NORM_EPS) * g_ref[...]).astype(BF16)
    mid = (_silu(_dot(xn, wg_ref[...])) * _dot(xn, wu_ref[...])).astype(BF16)
    out = x + _dot(mid, wd_ref[...])
    if final_norm:
        var = jnp.mean(out * out, axis=-1, keepdims=True)
        out = out * lax.rsqrt(var + NORM_EPS) * gf_ref[...]
    o_ref[...] = out


def _resident(shape):
    return pl.BlockSpec(shape, lambda i: (0,) * len(shape), pipeline_mode=pl.Buffered(1))


def _mix_ffn(a1, a2, w1, w2, x, g, wg, wu, wd, gf, *, final_norm):
    t, d = x.shape
    ff = wg.shape[1]
    k1, k2 = a1.shape[1], a2.shape[1]
    tm = ROW_TILE
    return pl.pallas_call(
        functools.partial(_mix_ffn_kernel, final_norm=final_norm),
        grid=(t // tm,),
        in_specs=[
            pl.BlockSpec((tm, k1), lambda i: (i, 0)),
            pl.BlockSpec((tm, k2), lambda i: (i, 0)),
            _resident((k1, d)),
            _resident((k2, d)),
            pl.BlockSpec((tm, d), lambda i: (i, 0)),
            _resident((1, d)),
            _resident((d, ff)),
            _resident((d, ff)),
            _resident((ff, d)),
            _resident((1, d)),
        ],
        out_specs=pl.BlockSpec((tm, d), lambda i: (i, 0)),
        out_shape=jax.ShapeDtypeStruct((t, d), F32),
        compiler_params=_params("parallel"),
        name="mix_ffn",
    )(a1, a2, w1, w2, x, g, wg, wu, wd, gf)


def _rope_tables(seq):
    half = HEAD_DIM // 2
    inv = ROPE_THETA ** (-jnp.arange(half, dtype=F32) * 2.0 / HEAD_DIM)
    ang = jnp.arange(seq, dtype=F32)[:, None] * inv[None, :]
    cos = jnp.cos(ang)
    sin = jnp.sin(ang)
    cos_t = jnp.concatenate([cos, cos, cos, cos], axis=-1)
    sin_t = jnp.concatenate([-sin, sin, -sin, sin], axis=-1)
    return cos_t, sin_t


def _dup_heads(w):
    d, n = w.shape
    w = w.reshape(d, n // HEAD_DIM, 1, HEAD_DIM)
    return jnp.broadcast_to(w, (d, n // HEAD_DIM, 2, HEAD_DIM)).reshape(d, 2 * n)


def kernel(x, ln_mix_e, w_in_e, b_fox_f, w_out_e, ln_ffn_e, w_ffn_gate, w_ffn_up, w_ffn_down,
           ln_mix_o, w_in_o, hgrn_lb_logits, hgrn_norm_g, diff_lambda, diff_norm_g, w_out_o,
           ln_ffn_o, w_router, w_exp_gate, w_exp_up, w_exp_down, ln_final):
    batch, seq, d = x.shape
    t = batch * seq
    depth = ln_mix_e.shape[0] + ln_mix_o.shape[0]
    xs = x.reshape(t, d)
    cos_t, sin_t = _rope_tables(seq)
    row = lambda v: v.reshape(1, -1)
    fw = FOX_HEADS * HEAD_DIM
    dw = DSA_HEADS * HEAD_DIM
    kvw = DSA_KV_HEADS * HEAD_DIM
    hw = HGRN_HEADS * HEAD_DIM
    qw = DIFF_HEADS * 2 * DIFF_DIM

    for layer in range(depth):
        j = layer // 2
        if layer % 2 == 0:
            w = w_in_e[j]
            o = 0
            fq, fk, fv = w[:, o:o + fw], w[:, o + fw:o + 2 * fw], w[:, o + 2 * fw:o + 3 * fw]
            o += 3 * fw
            ffw = w[:, o:o + FOX_HEADS]
            o += FOX_HEADS
            dq = w[:, o:o + dw]
            o += dw
            dk = w[:, o:o + kvw]
            o += kvw
            dv = w[:, o:o + kvw]
            o += kvw
            iqw = w[:, o:o + IDX_HEADS * IDX_DIM]
            o += IDX_HEADS * IDX_DIM
            ikw = w[:, o:o + IDX_DIM]
            o += IDX_DIM
            iww = w[:, o:o + IDX_HEADS]
            w1 = jnp.concatenate([fq, fk, fv, _dup_heads(dv)], axis=1).astype(BF16)
            w2 = jnp.concatenate([dq, _dup_heads(dk), iqw, _dup_heads(ikw)], axis=1).astype(BF16)
            w3 = jnp.concatenate([ffw, iww, jnp.zeros((d, LANES - FOX_HEADS - IDX_HEADS), F32)], axis=1).astype(BF16)
            g = row(ln_mix_e[j])
            p1, p2, z = _rms_proj(xs, g, cos_t, sin_t, (w1, w2, w3), (False, True, False), (BF16, BF16, F32), seq=seq)
            b_pad = jnp.concatenate([b_fox_f[j], jnp.zeros((LANES - FOX_HEADS,), F32)]).reshape(1, LANES)
            ccol, crow = _fox_gate(z, b_pad, batch=batch, seq=seq)
            fox_o = _fox_attention(p1, ccol, crow, batch=batch, seq=seq)
            dsa_o = _dsa_attention(p2, p1, z, batch=batch, seq=seq)
            wo = w_out_e[j].astype(BF16)
            xs = _mix_ffn(fox_o, dsa_o, wo[:fw], wo[fw:], xs, row(ln_ffn_e[j]), w_ffn_gate[j].astype(BF16),
                          w_ffn_up[j].astype(BF16), w_ffn_down[j].astype(BF16), row(ln_final),
                          final_norm=(layer == depth - 1))
        else:
            w = w_in_o[j]
            lam_init = 0.8 - 0.6 * math.exp(-0.3 * layer)
            g = row(ln_mix_o[j])
            w_qig = jnp.concatenate([w[:, :hw], w[:, 2 * hw:4 * hw]], axis=1).astype(BF16)
            w_f = w[:, hw:2 * hw].astype(BF16)
            w_qk = w[:, 4 * hw:4 * hw + 2 * qw].astype(BF16)
            w_v = w[:, 4 * hw + 2 * qw:].astype(BF16)
            h_qig, h_f, pqk, pv = _rms_proj(xs, g, cos_t, sin_t, (w_qig, w_f, w_qk, w_v),
                                            (False, False, True, False), (BF16, F32, BF16, BF16), seq=seq)
            gn = jnp.concatenate([hgrn_norm_g[j], hgrn_norm_g[j]]).reshape(1, LANES)
            o_h = _hgrn(h_qig, h_f, hgrn_lb_logits, gn, batch=batch, seq=seq, layer=layer)
            o_d = _diff_attention(pqk, pv, diff_lambda[j], row(diff_norm_g[j]), batch=batch, seq=seq,
                                  lam_init=lam_init)
            wo = w_out_o[j].astype(BF16)
            wr = jnp.concatenate([w_router[j], jnp.zeros((d, LANES - N_EXPERTS), F32)], axis=1)
            xs, info, xn = _mix_router(o_h, o_d, wo[:hw], wo[hw:], xs, row(ln_ffn_o[j]), wr)
            pos, tile_e, tile_act, n_rows = _moe_positions(info, t)
            pw = xn.shape[2]
            plane = jnp.arange(SC_PLANES, dtype=jnp.int32)[:, None] * n_rows
            dst0 = (plane + pos[None, :, 0]).reshape(SC_PLANES * t)
            dst1 = (plane + pos[None, :, 1]).reshape(SC_PLANES * t)
            rows = _sc_scatter_rows(xn.reshape(SC_PLANES * t, pw), dst0, dst1, SC_PLANES * n_rows)
            y = _moe_ffn_packed(tile_e, tile_act, rows.reshape(SC_PLANES, n_rows, pw), w_exp_gate[j].astype(BF16),
                                w_exp_up[j].astype(BF16), w_exp_down[j].astype(BF16), tf=1792)
            src = jnp.concatenate([(plane + pos[None, :, s]).reshape(SC_PLANES * t) for s in range(2)])
            y_pairs = _sc_gather_rows(y.reshape(SC_PLANES * n_rows, pw), src).reshape(2 * SC_PLANES, t, pw)
            xs = _moe_sum(xs, info, y_pairs, row(ln_final), final_norm=(layer == depth - 1))
    if depth % 2 == 1:
        pass
    return xs.reshape(batch, seq, d)
```

```python
import functools
import math

import jax
import jax.numpy as jnp
from jax import lax
from jax.experimental import pallas as pl
from jax.experimental.pallas import tpu as pltpu
from jax.experimental.pallas import tpu_sc as plsc

F32 = jnp.float32
BF16 = jnp.bfloat16

D_MODEL = 1024
HEAD_DIM = 64
FOX_HEADS = 8
DSA_HEADS = 8
DSA_KV_HEADS = 2
IDX_HEADS = 4
IDX_DIM = 64
DSA_TOPK = 256
HGRN_HEADS = 8
DIFF_HEADS = 4
DIFF_DIM = 64
ROPE_THETA = 10000.0
N_EXPERTS = 8
NORM_EPS = 1e-6
NEG_INF = -1e30

LANES = 128
SUBLANES = 8
VMEM_LIMIT_BYTES = 56 * 1024 * 1024

ROW_TILE = 512
ATTN_TILE = 256
HGRN_CHUNK = 256
SUB = 8
MOE_ROW_TILE = 512
SC_WINDOW = 128
SC_PLANES = 2

INT_MIN = -(2 ** 31)


def _params(*sem):
    return pltpu.CompilerParams(dimension_semantics=sem, vmem_limit_bytes=VMEM_LIMIT_BYTES)


def _dot(a, b):
    return jnp.dot(a, b, preferred_element_type=F32)


def _dot_nt(a, b):
    return lax.dot_general(a, b, (((1,), (1,)), ((), ())), preferred_element_type=F32)


def _dot_tn(a, b):
    return lax.dot_general(a, b, (((0,), (0,)), ((), ())), preferred_element_type=F32)


def _sigmoid(x):
    return 1.0 / (1.0 + jnp.exp(-x))


def _silu(x):
    return x * _sigmoid(x)


def _lane_half(shape):
    return lax.broadcasted_iota(jnp.int32, shape, len(shape) - 1) // HEAD_DIM


def _rms_proj_kernel(x_ref, g_ref, cos_ref, sin_ref, *refs, ropes):
    n = len(ropes)
    w_refs, o_refs = refs[:n], refs[n:]
    x = x_ref[...]
    var = jnp.mean(x * x, axis=-1, keepdims=True)
    xn = (x * lax.rsqrt(var + NORM_EPS) * g_ref[...]).astype(BF16)
    cos = cos_ref[...]
    sin = sin_ref[...]
    first = (lax.broadcasted_iota(jnp.int32, cos.shape, 1) % HEAD_DIM) < (HEAD_DIM // 2)
    for w_ref, o_ref, rope in zip(w_refs, o_refs, ropes):
        y = _dot(xn, w_ref[...])
        if not rope:
            o_ref[...] = y.astype(o_ref.dtype)
            continue
        for c in range(y.shape[1] // LANES):
            yc = y[:, c * LANES:(c + 1) * LANES]
            partner = jnp.where(first, pltpu.roll(yc, LANES - HEAD_DIM // 2, 1), pltpu.roll(yc, HEAD_DIM // 2, 1))
            o_ref[:, c * LANES:(c + 1) * LANES] = (yc * cos + partner * sin).astype(o_ref.dtype)


def _rms_proj(x, g, cos, sin, ws, ropes, out_dtypes, *, seq):
    t, d = x.shape
    tm = ROW_TILE
    nseq = seq // tm
    return pl.pallas_call(
        functools.partial(_rms_proj_kernel, ropes=tuple(ropes)),
        grid=(t // tm,),
        in_specs=[
            pl.BlockSpec((tm, d), lambda i: (i, 0)),
            pl.BlockSpec((1, d), lambda i: (0, 0)),
            pl.BlockSpec((tm, LANES), lambda i: (i % nseq, 0)),
            pl.BlockSpec((tm, LANES), lambda i: (i % nseq, 0)),
        ] + [pl.BlockSpec((d, w.shape[1]), lambda i: (0, 0)) for w in ws],
        out_specs=[pl.BlockSpec((tm, w.shape[1]), lambda i: (i, 0)) for w in ws],
        out_shape=[jax.ShapeDtypeStruct((t, w.shape[1]), dt) for w, dt in zip(ws, out_dtypes)],
        compiler_params=_params("parallel"),
        name="rms_proj",
    )(x, g, cos, sin, *ws)


def _cumsum_rows(x):
    rows = x.shape[0]
    row = lax.broadcasted_iota(jnp.int32, x.shape, 0)
    k = 1
    while k < rows:
        x = x + jnp.where(row >= k, pltpu.roll(x, k, 0), 0.0)
        k *= 2
    return x


def _cumsum_rows_mxu(x, tri):
    hi = x.astype(BF16)
    r1 = x - hi.astype(F32)
    mid = r1.astype(BF16)
    lo = (r1 - mid.astype(F32)).astype(BF16)
    return _dot(tri, hi) + _dot(tri, mid) + _dot(tri, lo)


def _fox_gate_kernel(z_ref, b_ref, ccol_ref, crow_ref, *, tk):
    z = z_ref[...] + b_ref[...]
    logf = jnp.minimum(z, 0.0) - jnp.log(1.0 + jnp.exp(-jnp.abs(z)))
    c = _cumsum_rows(logf)
    ccol_ref[...] = c
    for j in range(c.shape[0] // tk):
        ct = c[j * tk:(j + 1) * tk, :].T
        crow_ref[0, :, j * tk:(j + 1) * tk] = ct[:SUBLANES, :]


def _fox_gate(z, b_pad, *, batch, seq):
    tk = ATTN_TILE
    return pl.pallas_call(
        functools.partial(_fox_gate_kernel, tk=tk),
        grid=(batch,),
        in_specs=[
            pl.BlockSpec((seq, LANES), lambda b: (b, 0)),
            pl.BlockSpec((1, LANES), lambda b: (0, 0)),
        ],
        out_specs=[
            pl.BlockSpec((seq, LANES), lambda b: (b, 0)),
            pl.BlockSpec((1, SUBLANES, seq), lambda b: (b, 0, 0)),
        ],
        out_shape=[
            jax.ShapeDtypeStruct((batch * seq, LANES), F32),
            jax.ShapeDtypeStruct((batch, SUBLANES, seq), F32),
        ],
        compiler_params=_params("parallel"),
        name="fox_gate",
    )(z, b_pad)


def _softmax_pv(s, v):
    m = jnp.max(s, axis=-1, keepdims=True)
    p = jnp.exp(s - m)
    l = jnp.sum(p, axis=-1, keepdims=True)
    return _dot(p.astype(BF16), v) / l


def _causal_attend(qe, k_ref, v_ref, n, tq, row_bias=None, key_bias=None):
    causal = lax.broadcasted_iota(jnp.int32, (tq, tq), 1) <= lax.broadcasted_iota(jnp.int32, (tq, tq), 0)
    parts = []
    for lo, hi in ((0, n - tq), (n - tq, n)):
        if hi == lo:
            continue
        s = _dot_nt(qe, k_ref[lo:hi, :])
        if key_bias is not None:
            s = s - key_bias[:, lo:hi]
        if hi == n:
            s = jnp.where(causal, s, NEG_INF)
        parts.append((s, v_ref[lo:hi, :]))
    m = functools.reduce(jnp.maximum, [jnp.max(s, axis=-1, keepdims=True) for s, _ in parts])
    shift = -m if row_bias is None else row_bias - (m + row_bias)
    acc = 0.0
    l = 0.0
    for s, v in parts:
        p = jnp.exp(s + shift)
        l = l + jnp.sum(p, axis=-1, keepdims=True)
        acc = acc + _dot(p.astype(BF16), v)
    return acc / l


def _fox_kernel(q_ref, k_ref, v_ref, cc_ref, cr_ref, o_ref, *, tq, scale):
    hp = pl.program_id(1)
    seq = q_ref.shape[0]
    half = _lane_half((1, LANES))
    lane = lax.broadcasted_iota(jnp.int32, (tq, LANES), 1)
    for i in range(seq // tq):
        n = (i + 1) * tq
        q = q_ref[i * tq:n, :] * scale
        cc = cc_ref[i * tq:n, :]
        outs = []
        for e in range(2):
            h = 2 * hp + e
            qe = jnp.where(half == e, q, jnp.zeros_like(q))
            cq = jnp.sum(jnp.where(lane == h, cc, 0.0), axis=-1, keepdims=True)
            ck = cr_ref[0, pl.ds(h, 1), 0:n]
            outs.append(_causal_attend(qe, k_ref, v_ref, n, tq, row_bias=cq, key_bias=ck))
        o_ref[i * tq:n, :] = jnp.where(half == 0, outs[0], outs[1]).astype(o_ref.dtype)


def _fox_attention(p1, ccol, crow, *, batch, seq):
    tq = min(ATTN_TILE, seq)
    ng = FOX_HEADS // 2
    return pl.pallas_call(
        functools.partial(_fox_kernel, tq=tq, scale=HEAD_DIM ** -0.5),
        grid=(batch, ng),
        in_specs=[
            pl.BlockSpec((seq, LANES), lambda b, g: (b, g)),
            pl.BlockSpec((seq, LANES), lambda b, g: (b, ng + g)),
            pl.BlockSpec((seq, LANES), lambda b, g: (b, 2 * ng + g)),
            pl.BlockSpec((seq, LANES), lambda b, g: (b, 0)),
            pl.BlockSpec((1, SUBLANES, seq), lambda b, g: (b, 0, 0)),
        ],
        out_specs=pl.BlockSpec((seq, LANES), lambda b, g: (b, g)),
        out_shape=jax.ShapeDtypeStruct((batch * seq, ng * LANES), BF16),
        compiler_params=_params("parallel", "parallel"),
        name="fox_attention",
    )(p1, p1, p1, ccol, crow)


def _diff_kernel(q_ref, k_ref, v_ref, lam_ref, g_ref, o_ref, *, tq, scale, lam_init):
    seq = q_ref.shape[0]
    half = _lane_half((1, LANES))
    lv = lam_ref[...]
    lam = (jnp.exp(jnp.sum(lv[0:1, :] * lv[1:2, :], axis=-1, keepdims=True))
           - jnp.exp(jnp.sum(lv[2:3, :] * lv[3:4, :], axis=-1, keepdims=True)) + lam_init)
    gn = g_ref[...] * (1.0 - lam_init)
    for i in range(seq // tq):
        n = (i + 1) * tq
        q = q_ref[i * tq:n, :] * scale
        outs = []
        for e in range(2):
            qe = jnp.where(half == e, q, jnp.zeros_like(q))
            outs.append(_causal_attend(qe, k_ref, v_ref, n, tq))
        o = outs[0] - lam * outs[1]
        var = jnp.mean(o * o, axis=-1, keepdims=True)
        o_ref[i * tq:n, :] = (o * lax.rsqrt(var + NORM_EPS) * gn).astype(o_ref.dtype)


def _diff_attention(pqk, pv, lam_vecs, g, *, batch, seq, lam_init):
    tq = min(ATTN_TILE, seq)
    ng = DIFF_HEADS
    return pl.pallas_call(
        functools.partial(_diff_kernel, tq=tq, scale=DIFF_DIM ** -0.5, lam_init=lam_init),
        grid=(batch, ng),
        in_specs=[
            pl.BlockSpec((seq, LANES), lambda b, g: (b, g)),
            pl.BlockSpec((seq, LANES), lambda b, g: (b, ng + g)),
            pl.BlockSpec((seq, LANES), lambda b, g: (b, g)),
            pl.BlockSpec(lam_vecs.shape, lambda b, g: (0, 0)),
            pl.BlockSpec((1, LANES), lambda b, g: (0, 0)),
        ],
        out_specs=pl.BlockSpec((seq, LANES), lambda b, g: (b, g)),
        out_shape=jax.ShapeDtypeStruct((batch * seq, ng * LANES), BF16),
        compiler_params=_params("parallel", "parallel"),
        name="diff_attention",
    )(pqk, pqk, pv, lam_vecs, g)


def _sortable_key(x):
    bits = pltpu.bitcast(x, jnp.int32)
    return jnp.where(bits < 0, bits ^ jnp.int32(0x7FFFFFFF), bits)


def _dsa_kernel(q_ref, kd_ref, iq_ref, ik_ref, vd_ref, z_ref, o_ref, key_ref, madd_ref,
                *, tq, n_sel, scale, idx_scale, w_scale):
    i = pl.program_id(1)
    seq = key_ref.shape[1]
    nq = seq // tq
    span = 2 if nq % 2 == 0 else 1
    half = _lane_half((1, LANES))
    row = lax.broadcasted_iota(jnp.int32, (tq, tq), 0)
    col = lax.broadcasted_iota(jnp.int32, (tq, tq), 1)
    row_g = row + i * tq
    tri = jnp.where(row < col, 1.0, 0.0).astype(BF16)

    def block(n):
        z = z_ref[...]
        iq = iq_ref[...]
        ik = ik_ref[0:n, :]
        sc = jnp.zeros((tq, n), F32)
        for h in range(IDX_HEADS):
            grp = iq[:, (h // 2) * LANES:(h // 2 + 1) * LANES]
            iq_h = jnp.where(half == h % 2, grp, jnp.zeros_like(grp))
            w_h = z[:, SUBLANES + h:SUBLANES + h + 1] * w_scale
            sc = sc + w_h * jnp.maximum(_dot_nt(iq_h, ik) * idx_scale, 0.0)
        for j in range(n // tq):
            scj = jnp.where(col + j * tq <= row_g, sc[:, j * tq:(j + 1) * tq], -jnp.inf)
            key_ref[:, j * tq:(j + 1) * tq] = _sortable_key(scj)

        def count(pred):
            return jnp.sum(jnp.where(pred, 1.0, 0.0), axis=-1, keepdims=True)

        thr0 = jnp.where(count(key_ref[:, 0:n] >= 0) >= n_sel, 0, INT_MIN).astype(jnp.int32)

        def bit_body(it, thr):
            cand = thr | jnp.left_shift(jnp.int32(1), 30 - it)
            return jnp.where(count(key_ref[:, 0:n] >= cand) >= n_sel, cand, thr)

        thr = lax.fori_loop(0, 31, bit_body, thr0)
        need = n_sel - count(key_ref[:, 0:n] > thr)

        off = jnp.zeros((tq, 1), F32)
        for j in range(n // tq):
            kj = key_ref[:, j * tq:(j + 1) * tq]
            eq = kj == thr
            eqf = jnp.where(eq, 1.0, 0.0)
            rank = _dot(eqf.astype(BF16), tri) + off
            take = jnp.where(kj > thr, 1.0, jnp.where(eq, jnp.where(rank < need, 1.0, 0.0), 0.0))
            valid = col + j * tq <= row_g
            madd_ref[:, j * tq:(j + 1) * tq] = jnp.where(valid, jnp.where(take > 0.0, 0.0, NEG_INF), NEG_INF)
            off = off + jnp.sum(eqf, axis=-1, keepdims=True)

        q = q_ref[...] * scale
        for p in range(DSA_HEADS // 2):
            qg = q[:, p * LANES:(p + 1) * LANES]
            g = p // (DSA_HEADS // DSA_KV_HEADS // 2)
            k = kd_ref[0:n, g * LANES:(g + 1) * LANES]
            v = vd_ref[0:n, g * LANES:(g + 1) * LANES]
            outs = []
            for e in range(2):
                qe = jnp.where(half == e, qg, jnp.zeros_like(qg))
                outs.append(_softmax_pv(_dot_nt(qe, k) + madd_ref[:, 0:n], v))
            o_ref[:, p * LANES:(p + 1) * LANES] = jnp.where(half == 0, outs[0], outs[1]).astype(o_ref.dtype)

    for c in range(nq // span):
        pl.when(i // span == c)(functools.partial(block, (c + 1) * span * tq))


def _dsa_attention(p2, p1, z, *, batch, seq):
    tq = min(ATTN_TILE, seq)
    nq = seq // tq
    n_sel = min(DSA_TOPK, seq // 4)
    qw = DSA_HEADS * HEAD_DIM
    return pl.pallas_call(
        functools.partial(_dsa_kernel, tq=tq, n_sel=float(n_sel), scale=HEAD_DIM ** -0.5,
                          idx_scale=IDX_DIM ** -0.5, w_scale=IDX_HEADS ** -0.5),
        grid=(batch, nq),
        in_specs=[
            pl.BlockSpec((tq, qw), lambda b, i: (b * nq + i, 0)),
            pl.BlockSpec((seq, 2 * LANES), lambda b, i: (b, 2)),
            pl.BlockSpec((tq, 2 * LANES), lambda b, i: (b * nq + i, 3)),
            pl.BlockSpec((seq, LANES), lambda b, i: (b, 8)),
            pl.BlockSpec((seq, 2 * LANES), lambda b, i: (b, 6)),
            pl.BlockSpec((tq, LANES), lambda b, i: (b * nq + i, 0)),
        ],
        out_specs=pl.BlockSpec((tq, qw), lambda b, i: (b * nq + i, 0)),
        out_shape=jax.ShapeDtypeStruct((batch * seq, qw), BF16),
        scratch_shapes=[pltpu.VMEM((tq, seq), jnp.int32), pltpu.VMEM((tq, seq), F32)],
        compiler_params=_params("parallel", "arbitrary"),
        name="dsa_attention",
    )(p2, p2, p2, p2, p1, z)


def _hgrn_kernel(q_ref, f_ref, i_ref, g_ref, lbl_ref, gn_ref, o_ref, b_scr, *, chunk, layer):
    seq = q_ref.shape[0]
    half = _lane_half((1, LANES))
    logits = lbl_ref[...]
    pe = jnp.exp(logits - jnp.max(logits, axis=0, keepdims=True))
    prob = pe / jnp.sum(pe, axis=0, keepdims=True)
    csum = prob[0:1, :]
    for d in range(1, layer + 1):
        csum = csum + prob[d:d + 1, :]
    lb = csum - prob[0:1, :]

    rowi = lax.broadcasted_iota(jnp.int32, (chunk, 1), 0)
    xor_rc = (lax.broadcasted_iota(jnp.int32, (chunk, chunk), 0)
              ^ lax.broadcasted_iota(jnp.int32, (chunk, chunk), 1))
    same_head = (lax.broadcasted_iota(jnp.int32, (LANES, LANES), 0) // HEAD_DIM
                 == lax.broadcasted_iota(jnp.int32, (LANES, LANES), 1) // HEAD_DIM)
    ones_bd = jnp.where(same_head, 1.0, 0.0).astype(BF16)
    tri = jnp.where(lax.broadcasted_iota(jnp.int32, (chunk, chunk), 1)
                    <= lax.broadcasted_iota(jnp.int32, (chunk, chunk), 0), 1.0, 0.0).astype(BF16)
    gn = gn_ref[...]

    def chunk_body(c, state_t):
        r0 = pl.multiple_of(c * chunk, chunk)
        q = _silu(q_ref[pl.ds(r0, chunk), :].astype(F32))
        f = lb + (1.0 - lb) * _sigmoid(f_ref[pl.ds(r0, chunk), :])
        kc = 1.0 - f
        v = i_ref[pl.ds(r0, chunk), :].astype(F32)
        b = _cumsum_rows_mxu(jnp.log(f), tri)
        b_scr[...] = b
        b_last = b_scr[chunk - 1:chunk, :]

        o = _dot_nt((q * jnp.exp(b)).astype(BF16), state_t.astype(BF16))
        k2 = (kc * jnp.exp(b_last - b)).astype(BF16)
        upd = _dot_tn(v.astype(BF16), k2)
        new_state = state_t * jnp.exp(b_last) + jnp.where(same_head, upd, 0.0)

        s_tot = [jnp.zeros((chunk, chunk), F32), jnp.zeros((chunk, chunk), F32)]
        m = chunk
        while m > SUB:
            hm = m // 2
            pieces = [jnp.broadcast_to(b_scr[blk * m + hm - 1:blk * m + hm, :], (m, LANES))
                      for blk in range(chunk // m)]
            ref = pieces[0] if len(pieces) == 1 else jnp.concatenate(pieces, axis=0)
            right = (rowi % m) >= hm
            qm = jnp.where(right, q * jnp.exp(jnp.where(right, b - ref, 0.0)), 0.0).astype(BF16)
            km = jnp.where(right, 0.0, kc * jnp.exp(jnp.where(right, 0.0, ref - b))).astype(BF16)
            for e in range(2):
                s = _dot_nt(jnp.where(half == e, qm, jnp.zeros_like(qm)), km)
                s_tot[e] = s_tot[e] + (s if m == chunk else jnp.where(xor_rc < m, s, 0.0))
            m = hm
        vb = v.astype(BF16)
        for e in range(2):
            o = o + _dot(s_tot[e].astype(BF16), jnp.where(half == e, vb, jnp.zeros_like(vb)))

        for d in range(SUB):
            ok = (rowi % SUB) >= d
            bs = b if d == 0 else pltpu.roll(b, d, 0)
            ks = kc if d == 0 else pltpu.roll(kc, d, 0)
            vs = v if d == 0 else pltpu.roll(v, d, 0)
            x = jnp.where(ok, q * ks * jnp.exp(jnp.where(ok, b - bs, 0.0)), 0.0)
            o = o + _dot(x.astype(BF16), ones_bd) * vs

        sq = o * o
        s0 = jnp.sum(jnp.where(half == 0, sq, 0.0), axis=-1, keepdims=True)
        s1 = jnp.sum(jnp.where(half == 1, sq, 0.0), axis=-1, keepdims=True)
        var = jnp.where(half == 0, s0, s1) * (1.0 / HEAD_DIM)
        y = o * lax.rsqrt(var + NORM_EPS) * gn * _silu(g_ref[pl.ds(r0, chunk), :].astype(F32))
        o_ref[pl.ds(r0, chunk), :] = y.astype(o_ref.dtype)
        return new_state

    lax.fori_loop(0, seq // chunk, chunk_body, jnp.zeros((LANES, LANES), F32))


def _hgrn(h_qig, h_f, lb_logits, gn, *, batch, seq, layer):
    ng = HGRN_HEADS // 2
    chunk = min(HGRN_CHUNK, seq)
    spec = lambda off: pl.BlockSpec((seq, LANES), lambda b, g: (b, off * ng + g))
    return pl.pallas_call(
        functools.partial(_hgrn_kernel, chunk=chunk, layer=layer),
        grid=(batch, ng),
        in_specs=[spec(0), spec(0), spec(1), spec(2),
                  pl.BlockSpec((lb_logits.shape[0], LANES), lambda b, g: (0, g)),
                  pl.BlockSpec((1, LANES), lambda b, g: (0, 0))],
        out_specs=pl.BlockSpec((seq, LANES), lambda b, g: (b, g)),
        out_shape=jax.ShapeDtypeStruct((batch * seq, ng * LANES), BF16),
        scratch_shapes=[pltpu.VMEM((chunk, LANES), F32)],
        compiler_params=_params("parallel", "parallel"),
        name="hgrn2",
    )(h_qig, h_f, h_qig, h_qig, lb_logits, gn)


def _pack_pairs(x):
    hw = x.shape[1] // 2
    hi = pltpu.bitcast(x[:, :hw].astype(BF16).astype(F32), jnp.int32)
    lo = pltpu.bitcast(x[:, hw:].astype(BF16).astype(F32), jnp.int32)
    return hi | lax.shift_right_logical(lo, 16)


def _unpack_pairs(w):
    hi = pltpu.bitcast(w & jnp.int32(-65536), F32)
    lo = pltpu.bitcast(lax.shift_left(w, 16), F32)
    return hi, lo


def _store_planes(ref, words):
    pw = words.shape[1] // SC_PLANES
    for h in range(SC_PLANES):
        ref[h] = words[:, h * pw:(h + 1) * pw]


def _load_planes(ref):
    return jnp.concatenate([ref[h] for h in range(SC_PLANES)], axis=1)


def _router_kernel(a1_ref, a2_ref, w1_ref, w2_ref, x_ref, g_ref, w_ref, xo_ref, o_ref, xn_ref):
    x = x_ref[...] + _dot(a1_ref[...], w1_ref[...]) + _dot(a2_ref[...], w2_ref[...])
    xo_ref[...] = x
    var = jnp.mean(x * x, axis=-1, keepdims=True)
    xn = x * lax.rsqrt(var + NORM_EPS) * g_ref[...]
    _store_planes(xn_ref, _pack_pairs(xn))
    logits = jnp.dot(xn, w_ref[...], precision=lax.Precision.HIGHEST, preferred_element_type=F32)
    lane = lax.broadcasted_iota(jnp.int32, logits.shape, 1)
    lg = jnp.where(lane < N_EXPERTS, logits, -jnp.inf)
    m1 = jnp.max(lg, axis=-1, keepdims=True)
    i1 = jnp.min(jnp.where(lg == m1, lane, LANES), axis=-1, keepdims=True)
    lg2 = jnp.where(lane == i1, -jnp.inf, lg)
    m2 = jnp.max(lg2, axis=-1, keepdims=True)
    i2 = jnp.min(jnp.where(lg2 == m2, lane, LANES), axis=-1, keepdims=True)
    t = jnp.exp(m2 - m1)
    den = 1.0 + t
    o_ref[...] = (jnp.where(lane == 0, i1.astype(F32), 0.0) + jnp.where(lane == 1, i2.astype(F32), 0.0)
                  + jnp.where(lane == 2, 1.0 / den, 0.0) + jnp.where(lane == 3, t / den, 0.0))


def _mix_router(a1, a2, w1, w2, x, g, w_pad):
    t, d = x.shape
    k1, k2 = a1.shape[1], a2.shape[1]
    tm = ROW_TILE
    return pl.pallas_call(
        _router_kernel,
        grid=(t // tm,),
        in_specs=[pl.BlockSpec((tm, k1), lambda i: (i, 0)),
                  pl.BlockSpec((tm, k2), lambda i: (i, 0)),
                  pl.BlockSpec((k1, d), lambda i: (0, 0)),
                  pl.BlockSpec((k2, d), lambda i: (0, 0)),
                  pl.BlockSpec((tm, d), lambda i: (i, 0)),
                  pl.BlockSpec((1, d), lambda i: (0, 0)),
                  pl.BlockSpec((d, LANES), lambda i: (0, 0))],
        out_specs=[pl.BlockSpec((tm, d), lambda i: (i, 0)),
                   pl.BlockSpec((tm, LANES), lambda i: (i, 0)),
                   pl.BlockSpec((SC_PLANES, tm, d // 2 // SC_PLANES), lambda i: (0, i, 0))],
        out_shape=[jax.ShapeDtypeStruct((t, d), F32),
                   jax.ShapeDtypeStruct((t, LANES), F32),
                   jax.ShapeDtypeStruct((SC_PLANES, t, d // 2 // SC_PLANES), jnp.int32)],
        compiler_params=_params("parallel"),
        name="mix_router",
    )(a1, a2, w1, w2, x, g, w_pad)


def _sc_mesh():
    return plsc.VectorSubcoreMesh(core_axis_name="core", subcore_axis_name="subcore")


def _sc_scatter_rows(x, idx0, idx1, n_out):
    n, d = x.shape

    @functools.partial(pl.kernel, out_type=jax.ShapeDtypeStruct((n_out, d), x.dtype), mesh=_sc_mesh(),
                       scratch_types=[])
    def scatter(x_hbm, i0_hbm, i1_hbm, o_hbm):
        def body(x_vmem, i0_vmem, i1_vmem):
            pltpu.sync_copy(x_vmem, o_hbm.at[i0_vmem.at[0]])
            pltpu.sync_copy(x_vmem, o_hbm.at[i1_vmem.at[0]])

        pltpu.emit_pipeline(
            body,
            grid=(n // SC_WINDOW,),
            in_specs=[pl.BlockSpec((SC_WINDOW, d), index_map=lambda i: (i, 0)),
                      pl.BlockSpec((1, SC_WINDOW), index_map=lambda i: (0, i)),
                      pl.BlockSpec((1, SC_WINDOW), index_map=lambda i: (0, i))],
            out_specs=[],
            core_axis_name=("core", "subcore"),
            dimension_semantics=(pltpu.PARALLEL,),
        )(x_hbm, i0_hbm, i1_hbm)

    return scatter(x, idx0.reshape(1, n), idx1.reshape(1, n))


def _sc_gather_rows(x, idx):
    n = idx.shape[0]
    d = x.shape[1]

    @functools.partial(pl.kernel, out_type=jax.ShapeDtypeStruct((n, d), x.dtype), mesh=_sc_mesh(),
                       scratch_types=[])
    def gather(x_hbm, i_hbm, o_hbm):
        def body(i_vmem, o_vmem):
            pltpu.sync_copy(x_hbm.at[i_vmem.at[0]], o_vmem)

        pltpu.emit_pipeline(
            body,
            grid=(n // SC_WINDOW,),
            in_specs=[pl.BlockSpec((1, SC_WINDOW), index_map=lambda i: (0, i))],
            out_specs=[pl.BlockSpec((SC_WINDOW, d), index_map=lambda i: (i, 0))],
            core_axis_name=("core", "subcore"),
            dimension_semantics=(pltpu.PARALLEL,),
        )(i_hbm, o_hbm)

    return gather(x, idx.reshape(1, n))


def _moe_ffn_packed_kernel(te_ref, act_ref, x_ref, wg_ref, wu_ref, wd_ref, o_ref, xn_ref, acc_ref):
    i = pl.program_id(0)
    f = pl.program_id(1)
    hw = xn_ref.shape[1] // 2

    @pl.when(act_ref[i] == 1)
    def _():
        @pl.when(f == 0)
        def _():
            hi, lo = _unpack_pairs(_load_planes(x_ref))
            xn_ref[:, :hw] = hi.astype(BF16)
            xn_ref[:, hw:] = lo.astype(BF16)
            acc_ref[...] = jnp.zeros_like(acc_ref)

        x = xn_ref[...]
        mid = (_silu(_dot(x, wg_ref[0])) * _dot(x, wu_ref[0])).astype(BF16)
        acc_ref[...] += _dot(mid, wd_ref[0])

        @pl.when(f == pl.num_programs(1) - 1)
        def _():
            _store_planes(o_ref, _pack_pairs(acc_ref[...]))


def _moe_ffn_packed(tile_e, tile_act, xs, wg, wu, wd, *, tf):
    _, p, pw = xs.shape
    d = 2 * pw * SC_PLANES
    ff = wg.shape[2]
    tm = MOE_ROW_TILE
    return pl.pallas_call(
        _moe_ffn_packed_kernel,
        grid_spec=pltpu.PrefetchScalarGridSpec(
            num_scalar_prefetch=2,
            grid=(p // tm, ff // tf),
            in_specs=[pl.BlockSpec((SC_PLANES, tm, pw), lambda i, f, te, ta: (0, i, 0)),
                      pl.BlockSpec((1, d, tf), lambda i, f, te, ta: (te[i], 0, f * ta[i])),
                      pl.BlockSpec((1, d, tf), lambda i, f, te, ta: (te[i], 0, f * ta[i])),
                      pl.BlockSpec((1, tf, d), lambda i, f, te, ta: (te[i], f * ta[i], 0))],
            out_specs=pl.BlockSpec((SC_PLANES, tm, pw), lambda i, f, te, ta: (0, i, 0)),
            scratch_shapes=[pltpu.VMEM((tm, d), BF16), pltpu.VMEM((tm, d), F32)],
        ),
        out_shape=jax.ShapeDtypeStruct((SC_PLANES, p, pw), jnp.int32),
        compiler_params=_params("arbitrary", "arbitrary"),
        name="moe_ffn",
    )(tile_e, tile_act, xs, wg, wu, wd)


def _moe_sum_kernel(x_ref, info_ref, y_ref, gf_ref, o_ref, *, final_norm):
    hw = x_ref.shape[1] // 2
    info = info_ref[...]
    g1 = info[:, 2:3]
    g2 = info[:, 3:4]
    hi1, lo1 = _unpack_pairs(jnp.concatenate([y_ref[h] for h in range(SC_PLANES)], axis=1))
    hi2, lo2 = _unpack_pairs(jnp.concatenate([y_ref[SC_PLANES + h] for h in range(SC_PLANES)], axis=1))
    x = x_ref[...]
    out_a = x[:, :hw] + g1 * hi1 + g2 * hi2
    out_b = x[:, hw:] + g1 * lo1 + g2 * lo2
    if final_norm:
        ss = jnp.sum(out_a * out_a, axis=-1, keepdims=True) + jnp.sum(out_b * out_b, axis=-1, keepdims=True)
        r = lax.rsqrt(ss * (1.0 / (2 * hw)) + NORM_EPS)
        gf = gf_ref[...]
        out_a = out_a * r * gf[:, :hw]
        out_b = out_b * r * gf[:, hw:]
    o_ref[:, :hw] = out_a
    o_ref[:, hw:] = out_b


def _moe_sum(x, info, y_pairs, gf, *, final_norm):
    t, d = x.shape
    tm = ROW_TILE
    return pl.pallas_call(
        functools.partial(_moe_sum_kernel, final_norm=final_norm),
        grid=(t // tm,),
        in_specs=[pl.BlockSpec((tm, d), lambda i: (i, 0)),
                  pl.BlockSpec((tm, LANES), lambda i: (i, 0)),
                  pl.BlockSpec((2 * SC_PLANES, tm, y_pairs.shape[2]), lambda i: (0, i, 0)),
                  pl.BlockSpec((1, d), lambda i: (0, 0))],
        out_specs=pl.BlockSpec((tm, d), lambda i: (i, 0)),
        out_shape=jax.ShapeDtypeStruct((t, d), F32),
        compiler_params=_params("parallel"),
        name="moe_sum",
    )(x, info, y_pairs, gf)


def _moe_positions(info, t):
    tm = MOE_ROW_TILE
    n_e = N_EXPERTS
    a = 2 * t
    p = a + n_e * tm
    e_flat = info[:, 0:2].astype(jnp.int32).reshape(a)
    onehot = (e_flat[:, None] == jnp.arange(n_e, dtype=jnp.int32)[None, :]).astype(jnp.int32)
    csum = jnp.cumsum(onehot, axis=0)
    rank = jnp.sum(csum * onehot, axis=1) - 1
    counts = csum[-1]
    padded = ((counts + tm - 1) // tm) * tm
    ends = jnp.cumsum(padded)
    starts = ends - padded
    pos = jnp.sum(onehot * starts[None, :], axis=1) + rank
    tile_start = jnp.arange(p // tm, dtype=jnp.int32) * tm
    tile_e = jnp.minimum(jnp.sum((tile_start[:, None] >= ends[None, :]).astype(jnp.int32), axis=1), n_e - 1)
    tile_act = (tile_start < ends[-1]).astype(jnp.int32)
    return pos.astype(jnp.int32).reshape(t, 2), tile_e.astype(jnp.int32), tile_act, p


def _mix_ffn_kernel(a1_ref, a2_ref, w1_ref, w2_ref, x_ref, g_ref, wg_ref, wu_ref, wd_ref, gf_ref, o_ref,
                    *, final_norm):
    x = x_ref[...] + _dot(a1_ref[...], w1_ref[...]) + _dot(a2_ref[...], w2_ref[...])
    var = jnp.mean(x * x, axis=-1, keepdims=True)
    xn = (x * lax.rsqrt(var + NORM_EPS) * g_ref[...]).astype(BF16)
    mid = (_silu(_dot(xn, wg_ref[...])) * _dot(xn, wu_ref[...])).astype(BF16)
    out = x + _dot(mid, wd_ref[...])
    if final_norm:
        var = jnp.mean(out * out, axis=-1, keepdims=True)
        out = out * lax.rsqrt(var + NORM_EPS) * gf_ref[...]
    o_ref[...] = out


def _resident(shape):
    return pl.BlockSpec(shape, lambda i: (0,) * len(shape), pipeline_mode=pl.Buffered(1))


def _mix_ffn(a1, a2, w1, w2, x, g, wg, wu, wd, gf, *, final_norm):
    t, d = x.shape
    ff = wg.shape[1]
    k1, k2 = a1.shape[1], a2.shape[1]
    tm = ROW_TILE
    return pl.pallas_call(
        functools.partial(_mix_ffn_kernel, final_norm=final_norm),
        grid=(t // tm,),
        in_specs=[
            pl.BlockSpec((tm, k1), lambda i: (i, 0)),
            pl.BlockSpec((tm, k2), lambda i: (i, 0)),
            _resident((k1, d)),
            _resident((k2, d)),
            pl.BlockSpec((tm, d), lambda i: (i, 0)),
            _resident((1, d)),
            _resident((d, ff)),
            _resident((d, ff)),
            _resident((ff, d)),
            _resident((1, d)),
        ],
        out_specs=pl.BlockSpec((tm, d), lambda i: (i, 0)),
        out_shape=jax.ShapeDtypeStruct((t, d), F32),
        compiler_params=_params("parallel"),
        name="mix_ffn",
    )(a1, a2, w1, w2, x, g, wg, wu, wd, gf)


def _rope_tables(seq):
    half = HEAD_DIM // 2
    inv = ROPE_THETA ** (-jnp.arange(half, dtype=F32) * 2.0 / HEAD_DIM)
    ang = jnp.arange(seq, dtype=F32)[:, None] * inv[None, :]
    cos = jnp.cos(ang)
    sin = jnp.sin(ang)
    cos_t = jnp.concatenate([cos, cos, cos, cos], axis=-1)
    sin_t = jnp.concatenate([-sin, sin, -sin, sin], axis=-1)
    return cos_t, sin_t


def _dup_heads(w):
    d, n = w.shape
    w = w.reshape(d, n // HEAD_DIM, 1, HEAD_DIM)
    return jnp.broadcast_to(w, (d, n // HEAD_DIM, 2, HEAD_DIM)).reshape(d, 2 * n)


def kernel(x, ln_mix_e, w_in_e, b_fox_f, w_out_e, ln_ffn_e, w_ffn_gate, w_ffn_up, w_ffn_down,
           ln_mix_o, w_in_o, hgrn_lb_logits, hgrn_norm_g, diff_lambda, diff_norm_g, w_out_o,
           ln_ffn_o, w_router, w_exp_gate, w_exp_up, w_exp_down, ln_final):
    batch, seq, d = x.shape
    t = batch * seq
    depth = ln_mix_e.shape[0] + ln_mix_o.shape[0]
    xs = x.reshape(t, d)
    cos_t, sin_t = _rope_tables(seq)
    row = lambda v: v.reshape(1, -1)
    fw = FOX_HEADS * HEAD_DIM
    dw = DSA_HEADS * HEAD_DIM
    kvw = DSA_KV_HEADS * HEAD_DIM
    hw = HGRN_HEADS * HEAD_DIM
    qw = DIFF_HEADS * 2 * DIFF_DIM

    for layer in range(depth):
        j = layer // 2
        if layer % 2 == 0:
            w = w_in_e[j]
            o = 0
            fq, fk, fv = w[:, o:o + fw], w[:, o + fw:o + 2 * fw], w[:, o + 2 * fw:o + 3 * fw]
            o += 3 * fw
            ffw = w[:, o:o + FOX_HEADS]
            o += FOX_HEADS
            dq = w[:, o:o + dw]
            o += dw
            dk = w[:, o:o + kvw]
            o += kvw
            dv = w[:, o:o + kvw]
            o += kvw
            iqw = w[:, o:o + IDX_HEADS * IDX_DIM]
            o += IDX_HEADS * IDX_DIM
            ikw = w[:, o:o + IDX_DIM]
            o += IDX_DIM
            iww = w[:, o:o + IDX_HEADS]
            w1 = jnp.concatenate([fq, fk, fv, _dup_heads(dv)], axis=1).astype(BF16)
            w2 = jnp.concatenate([dq, _dup_heads(dk), iqw, _dup_heads(ikw)], axis=1).astype(BF16)
            w3 = jnp.concatenate([ffw, iww, jnp.zeros((d, LANES - FOX_HEADS - IDX_HEADS), F32)], axis=1).astype(BF16)
            g = row(ln_mix_e[j])
            p1, p2, z = _rms_proj(xs, g, cos_t, sin_t, (w1, w2, w3), (False, True, False), (BF16, BF16, F32), seq=seq)
            b_pad = jnp.concatenate([b_fox_f[j], jnp.zeros((LANES - FOX_HEADS,), F32)]).reshape(1, LANES)
            ccol, crow = _fox_gate(z, b_pad, batch=batch, seq=seq)
            fox_o = _fox_attention(p1, ccol, crow, batch=batch, seq=seq)
            dsa_o = _dsa_attention(p2, p1, z, batch=batch, seq=seq)
            wo = w_out_e[j].astype(BF16)
            xs = _mix_ffn(fox_o, dsa_o, wo[:fw], wo[fw:], xs, row(ln_ffn_e[j]), w_ffn_gate[j].astype(BF16),
                          w_ffn_up[j].astype(BF16), w_ffn_down[j].astype(BF16), row(ln_final),
                          final_norm=(layer == depth - 1))
        else:
            w = w_in_o[j]
            lam_init = 0.8 - 0.6 * math.exp(-0.3 * layer)
            g = row(ln_mix_o[j])
            w_qig = jnp.concatenate([w[:, :hw], w[:, 2 * hw:4 * hw]], axis=1).astype(BF16)
            w_f = w[:, hw:2 * hw].astype(BF16)
            w_qk = w[:, 4 * hw:4 * hw + 2 * qw].astype(BF16)
            w_v = w[:, 4 * hw + 2 * qw:].astype(BF16)
            h_qig, h_f, pqk, pv = _rms_proj(xs, g, cos_t, sin_t, (w_qig, w_f, w_qk, w_v),
                                            (False, False, True, False), (BF16, F32, BF16, BF16), seq=seq)
            gn = jnp.concatenate([hgrn_norm_g[j], hgrn_norm_g[j]]).reshape(1, LANES)
            o_h = _hgrn(h_qig, h_f, hgrn_lb_logits, gn, batch=batch, seq=seq, layer=layer)
            o_d = _diff_attention(pqk, pv, diff_lambda[j], row(diff_norm_g[j]), batch=batch, seq=seq,
                                  lam_init=lam_init)
            wo = w_out_o[j].astype(BF16)
            wr = jnp.concatenate([w_router[j], jnp.zeros((d, LANES - N_EXPERTS), F32)], axis=1)
            xs, info, xn = _mix_router(o_h, o_d, wo[:hw], wo[hw:], xs, row(ln_ffn_o[j]), wr)
            pos, tile_e, tile_act, n_rows = _moe_positions(info, t)
            pw = xn.shape[2]
            plane = jnp.arange(SC_PLANES, dtype=jnp.int32)[:, None] * n_rows
            dst0 = (plane + pos[None, :, 0]).reshape(SC_PLANES * t)
            dst1 = (plane + pos[None, :, 1]).reshape(SC_PLANES * t)
            rows = _sc_scatter_rows(xn.reshape(SC_PLANES * t, pw), dst0, dst1, SC_PLANES * n_rows)
            y = _moe_ffn_packed(tile_e, tile_act, rows.reshape(SC_PLANES, n_rows, pw), w_exp_gate[j].astype(BF16),
                                w_exp_up[j].astype(BF16), w_exp_down[j].astype(BF16), tf=1792)
            src = jnp.concatenate([(plane + pos[None, :, s]).reshape(SC_PLANES * t) for s in range(2)])
            y_pairs = _sc_gather_rows(y.reshape(SC_PLANES * n_rows, pw), src).reshape(2 * SC_PLANES, t, pw)
            xs = _moe_sum(xs, info, y_pairs, row(ln_final), final_norm=(layer == depth - 1))
    if depth % 2 == 1:
        pass
    return xs.reshape(batch, seq, d)
```

```python
import functools
import math

import jax
import jax.numpy as jnp
from jax import lax
from jax.experimental import pallas as pl
from jax.experimental.pallas import tpu as pltpu
from jax.experimental.pallas import tpu_sc as plsc

F32 = jnp.float32
BF16 = jnp.bfloat16

D_MODEL = 1024
HEAD_DIM = 64
FOX_HEADS = 8
DSA_HEADS = 8
DSA_KV_HEADS = 2
IDX_HEADS = 4
IDX_DIM = 64
DSA_TOPK = 256
HGRN_HEADS = 8
DIFF_HEADS = 4
DIFF_DIM = 64
ROPE_THETA = 10000.0
N_EXPERTS = 8
NORM_EPS = 1e-6
NEG_INF = -1e30

LANES = 128
SUBLANES = 8
VMEM_LIMIT_BYTES = 56 * 1024 * 1024

ROW_TILE = 512
ATTN_TILE = 256
HGRN_CHUNK = 256
SUB = 8
MOE_ROW_TILE = 512
SC_WINDOW = 128
SC_PLANES = 2

INT_MIN = -(2 ** 31)


def _params(*sem):
    return pltpu.CompilerParams(dimension_semantics=sem, vmem_limit_bytes=VMEM_LIMIT_BYTES)


def _dot(a, b):
    return jnp.dot(a, b, preferred_element_type=F32)


def _dot_nt(a, b):
    return lax.dot_general(a, b, (((1,), (1,)), ((), ())), preferred_element_type=F32)


def _dot_tn(a, b):
    return lax.dot_general(a, b, (((0,), (0,)), ((), ())), preferred_element_type=F32)


def _sigmoid(x):
    return 1.0 / (1.0 + jnp.exp(-x))


def _silu(x):
    return x * _sigmoid(x)


def _lane_half(shape):
    return lax.broadcasted_iota(jnp.int32, shape, len(shape) - 1) // HEAD_DIM


def _rms_proj_kernel(x_ref, g_ref, cos_ref, sin_ref, *refs, ropes):
    n = len(ropes)
    w_refs, o_refs = refs[:n], refs[n:]
    x = x_ref[...]
    var = jnp.mean(x * x, axis=-1, keepdims=True)
    xn = (x * lax.rsqrt(var + NORM_EPS) * g_ref[...]).astype(BF16)
    cos = cos_ref[...]
    sin = sin_ref[...]
    first = (lax.broadcasted_iota(jnp.int32, cos.shape, 1) % HEAD_DIM) < (HEAD_DIM // 2)
    for w_ref, o_ref, rope in zip(w_refs, o_refs, ropes):
        y = _dot(xn, w_ref[...])
        if not rope:
            o_ref[...] = y.astype(o_ref.dtype)
            continue
        for c in range(y.shape[1] // LANES):
            yc = y[:, c * LANES:(c + 1) * LANES]
            partner = jnp.where(first, pltpu.roll(yc, LANES - HEAD_DIM // 2, 1), pltpu.roll(yc, HEAD_DIM // 2, 1))
            o_ref[:, c * LANES:(c + 1) * LANES] = (yc * cos + partner * sin).astype(o_ref.dtype)


def _rms_proj(x, g, cos, sin, ws, ropes, out_dtypes, *, seq):
    t, d = x.shape
    tm = ROW_TILE
    nseq = seq // tm
    return pl.pallas_call(
        functools.partial(_rms_proj_kernel, ropes=tuple(ropes)),
        grid=(t // tm,),
        in_specs=[
            pl.BlockSpec((tm, d), lambda i: (i, 0)),
            pl.BlockSpec((1, d), lambda i: (0, 0)),
            pl.BlockSpec((tm, LANES), lambda i: (i % nseq, 0)),
            pl.BlockSpec((tm, LANES), lambda i: (i % nseq, 0)),
        ] + [pl.BlockSpec((d, w.shape[1]), lambda i: (0, 0)) for w in ws],
        out_specs=[pl.BlockSpec((tm, w.shape[1]), lambda i: (i, 0)) for w in ws],
        out_shape=[jax.ShapeDtypeStruct((t, w.shape[1]), dt) for w, dt in zip(ws, out_dtypes)],
        compiler_params=_params("parallel"),
        name="rms_proj",
    )(x, g, cos, sin, *ws)


def _cumsum_rows(x):
    rows = x.shape[0]
    row = lax.broadcasted_iota(jnp.int32, x.shape, 0)
    k = 1
    while k < rows:
        x = x + jnp.where(row >= k, pltpu.roll(x, k, 0), 0.0)
        k *= 2
    return x


def _fox_gate_kernel(z_ref, b_ref, ccol_ref, crow_ref, *, tk):
    z = z_ref[...] + b_ref[...]
    logf = jnp.minimum(z, 0.0) - jnp.log(1.0 + jnp.exp(-jnp.abs(z)))
    c = _cumsum_rows(logf)
    ccol_ref[...] = c
    for j in range(c.shape[0] // tk):
        ct = c[j * tk:(j + 1) * tk, :].T
        crow_ref[0, :, j * tk:(j + 1) * tk] = ct[:SUBLANES, :]


def _fox_gate(z, b_pad, *, batch, seq):
    tk = ATTN_TILE
    return pl.pallas_call(
        functools.partial(_fox_gate_kernel, tk=tk),
        grid=(batch,),
        in_specs=[
            pl.BlockSpec((seq, LANES), lambda b: (b, 0)),
            pl.BlockSpec((1, LANES), lambda b: (0, 0)),
        ],
        out_specs=[
            pl.BlockSpec((seq, LANES), lambda b: (b, 0)),
            pl.BlockSpec((1, SUBLANES, seq), lambda b: (b, 0, 0)),
        ],
        out_shape=[
            jax.ShapeDtypeStruct((batch * seq, LANES), F32),
            jax.ShapeDtypeStruct((batch, SUBLANES, seq), F32),
        ],
        compiler_params=_params("parallel"),
        name="fox_gate",
    )(z, b_pad)


def _softmax_pv(s, v):
    m = jnp.max(s, axis=-1, keepdims=True)
    p = jnp.exp(s - m)
    l = jnp.sum(p, axis=-1, keepdims=True)
    return _dot(p.astype(BF16), v) / l


def _causal_attend(qe, k_ref, v_ref, n, tq, row_bias=None, key_bias=None):
    causal = lax.broadcasted_iota(jnp.int32, (tq, tq), 1) <= lax.broadcasted_iota(jnp.int32, (tq, tq), 0)
    parts = []
    for lo, hi in ((0, n - tq), (n - tq, n)):
        if hi == lo:
            continue
        s = _dot_nt(qe, k_ref[lo:hi, :])
        if key_bias is not None:
            s = s - key_bias[:, lo:hi]
        if hi == n:
            s = jnp.where(causal, s, NEG_INF)
        parts.append((s, v_ref[lo:hi, :]))
    m = functools.reduce(jnp.maximum, [jnp.max(s, axis=-1, keepdims=True) for s, _ in parts])
    shift = -m if row_bias is None else row_bias - (m + row_bias)
    acc = 0.0
    l = 0.0
    for s, v in parts:
        p = jnp.exp(s + shift)
        l = l + jnp.sum(p, axis=-1, keepdims=True)
        acc = acc + _dot(p.astype(BF16), v)
    return acc / l


def _fox_kernel(q_ref, k_ref, v_ref, cc_ref, cr_ref, o_ref, *, tq, scale):
    hp = pl.program_id(1)
    seq = q_ref.shape[0]
    half = _lane_half((1, LANES))
    lane = lax.broadcasted_iota(jnp.int32, (tq, LANES), 1)
    for i in range(seq // tq):
        n = (i + 1) * tq
        q = q_ref[i * tq:n, :] * scale
        cc = cc_ref[i * tq:n, :]
        outs = []
        for e in range(2):
            h = 2 * hp + e
            qe = jnp.where(half == e, q, jnp.zeros_like(q))
            cq = jnp.sum(jnp.where(lane == h, cc, 0.0), axis=-1, keepdims=True)
            ck = cr_ref[0, pl.ds(h, 1), 0:n]
            outs.append(_causal_attend(qe, k_ref, v_ref, n, tq, row_bias=cq, key_bias=ck))
        o_ref[i * tq:n, :] = jnp.where(half == 0, outs[0], outs[1]).astype(o_ref.dtype)


def _fox_attention(p1, ccol, crow, *, batch, seq):
    tq = min(ATTN_TILE, seq)
    ng = FOX_HEADS // 2
    return pl.pallas_call(
        functools.partial(_fox_kernel, tq=tq, scale=HEAD_DIM ** -0.5),
        grid=(batch, ng),
        in_specs=[
            pl.BlockSpec((seq, LANES), lambda b, g: (b, g)),
            pl.BlockSpec((seq, LANES), lambda b, g: (b, ng + g)),
            pl.BlockSpec((seq, LANES), lambda b, g: (b, 2 * ng + g)),
            pl.BlockSpec((seq, LANES), lambda b, g: (b, 0)),
            pl.BlockSpec((1, SUBLANES, seq), lambda b, g: (b, 0, 0)),
        ],
        out_specs=pl.BlockSpec((seq, LANES), lambda b, g: (b, g)),
        out_shape=jax.ShapeDtypeStruct((batch * seq, ng * LANES), BF16),
        compiler_params=_params("parallel", "parallel"),
        name="fox_attention",
    )(p1, p1, p1, ccol, crow)


def _diff_kernel(q_ref, k_ref, v_ref, lam_ref, g_ref, o_ref, *, tq, scale, lam_init):
    seq = q_ref.shape[0]
    half = _lane_half((1, LANES))
    lv = lam_ref[...]
    lam = (jnp.exp(jnp.sum(lv[0:1, :] * lv[1:2, :], axis=-1, keepdims=True))
           - jnp.exp(jnp.sum(lv[2:3, :] * lv[3:4, :], axis=-1, keepdims=True)) + lam_init)
    gn = g_ref[...] * (1.0 - lam_init)
    for i in range(seq // tq):
        n = (i + 1) * tq
        q = q_ref[i * tq:n, :] * scale
        outs = []
        for e in range(2):
            qe = jnp.where(half == e, q, jnp.zeros_like(q))
            outs.append(_causal_attend(qe, k_ref, v_ref, n, tq))
        o = outs[0] - lam * outs[1]
        var = jnp.mean(o * o, axis=-1, keepdims=True)
        o_ref[i * tq:n, :] = (o * lax.rsqrt(var + NORM_EPS) * gn).astype(o_ref.dtype)


def _diff_attention(pqk, pv, lam_vecs, g, *, batch, seq, lam_init):
    tq = min(ATTN_TILE, seq)
    ng = DIFF_HEADS
    return pl.pallas_call(
        functools.partial(_diff_kernel, tq=tq, scale=DIFF_DIM ** -0.5, lam_init=lam_init),
        grid=(batch, ng),
        in_specs=[
            pl.BlockSpec((seq, LANES), lambda b, g: (b, g)),
            pl.BlockSpec((seq, LANES), lambda b, g: (b, ng + g)),
            pl.BlockSpec((seq, LANES), lambda b, g: (b, g)),
            pl.BlockSpec(lam_vecs.shape, lambda b, g: (0, 0)),
            pl.BlockSpec((1, LANES), lambda b, g: (0, 0)),
        ],
        out_specs=pl.BlockSpec((seq, LANES), lambda b, g: (b, g)),
        out_shape=jax.ShapeDtypeStruct((batch * seq, ng * LANES), BF16),
        compiler_params=_params("parallel", "parallel"),
        name="diff_attention",
    )(pqk, pqk, pv, lam_vecs, g)


def _sortable_key(x):
    bits = pltpu.bitcast(x, jnp.int32)
    return jnp.where(bits < 0, bits ^ jnp.int32(0x7FFFFFFF), bits)


def _dsa_kernel(q_ref, kd_ref, iq_ref, ik_ref, vd_ref, z_ref, o_ref, key_ref, madd_ref,
                *, tq, n_sel, scale, idx_scale, w_scale):
    i = pl.program_id(1)
    seq = key_ref.shape[1]
    nq = seq // tq
    span = 2 if nq % 2 == 0 else 1
    half = _lane_half((1, LANES))
    row = lax.broadcasted_iota(jnp.int32, (tq, tq), 0)
    col = lax.broadcasted_iota(jnp.int32, (tq, tq), 1)
    row_g = row + i * tq
    tri = jnp.where(row < col, 1.0, 0.0).astype(BF16)

    def block(n):
        z = z_ref[...]
        iq = iq_ref[...]
        ik = ik_ref[0:n, :]
        sc = jnp.zeros((tq, n), F32)
        for h in range(IDX_HEADS):
            grp = iq[:, (h // 2) * LANES:(h // 2 + 1) * LANES]
            iq_h = jnp.where(half == h % 2, grp, jnp.zeros_like(grp))
            w_h = z[:, SUBLANES + h:SUBLANES + h + 1] * w_scale
            sc = sc + w_h * jnp.maximum(_dot_nt(iq_h, ik) * idx_scale, 0.0)
        for j in range(n // tq):
            scj = jnp.where(col + j * tq <= row_g, sc[:, j * tq:(j + 1) * tq], -jnp.inf)
            key_ref[:, j * tq:(j + 1) * tq] = _sortable_key(scj)

        def count(pred):
            return jnp.sum(jnp.where(pred, 1.0, 0.0), axis=-1, keepdims=True)

        thr0 = jnp.where(count(key_ref[:, 0:n] >= 0) >= n_sel, 0, INT_MIN).astype(jnp.int32)

        def bit_body(it, thr):
            cand = thr | jnp.left_shift(jnp.int32(1), 30 - it)
            return jnp.where(count(key_ref[:, 0:n] >= cand) >= n_sel, cand, thr)

        thr = lax.fori_loop(0, 31, bit_body, thr0)
        need = n_sel - count(key_ref[:, 0:n] > thr)

        off = jnp.zeros((tq, 1), F32)
        for j in range(n // tq):
            kj = key_ref[:, j * tq:(j + 1) * tq]
            eq = kj == thr
            eqf = jnp.where(eq, 1.0, 0.0)
            rank = _dot(eqf.astype(BF16), tri) + off
            take = jnp.where(kj > thr, 1.0, jnp.where(eq, jnp.where(rank < need, 1.0, 0.0), 0.0))
            valid = col + j * tq <= row_g
            madd_ref[:, j * tq:(j + 1) * tq] = jnp.where(valid, jnp.where(take > 0.0, 0.0, NEG_INF), NEG_INF)
            off = off + jnp.sum(eqf, axis=-1, keepdims=True)

        q = q_ref[...] * scale
        for p in range(DSA_HEADS // 2):
            qg = q[:, p * LANES:(p + 1) * LANES]
            g = p // (DSA_HEADS // DSA_KV_HEADS // 2)
            k = kd_ref[0:n, g * LANES:(g + 1) * LANES]
            v = vd_ref[0:n, g * LANES:(g + 1) * LANES]
            outs = []
            for e in range(2):
                qe = jnp.where(half == e, qg, jnp.zeros_like(qg))
                outs.append(_softmax_pv(_dot_nt(qe, k) + madd_ref[:, 0:n], v))
            o_ref[:, p * LANES:(p + 1) * LANES] = jnp.where(half == 0, outs[0], outs[1]).astype(o_ref.dtype)

    for c in range(nq // span):
        pl.when(i // span == c)(functools.partial(block, (c + 1) * span * tq))


def _dsa_attention(p2, p1, z, *, batch, seq):
    tq = min(ATTN_TILE, seq)
    nq = seq // tq
    n_sel = min(DSA_TOPK, seq // 4)
    qw = DSA_HEADS * HEAD_DIM
    return pl.pallas_call(
        functools.partial(_dsa_kernel, tq=tq, n_sel=float(n_sel), scale=HEAD_DIM ** -0.5,
                          idx_scale=IDX_DIM ** -0.5, w_scale=IDX_HEADS ** -0.5),
        grid=(batch, nq),
        in_specs=[
            pl.BlockSpec((tq, qw), lambda b, i: (b * nq + i, 0)),
            pl.BlockSpec((seq, 2 * LANES), lambda b, i: (b, 2)),
            pl.BlockSpec((tq, 2 * LANES), lambda b, i: (b * nq + i, 3)),
            pl.BlockSpec((seq, LANES), lambda b, i: (b, 8)),
            pl.BlockSpec((seq, 2 * LANES), lambda b, i: (b, 6)),
            pl.BlockSpec((tq, LANES), lambda b, i: (b * nq + i, 0)),
        ],
        out_specs=pl.BlockSpec((tq, qw), lambda b, i: (b * nq + i, 0)),
        out_shape=jax.ShapeDtypeStruct((batch * seq, qw), BF16),
        scratch_shapes=[pltpu.VMEM((tq, seq), jnp.int32), pltpu.VMEM((tq, seq), F32)],
        compiler_params=_params("parallel", "arbitrary"),
        name="dsa_attention",
    )(p2, p2, p2, p2, p1, z)


def _hgrn_kernel(q_ref, f_ref, i_ref, g_ref, lbl_ref, gn_ref, o_ref, b_scr, *, chunk, layer):
    seq = q_ref.shape[0]
    half = _lane_half((1, LANES))
    logits = lbl_ref[...]
    pe = jnp.exp(logits - jnp.max(logits, axis=0, keepdims=True))
    prob = pe / jnp.sum(pe, axis=0, keepdims=True)
    csum = prob[0:1, :]
    for d in range(1, layer + 1):
        csum = csum + prob[d:d + 1, :]
    lb = csum - prob[0:1, :]

    rowi = lax.broadcasted_iota(jnp.int32, (chunk, 1), 0)
    xor_rc = (lax.broadcasted_iota(jnp.int32, (chunk, chunk), 0)
              ^ lax.broadcasted_iota(jnp.int32, (chunk, chunk), 1))
    same_head = (lax.broadcasted_iota(jnp.int32, (LANES, LANES), 0) // HEAD_DIM
                 == lax.broadcasted_iota(jnp.int32, (LANES, LANES), 1) // HEAD_DIM)
    ones_bd = jnp.where(same_head, 1.0, 0.0).astype(BF16)
    gn = gn_ref[...]

    def chunk_body(c, state_t):
        r0 = pl.multiple_of(c * chunk, chunk)
        q = _silu(q_ref[pl.ds(r0, chunk), :].astype(F32))
        f = lb + (1.0 - lb) * _sigmoid(f_ref[pl.ds(r0, chunk), :])
        kc = 1.0 - f
        v = i_ref[pl.ds(r0, chunk), :].astype(F32)
        b = _cumsum_rows(jnp.log(f))
        b_scr[...] = b
        b_last = b_scr[chunk - 1:chunk, :]

        o = _dot_nt((q * jnp.exp(b)).astype(BF16), state_t.astype(BF16))
        k2 = (kc * jnp.exp(b_last - b)).astype(BF16)
        upd = _dot_tn(v.astype(BF16), k2)
        new_state = state_t * jnp.exp(b_last) + jnp.where(same_head, upd, 0.0)

        s_tot = [jnp.zeros((chunk, chunk), F32), jnp.zeros((chunk, chunk), F32)]
        m = chunk
        while m > SUB:
            hm = m // 2
            pieces = [jnp.broadcast_to(b_scr[blk * m + hm - 1:blk * m + hm, :], (m, LANES))
                      for blk in range(chunk // m)]
            ref = pieces[0] if len(pieces) == 1 else jnp.concatenate(pieces, axis=0)
            right = (rowi % m) >= hm
            qm = jnp.where(right, q * jnp.exp(jnp.where(right, b - ref, 0.0)), 0.0).astype(BF16)
            km = jnp.where(right, 0.0, kc * jnp.exp(jnp.where(right, 0.0, ref - b))).astype(BF16)
            for e in range(2):
                s = _dot_nt(jnp.where(half == e, qm, jnp.zeros_like(qm)), km)
                s_tot[e] = s_tot[e] + (s if m == chunk else jnp.where(xor_rc < m, s, 0.0))
            m = hm
        vb = v.astype(BF16)
        for e in range(2):
            o = o + _dot(s_tot[e].astype(BF16), jnp.where(half == e, vb, jnp.zeros_like(vb)))

        for d in range(SUB):
            ok = (rowi % SUB) >= d
            bs = b if d == 0 else pltpu.roll(b, d, 0)
            ks = kc if d == 0 else pltpu.roll(kc, d, 0)
            vs = v if d == 0 else pltpu.roll(v, d, 0)
            x = jnp.where(ok, q * ks * jnp.exp(jnp.where(ok, b - bs, 0.0)), 0.0)
            o = o + _dot(x.astype(BF16), ones_bd) * vs

        sq = o * o
        s0 = jnp.sum(jnp.where(half == 0, sq, 0.0), axis=-1, keepdims=True)
        s1 = jnp.sum(jnp.where(half == 1, sq, 0.0), axis=-1, keepdims=True)
        var = jnp.where(half == 0, s0, s1) * (1.0 / HEAD_DIM)
        y = o * lax.rsqrt(var + NORM_EPS) * gn * _silu(g_ref[pl.ds(r0, chunk), :].astype(F32))
        o_ref[pl.ds(r0, chunk), :] = y.astype(o_ref.dtype)
        return new_state

    lax.fori_loop(0, seq // chunk, chunk_body, jnp.zeros((LANES, LANES), F32))


def _hgrn(h_qig, h_f, lb_logits, gn, *, batch, seq, layer):
    ng = HGRN_HEADS // 2
    chunk = min(HGRN_CHUNK, seq)
    spec = lambda off: pl.BlockSpec((seq, LANES), lambda b, g: (b, off * ng + g))
    return pl.pallas_call(
        functools.partial(_hgrn_kernel, chunk=chunk, layer=layer),
        grid=(batch, ng),
        in_specs=[spec(0), spec(0), spec(1), spec(2),
                  pl.BlockSpec((lb_logits.shape[0], LANES), lambda b, g: (0, g)),
                  pl.BlockSpec((1, LANES), lambda b, g: (0, 0))],
        out_specs=pl.BlockSpec((seq, LANES), lambda b, g: (b, g)),
        out_shape=jax.ShapeDtypeStruct((batch * seq, ng * LANES), BF16),
        scratch_shapes=[pltpu.VMEM((chunk, LANES), F32)],
        compiler_params=_params("parallel", "parallel"),
        name="hgrn2",
    )(h_qig, h_f, h_qig, h_qig, lb_logits, gn)


def _pack_pairs(x):
    hw = x.shape[1] // 2
    hi = pltpu.bitcast(x[:, :hw].astype(BF16).astype(F32), jnp.int32)
    lo = pltpu.bitcast(x[:, hw:].astype(BF16).astype(F32), jnp.int32)
    return hi | lax.shift_right_logical(lo, 16)


def _unpack_pairs(w):
    hi = pltpu.bitcast(w & jnp.int32(-65536), F32)
    lo = pltpu.bitcast(lax.shift_left(w, 16), F32)
    return hi, lo


def _store_planes(ref, words):
    pw = words.shape[1] // SC_PLANES
    for h in range(SC_PLANES):
        ref[h] = words[:, h * pw:(h + 1) * pw]


def _load_planes(ref):
    return jnp.concatenate([ref[h] for h in range(SC_PLANES)], axis=1)


def _router_kernel(a1_ref, a2_ref, w1_ref, w2_ref, x_ref, g_ref, w_ref, xo_ref, o_ref, xn_ref):
    x = x_ref[...] + _dot(a1_ref[...], w1_ref[...]) + _dot(a2_ref[...], w2_ref[...])
    xo_ref[...] = x
    var = jnp.mean(x * x, axis=-1, keepdims=True)
    xn = x * lax.rsqrt(var + NORM_EPS) * g_ref[...]
    _store_planes(xn_ref, _pack_pairs(xn))
    x_hi = xn.astype(BF16)
    x_lo = (xn - x_hi.astype(F32)).astype(BF16)
    w = w_ref[...]
    w_hi = w.astype(BF16)
    w_lo = (w - w_hi.astype(F32)).astype(BF16)
    logits = _dot(x_hi, w_hi) + _dot(x_hi, w_lo) + _dot(x_lo, w_hi)
    lane = lax.broadcasted_iota(jnp.int32, logits.shape, 1)
    lg = jnp.where(lane < N_EXPERTS, logits, -jnp.inf)
    m1 = jnp.max(lg, axis=-1, keepdims=True)
    i1 = jnp.min(jnp.where(lg == m1, lane, LANES), axis=-1, keepdims=True)
    lg2 = jnp.where(lane == i1, -jnp.inf, lg)
    m2 = jnp.max(lg2, axis=-1, keepdims=True)
    i2 = jnp.min(jnp.where(lg2 == m2, lane, LANES), axis=-1, keepdims=True)
    t = jnp.exp(m2 - m1)
    den = 1.0 + t
    o_ref[...] = (jnp.where(lane == 0, i1.astype(F32), 0.0) + jnp.where(lane == 1, i2.astype(F32), 0.0)
                  + jnp.where(lane == 2, 1.0 / den, 0.0) + jnp.where(lane == 3, t / den, 0.0))


def _mix_router(a1, a2, w1, w2, x, g, w_pad):
    t, d = x.shape
    k1, k2 = a1.shape[1], a2.shape[1]
    tm = ROW_TILE
    return pl.pallas_call(
        _router_kernel,
        grid=(t // tm,),
        in_specs=[pl.BlockSpec((tm, k1), lambda i: (i, 0)),
                  pl.BlockSpec((tm, k2), lambda i: (i, 0)),
                  pl.BlockSpec((k1, d), lambda i: (0, 0)),
                  pl.BlockSpec((k2, d), lambda i: (0, 0)),
                  pl.BlockSpec((tm, d), lambda i: (i, 0)),
                  pl.BlockSpec((1, d), lambda i: (0, 0)),
                  pl.BlockSpec((d, LANES), lambda i: (0, 0))],
        out_specs=[pl.BlockSpec((tm, d), lambda i: (i, 0)),
                   pl.BlockSpec((tm, LANES), lambda i: (i, 0)),
                   pl.BlockSpec((SC_PLANES, tm, d // 2 // SC_PLANES), lambda i: (0, i, 0))],
        out_shape=[jax.ShapeDtypeStruct((t, d), F32),
                   jax.ShapeDtypeStruct((t, LANES), F32),
                   jax.ShapeDtypeStruct((SC_PLANES, t, d // 2 // SC_PLANES), jnp.int32)],
        compiler_params=_params("parallel"),
        name="mix_router",
    )(a1, a2, w1, w2, x, g, w_pad)


def _sc_mesh():
    return plsc.VectorSubcoreMesh(core_axis_name="core", subcore_axis_name="subcore")


def _sc_scatter_rows(x, idx0, idx1, n_out):
    n, d = x.shape

    @functools.partial(pl.kernel, out_type=jax.ShapeDtypeStruct((n_out, d), x.dtype), mesh=_sc_mesh(),
                       scratch_types=[])
    def scatter(x_hbm, i0_hbm, i1_hbm, o_hbm):
        def body(x_vmem, i0_vmem, i1_vmem):
            pltpu.sync_copy(x_vmem, o_hbm.at[i0_vmem.at[0]])
            pltpu.sync_copy(x_vmem, o_hbm.at[i1_vmem.at[0]])

        pltpu.emit_pipeline(
            body,
            grid=(n // SC_WINDOW,),
            in_specs=[pl.BlockSpec((SC_WINDOW, d), index_map=lambda i: (i, 0)),
                      pl.BlockSpec((1, SC_WINDOW), index_map=lambda i: (0, i)),
                      pl.BlockSpec((1, SC_WINDOW), index_map=lambda i: (0, i))],
            out_specs=[],
            core_axis_name=("core", "subcore"),
            dimension_semantics=(pltpu.PARALLEL,),
        )(x_hbm, i0_hbm, i1_hbm)

    return scatter(x, idx0.reshape(1, n), idx1.reshape(1, n))


def _sc_gather_rows(x, idx):
    n = idx.shape[0]
    d = x.shape[1]

    @functools.partial(pl.kernel, out_type=jax.ShapeDtypeStruct((n, d), x.dtype), mesh=_sc_mesh(),
                       scratch_types=[])
    def gather(x_hbm, i_hbm, o_hbm):
        def body(i_vmem, o_vmem):
            pltpu.sync_copy(x_hbm.at[i_vmem.at[0]], o_vmem)

        pltpu.emit_pipeline(
            body,
            grid=(n // SC_WINDOW,),
            in_specs=[pl.BlockSpec((1, SC_WINDOW), index_map=lambda i: (0, i))],
            out_specs=[pl.BlockSpec((SC_WINDOW, d), index_map=lambda i: (i, 0))],
            core_axis_name=("core", "subcore"),
            dimension_semantics=(pltpu.PARALLEL,),
        )(i_hbm, o_hbm)

    return gather(x, idx.reshape(1, n))


def _moe_ffn_packed_kernel(te_ref, act_ref, x_ref, wg_ref, wu_ref, wd_ref, o_ref, xn_ref, acc_ref):
    i = pl.program_id(0)
    f = pl.program_id(1)
    hw = xn_ref.shape[1] // 2

    @pl.when(act_ref[i] == 1)
    def _():
        @pl.when(f == 0)
        def _():
            hi, lo = _unpack_pairs(_load_planes(x_ref))
            xn_ref[:, :hw] = hi.astype(BF16)
            xn_ref[:, hw:] = lo.astype(BF16)
            acc_ref[...] = jnp.zeros_like(acc_ref)

        x = xn_ref[...]
        mid = (_silu(_dot(x, wg_ref[0])) * _dot(x, wu_ref[0])).astype(BF16)
        acc_ref[...] += _dot(mid, wd_ref[0])

        @pl.when(f == pl.num_programs(1) - 1)
        def _():
            _store_planes(o_ref, _pack_pairs(acc_ref[...]))


def _moe_ffn_packed(tile_e, tile_act, xs, wg, wu, wd, *, tf):
    _, p, pw = xs.shape
    d = 2 * pw * SC_PLANES
    ff = wg.shape[2]
    tm = MOE_ROW_TILE
    return pl.pallas_call(
        _moe_ffn_packed_kernel,
        grid_spec=pltpu.PrefetchScalarGridSpec(
            num_scalar_prefetch=2,
            grid=(p // tm, ff // tf),
            in_specs=[pl.BlockSpec((SC_PLANES, tm, pw), lambda i, f, te, ta: (0, i, 0)),
                      pl.BlockSpec((1, d, tf), lambda i, f, te, ta: (te[i], 0, f * ta[i])),
                      pl.BlockSpec((1, d, tf), lambda i, f, te, ta: (te[i], 0, f * ta[i])),
                      pl.BlockSpec((1, tf, d), lambda i, f, te, ta: (te[i], f * ta[i], 0))],
            out_specs=pl.BlockSpec((SC_PLANES, tm, pw), lambda i, f, te, ta: (0, i, 0)),
            scratch_shapes=[pltpu.VMEM((tm, d), BF16), pltpu.VMEM((tm, d), F32)],
        ),
        out_shape=jax.ShapeDtypeStruct((SC_PLANES, p, pw), jnp.int32),
        compiler_params=_params("arbitrary", "arbitrary"),
        name="moe_ffn",
    )(tile_e, tile_act, xs, wg, wu, wd)


def _moe_sum_kernel(x_ref, info_ref, y_ref, gf_ref, o_ref, *, final_norm):
    hw = x_ref.shape[1] // 2
    info = info_ref[...]
    g1 = info[:, 2:3]
    g2 = info[:, 3:4]
    hi1, lo1 = _unpack_pairs(jnp.concatenate([y_ref[h] for h in range(SC_PLANES)], axis=1))
    hi2, lo2 = _unpack_pairs(jnp.concatenate([y_ref[SC_PLANES + h] for h in range(SC_PLANES)], axis=1))
    x = x_ref[...]
    out_a = x[:, :hw] + g1 * hi1 + g2 * hi2
    out_b = x[:, hw:] + g1 * lo1 + g2 * lo2
    if final_norm:
        ss = jnp.sum(out_a * out_a, axis=-1, keepdims=True) + jnp.sum(out_b * out_b, axis=-1, keepdims=True)
        r = lax.rsqrt(ss * (1.0 / (2 * hw)) + NORM_EPS)
        gf = gf_ref[...]
        out_a = out_a * r * gf[:, :hw]
        out_b = out_b * r * gf[:, hw:]
    o_ref[:, :hw] = out_a
    o_ref[:, hw:] = out_b


def _moe_sum(x, info, y_pairs, gf, *, final_norm):
    t, d = x.shape
    tm = ROW_TILE
    return pl.pallas_call(
        functools.partial(_moe_sum_kernel, final_norm=final_norm),
        grid=(t // tm,),
        in_specs=[pl.BlockSpec((tm, d), lambda i: (i, 0)),
                  pl.BlockSpec((tm, LANES), lambda i: (i, 0)),
                  pl.BlockSpec((2 * SC_PLANES, tm, y_pairs.shape[2]), lambda i: (0, i, 0)),
                  pl.BlockSpec((1, d), lambda i: (0, 0))],
        out_specs=pl.BlockSpec((tm, d), lambda i: (i, 0)),
        out_shape=jax.ShapeDtypeStruct((t, d), F32),
        compiler_params=_params("parallel"),
        name="moe_sum",
    )(x, info, y_pairs, gf)


def _moe_positions(info, t):
    tm = MOE_ROW_TILE
    n_e = N_EXPERTS
    a = 2 * t
    p = a + n_e * tm
    e_flat = info[:, 0:2].astype(jnp.int32).reshape(a)
    onehot = (e_flat[:, None] == jnp.arange(n_e, dtype=jnp.int32)[None, :]).astype(jnp.int32)
    csum = jnp.cumsum(onehot, axis=0)
    rank = jnp.sum(csum * onehot, axis=1) - 1
    counts = csum[-1]
    padded = ((counts + tm - 1) // tm) * tm
    ends = jnp.cumsum(padded)
    starts = ends - padded
    pos = jnp.sum(onehot * starts[None, :], axis=1) + rank
    tile_start = jnp.arange(p // tm, dtype=jnp.int32) * tm
    tile_e = jnp.minimum(jnp.sum((tile_start[:, None] >= ends[None, :]).astype(jnp.int32), axis=1), n_e - 1)
    tile_act = (tile_start < ends[-1]).astype(jnp.int32)
    return pos.astype(jnp.int32).reshape(t, 2), tile_e.astype(jnp.int32), tile_act, p


def _mix_ffn_kernel(a1_ref, a2_ref, w1_ref, w2_ref, x_ref, g_ref, wg_ref, wu_ref, wd_ref, gf_ref, o_ref,
                    *, final_norm):
    x = x_ref[...] + _dot(a1_ref[...], w1_ref[...]) + _dot(a2_ref[...], w2_ref[...])
    var = jnp.mean(x * x, axis=-1, keepdims=True)
    xn = (x * lax.rsqrt(var + NORM_EPS) * g_ref[...]).astype(BF16)
    mid = (_silu(_dot(xn, wg_ref[...])) * _dot(xn, wu_ref[...])).astype(BF16)
    out = x + _dot(mid, wd_ref[...])
    if final_norm:
        var = jnp.mean(out * out, axis=-1, keepdims=True)
        out = out * lax.rsqrt(var + NORM_EPS) * gf_ref[...]
    o_ref[...] = out


def _resident(shape):
    return pl.BlockSpec(shape, lambda i: (0,) * len(shape), pipeline_mode=pl.Buffered(1))


def _mix_ffn(a1, a2, w1, w2, x, g, wg, wu, wd, gf, *, final_norm):
    t, d = x.shape
    ff = wg.shape[1]
    k1, k2 = a1.shape[1], a2.shape[1]
    tm = ROW_TILE
    return pl.pallas_call(
        functools.partial(_mix_ffn_kernel, final_norm=final_norm),
        grid=(t // tm,),
        in_specs=[
            pl.BlockSpec((tm, k1), lambda i: (i, 0)),
            pl.BlockSpec((tm, k2), lambda i: (i, 0)),
            _resident((k1, d)),
            _resident((k2, d)),
            pl.BlockSpec((tm, d), lambda i: (i, 0)),
            _resident((1, d)),
            _resident((d, ff)),
            _resident((d, ff)),
            _resident((ff, d)),
            _resident((1, d)),
        ],
        out_specs=pl.BlockSpec((tm, d), lambda i: (i, 0)),
        out_shape=jax.ShapeDtypeStruct((t, d), F32),
        compiler_params=_params("parallel"),
        name="mix_ffn",
    )(a1, a2, w1, w2, x, g, wg, wu, wd, gf)


def _rope_tables(seq):
    half = HEAD_DIM // 2
    inv = ROPE_THETA ** (-jnp.arange(half, dtype=F32) * 2.0 / HEAD_DIM)
    ang = jnp.arange(seq, dtype=F32)[:, None] * inv[None, :]
    cos = jnp.cos(ang)
    sin = jnp.sin(ang)
    cos_t = jnp.concatenate([cos, cos, cos, cos], axis=-1)
    sin_t = jnp.concatenate([-sin, sin, -sin, sin], axis=-1)
    return cos_t, sin_t


def _dup_heads(w):
    d, n = w.shape
    w = w.reshape(d, n // HEAD_DIM, 1, HEAD_DIM)
    return jnp.broadcast_to(w, (d, n // HEAD_DIM, 2, HEAD_DIM)).reshape(d, 2 * n)


def kernel(x, ln_mix_e, w_in_e, b_fox_f, w_out_e, ln_ffn_e, w_ffn_gate, w_ffn_up, w_ffn_down,
           ln_mix_o, w_in_o, hgrn_lb_logits, hgrn_norm_g, diff_lambda, diff_norm_g, w_out_o,
           ln_ffn_o, w_router, w_exp_gate, w_exp_up, w_exp_down, ln_final):
    batch, seq, d = x.shape
    t = batch * seq
    depth = ln_mix_e.shape[0] + ln_mix_o.shape[0]
    xs = x.reshape(t, d)
    cos_t, sin_t = _rope_tables(seq)
    row = lambda v: v.reshape(1, -1)
    fw = FOX_HEADS * HEAD_DIM
    dw = DSA_HEADS * HEAD_DIM
    kvw = DSA_KV_HEADS * HEAD_DIM
    hw = HGRN_HEADS * HEAD_DIM
    qw = DIFF_HEADS * 2 * DIFF_DIM

    for layer in range(depth):
        j = layer // 2
        if layer % 2 == 0:
            w = w_in_e[j]
            o = 0
            fq, fk, fv = w[:, o:o + fw], w[:, o + fw:o + 2 * fw], w[:, o + 2 * fw:o + 3 * fw]
            o += 3 * fw
            ffw = w[:, o:o + FOX_HEADS]
            o += FOX_HEADS
            dq = w[:, o:o + dw]
            o += dw
            dk = w[:, o:o + kvw]
            o += kvw
            dv = w[:, o:o + kvw]
            o += kvw
            iqw = w[:, o:o + IDX_HEADS * IDX_DIM]
            o += IDX_HEADS * IDX_DIM
            ikw = w[:, o:o + IDX_DIM]
            o += IDX_DIM
            iww = w[:, o:o + IDX_HEADS]
            w1 = jnp.concatenate([fq, fk, fv, _dup_heads(dv)], axis=1).astype(BF16)
            w2 = jnp.concatenate([dq, _dup_heads(dk), iqw, _dup_heads(ikw)], axis=1).astype(BF16)
            w3 = jnp.concatenate([ffw, iww, jnp.zeros((d, LANES - FOX_HEADS - IDX_HEADS), F32)], axis=1).astype(BF16)
            g = row(ln_mix_e[j])
            p1, p2, z = _rms_proj(xs, g, cos_t, sin_t, (w1, w2, w3), (False, True, False), (BF16, BF16, F32), seq=seq)
            b_pad = jnp.concatenate([b_fox_f[j], jnp.zeros((LANES - FOX_HEADS,), F32)]).reshape(1, LANES)
            ccol, crow = _fox_gate(z, b_pad, batch=batch, seq=seq)
            fox_o = _fox_attention(p1, ccol, crow, batch=batch, seq=seq)
            dsa_o = _dsa_attention(p2, p1, z, batch=batch, seq=seq)
            wo = w_out_e[j].astype(BF16)
            xs = _mix_ffn(fox_o, dsa_o, wo[:fw], wo[fw:], xs, row(ln_ffn_e[j]), w_ffn_gate[j].astype(BF16),
                          w_ffn_up[j].astype(BF16), w_ffn_down[j].astype(BF16), row(ln_final),
                          final_norm=(layer == depth - 1))
        else:
            w = w_in_o[j]
            lam_init = 0.8 - 0.6 * math.exp(-0.3 * layer)
            g = row(ln_mix_o[j])
            w_qig = jnp.concatenate([w[:, :hw], w[:, 2 * hw:4 * hw]], axis=1).astype(BF16)
            w_f = w[:, hw:2 * hw].astype(BF16)
            w_qk = w[:, 4 * hw:4 * hw + 2 * qw].astype(BF16)
            w_v = w[:, 4 * hw + 2 * qw:].astype(BF16)
            h_qig, h_f, pqk, pv = _rms_proj(xs, g, cos_t, sin_t, (w_qig, w_f, w_qk, w_v),
                                            (False, False, True, False), (BF16, F32, BF16, BF16), seq=seq)
            gn = jnp.concatenate([hgrn_norm_g[j], hgrn_norm_g[j]]).reshape(1, LANES)
            o_h = _hgrn(h_qig, h_f, hgrn_lb_logits, gn, batch=batch, seq=seq, layer=layer)
            o_d = _diff_attention(pqk, pv, diff_lambda[j], row(diff_norm_g[j]), batch=batch, seq=seq,
                                  lam_init=lam_init)
            wo = w_out_o[j].astype(BF16)
            wr = jnp.concatenate([w_router[j], jnp.zeros((d, LANES - N_EXPERTS), F32)], axis=1)
            xs, info, xn = _mix_router(o_h, o_d, wo[:hw], wo[hw:], xs, row(ln_ffn_o[j]), wr)
            pos, tile_e, tile_act, n_rows = _moe_positions(info, t)
            pw = xn.shape[2]
            plane = jnp.arange(SC_PLANES, dtype=jnp.int32)[:, None] * n_rows
            dst0 = (plane + pos[None, :, 0]).reshape(SC_PLANES * t)
            dst1 = (plane + pos[None, :, 1]).reshape(SC_PLANES * t)
            rows = _sc_scatter_rows(xn.reshape(SC_PLANES * t, pw), dst0, dst1, SC_PLANES * n_rows)
            y = _moe_ffn_packed(tile_e, tile_act, rows.reshape(SC_PLANES, n_rows, pw), w_exp_gate[j].astype(BF16),
                                w_exp_up[j].astype(BF16), w_exp_down[j].astype(BF16), tf=1792)
            src = jnp.concatenate([(plane + pos[None, :, s]).reshape(SC_PLANES * t) for s in range(2)])
            y_pairs = _sc_gather_rows(y.reshape(SC_PLANES * n_rows, pw), src).reshape(2 * SC_PLANES, t, pw)
            xs = _moe_sum(xs, info, y_pairs, row(ln_final), final_norm=(layer == depth - 1))
    if depth % 2 == 1:
        pass
    return xs.reshape(batch, seq, d)
```

```python
import functools
import math

import jax
import jax.numpy as jnp
from jax import lax
from jax.experimental import pallas as pl
from jax.experimental.pallas import tpu as pltpu
from jax.experimental.pallas import tpu_sc as plsc

F32 = jnp.float32
BF16 = jnp.bfloat16

D_MODEL = 1024
HEAD_DIM = 64
FOX_HEADS = 8
DSA_HEADS = 8
DSA_KV_HEADS = 2
IDX_HEADS = 4
IDX_DIM = 64
DSA_TOPK = 256
HGRN_HEADS = 8
DIFF_HEADS = 4
DIFF_DIM = 64
ROPE_THETA = 10000.0
N_EXPERTS = 8
NORM_EPS = 1e-6
NEG_INF = -1e30

LANES = 128
SUBLANES = 8
VMEM_LIMIT_BYTES = 56 * 1024 * 1024

ROW_TILE = 512
ATTN_TILE = 256
HGRN_CHUNK = 256
SUB = 8
MOE_ROW_TILE = 512
SC_WINDOW = 128
SC_PLANES = 2


def _params(*sem):
    return pltpu.CompilerParams(dimension_semantics=sem, vmem_limit_bytes=VMEM_LIMIT_BYTES)


def _dot(a, b):
    return jnp.dot(a, b, preferred_element_type=F32)


def _dot_nt(a, b):
    return lax.dot_general(a, b, (((1,), (1,)), ((), ())), preferred_element_type=F32)


def _dot_tn(a, b):
    return lax.dot_general(a, b, (((0,), (0,)), ((), ())), preferred_element_type=F32)


def _sigmoid(x):
    return 1.0 / (1.0 + jnp.exp(-x))


def _silu(x):
    return x * _sigmoid(x)


def _lane_half(shape):
    return lax.broadcasted_iota(jnp.int32, shape, len(shape) - 1) // HEAD_DIM


def _rms_proj_kernel(x_ref, g_ref, cos_ref, sin_ref, *refs, ropes):
    n = len(ropes)
    w_refs, o_refs = refs[:n], refs[n:]
    x = x_ref[...]
    var = jnp.mean(x * x, axis=-1, keepdims=True)
    xn = (x * lax.rsqrt(var + NORM_EPS) * g_ref[...]).astype(BF16)
    cos = cos_ref[...]
    sin = sin_ref[...]
    first = (lax.broadcasted_iota(jnp.int32, cos.shape, 1) % HEAD_DIM) < (HEAD_DIM // 2)
    for w_ref, o_ref, rope in zip(w_refs, o_refs, ropes):
        y = _dot(xn, w_ref[...])
        if not rope:
            o_ref[...] = y.astype(o_ref.dtype)
            continue
        for c in range(y.shape[1] // LANES):
            yc = y[:, c * LANES:(c + 1) * LANES]
            partner = jnp.where(first, pltpu.roll(yc, LANES - HEAD_DIM // 2, 1), pltpu.roll(yc, HEAD_DIM // 2, 1))
            o_ref[:, c * LANES:(c + 1) * LANES] = (yc * cos + partner * sin).astype(o_ref.dtype)


def _rms_proj(x, g, cos, sin, ws, ropes, out_dtypes, *, seq):
    t, d = x.shape
    tm = ROW_TILE
    nseq = seq // tm
    return pl.pallas_call(
        functools.partial(_rms_proj_kernel, ropes=tuple(ropes)),
        grid=(t // tm,),
        in_specs=[
            pl.BlockSpec((tm, d), lambda i: (i, 0)),
            pl.BlockSpec((1, d), lambda i: (0, 0)),
            pl.BlockSpec((tm, LANES), lambda i: (i % nseq, 0)),
            pl.BlockSpec((tm, LANES), lambda i: (i % nseq, 0)),
        ] + [pl.BlockSpec((d, w.shape[1]), lambda i: (0, 0)) for w in ws],
        out_specs=[pl.BlockSpec((tm, w.shape[1]), lambda i: (i, 0)) for w in ws],
        out_shape=[jax.ShapeDtypeStruct((t, w.shape[1]), dt) for w, dt in zip(ws, out_dtypes)],
        compiler_params=_params("parallel"),
        name="rms_proj",
    )(x, g, cos, sin, *ws)


def _cumsum_rows(x):
    rows = x.shape[0]
    row = lax.broadcasted_iota(jnp.int32, x.shape, 0)
    k = 1
    while k < rows:
        x = x + jnp.where(row >= k, pltpu.roll(x, k, 0), 0.0)
        k *= 2
    return x


def _fox_gate_kernel(z_ref, b_ref, ccol_ref, crow_ref, *, tk):
    z = z_ref[...] + b_ref[...]
    logf = jnp.minimum(z, 0.0) - jnp.log(1.0 + jnp.exp(-jnp.abs(z)))
    c = _cumsum_rows(logf)
    ccol_ref[...] = c
    for j in range(c.shape[0] // tk):
        ct = c[j * tk:(j + 1) * tk, :].T
        crow_ref[0, :, j * tk:(j + 1) * tk] = ct[:SUBLANES, :]


def _fox_gate(z, b_pad, *, batch, seq):
    tk = ATTN_TILE
    return pl.pallas_call(
        functools.partial(_fox_gate_kernel, tk=tk),
        grid=(batch,),
        in_specs=[
            pl.BlockSpec((seq, LANES), lambda b: (b, 0)),
            pl.BlockSpec((1, LANES), lambda b: (0, 0)),
        ],
        out_specs=[
            pl.BlockSpec((seq, LANES), lambda b: (b, 0)),
            pl.BlockSpec((1, SUBLANES, seq), lambda b: (b, 0, 0)),
        ],
        out_shape=[
            jax.ShapeDtypeStruct((batch * seq, LANES), F32),
            jax.ShapeDtypeStruct((batch, SUBLANES, seq), F32),
        ],
        compiler_params=_params("parallel"),
        name="fox_gate",
    )(z, b_pad)


def _softmax_pv(s, v):
    m = jnp.max(s, axis=-1, keepdims=True)
    p = jnp.exp(s - m)
    l = jnp.sum(p, axis=-1, keepdims=True)
    return _dot(p.astype(BF16), v) / l


def _causal_attend(qe, k_ref, v_ref, n, tq, row_bias=None, key_bias=None):
    causal = lax.broadcasted_iota(jnp.int32, (tq, tq), 1) <= lax.broadcasted_iota(jnp.int32, (tq, tq), 0)
    parts = []
    for lo, hi in ((0, n - tq), (n - tq, n)):
        if hi == lo:
            continue
        s = _dot_nt(qe, k_ref[lo:hi, :])
        if key_bias is not None:
            s = s - key_bias[:, lo:hi]
        if hi == n:
            s = jnp.where(causal, s, NEG_INF)
        parts.append((s, v_ref[lo:hi, :]))
    m = functools.reduce(jnp.maximum, [jnp.max(s, axis=-1, keepdims=True) for s, _ in parts])
    shift = -m if row_bias is None else row_bias - (m + row_bias)
    acc = 0.0
    l = 0.0
    for s, v in parts:
        p = jnp.exp(s + shift)
        l = l + jnp.sum(p, axis=-1, keepdims=True)
        acc = acc + _dot(p.astype(BF16), v)
    return acc / l


def _fox_kernel(q_ref, k_ref, v_ref, cc_ref, cr_ref, o_ref, *, tq, scale):
    hp = pl.program_id(1)
    seq = q_ref.shape[0]
    half = _lane_half((1, LANES))
    lane = lax.broadcasted_iota(jnp.int32, (tq, LANES), 1)
    for i in range(seq // tq):
        n = (i + 1) * tq
        q = q_ref[i * tq:n, :] * scale
        cc = cc_ref[i * tq:n, :]
        outs = []
        for e in range(2):
            h = 2 * hp + e
            qe = jnp.where(half == e, q, jnp.zeros_like(q))
            cq = jnp.sum(jnp.where(lane == h, cc, 0.0), axis=-1, keepdims=True)
            ck = cr_ref[0, pl.ds(h, 1), 0:n]
            outs.append(_causal_attend(qe, k_ref, v_ref, n, tq, row_bias=cq, key_bias=ck))
        o_ref[i * tq:n, :] = jnp.where(half == 0, outs[0], outs[1]).astype(o_ref.dtype)


def _fox_attention(p1, ccol, crow, *, batch, seq):
    tq = min(ATTN_TILE, seq)
    ng = FOX_HEADS // 2
    return pl.pallas_call(
        functools.partial(_fox_kernel, tq=tq, scale=HEAD_DIM ** -0.5),
        grid=(batch, ng),
        in_specs=[
            pl.BlockSpec((seq, LANES), lambda b, g: (b, g)),
            pl.BlockSpec((seq, LANES), lambda b, g: (b, ng + g)),
            pl.BlockSpec((seq, LANES), lambda b, g: (b, 2 * ng + g)),
            pl.BlockSpec((seq, LANES), lambda b, g: (b, 0)),
            pl.BlockSpec((1, SUBLANES, seq), lambda b, g: (b, 0, 0)),
        ],
        out_specs=pl.BlockSpec((seq, LANES), lambda b, g: (b, g)),
        out_shape=jax.ShapeDtypeStruct((batch * seq, ng * LANES), BF16),
        compiler_params=_params("parallel", "parallel"),
        name="fox_attention",
    )(p1, p1, p1, ccol, crow)


def _diff_kernel(q_ref, k_ref, v_ref, lam_ref, g_ref, o_ref, *, tq, scale, lam_init):
    seq = q_ref.shape[0]
    half = _lane_half((1, LANES))
    lv = lam_ref[...]
    lam = (jnp.exp(jnp.sum(lv[0:1, :] * lv[1:2, :], axis=-1, keepdims=True))
           - jnp.exp(jnp.sum(lv[2:3, :] * lv[3:4, :], axis=-1, keepdims=True)) + lam_init)
    gn = g_ref[...] * (1.0 - lam_init)
    for i in range(seq // tq):
        n = (i + 1) * tq
        q = q_ref[i * tq:n, :] * scale
        outs = []
        for e in range(2):
            qe = jnp.where(half == e, q, jnp.zeros_like(q))
            outs.append(_causal_attend(qe, k_ref, v_ref, n, tq))
        o = outs[0] - lam * outs[1]
        var = jnp.mean(o * o, axis=-1, keepdims=True)
        o_ref[i * tq:n, :] = (o * lax.rsqrt(var + NORM_EPS) * gn).astype(o_ref.dtype)


def _diff_attention(pqk, pv, lam_vecs, g, *, batch, seq, lam_init):
    tq = min(ATTN_TILE, seq)
    ng = DIFF_HEADS
    return pl.pallas_call(
        functools.partial(_diff_kernel, tq=tq, scale=DIFF_DIM ** -0.5, lam_init=lam_init),
        grid=(batch, ng),
        in_specs=[
            pl.BlockSpec((seq, LANES), lambda b, g: (b, g)),
            pl.BlockSpec((seq, LANES), lambda b, g: (b, ng + g)),
            pl.BlockSpec((seq, LANES), lambda b, g: (b, g)),
            pl.BlockSpec(lam_vecs.shape, lambda b, g: (0, 0)),
            pl.BlockSpec((1, LANES), lambda b, g: (0, 0)),
        ],
        out_specs=pl.BlockSpec((seq, LANES), lambda b, g: (b, g)),
        out_shape=jax.ShapeDtypeStruct((batch * seq, ng * LANES), BF16),
        compiler_params=_params("parallel", "parallel"),
        name="diff_attention",
    )(pqk, pqk, pv, lam_vecs, g)


def _sortable_key(x):
    bits = pltpu.bitcast(x, jnp.int32)
    return jnp.where(bits < 0, bits ^ jnp.int32(0x7FFFFFFF), bits)


def _dsa_kernel(q_ref, kd_ref, iq_ref, ik_ref, vd_ref, z_ref, o_ref, key_ref, madd_ref, half_ref,
                *, tq, n_sel, scale, idx_scale, w_scale):
    i = pl.program_id(1)
    seq = key_ref.shape[1]
    nq = seq // tq
    span = 2 if nq % 2 == 0 else 1
    half = _lane_half((1, LANES))
    row = lax.broadcasted_iota(jnp.int32, (tq, tq), 0)
    col = lax.broadcasted_iota(jnp.int32, (tq, tq), 1)
    row_g = row + i * tq
    tri = jnp.where(row < col, 1.0, 0.0).astype(BF16)

    def block(n):
        z = z_ref[...]
        iq = iq_ref[...]
        ik = ik_ref[0:n, :]
        sc = jnp.zeros((tq, n), F32)
        for h in range(IDX_HEADS):
            grp = iq[:, (h // 2) * LANES:(h // 2 + 1) * LANES]
            iq_h = jnp.where(half == h % 2, grp, jnp.zeros_like(grp))
            w_h = z[:, SUBLANES + h:SUBLANES + h + 1] * w_scale
            sc = sc + w_h * jnp.maximum(_dot_nt(iq_h, ik) * idx_scale, 0.0)
        for j in range(n // tq):
            scj = jnp.where(col + j * tq <= row_g, sc[:, j * tq:(j + 1) * tq], -jnp.inf)
            kj = _sortable_key(scj)
            key_ref[:, j * tq:(j + 1) * tq] = kj
            half_ref[1, :, j * tq:(j + 1) * tq] = lax.shift_right_arithmetic(kj, 16).astype(jnp.int16)
            half_ref[0, :, j * tq:(j + 1) * tq] = ((kj & 65535) - 32768).astype(jnp.int16)

        one = jnp.ones((tq, LANES), jnp.int16)
        zero = jnp.zeros((tq, LANES), jnp.int16)

        def count(b, pred):
            acc = zero
            for c in range(n // LANES):
                acc = acc + jnp.where(pred(half_ref[b, :, c * LANES:(c + 1) * LANES]), one, zero)
            return jnp.sum(acc.astype(jnp.int32).astype(F32), axis=-1, keepdims=True)

        above = jnp.zeros((tq, 1), F32)
        halves = []
        for b in (1, 0):
            def bit_body(it, tb, b=b, above=above):
                cand = tb + lax.shift_left(jnp.int32(1), 15 - it)
                cand_b = jnp.broadcast_to(cand, (tq, LANES)).astype(jnp.int16)
                return jnp.where(above + count(b, lambda m: m >= cand_b) >= n_sel, cand, tb)

            tb = lax.fori_loop(0, 16, bit_body, jnp.full((tq, 1), -32768, jnp.int32))
            tb_b = jnp.broadcast_to(tb, (tq, LANES)).astype(jnp.int16)
            above = above + count(b, lambda m: m > tb_b)
            if b == 1:
                for c in range(n // LANES):
                    cs = slice(c * LANES, (c + 1) * LANES)
                    half_ref[0, :, cs] = jnp.where(half_ref[1, :, cs] == tb_b, half_ref[0, :, cs], -32768 * one)
            halves.append(tb)
        thr = lax.shift_left(halves[0], 16) | ((halves[1] + 32768) & 65535)
        need = n_sel - above

        off = jnp.zeros((tq, 1), F32)
        for j in range(n // tq):
            kj = key_ref[:, j * tq:(j + 1) * tq]
            eq = kj == thr
            eqf = jnp.where(eq, 1.0, 0.0)
            rank = _dot(eqf.astype(BF16), tri) + off
            take = jnp.where(kj > thr, 1.0, jnp.where(eq, jnp.where(rank < need, 1.0, 0.0), 0.0))
            valid = col + j * tq <= row_g
            madd_ref[:, j * tq:(j + 1) * tq] = jnp.where(valid, jnp.where(take > 0.0, 0.0, NEG_INF), NEG_INF)
            off = off + jnp.sum(eqf, axis=-1, keepdims=True)

        q = q_ref[...] * scale
        for p in range(DSA_HEADS // 2):
            qg = q[:, p * LANES:(p + 1) * LANES]
            g = p // (DSA_HEADS // DSA_KV_HEADS // 2)
            k = kd_ref[0:n, g * LANES:(g + 1) * LANES]
            v = vd_ref[0:n, g * LANES:(g + 1) * LANES]
            outs = []
            for e in range(2):
                qe = jnp.where(half == e, qg, jnp.zeros_like(qg))
                outs.append(_softmax_pv(_dot_nt(qe, k) + madd_ref[:, 0:n], v))
            o_ref[:, p * LANES:(p + 1) * LANES] = jnp.where(half == 0, outs[0], outs[1]).astype(o_ref.dtype)

    for c in range(nq // span):
        pl.when(i // span == c)(functools.partial(block, (c + 1) * span * tq))


def _dsa_attention(p2, p1, z, *, batch, seq):
    tq = min(ATTN_TILE, seq)
    nq = seq // tq
    n_sel = min(DSA_TOPK, seq // 4)
    qw = DSA_HEADS * HEAD_DIM
    return pl.pallas_call(
        functools.partial(_dsa_kernel, tq=tq, n_sel=float(n_sel), scale=HEAD_DIM ** -0.5,
                          idx_scale=IDX_DIM ** -0.5, w_scale=IDX_HEADS ** -0.5),
        grid=(batch, nq),
        in_specs=[
            pl.BlockSpec((tq, qw), lambda b, i: (b * nq + i, 0)),
            pl.BlockSpec((seq, 2 * LANES), lambda b, i: (b, 2)),
            pl.BlockSpec((tq, 2 * LANES), lambda b, i: (b * nq + i, 3)),
            pl.BlockSpec((seq, LANES), lambda b, i: (b, 8)),
            pl.BlockSpec((seq, 2 * LANES), lambda b, i: (b, 6)),
            pl.BlockSpec((tq, LANES), lambda b, i: (b * nq + i, 0)),
        ],
        out_specs=pl.BlockSpec((tq, qw), lambda b, i: (b * nq + i, 0)),
        out_shape=jax.ShapeDtypeStruct((batch * seq, qw), BF16),
        scratch_shapes=[pltpu.VMEM((tq, seq), jnp.int32), pltpu.VMEM((tq, seq), F32),
                        pltpu.VMEM((2, tq, seq), jnp.int16)],
        compiler_params=_params("parallel", "arbitrary"),
        name="dsa_attention",
    )(p2, p2, p2, p2, p1, z)


def _hgrn_kernel(q_ref, f_ref, i_ref, g_ref, lbl_ref, gn_ref, o_ref, b_scr, *, chunk, layer):
    seq = q_ref.shape[0]
    half = _lane_half((1, LANES))
    logits = lbl_ref[...]
    pe = jnp.exp(logits - jnp.max(logits, axis=0, keepdims=True))
    prob = pe / jnp.sum(pe, axis=0, keepdims=True)
    csum = prob[0:1, :]
    for d in range(1, layer + 1):
        csum = csum + prob[d:d + 1, :]
    lb = csum - prob[0:1, :]

    rowi = lax.broadcasted_iota(jnp.int32, (chunk, 1), 0)
    xor_rc = (lax.broadcasted_iota(jnp.int32, (chunk, chunk), 0)
              ^ lax.broadcasted_iota(jnp.int32, (chunk, chunk), 1))
    same_head = (lax.broadcasted_iota(jnp.int32, (LANES, LANES), 0) // HEAD_DIM
                 == lax.broadcasted_iota(jnp.int32, (LANES, LANES), 1) // HEAD_DIM)
    ones_bd = jnp.where(same_head, 1.0, 0.0).astype(BF16)
    gn = gn_ref[...]

    def chunk_body(c, state_t):
        r0 = pl.multiple_of(c * chunk, chunk)
        q = _silu(q_ref[pl.ds(r0, chunk), :].astype(F32))
        f = lb + (1.0 - lb) * _sigmoid(f_ref[pl.ds(r0, chunk), :])
        kc = 1.0 - f
        v = i_ref[pl.ds(r0, chunk), :].astype(F32)
        b = _cumsum_rows(jnp.log(f))
        b_scr[...] = b
        b_last = b_scr[chunk - 1:chunk, :]

        o = _dot_nt((q * jnp.exp(b)).astype(BF16), state_t.astype(BF16))
        k2 = (kc * jnp.exp(b_last - b)).astype(BF16)
        upd = _dot_tn(v.astype(BF16), k2)
        new_state = state_t * jnp.exp(b_last) + jnp.where(same_head, upd, 0.0)

        s_tot = [jnp.zeros((chunk, chunk), F32), jnp.zeros((chunk, chunk), F32)]
        m = chunk
        while m > SUB:
            hm = m // 2
            pieces = [jnp.broadcast_to(b_scr[blk * m + hm - 1:blk * m + hm, :], (m, LANES))
                      for blk in range(chunk // m)]
            ref = pieces[0] if len(pieces) == 1 else jnp.concatenate(pieces, axis=0)
            right = (rowi % m) >= hm
            qm = jnp.where(right, q * jnp.exp(jnp.where(right, b - ref, 0.0)), 0.0).astype(BF16)
            km = jnp.where(right, 0.0, kc * jnp.exp(jnp.where(right, 0.0, ref - b))).astype(BF16)
            for e in range(2):
                s = _dot_nt(jnp.where(half == e, qm, jnp.zeros_like(qm)), km)
                s_tot[e] = s_tot[e] + (s if m == chunk else jnp.where(xor_rc < m, s, 0.0))
            m = hm
        vb = v.astype(BF16)
        for e in range(2):
            o = o + _dot(s_tot[e].astype(BF16), jnp.where(half == e, vb, jnp.zeros_like(vb)))

        for d in range(SUB):
            ok = (rowi % SUB) >= d
            bs = b if d == 0 else pltpu.roll(b, d, 0)
            ks = kc if d == 0 else pltpu.roll(kc, d, 0)
            vs = v if d == 0 else pltpu.roll(v, d, 0)
            x = jnp.where(ok, q * ks * jnp.exp(jnp.where(ok, b - bs, 0.0)), 0.0)
            o = o + _dot(x.astype(BF16), ones_bd) * vs

        sq = o * o
        s0 = jnp.sum(jnp.where(half == 0, sq, 0.0), axis=-1, keepdims=True)
        s1 = jnp.sum(jnp.where(half == 1, sq, 0.0), axis=-1, keepdims=True)
        var = jnp.where(half == 0, s0, s1) * (1.0 / HEAD_DIM)
        y = o * lax.rsqrt(var + NORM_EPS) * gn * _silu(g_ref[pl.ds(r0, chunk), :].astype(F32))
        o_ref[pl.ds(r0, chunk), :] = y.astype(o_ref.dtype)
        return new_state

    lax.fori_loop(0, seq // chunk, chunk_body, jnp.zeros((LANES, LANES), F32))


def _hgrn(h_qig, h_f, lb_logits, gn, *, batch, seq, layer):
    ng = HGRN_HEADS // 2
    chunk = min(HGRN_CHUNK, seq)
    spec = lambda off: pl.BlockSpec((seq, LANES), lambda b, g: (b, off * ng + g))
    return pl.pallas_call(
        functools.partial(_hgrn_kernel, chunk=chunk, layer=layer),
        grid=(batch, ng),
        in_specs=[spec(0), spec(0), spec(1), spec(2),
                  pl.BlockSpec((lb_logits.shape[0], LANES), lambda b, g: (0, g)),
                  pl.BlockSpec((1, LANES), lambda b, g: (0, 0))],
        out_specs=pl.BlockSpec((seq, LANES), lambda b, g: (b, g)),
        out_shape=jax.ShapeDtypeStruct((batch * seq, ng * LANES), BF16),
        scratch_shapes=[pltpu.VMEM((chunk, LANES), F32)],
        compiler_params=_params("parallel", "parallel"),
        name="hgrn2",
    )(h_qig, h_f, h_qig, h_qig, lb_logits, gn)


def _pack_pairs(x):
    hw = x.shape[1] // 2
    hi = pltpu.bitcast(x[:, :hw].astype(BF16).astype(F32), jnp.int32)
    lo = pltpu.bitcast(x[:, hw:].astype(BF16).astype(F32), jnp.int32)
    return hi | lax.shift_right_logical(lo, 16)


def _unpack_pairs(w):
    hi = pltpu.bitcast(w & jnp.int32(-65536), F32)
    lo = pltpu.bitcast(lax.shift_left(w, 16), F32)
    return hi, lo


def _store_planes(ref, words):
    pw = words.shape[1] // SC_PLANES
    for h in range(SC_PLANES):
        ref[h] = words[:, h * pw:(h + 1) * pw]


def _load_planes(ref):
    return jnp.concatenate([ref[h] for h in range(SC_PLANES)], axis=1)


def _router_kernel(a1_ref, a2_ref, w1_ref, w2_ref, x_ref, g_ref, w_ref, xo_ref, o_ref, xn_ref):
    x = x_ref[...] + _dot(a1_ref[...], w1_ref[...]) + _dot(a2_ref[...], w2_ref[...])
    xo_ref[...] = x
    var = jnp.mean(x * x, axis=-1, keepdims=True)
    xn = x * lax.rsqrt(var + NORM_EPS) * g_ref[...]
    _store_planes(xn_ref, _pack_pairs(xn))
    x_hi = xn.astype(BF16)
    x_lo = (xn - x_hi.astype(F32)).astype(BF16)
    w = w_ref[...]
    w_hi = w.astype(BF16)
    w_lo = (w - w_hi.astype(F32)).astype(BF16)
    logits = _dot(x_hi, w_hi) + _dot(x_hi, w_lo) + _dot(x_lo, w_hi)
    lane = lax.broadcasted_iota(jnp.int32, logits.shape, 1)
    lg = jnp.where(lane < N_EXPERTS, logits, -jnp.inf)
    m1 = jnp.max(lg, axis=-1, keepdims=True)
    i1 = jnp.min(jnp.where(lg == m1, lane, LANES), axis=-1, keepdims=True)
    lg2 = jnp.where(lane == i1, -jnp.inf, lg)
    m2 = jnp.max(lg2, axis=-1, keepdims=True)
    i2 = jnp.min(jnp.where(lg2 == m2, lane, LANES), axis=-1, keepdims=True)
    t = jnp.exp(m2 - m1)
    den = 1.0 + t
    o_ref[...] = (jnp.where(lane == 0, i1.astype(F32), 0.0) + jnp.where(lane == 1, i2.astype(F32), 0.0)
                  + jnp.where(lane == 2, 1.0 / den, 0.0) + jnp.where(lane == 3, t / den, 0.0))


def _mix_router(a1, a2, w1, w2, x, g, w_pad):
    t, d = x.shape
    k1, k2 = a1.shape[1], a2.shape[1]
    tm = ROW_TILE
    return pl.pallas_call(
        _router_kernel,
        grid=(t // tm,),
        in_specs=[pl.BlockSpec((tm, k1), lambda i: (i, 0)),
                  pl.BlockSpec((tm, k2), lambda i: (i, 0)),
                  pl.BlockSpec((k1, d), lambda i: (0, 0)),
                  pl.BlockSpec((k2, d), lambda i: (0, 0)),
                  pl.BlockSpec((tm, d), lambda i: (i, 0)),
                  pl.BlockSpec((1, d), lambda i: (0, 0)),
                  pl.BlockSpec((d, LANES), lambda i: (0, 0))],
        out_specs=[pl.BlockSpec((tm, d), lambda i: (i, 0)),
                   pl.BlockSpec((tm, LANES), lambda i: (i, 0)),
                   pl.BlockSpec((SC_PLANES, tm, d // 2 // SC_PLANES), lambda i: (0, i, 0))],
        out_shape=[jax.ShapeDtypeStruct((t, d), F32),
                   jax.ShapeDtypeStruct((t, LANES), F32),
                   jax.ShapeDtypeStruct((SC_PLANES, t, d // 2 // SC_PLANES), jnp.int32)],
        compiler_params=_params("parallel"),
        name="mix_router",
    )(a1, a2, w1, w2, x, g, w_pad)


def _sc_mesh():
    return plsc.VectorSubcoreMesh(core_axis_name="core", subcore_axis_name="subcore")


def _sc_scatter_rows(x, idx0, idx1, n_out):
    n, d = x.shape

    @functools.partial(pl.kernel, out_type=jax.ShapeDtypeStruct((n_out, d), x.dtype), mesh=_sc_mesh(),
                       scratch_types=[])
    def scatter(x_hbm, i0_hbm, i1_hbm, o_hbm):
        def body(x_vmem, i0_vmem, i1_vmem):
            pltpu.sync_copy(x_vmem, o_hbm.at[i0_vmem.at[0]])
            pltpu.sync_copy(x_vmem, o_hbm.at[i1_vmem.at[0]])

        pltpu.emit_pipeline(
            body,
            grid=(n // SC_WINDOW,),
            in_specs=[pl.BlockSpec((SC_WINDOW, d), index_map=lambda i: (i, 0)),
                      pl.BlockSpec((1, SC_WINDOW), index_map=lambda i: (0, i)),
                      pl.BlockSpec((1, SC_WINDOW), index_map=lambda i: (0, i))],
            out_specs=[],
            core_axis_name=("core", "subcore"),
            dimension_semantics=(pltpu.PARALLEL,),
        )(x_hbm, i0_hbm, i1_hbm)

    return scatter(x, idx0.reshape(1, n), idx1.reshape(1, n))


def _sc_gather_rows(x, idx):
    n = idx.shape[0]
    d = x.shape[1]

    @functools.partial(pl.kernel, out_type=jax.ShapeDtypeStruct((n, d), x.dtype), mesh=_sc_mesh(),
                       scratch_types=[])
    def gather(x_hbm, i_hbm, o_hbm):
        def body(i_vmem, o_vmem):
            pltpu.sync_copy(x_hbm.at[i_vmem.at[0]], o_vmem)

        pltpu.emit_pipeline(
            body,
            grid=(n // SC_WINDOW,),
            in_specs=[pl.BlockSpec((1, SC_WINDOW), index_map=lambda i: (0, i))],
            out_specs=[pl.BlockSpec((SC_WINDOW, d), index_map=lambda i: (i, 0))],
            core_axis_name=("core", "subcore"),
            dimension_semantics=(pltpu.PARALLEL,),
        )(i_hbm, o_hbm)

    return gather(x, idx.reshape(1, n))


def _moe_ffn_packed_kernel(te_ref, act_ref, x_ref, wg_ref, wu_ref, wd_ref, o_ref, xn_ref, acc_ref):
    i = pl.program_id(0)
    f = pl.program_id(1)
    hw = xn_ref.shape[1] // 2

    @pl.when(act_ref[i] == 1)
    def _():
        @pl.when(f == 0)
        def _():
            hi, lo = _unpack_pairs(_load_planes(x_ref))
            xn_ref[:, :hw] = hi.astype(BF16)
            xn_ref[:, hw:] = lo.astype(BF16)
            acc_ref[...] = jnp.zeros_like(acc_ref)

        x = xn_ref[...]
        mid = (_silu(_dot(x, wg_ref[0])) * _dot(x, wu_ref[0])).astype(BF16)
        acc_ref[...] += _dot(mid, wd_ref[0])

        @pl.when(f == pl.num_programs(1) - 1)
        def _():
            _store_planes(o_ref, _pack_pairs(acc_ref[...]))


def _moe_ffn_packed(tile_e, tile_act, xs, wg, wu, wd, *, tf):
    _, p, pw = xs.shape
    d = 2 * pw * SC_PLANES
    ff = wg.shape[2]
    tm = MOE_ROW_TILE
    return pl.pallas_call(
        _moe_ffn_packed_kernel,
        grid_spec=pltpu.PrefetchScalarGridSpec(
            num_scalar_prefetch=2,
            grid=(p // tm, ff // tf),
            in_specs=[pl.BlockSpec((SC_PLANES, tm, pw), lambda i, f, te, ta: (0, i, 0)),
                      pl.BlockSpec((1, d, tf), lambda i, f, te, ta: (te[i], 0, f * ta[i])),
                      pl.BlockSpec((1, d, tf), lambda i, f, te, ta: (te[i], 0, f * ta[i])),
                      pl.BlockSpec((1, tf, d), lambda i, f, te, ta: (te[i], f * ta[i], 0))],
            out_specs=pl.BlockSpec((SC_PLANES, tm, pw), lambda i, f, te, ta: (0, i, 0)),
            scratch_shapes=[pltpu.VMEM((tm, d), BF16), pltpu.VMEM((tm, d), F32)],
        ),
        out_shape=jax.ShapeDtypeStruct((SC_PLANES, p, pw), jnp.int32),
        compiler_params=_params("arbitrary", "arbitrary"),
        name="moe_ffn",
    )(tile_e, tile_act, xs, wg, wu, wd)


def _moe_sum_kernel(x_ref, info_ref, y_ref, gf_ref, o_ref, *, final_norm):
    hw = x_ref.shape[1] // 2
    info = info_ref[...]
    g1 = info[:, 2:3]
    g2 = info[:, 3:4]
    hi1, lo1 = _unpack_pairs(jnp.concatenate([y_ref[h] for h in range(SC_PLANES)], axis=1))
    hi2, lo2 = _unpack_pairs(jnp.concatenate([y_ref[SC_PLANES + h] for h in range(SC_PLANES)], axis=1))
    x = x_ref[...]
    out_a = x[:, :hw] + g1 * hi1 + g2 * hi2
    out_b = x[:, hw:] + g1 * lo1 + g2 * lo2
    if final_norm:
        ss = jnp.sum(out_a * out_a, axis=-1, keepdims=True) + jnp.sum(out_b * out_b, axis=-1, keepdims=True)
        r = lax.rsqrt(ss * (1.0 / (2 * hw)) + NORM_EPS)
        gf = gf_ref[...]
        out_a = out_a * r * gf[:, :hw]
        out_b = out_b * r * gf[:, hw:]
    o_ref[:, :hw] = out_a
    o_ref[:, hw:] = out_b


def _moe_sum(x, info, y_pairs, gf, *, final_norm):
    t, d = x.shape
    tm = ROW_TILE
    return pl.pallas_call(
        functools.partial(_moe_sum_kernel, final_norm=final_norm),
        grid=(t // tm,),
        in_specs=[pl.BlockSpec((tm, d), lambda i: (i, 0)),
                  pl.BlockSpec((tm, LANES), lambda i: (i, 0)),
                  pl.BlockSpec((2 * SC_PLANES, tm, y_pairs.shape[2]), lambda i: (0, i, 0)),
                  pl.BlockSpec((1, d), lambda i: (0, 0))],
        out_specs=pl.BlockSpec((tm, d), lambda i: (i, 0)),
        out_shape=jax.ShapeDtypeStruct((t, d), F32),
        compiler_params=_params("parallel"),
        name="moe_sum",
    )(x, info, y_pairs, gf)


def _moe_positions(info, t):
    tm = MOE_ROW_TILE
    n_e = N_EXPERTS
    a = 2 * t
    p = a + n_e * tm
    e_flat = info[:, 0:2].astype(jnp.int32).reshape(a)
    onehot = (e_flat[:, None] == jnp.arange(n_e, dtype=jnp.int32)[None, :]).astype(jnp.int32)
    csum = jnp.cumsum(onehot, axis=0)
    rank = jnp.sum(csum * onehot, axis=1) - 1
    counts = csum[-1]
    padded = ((counts + tm - 1) // tm) * tm
    ends = jnp.cumsum(padded)
    starts = ends - padded
    pos = jnp.sum(onehot * starts[None, :], axis=1) + rank
    tile_start = jnp.arange(p // tm, dtype=jnp.int32) * tm
    tile_e = jnp.minimum(jnp.sum((tile_start[:, None] >= ends[None, :]).astype(jnp.int32), axis=1), n_e - 1)
    tile_act = (tile_start < ends[-1]).astype(jnp.int32)
    return pos.astype(jnp.int32).reshape(t, 2), tile_e.astype(jnp.int32), tile_act, p


def _mix_ffn_kernel(a1_ref, a2_ref, w1_ref, w2_ref, x_ref, g_ref, wg_ref, wu_ref, wd_ref, gf_ref, o_ref,
                    *, final_norm):
    x = x_ref[...] + _dot(a1_ref[...], w1_ref[...]) + _dot(a2_ref[...], w2_ref[...])
    var = jnp.mean(x * x, axis=-1, keepdims=True)
    xn = (x * lax.rsqrt(var + NORM_EPS) * g_ref[...]).astype(BF16)
    mid = (_silu(_dot(xn, wg_ref[...])) * _dot(xn, wu_ref[...])).astype(BF16)
    out = x + _dot(mid, wd_ref[...])
    if final_norm:
        var = jnp.mean(out * out, axis=-1, keepdims=True)
        out = out * lax.rsqrt(var + NORM_EPS) * gf_ref[...]
    o_ref[...] = out


def _resident(shape):
    return pl.BlockSpec(shape, lambda i: (0,) * len(shape), pipeline_mode=pl.Buffered(1))


def _mix_ffn(a1, a2, w1, w2, x, g, wg, wu, wd, gf, *, final_norm):
    t, d = x.shape
    ff = wg.shape[1]
    k1, k2 = a1.shape[1], a2.shape[1]
    tm = ROW_TILE
    return pl.pallas_call(
        functools.partial(_mix_ffn_kernel, final_norm=final_norm),
        grid=(t // tm,),
        in_specs=[
            pl.BlockSpec((tm, k1), lambda i: (i, 0)),
            pl.BlockSpec((tm, k2), lambda i: (i, 0)),
            _resident((k1, d)),
            _resident((k2, d)),
            pl.BlockSpec((tm, d), lambda i: (i, 0)),
            _resident((1, d)),
            _resident((d, ff)),
            _resident((d, ff)),
            _resident((ff, d)),
            _resident((1, d)),
        ],
        out_specs=pl.BlockSpec((tm, d), lambda i: (i, 0)),
        out_shape=jax.ShapeDtypeStruct((t, d), F32),
        compiler_params=_params("parallel"),
        name="mix_ffn",
    )(a1, a2, w1, w2, x, g, wg, wu, wd, gf)


def _rope_tables(seq):
    half = HEAD_DIM // 2
    inv = ROPE_THETA ** (-jnp.arange(half, dtype=F32) * 2.0 / HEAD_DIM)
    ang = jnp.arange(seq, dtype=F32)[:, None] * inv[None, :]
    cos = jnp.cos(ang)
    sin = jnp.sin(ang)
    cos_t = jnp.concatenate([cos, cos, cos, cos], axis=-1)
    sin_t = jnp.concatenate([-sin, sin, -sin, sin], axis=-1)
    return cos_t, sin_t


def _dup_heads(w):
    d, n = w.shape
    w = w.reshape(d, n // HEAD_DIM, 1, HEAD_DIM)
    return jnp.broadcast_to(w, (d, n // HEAD_DIM, 2, HEAD_DIM)).reshape(d, 2 * n)


def kernel(x, ln_mix_e, w_in_e, b_fox_f, w_out_e, ln_ffn_e, w_ffn_gate, w_ffn_up, w_ffn_down,
           ln_mix_o, w_in_o, hgrn_lb_logits, hgrn_norm_g, diff_lambda, diff_norm_g, w_out_o,
           ln_ffn_o, w_router, w_exp_gate, w_exp_up, w_exp_down, ln_final):
    batch, seq, d = x.shape
    t = batch * seq
    depth = ln_mix_e.shape[0] + ln_mix_o.shape[0]
    xs = x.reshape(t, d)
    cos_t, sin_t = _rope_tables(seq)
    row = lambda v: v.reshape(1, -1)
    fw = FOX_HEADS * HEAD_DIM
    dw = DSA_HEADS * HEAD_DIM
    kvw = DSA_KV_HEADS * HEAD_DIM
    hw = HGRN_HEADS * HEAD_DIM
    qw = DIFF_HEADS * 2 * DIFF_DIM

    for layer in range(depth):
        j = layer // 2
        if layer % 2 == 0:
            w = w_in_e[j]
            o = 0
            fq, fk, fv = w[:, o:o + fw], w[:, o + fw:o + 2 * fw], w[:, o + 2 * fw:o + 3 * fw]
            o += 3 * fw
            ffw = w[:, o:o + FOX_HEADS]
            o += FOX_HEADS
            dq = w[:, o:o + dw]
            o += dw
            dk = w[:, o:o + kvw]
            o += kvw
            dv = w[:, o:o + kvw]
            o += kvw
            iqw = w[:, o:o + IDX_HEADS * IDX_DIM]
            o += IDX_HEADS * IDX_DIM
            ikw = w[:, o:o + IDX_DIM]
            o += IDX_DIM
            iww = w[:, o:o + IDX_HEADS]
            w1 = jnp.concatenate([fq, fk, fv, _dup_heads(dv)], axis=1).astype(BF16)
            w2 = jnp.concatenate([dq, _dup_heads(dk), iqw, _dup_heads(ikw)], axis=1).astype(BF16)
            w3 = jnp.concatenate([ffw, iww, jnp.zeros((d, LANES - FOX_HEADS - IDX_HEADS), F32)], axis=1).astype(BF16)
            g = row(ln_mix_e[j])
            p1, p2, z = _rms_proj(xs, g, cos_t, sin_t, (w1, w2, w3), (False, True, False), (BF16, BF16, F32), seq=seq)
            b_pad = jnp.concatenate([b_fox_f[j], jnp.zeros((LANES - FOX_HEADS,), F32)]).reshape(1, LANES)
            ccol, crow = _fox_gate(z, b_pad, batch=batch, seq=seq)
            fox_o = _fox_attention(p1, ccol, crow, batch=batch, seq=seq)
            dsa_o = _dsa_attention(p2, p1, z, batch=batch, seq=seq)
            wo = w_out_e[j].astype(BF16)
            xs = _mix_ffn(fox_o, dsa_o, wo[:fw], wo[fw:], xs, row(ln_ffn_e[j]), w_ffn_gate[j].astype(BF16),
                          w_ffn_up[j].astype(BF16), w_ffn_down[j].astype(BF16), row(ln_final),
                          final_norm=(layer == depth - 1))
        else:
            w = w_in_o[j]
            lam_init = 0.8 - 0.6 * math.exp(-0.3 * layer)
            g = row(ln_mix_o[j])
            w_qig = jnp.concatenate([w[:, :hw], w[:, 2 * hw:4 * hw]], axis=1).astype(BF16)
            w_f = w[:, hw:2 * hw].astype(BF16)
            w_qk = w[:, 4 * hw:4 * hw + 2 * qw].astype(BF16)
            w_v = w[:, 4 * hw + 2 * qw:].astype(BF16)
            h_qig, h_f, pqk, pv = _rms_proj(xs, g, cos_t, sin_t, (w_qig, w_f, w_qk, w_v),
                                            (False, False, True, False), (BF16, F32, BF16, BF16), seq=seq)
            gn = jnp.concatenate([hgrn_norm_g[j], hgrn_norm_g[j]]).reshape(1, LANES)
            o_h = _hgrn(h_qig, h_f, hgrn_lb_logits, gn, batch=batch, seq=seq, layer=layer)
            o_d = _diff_attention(pqk, pv, diff_lambda[j], row(diff_norm_g[j]), batch=batch, seq=seq,
                                  lam_init=lam_init)
            wo = w_out_o[j].astype(BF16)
            wr = jnp.concatenate([w_router[j], jnp.zeros((d, LANES - N_EXPERTS), F32)], axis=1)
            xs, info, xn = _mix_router(o_h, o_d, wo[:hw], wo[hw:], xs, row(ln_ffn_o[j]), wr)
            pos, tile_e, tile_act, n_rows = _moe_positions(info, t)
            pw = xn.shape[2]
            plane = jnp.arange(SC_PLANES, dtype=jnp.int32)[:, None] * n_rows
            dst0 = (plane + pos[None, :, 0]).reshape(SC_PLANES * t)
            dst1 = (plane + pos[None, :, 1]).reshape(SC_PLANES * t)
            rows = _sc_scatter_rows(xn.reshape(SC_PLANES * t, pw), dst0, dst1, SC_PLANES * n_rows)
            y = _moe_ffn_packed(tile_e, tile_act, rows.reshape(SC_PLANES, n_rows, pw), w_exp_gate[j].astype(BF16),
                                w_exp_up[j].astype(BF16), w_exp_down[j].astype(BF16), tf=1792)
            src = jnp.concatenate([(plane + pos[None, :, s]).reshape(SC_PLANES * t) for s in range(2)])
            y_pairs = _sc_gather_rows(y.reshape(SC_PLANES * n_rows, pw), src).reshape(2 * SC_PLANES, t, pw)
            xs = _moe_sum(xs, info, y_pairs, row(ln_final), final_norm=(layer == depth - 1))
    if depth % 2 == 1:
        pass
    return xs.reshape(batch, seq, d)
```

```python
import functools
import math

import jax
import jax.numpy as jnp
from jax import lax
from jax.experimental import pallas as pl
from jax.experimental.pallas import tpu as pltpu
from jax.experimental.pallas import tpu_sc as plsc

F32 = jnp.float32
BF16 = jnp.bfloat16

D_MODEL = 1024
HEAD_DIM = 64
FOX_HEADS = 8
DSA_HEADS = 8
DSA_KV_HEADS = 2
IDX_HEADS = 4
IDX_DIM = 64
DSA_TOPK = 256
HGRN_HEADS = 8
DIFF_HEADS = 4
DIFF_DIM = 64
ROPE_THETA = 10000.0
N_EXPERTS = 8
NORM_EPS = 1e-6
NEG_INF = -1e30

LANES = 128
SUBLANES = 8
VMEM_LIMIT_BYTES = 56 * 1024 * 1024

ROW_TILE = 512
ATTN_TILE = 256
HGRN_CHUNK = 256
SUB = 8
MOE_ROW_TILE = 512
SC_WINDOW = 128
SC_PLANES = 2


def _params(*sem):
    return pltpu.CompilerParams(dimension_semantics=sem, vmem_limit_bytes=VMEM_LIMIT_BYTES)


def _dot(a, b):
    return jnp.dot(a, b, preferred_element_type=F32)


def _dot_nt(a, b):
    return lax.dot_general(a, b, (((1,), (1,)), ((), ())), preferred_element_type=F32)


def _dot_tn(a, b):
    return lax.dot_general(a, b, (((0,), (0,)), ((), ())), preferred_element_type=F32)


def _sigmoid(x):
    return 1.0 / (1.0 + jnp.exp(-x))


def _silu(x):
    return x * _sigmoid(x)


def _lane_half(shape):
    return lax.broadcasted_iota(jnp.int32, shape, len(shape) - 1) // HEAD_DIM


def _rms_proj_kernel(x_ref, g_ref, cos_ref, sin_ref, *refs, ropes):
    n = len(ropes)
    w_refs, o_refs = refs[:n], refs[n:]
    x = x_ref[...]
    var = jnp.mean(x * x, axis=-1, keepdims=True)
    xn = (x * lax.rsqrt(var + NORM_EPS) * g_ref[...]).astype(BF16)
    cos = cos_ref[...]
    sin = sin_ref[...]
    first = (lax.broadcasted_iota(jnp.int32, cos.shape, 1) % HEAD_DIM) < (HEAD_DIM // 2)
    for w_ref, o_ref, rope in zip(w_refs, o_refs, ropes):
        y = _dot(xn, w_ref[...])
        if not rope:
            o_ref[...] = y.astype(o_ref.dtype)
            continue
        for c in range(y.shape[1] // LANES):
            yc = y[:, c * LANES:(c + 1) * LANES]
            partner = jnp.where(first, pltpu.roll(yc, LANES - HEAD_DIM // 2, 1), pltpu.roll(yc, HEAD_DIM // 2, 1))
            o_ref[:, c * LANES:(c + 1) * LANES] = (yc * cos + partner * sin).astype(o_ref.dtype)


def _rms_proj(x, g, cos, sin, ws, ropes, out_dtypes, *, seq):
    t, d = x.shape
    tm = ROW_TILE
    nseq = seq // tm
    return pl.pallas_call(
        functools.partial(_rms_proj_kernel, ropes=tuple(ropes)),
        grid=(t // tm,),
        in_specs=[
            pl.BlockSpec((tm, d), lambda i: (i, 0)),
            pl.BlockSpec((1, d), lambda i: (0, 0)),
            pl.BlockSpec((tm, LANES), lambda i: (i % nseq, 0)),
            pl.BlockSpec((tm, LANES), lambda i: (i % nseq, 0)),
        ] + [pl.BlockSpec((d, w.shape[1]), lambda i: (0, 0)) for w in ws],
        out_specs=[pl.BlockSpec((tm, w.shape[1]), lambda i: (i, 0)) for w in ws],
        out_shape=[jax.ShapeDtypeStruct((t, w.shape[1]), dt) for w, dt in zip(ws, out_dtypes)],
        compiler_params=_params("parallel"),
        name="rms_proj",
    )(x, g, cos, sin, *ws)


def _cumsum_rows(x):
    rows = x.shape[0]
    row = lax.broadcasted_iota(jnp.int32, x.shape, 0)
    k = 1
    while k < rows:
        x = x + jnp.where(row >= k, pltpu.roll(x, k, 0), 0.0)
        k *= 2
    return x


def _fox_gate_kernel(z_ref, b_ref, ccol_ref, crow_ref, *, tk):
    z = z_ref[...] + b_ref[...]
    logf = jnp.minimum(z, 0.0) - jnp.log(1.0 + jnp.exp(-jnp.abs(z)))
    c = _cumsum_rows(logf)
    ccol_ref[...] = c
    for j in range(c.shape[0] // tk):
        ct = c[j * tk:(j + 1) * tk, :].T
        crow_ref[0, :, j * tk:(j + 1) * tk] = ct[:SUBLANES, :]


def _fox_gate(z, b_pad, *, batch, seq):
    tk = ATTN_TILE
    return pl.pallas_call(
        functools.partial(_fox_gate_kernel, tk=tk),
        grid=(batch,),
        in_specs=[
            pl.BlockSpec((seq, LANES), lambda b: (b, 0)),
            pl.BlockSpec((1, LANES), lambda b: (0, 0)),
        ],
        out_specs=[
            pl.BlockSpec((seq, LANES), lambda b: (b, 0)),
            pl.BlockSpec((1, SUBLANES, seq), lambda b: (b, 0, 0)),
        ],
        out_shape=[
            jax.ShapeDtypeStruct((batch * seq, LANES), F32),
            jax.ShapeDtypeStruct((batch, SUBLANES, seq), F32),
        ],
        compiler_params=_params("parallel"),
        name="fox_gate",
    )(z, b_pad)


def _softmax_pv(s, v):
    m = jnp.max(s, axis=-1, keepdims=True)
    p = jnp.exp(s - m)
    l = jnp.sum(p, axis=-1, keepdims=True)
    return _dot(p.astype(BF16), v) / l


def _causal_attend(qe, k_ref, v_ref, n, tq, row_bias=None, key_bias=None):
    causal = lax.broadcasted_iota(jnp.int32, (tq, tq), 1) <= lax.broadcasted_iota(jnp.int32, (tq, tq), 0)
    parts = []
    for lo, hi in ((0, n - tq), (n - tq, n)):
        if hi == lo:
            continue
        s = _dot_nt(qe, k_ref[lo:hi, :])
        if key_bias is not None:
            s = s - key_bias[:, lo:hi]
        if hi == n:
            s = jnp.where(causal, s, NEG_INF)
        parts.append((s, v_ref[lo:hi, :]))
    m = functools.reduce(jnp.maximum, [jnp.max(s, axis=-1, keepdims=True) for s, _ in parts])
    shift = -m if row_bias is None else row_bias - (m + row_bias)
    acc = 0.0
    l = 0.0
    for s, v in parts:
        p = jnp.exp(s + shift)
        l = l + jnp.sum(p, axis=-1, keepdims=True)
        acc = acc + _dot(p.astype(BF16), v)
    return acc / l


def _fox_kernel(q_ref, k_ref, v_ref, cc_ref, cr_ref, o_ref, *, tq, scale):
    hp = pl.program_id(1)
    seq = q_ref.shape[0]
    half = _lane_half((1, LANES))
    lane = lax.broadcasted_iota(jnp.int32, (tq, LANES), 1)
    for i in range(seq // tq):
        n = (i + 1) * tq
        q = q_ref[i * tq:n, :] * scale
        cc = cc_ref[i * tq:n, :]
        outs = []
        for e in range(2):
            h = 2 * hp + e
            qe = jnp.where(half == e, q, jnp.zeros_like(q))
            cq = jnp.sum(jnp.where(lane == h, cc, 0.0), axis=-1, keepdims=True)
            ck = cr_ref[0, pl.ds(h, 1), 0:n]
            outs.append(_causal_attend(qe, k_ref, v_ref, n, tq, row_bias=cq, key_bias=ck))
        o_ref[i * tq:n, :] = jnp.where(half == 0, outs[0], outs[1]).astype(o_ref.dtype)


def _fox_attention(p1, ccol, crow, *, batch, seq):
    tq = min(ATTN_TILE, seq)
    ng = FOX_HEADS // 2
    return pl.pallas_call(
        functools.partial(_fox_kernel, tq=tq, scale=HEAD_DIM ** -0.5),
        grid=(batch, ng),
        in_specs=[
            pl.BlockSpec((seq, LANES), lambda b, g: (b, g)),
            pl.BlockSpec((seq, LANES), lambda b, g: (b, ng + g)),
            pl.BlockSpec((seq, LANES), lambda b, g: (b, 2 * ng + g)),
            pl.BlockSpec((seq, LANES), lambda b, g: (b, 0)),
            pl.BlockSpec((1, SUBLANES, seq), lambda b, g: (b, 0, 0)),
        ],
        out_specs=pl.BlockSpec((seq, LANES), lambda b, g: (b, g)),
        out_shape=jax.ShapeDtypeStruct((batch * seq, ng * LANES), BF16),
        compiler_params=_params("parallel", "parallel"),
        name="fox_attention",
    )(p1, p1, p1, ccol, crow)


def _diff_kernel(q_ref, k_ref, v_ref, lam_ref, g_ref, o_ref, *, tq, scale, lam_init):
    seq = q_ref.shape[0]
    half = _lane_half((1, LANES))
    lv = lam_ref[...]
    lam = (jnp.exp(jnp.sum(lv[0:1, :] * lv[1:2, :], axis=-1, keepdims=True))
           - jnp.exp(jnp.sum(lv[2:3, :] * lv[3:4, :], axis=-1, keepdims=True)) + lam_init)
    gn = g_ref[...] * (1.0 - lam_init)
    for i in range(seq // tq):
        n = (i + 1) * tq
        q = q_ref[i * tq:n, :] * scale
        outs = []
        for e in range(2):
            qe = jnp.where(half == e, q, jnp.zeros_like(q))
            outs.append(_causal_attend(qe, k_ref, v_ref, n, tq))
        o = outs[0] - lam * outs[1]
        var = jnp.mean(o * o, axis=-1, keepdims=True)
        o_ref[i * tq:n, :] = (o * lax.rsqrt(var + NORM_EPS) * gn).astype(o_ref.dtype)


def _diff_attention(pqk, pv, lam_vecs, g, *, batch, seq, lam_init):
    tq = min(ATTN_TILE, seq)
    ng = DIFF_HEADS
    return pl.pallas_call(
        functools.partial(_diff_kernel, tq=tq, scale=DIFF_DIM ** -0.5, lam_init=lam_init),
        grid=(batch, ng),
        in_specs=[
            pl.BlockSpec((seq, LANES), lambda b, g: (b, g)),
            pl.BlockSpec((seq, LANES), lambda b, g: (b, ng + g)),
            pl.BlockSpec((seq, LANES), lambda b, g: (b, g)),
            pl.BlockSpec(lam_vecs.shape, lambda b, g: (0, 0)),
            pl.BlockSpec((1, LANES), lambda b, g: (0, 0)),
        ],
        out_specs=pl.BlockSpec((seq, LANES), lambda b, g: (b, g)),
        out_shape=jax.ShapeDtypeStruct((batch * seq, ng * LANES), BF16),
        compiler_params=_params("parallel", "parallel"),
        name="diff_attention",
    )(pqk, pqk, pv, lam_vecs, g)


def _sortable_key(x):
    bits = pltpu.bitcast(x, jnp.int32)
    return jnp.where(bits < 0, bits ^ jnp.int32(0x7FFFFFFF), bits)


def _dsa_kernel(q_ref, kd_ref, iq_ref, ik_ref, vd_ref, z_ref, o_ref, key_ref, madd_ref, half_ref,
                *, tq, n_sel, scale, idx_scale, w_scale):
    i = pl.program_id(1)
    seq = key_ref.shape[1]
    nq = seq // tq
    span = 2 if nq % 2 == 0 else 1
    half = _lane_half((1, LANES))
    row = lax.broadcasted_iota(jnp.int32, (tq, tq), 0)
    col = lax.broadcasted_iota(jnp.int32, (tq, tq), 1)
    row_g = row + i * tq
    tri = jnp.where(row < col, 1.0, 0.0).astype(BF16)

    def block(n):
        z = z_ref[...]
        iq = iq_ref[...]
        ik = ik_ref[0:n, :]
        sc = jnp.zeros((tq, n), F32)
        for h in range(IDX_HEADS):
            grp = iq[:, (h // 2) * LANES:(h // 2 + 1) * LANES]
            iq_h = jnp.where(half == h % 2, grp, jnp.zeros_like(grp))
            w_h = z[:, SUBLANES + h:SUBLANES + h + 1] * w_scale
            sc = sc + w_h * jnp.maximum(_dot_nt(iq_h, ik) * idx_scale, 0.0)
        for j in range(n // tq):
            scj = jnp.where(col + j * tq <= row_g, sc[:, j * tq:(j + 1) * tq], -jnp.inf)
            kj = _sortable_key(scj)
            key_ref[:, j * tq:(j + 1) * tq] = kj
            half_ref[1, :, j * tq:(j + 1) * tq] = lax.shift_right_arithmetic(kj, 16).astype(jnp.int16)
            half_ref[0, :, j * tq:(j + 1) * tq] = ((kj & 65535) - 32768).astype(jnp.int16)

        pack = 16
        one = jnp.ones((pack, LANES), jnp.int16)
        zero = jnp.zeros((pack, LANES), jnp.int16)

        def count(b, pred):
            accs = []
            for r in range(tq // pack):
                rs = slice(r * pack, (r + 1) * pack)
                acc = zero
                for c in range(n // LANES):
                    hit = pred(half_ref[b, rs, c * LANES:(c + 1) * LANES], rs)
                    acc = acc + jnp.where(hit, one, zero)
                accs.append(acc)
            acc = jnp.concatenate(accs, axis=0)
            return jnp.sum(acc.astype(jnp.int32).astype(F32), axis=-1, keepdims=True)

        above = jnp.zeros((tq, 1), F32)
        halves = []
        for b in (1, 0):
            def bit_body(it, tb, b=b, above=above):
                cand = tb + lax.shift_left(jnp.int32(1), 15 - it)
                cand_b = jnp.broadcast_to(cand, (tq, LANES)).astype(jnp.int16)
                return jnp.where(above + count(b, lambda m, rs: m >= cand_b[rs]) >= n_sel, cand, tb)

            tb = lax.fori_loop(0, 16, bit_body, jnp.full((tq, 1), -32768, jnp.int32))
            tb_b = jnp.broadcast_to(tb, (tq, LANES)).astype(jnp.int16)
            above = above + count(b, lambda m, rs: m > tb_b[rs])
            if b == 1:
                for r in range(tq // pack):
                    rs = slice(r * pack, (r + 1) * pack)
                    for c in range(n // LANES):
                        cs = slice(c * LANES, (c + 1) * LANES)
                        half_ref[0, rs, cs] = jnp.where(half_ref[1, rs, cs] == tb_b[rs], half_ref[0, rs, cs],
                                                        -32768 * one)
            halves.append(tb)
        thr = lax.shift_left(halves[0], 16) | ((halves[1] + 32768) & 65535)
        need = n_sel - above

        off = jnp.zeros((tq, 1), F32)
        for j in range(n // tq):
            kj = key_ref[:, j * tq:(j + 1) * tq]
            eq = kj == thr
            eqf = jnp.where(eq, 1.0, 0.0)
            rank = _dot(eqf.astype(BF16), tri) + off
            take = jnp.where(kj > thr, 1.0, jnp.where(eq, jnp.where(rank < need, 1.0, 0.0), 0.0))
            valid = col + j * tq <= row_g
            madd_ref[:, j * tq:(j + 1) * tq] = jnp.where(valid, jnp.where(take > 0.0, 0.0, NEG_INF), NEG_INF)
            off = off + jnp.sum(eqf, axis=-1, keepdims=True)

        q = q_ref[...] * scale
        for p in range(DSA_HEADS // 2):
            qg = q[:, p * LANES:(p + 1) * LANES]
            g = p // (DSA_HEADS // DSA_KV_HEADS // 2)
            k = kd_ref[0:n, g * LANES:(g + 1) * LANES]
            v = vd_ref[0:n, g * LANES:(g + 1) * LANES]
            outs = []
            for e in range(2):
                qe = jnp.where(half == e, qg, jnp.zeros_like(qg))
                outs.append(_softmax_pv(_dot_nt(qe, k) + madd_ref[:, 0:n], v))
            o_ref[:, p * LANES:(p + 1) * LANES] = jnp.where(half == 0, outs[0], outs[1]).astype(o_ref.dtype)

    for c in range(nq // span):
        pl.when(i // span == c)(functools.partial(block, (c + 1) * span * tq))


def _dsa_attention(p2, p1, z, *, batch, seq):
    tq = min(ATTN_TILE, seq)
    nq = seq // tq
    n_sel = min(DSA_TOPK, seq // 4)
    qw = DSA_HEADS * HEAD_DIM
    return pl.pallas_call(
        functools.partial(_dsa_kernel, tq=tq, n_sel=float(n_sel), scale=HEAD_DIM ** -0.5,
                          idx_scale=IDX_DIM ** -0.5, w_scale=IDX_HEADS ** -0.5),
        grid=(batch, nq),
        in_specs=[
            pl.BlockSpec((tq, qw), lambda b, i: (b * nq + i, 0)),
            pl.BlockSpec((seq, 2 * LANES), lambda b, i: (b, 2)),
            pl.BlockSpec((tq, 2 * LANES), lambda b, i: (b * nq + i, 3)),
            pl.BlockSpec((seq, LANES), lambda b, i: (b, 8)),
            pl.BlockSpec((seq, 2 * LANES), lambda b, i: (b, 6)),
            pl.BlockSpec((tq, LANES), lambda b, i: (b * nq + i, 0)),
        ],
        out_specs=pl.BlockSpec((tq, qw), lambda b, i: (b * nq + i, 0)),
        out_shape=jax.ShapeDtypeStruct((batch * seq, qw), BF16),
        scratch_shapes=[pltpu.VMEM((tq, seq), jnp.int32), pltpu.VMEM((tq, seq), F32),
                        pltpu.VMEM((2, tq, seq), jnp.int16)],
        compiler_params=_params("parallel", "arbitrary"),
        name="dsa_attention",
    )(p2, p2, p2, p2, p1, z)


def _hgrn_kernel(q_ref, f_ref, i_ref, g_ref, lbl_ref, gn_ref, o_ref, b_scr, *, chunk, layer):
    seq = q_ref.shape[0]
    half = _lane_half((1, LANES))
    logits = lbl_ref[...]
    pe = jnp.exp(logits - jnp.max(logits, axis=0, keepdims=True))
    prob = pe / jnp.sum(pe, axis=0, keepdims=True)
    csum = prob[0:1, :]
    for d in range(1, layer + 1):
        csum = csum + prob[d:d + 1, :]
    lb = csum - prob[0:1, :]

    rowi = lax.broadcasted_iota(jnp.int32, (chunk, 1), 0)
    xor_rc = (lax.broadcasted_iota(jnp.int32, (chunk, chunk), 0)
              ^ lax.broadcasted_iota(jnp.int32, (chunk, chunk), 1))
    same_head = (lax.broadcasted_iota(jnp.int32, (LANES, LANES), 0) // HEAD_DIM
                 == lax.broadcasted_iota(jnp.int32, (LANES, LANES), 1) // HEAD_DIM)
    ones_bd = jnp.where(same_head, 1.0, 0.0).astype(BF16)
    gn = gn_ref[...]

    def chunk_body(c, state_t):
        r0 = pl.multiple_of(c * chunk, chunk)
        q = _silu(q_ref[pl.ds(r0, chunk), :].astype(F32))
        f = lb + (1.0 - lb) * _sigmoid(f_ref[pl.ds(r0, chunk), :])
        kc = 1.0 - f
        v = i_ref[pl.ds(r0, chunk), :].astype(F32)
        b = _cumsum_rows(jnp.log(f))
        b_scr[...] = b
        b_last = b_scr[chunk - 1:chunk, :]

        o = _dot_nt((q * jnp.exp(b)).astype(BF16), state_t.astype(BF16))
        k2 = (kc * jnp.exp(b_last - b)).astype(BF16)
        upd = _dot_tn(v.astype(BF16), k2)
        new_state = state_t * jnp.exp(b_last) + jnp.where(same_head, upd, 0.0)

        s_tot = [jnp.zeros((chunk, chunk), F32), jnp.zeros((chunk, chunk), F32)]
        m = chunk
        while m > SUB:
            hm = m // 2
            pieces = [jnp.broadcast_to(b_scr[blk * m + hm - 1:blk * m + hm, :], (m, LANES))
                      for blk in range(chunk // m)]
            ref = pieces[0] if len(pieces) == 1 else jnp.concatenate(pieces, axis=0)
            right = (rowi % m) >= hm
            qm = jnp.where(right, q * jnp.exp(jnp.where(right, b - ref, 0.0)), 0.0).astype(BF16)
            km = jnp.where(right, 0.0, kc * jnp.exp(jnp.where(right, 0.0, ref - b))).astype(BF16)
            for e in range(2):
                s = _dot_nt(jnp.where(half == e, qm, jnp.zeros_like(qm)), km)
                s_tot[e] = s_tot[e] + (s if m == chunk else jnp.where(xor_rc < m, s, 0.0))
            m = hm
        vb = v.astype(BF16)
        for e in range(2):
            o = o + _dot(s_tot[e].astype(BF16), jnp.where(half == e, vb, jnp.zeros_like(vb)))

        for d in range(SUB):
            ok = (rowi % SUB) >= d
            bs = b if d == 0 else pltpu.roll(b, d, 0)
            ks = kc if d == 0 else pltpu.roll(kc, d, 0)
            vs = v if d == 0 else pltpu.roll(v, d, 0)
            x = jnp.where(ok, q * ks * jnp.exp(jnp.where(ok, b - bs, 0.0)), 0.0)
            o = o + _dot(x.astype(BF16), ones_bd) * vs

        sq = o * o
        s0 = jnp.sum(jnp.where(half == 0, sq, 0.0), axis=-1, keepdims=True)
        s1 = jnp.sum(jnp.where(half == 1, sq, 0.0), axis=-1, keepdims=True)
        var = jnp.where(half == 0, s0, s1) * (1.0 / HEAD_DIM)
        y = o * lax.rsqrt(var + NORM_EPS) * gn * _silu(g_ref[pl.ds(r0, chunk), :].astype(F32))
        o_ref[pl.ds(r0, chunk), :] = y.astype(o_ref.dtype)
        return new_state

    lax.fori_loop(0, seq // chunk, chunk_body, jnp.zeros((LANES, LANES), F32))


def _hgrn(h_qig, h_f, lb_logits, gn, *, batch, seq, layer):
    ng = HGRN_HEADS // 2
    chunk = min(HGRN_CHUNK, seq)
    spec = lambda off: pl.BlockSpec((seq, LANES), lambda b, g: (b, off * ng + g))
    return pl.pallas_call(
        functools.partial(_hgrn_kernel, chunk=chunk, layer=layer),
        grid=(batch, ng),
        in_specs=[spec(0), spec(0), spec(1), spec(2),
                  pl.BlockSpec((lb_logits.shape[0], LANES), lambda b, g: (0, g)),
                  pl.BlockSpec((1, LANES), lambda b, g: (0, 0))],
        out_specs=pl.BlockSpec((seq, LANES), lambda b, g: (b, g)),
        out_shape=jax.ShapeDtypeStruct((batch * seq, ng * LANES), BF16),
        scratch_shapes=[pltpu.VMEM((chunk, LANES), F32)],
        compiler_params=_params("parallel", "parallel"),
        name="hgrn2",
    )(h_qig, h_f, h_qig, h_qig, lb_logits, gn)


def _pack_pairs(x):
    hw = x.shape[1] // 2
    hi = pltpu.bitcast(x[:, :hw].astype(BF16).astype(F32), jnp.int32)
    lo = pltpu.bitcast(x[:, hw:].astype(BF16).astype(F32), jnp.int32)
    return hi | lax.shift_right_logical(lo, 16)


def _unpack_pairs(w):
    hi = pltpu.bitcast(w & jnp.int32(-65536), F32)
    lo = pltpu.bitcast(lax.shift_left(w, 16), F32)
    return hi, lo


def _store_planes(ref, words):
    pw = words.shape[1] // SC_PLANES
    for h in range(SC_PLANES):
        ref[h] = words[:, h * pw:(h + 1) * pw]


def _load_planes(ref):
    return jnp.concatenate([ref[h] for h in range(SC_PLANES)], axis=1)


def _router_kernel(a1_ref, a2_ref, w1_ref, w2_ref, x_ref, g_ref, w_ref, xo_ref, o_ref, xn_ref):
    x = x_ref[...] + _dot(a1_ref[...], w1_ref[...]) + _dot(a2_ref[...], w2_ref[...])
    xo_ref[...] = x
    var = jnp.mean(x * x, axis=-1, keepdims=True)
    xn = x * lax.rsqrt(var + NORM_EPS) * g_ref[...]
    _store_planes(xn_ref, _pack_pairs(xn))
    x_hi = xn.astype(BF16)
    x_lo = (xn - x_hi.astype(F32)).astype(BF16)
    w = w_ref[...]
    w_hi = w.astype(BF16)
    w_lo = (w - w_hi.astype(F32)).astype(BF16)
    logits = _dot(x_hi, w_hi) + _dot(x_hi, w_lo) + _dot(x_lo, w_hi)
    lane = lax.broadcasted_iota(jnp.int32, logits.shape, 1)
    lg = jnp.where(lane < N_EXPERTS, logits, -jnp.inf)
    m1 = jnp.max(lg, axis=-1, keepdims=True)
    i1 = jnp.min(jnp.where(lg == m1, lane, LANES), axis=-1, keepdims=True)
    lg2 = jnp.where(lane == i1, -jnp.inf, lg)
    m2 = jnp.max(lg2, axis=-1, keepdims=True)
    i2 = jnp.min(jnp.where(lg2 == m2, lane, LANES), axis=-1, keepdims=True)
    t = jnp.exp(m2 - m1)
    den = 1.0 + t
    o_ref[...] = (jnp.where(lane == 0, i1.astype(F32), 0.0) + jnp.where(lane == 1, i2.astype(F32), 0.0)
                  + jnp.where(lane == 2, 1.0 / den, 0.0) + jnp.where(lane == 3, t / den, 0.0))


def _mix_router(a1, a2, w1, w2, x, g, w_pad):
    t, d = x.shape
    k1, k2 = a1.shape[1], a2.shape[1]
    tm = ROW_TILE
    return pl.pallas_call(
        _router_kernel,
        grid=(t // tm,),
        in_specs=[pl.BlockSpec((tm, k1), lambda i: (i, 0)),
                  pl.BlockSpec((tm, k2), lambda i: (i, 0)),
                  pl.BlockSpec((k1, d), lambda i: (0, 0)),
                  pl.BlockSpec((k2, d), lambda i: (0, 0)),
                  pl.BlockSpec((tm, d), lambda i: (i, 0)),
                  pl.BlockSpec((1, d), lambda i: (0, 0)),
                  pl.BlockSpec((d, LANES), lambda i: (0, 0))],
        out_specs=[pl.BlockSpec((tm, d), lambda i: (i, 0)),
                   pl.BlockSpec((tm, LANES), lambda i: (i, 0)),
                   pl.BlockSpec((SC_PLANES, tm, d // 2 // SC_PLANES), lambda i: (0, i, 0))],
        out_shape=[jax.ShapeDtypeStruct((t, d), F32),
                   jax.ShapeDtypeStruct((t, LANES), F32),
                   jax.ShapeDtypeStruct((SC_PLANES, t, d // 2 // SC_PLANES), jnp.int32)],
        compiler_params=_params("parallel"),
        name="mix_router",
    )(a1, a2, w1, w2, x, g, w_pad)


def _sc_mesh():
    return plsc.VectorSubcoreMesh(core_axis_name="core", subcore_axis_name="subcore")


def _sc_scatter_rows(x, idx0, idx1, n_out):
    n, d = x.shape

    @functools.partial(pl.kernel, out_type=jax.ShapeDtypeStruct((n_out, d), x.dtype), mesh=_sc_mesh(),
                       scratch_types=[])
    def scatter(x_hbm, i0_hbm, i1_hbm, o_hbm):
        def body(x_vmem, i0_vmem, i1_vmem):
            pltpu.sync_copy(x_vmem, o_hbm.at[i0_vmem.at[0]])
            pltpu.sync_copy(x_vmem, o_hbm.at[i1_vmem.at[0]])

        pltpu.emit_pipeline(
            body,
            grid=(n // SC_WINDOW,),
            in_specs=[pl.BlockSpec((SC_WINDOW, d), index_map=lambda i: (i, 0)),
                      pl.BlockSpec((1, SC_WINDOW), index_map=lambda i: (0, i)),
                      pl.BlockSpec((1, SC_WINDOW), index_map=lambda i: (0, i))],
            out_specs=[],
            core_axis_name=("core", "subcore"),
            dimension_semantics=(pltpu.PARALLEL,),
        )(x_hbm, i0_hbm, i1_hbm)

    return scatter(x, idx0.reshape(1, n), idx1.reshape(1, n))


def _sc_gather_rows(x, idx):
    n = idx.shape[0]
    d = x.shape[1]

    @functools.partial(pl.kernel, out_type=jax.ShapeDtypeStruct((n, d), x.dtype), mesh=_sc_mesh(),
                       scratch_types=[])
    def gather(x_hbm, i_hbm, o_hbm):
        def body(i_vmem, o_vmem):
            pltpu.sync_copy(x_hbm.at[i_vmem.at[0]], o_vmem)

        pltpu.emit_pipeline(
            body,
            grid=(n // SC_WINDOW,),
            in_specs=[pl.BlockSpec((1, SC_WINDOW), index_map=lambda i: (0, i))],
            out_specs=[pl.BlockSpec((SC_WINDOW, d), index_map=lambda i: (i, 0))],
            core_axis_name=("core", "subcore"),
            dimension_semantics=(pltpu.PARALLEL,),
        )(i_hbm, o_hbm)

    return gather(x, idx.reshape(1, n))


def _moe_ffn_packed_kernel(te_ref, act_ref, x_ref, wg_ref, wu_ref, wd_ref, o_ref, xn_ref, acc_ref):
    i = pl.program_id(0)
    f = pl.program_id(1)
    hw = xn_ref.shape[1] // 2

    @pl.when(act_ref[i] == 1)
    def _():
        @pl.when(f == 0)
        def _():
            hi, lo = _unpack_pairs(_load_planes(x_ref))
            xn_ref[:, :hw] = hi.astype(BF16)
            xn_ref[:, hw:] = lo.astype(BF16)
            acc_ref[...] = jnp.zeros_like(acc_ref)

        x = xn_ref[...]
        mid = (_silu(_dot(x, wg_ref[0])) * _dot(x, wu_ref[0])).astype(BF16)
        acc_ref[...] += _dot(mid, wd_ref[0])

        @pl.when(f == pl.num_programs(1) - 1)
        def _():
            _store_planes(o_ref, _pack_pairs(acc_ref[...]))


def _moe_ffn_packed(tile_e, tile_act, xs, wg, wu, wd, *, tf):
    _, p, pw = xs.shape
    d = 2 * pw * SC_PLANES
    ff = wg.shape[2]
    tm = MOE_ROW_TILE
    return pl.pallas_call(
        _moe_ffn_packed_kernel,
        grid_spec=pltpu.PrefetchScalarGridSpec(
            num_scalar_prefetch=2,
            grid=(p // tm, ff // tf),
            in_specs=[pl.BlockSpec((SC_PLANES, tm, pw), lambda i, f, te, ta: (0, i, 0)),
                      pl.BlockSpec((1, d, tf), lambda i, f, te, ta: (te[i], 0, f * ta[i])),
                      pl.BlockSpec((1, d, tf), lambda i, f, te, ta: (te[i], 0, f * ta[i])),
                      pl.BlockSpec((1, tf, d), lambda i, f, te, ta: (te[i], f * ta[i], 0))],
            out_specs=pl.BlockSpec((SC_PLANES, tm, pw), lambda i, f, te, ta: (0, i, 0)),
            scratch_shapes=[pltpu.VMEM((tm, d), BF16), pltpu.VMEM((tm, d), F32)],
        ),
        out_shape=jax.ShapeDtypeStruct((SC_PLANES, p, pw), jnp.int32),
        compiler_params=_params("arbitrary", "arbitrary"),
        name="moe_ffn",
    )(tile_e, tile_act, xs, wg, wu, wd)


def _moe_sum_kernel(x_ref, info_ref, y_ref, gf_ref, o_ref, *, final_norm):
    hw = x_ref.shape[1] // 2
    info = info_ref[...]
    g1 = info[:, 2:3]
    g2 = info[:, 3:4]
    hi1, lo1 = _unpack_pairs(jnp.concatenate([y_ref[h] for h in range(SC_PLANES)], axis=1))
    hi2, lo2 = _unpack_pairs(jnp.concatenate([y_ref[SC_PLANES + h] for h in range(SC_PLANES)], axis=1))
    x = x_ref[...]
    out_a = x[:, :hw] + g1 * hi1 + g2 * hi2
    out_b = x[:, hw:] + g1 * lo1 + g2 * lo2
    if final_norm:
        ss = jnp.sum(out_a * out_a, axis=-1, keepdims=True) + jnp.sum(out_b * out_b, axis=-1, keepdims=True)
        r = lax.rsqrt(ss * (1.0 / (2 * hw)) + NORM_EPS)
        gf = gf_ref[...]
        out_a = out_a * r * gf[:, :hw]
        out_b = out_b * r * gf[:, hw:]
    o_ref[:, :hw] = out_a
    o_ref[:, hw:] = out_b


def _moe_sum(x, info, y_pairs, gf, *, final_norm):
    t, d = x.shape
    tm = ROW_TILE
    return pl.pallas_call(
        functools.partial(_moe_sum_kernel, final_norm=final_norm),
        grid=(t // tm,),
        in_specs=[pl.BlockSpec((tm, d), lambda i: (i, 0)),
                  pl.BlockSpec((tm, LANES), lambda i: (i, 0)),
                  pl.BlockSpec((2 * SC_PLANES, tm, y_pairs.shape[2]), lambda i: (0, i, 0)),
                  pl.BlockSpec((1, d), lambda i: (0, 0))],
        out_specs=pl.BlockSpec((tm, d), lambda i: (i, 0)),
        out_shape=jax.ShapeDtypeStruct((t, d), F32),
        compiler_params=_params("parallel"),
        name="moe_sum",
    )(x, info, y_pairs, gf)


def _moe_positions(info, t):
    tm = MOE_ROW_TILE
    n_e = N_EXPERTS
    a = 2 * t
    p = a + n_e * tm
    e_flat = info[:, 0:2].astype(jnp.int32).reshape(a)
    onehot = (e_flat[:, None] == jnp.arange(n_e, dtype=jnp.int32)[None, :]).astype(jnp.int32)
    csum = jnp.cumsum(onehot, axis=0)
    rank = jnp.sum(csum * onehot, axis=1) - 1
    counts = csum[-1]
    padded = ((counts + tm - 1) // tm) * tm
    ends = jnp.cumsum(padded)
    starts = ends - padded
    pos = jnp.sum(onehot * starts[None, :], axis=1) + rank
    tile_start = jnp.arange(p // tm, dtype=jnp.int32) * tm
    tile_e = jnp.minimum(jnp.sum((tile_start[:, None] >= ends[None, :]).astype(jnp.int32), axis=1), n_e - 1)
    tile_act = (tile_start < ends[-1]).astype(jnp.int32)
    return pos.astype(jnp.int32).reshape(t, 2), tile_e.astype(jnp.int32), tile_act, p


def _mix_ffn_kernel(a1_ref, a2_ref, w1_ref, w2_ref, x_ref, g_ref, wg_ref, wu_ref, wd_ref, gf_ref, o_ref,
                    *, final_norm):
    x = x_ref[...] + _dot(a1_ref[...], w1_ref[...]) + _dot(a2_ref[...], w2_ref[...])
    var = jnp.mean(x * x, axis=-1, keepdims=True)
    xn = (x * lax.rsqrt(var + NORM_EPS) * g_ref[...]).astype(BF16)
    mid = (_silu(_dot(xn, wg_ref[...])) * _dot(xn, wu_ref[...])).astype(BF16)
    out = x + _dot(mid, wd_ref[...])
    if final_norm:
        var = jnp.mean(out * out, axis=-1, keepdims=True)
        out = out * lax.rsqrt(var + NORM_EPS) * gf_ref[...]
    o_ref[...] = out


def _resident(shape):
    return pl.BlockSpec(shape, lambda i: (0,) * len(shape), pipeline_mode=pl.Buffered(1))


def _mix_ffn(a1, a2, w1, w2, x, g, wg, wu, wd, gf, *, final_norm):
    t, d = x.shape
    ff = wg.shape[1]
    k1, k2 = a1.shape[1], a2.shape[1]
    tm = ROW_TILE
    return pl.pallas_call(
        functools.partial(_mix_ffn_kernel, final_norm=final_norm),
        grid=(t // tm,),
        in_specs=[
            pl.BlockSpec((tm, k1), lambda i: (i, 0)),
            pl.BlockSpec((tm, k2), lambda i: (i, 0)),
            _resident((k1, d)),
            _resident((k2, d)),
            pl.BlockSpec((tm, d), lambda i: (i, 0)),
            _resident((1, d)),
            _resident((d, ff)),
            _resident((d, ff)),
            _resident((ff, d)),
            _resident((1, d)),
        ],
        out_specs=pl.BlockSpec((tm, d), lambda i: (i, 0)),
        out_shape=jax.ShapeDtypeStruct((t, d), F32),
        compiler_params=_params("parallel"),
        name="mix_ffn",
    )(a1, a2, w1, w2, x, g, wg, wu, wd, gf)


def _rope_tables(seq):
    half = HEAD_DIM // 2
    inv = ROPE_THETA ** (-jnp.arange(half, dtype=F32) * 2.0 / HEAD_DIM)
    ang = jnp.arange(seq, dtype=F32)[:, None] * inv[None, :]
    cos = jnp.cos(ang)
    sin = jnp.sin(ang)
    cos_t = jnp.concatenate([cos, cos, cos, cos], axis=-1)
    sin_t = jnp.concatenate([-sin, sin, -sin, sin], axis=-1)
    return cos_t, sin_t


def _dup_heads(w):
    d, n = w.shape
    w = w.reshape(d, n // HEAD_DIM, 1, HEAD_DIM)
    return jnp.broadcast_to(w, (d, n // HEAD_DIM, 2, HEAD_DIM)).reshape(d, 2 * n)


def kernel(x, ln_mix_e, w_in_e, b_fox_f, w_out_e, ln_ffn_e, w_ffn_gate, w_ffn_up, w_ffn_down,
           ln_mix_o, w_in_o, hgrn_lb_logits, hgrn_norm_g, diff_lambda, diff_norm_g, w_out_o,
           ln_ffn_o, w_router, w_exp_gate, w_exp_up, w_exp_down, ln_final):
    batch, seq, d = x.shape
    t = batch * seq
    depth = ln_mix_e.shape[0] + ln_mix_o.shape[0]
    xs = x.reshape(t, d)
    cos_t, sin_t = _rope_tables(seq)
    row = lambda v: v.reshape(1, -1)
    fw = FOX_HEADS * HEAD_DIM
    dw = DSA_HEADS * HEAD_DIM
    kvw = DSA_KV_HEADS * HEAD_DIM
    hw = HGRN_HEADS * HEAD_DIM
    qw = DIFF_HEADS * 2 * DIFF_DIM

    for layer in range(depth):
        j = layer // 2
        if layer % 2 == 0:
            w = w_in_e[j]
            o = 0
            fq, fk, fv = w[:, o:o + fw], w[:, o + fw:o + 2 * fw], w[:, o + 2 * fw:o + 3 * fw]
            o += 3 * fw
            ffw = w[:, o:o + FOX_HEADS]
            o += FOX_HEADS
            dq = w[:, o:o + dw]
            o += dw
            dk = w[:, o:o + kvw]
            o += kvw
            dv = w[:, o:o + kvw]
            o += kvw
            iqw = w[:, o:o + IDX_HEADS * IDX_DIM]
            o += IDX_HEADS * IDX_DIM
            ikw = w[:, o:o + IDX_DIM]
            o += IDX_DIM
            iww = w[:, o:o + IDX_HEADS]
            w1 = jnp.concatenate([fq, fk, fv, _dup_heads(dv)], axis=1).astype(BF16)
            w2 = jnp.concatenate([dq, _dup_heads(dk), iqw, _dup_heads(ikw)], axis=1).astype(BF16)
            w3 = jnp.concatenate([ffw, iww, jnp.zeros((d, LANES - FOX_HEADS - IDX_HEADS), F32)], axis=1).astype(BF16)
            g = row(ln_mix_e[j])
            p1, p2, z = _rms_proj(xs, g, cos_t, sin_t, (w1, w2, w3), (False, True, False), (BF16, BF16, F32), seq=seq)
            b_pad = jnp.concatenate([b_fox_f[j], jnp.zeros((LANES - FOX_HEADS,), F32)]).reshape(1, LANES)
            ccol, crow = _fox_gate(z, b_pad, batch=batch, seq=seq)
            fox_o = _fox_attention(p1, ccol, crow, batch=batch, seq=seq)
            dsa_o = _dsa_attention(p2, p1, z, batch=batch, seq=seq)
            wo = w_out_e[j].astype(BF16)
            xs = _mix_ffn(fox_o, dsa_o, wo[:fw], wo[fw:], xs, row(ln_ffn_e[j]), w_ffn_gate[j].astype(BF16),
                          w_ffn_up[j].astype(BF16), w_ffn_down[j].astype(BF16), row(ln_final),
                          final_norm=(layer == depth - 1))
        else:
            w = w_in_o[j]
            lam_init = 0.8 - 0.6 * math.exp(-0.3 * layer)
            g = row(ln_mix_o[j])
            w_qig = jnp.concatenate([w[:, :hw], w[:, 2 * hw:4 * hw]], axis=1).astype(BF16)
            w_f = w[:, hw:2 * hw].astype(BF16)
            w_qk = w[:, 4 * hw:4 * hw + 2 * qw].astype(BF16)
            w_v = w[:, 4 * hw + 2 * qw:].astype(BF16)
            h_qig, h_f, pqk, pv = _rms_proj(xs, g, cos_t, sin_t, (w_qig, w_f, w_qk, w_v),
                                            (False, False, True, False), (BF16, F32, BF16, BF16), seq=seq)
            gn = jnp.concatenate([hgrn_norm_g[j], hgrn_norm_g[j]]).reshape(1, LANES)
            o_h = _hgrn(h_qig, h_f, hgrn_lb_logits, gn, batch=batch, seq=seq, layer=layer)
            o_d = _diff_attention(pqk, pv, diff_lambda[j], row(diff_norm_g[j]), batch=batch, seq=seq,
                                  lam_init=lam_init)
            wo = w_out_o[j].astype(BF16)
            wr = jnp.concatenate([w_router[j], jnp.zeros((d, LANES - N_EXPERTS), F32)], axis=1)
            xs, info, xn = _mix_router(o_h, o_d, wo[:hw], wo[hw:], xs, row(ln_ffn_o[j]), wr)
            pos, tile_e, tile_act, n_rows = _moe_positions(info, t)
            pw = xn.shape[2]
            plane = jnp.arange(SC_PLANES, dtype=jnp.int32)[:, None] * n_rows
            dst0 = (plane + pos[None, :, 0]).reshape(SC_PLANES * t)
            dst1 = (plane + pos[None, :, 1]).reshape(SC_PLANES * t)
            rows = _sc_scatter_rows(xn.reshape(SC_PLANES * t, pw), dst0, dst1, SC_PLANES * n_rows)
            y = _moe_ffn_packed(tile_e, tile_act, rows.reshape(SC_PLANES, n_rows, pw), w_exp_gate[j].astype(BF16),
                                w_exp_up[j].astype(BF16), w_exp_down[j].astype(BF16), tf=1792)
            src = jnp.concatenate([(plane + pos[None, :, s]).reshape(SC_PLANES * t) for s in range(2)])
            y_pairs = _sc_gather_rows(y.reshape(SC_PLANES * n_rows, pw), src).reshape(2 * SC_PLANES, t, pw)
            xs = _moe_sum(xs, info, y_pairs, row(ln_final), final_norm=(layer == depth - 1))
    if depth % 2 == 1:
        pass
    return xs.reshape(batch, seq, d)
```

```python
import functools
import math

import jax
import jax.numpy as jnp
from jax import lax
from jax.experimental import pallas as pl
from jax.experimental.pallas import tpu as pltpu
from jax.experimental.pallas import tpu_sc as plsc

F32 = jnp.float32
BF16 = jnp.bfloat16

HEAD_DIM = 64
FOX_HEADS = 8
DSA_HEADS = 8
DSA_KV_HEADS = 2
IDX_HEADS = 4
IDX_DIM = 64
DSA_TOPK = 256
HGRN_HEADS = 8
DIFF_HEADS = 4
DIFF_DIM = 64
ROPE_THETA = 10000.0
N_EXPERTS = 8
NORM_EPS = 1e-6
NEG_INF = -1e30

LANES = 128
SUBLANES = 8
VMEM_LIMIT_BYTES = 56 * 1024 * 1024

ROW_TILE = 512
ATTN_TILE = 256
QUERY_TILE = 512
HGRN_CHUNK = 256
SUB = 8
MOE_ROW_TILE = 512
SC_WINDOW = 128
SC_PLANES = 2

INT_MIN = -(2 ** 31)


def _params(*sem):
    return pltpu.CompilerParams(dimension_semantics=sem, vmem_limit_bytes=VMEM_LIMIT_BYTES)


def _dot(a, b):
    return jnp.dot(a, b, preferred_element_type=F32)


def _dot_nt(a, b):
    return lax.dot_general(a, b, (((1,), (1,)), ((), ())), preferred_element_type=F32)


def _dot_tn(a, b):
    return lax.dot_general(a, b, (((0,), (0,)), ((), ())), preferred_element_type=F32)


def _sigmoid(x):
    return 1.0 / (1.0 + jnp.exp(-x))


def _silu(x):
    return x * _sigmoid(x)


def _lane_half(shape):
    return lax.broadcasted_iota(jnp.int32, shape, len(shape) - 1) // HEAD_DIM


def _rms_proj_kernel(x_ref, g_ref, cos_ref, sin_ref, *refs, ropes):
    n = len(ropes)
    w_refs, o_refs = refs[:n], refs[n:]
    x = x_ref[...]
    var = jnp.mean(x * x, axis=-1, keepdims=True)
    xn = (x * lax.rsqrt(var + NORM_EPS) * g_ref[...]).astype(BF16)
    cos = cos_ref[...]
    sin = sin_ref[...]
    first = (lax.broadcasted_iota(jnp.int32, cos.shape, 1) % HEAD_DIM) < (HEAD_DIM // 2)
    for w_ref, o_ref, rope in zip(w_refs, o_refs, ropes):
        y = _dot(xn, w_ref[...])
        if not rope:
            o_ref[...] = y.astype(o_ref.dtype)
            continue
        for c in range(y.shape[1] // LANES):
            yc = y[:, c * LANES:(c + 1) * LANES]
            partner = jnp.where(first, pltpu.roll(yc, LANES - HEAD_DIM // 2, 1), pltpu.roll(yc, HEAD_DIM // 2, 1))
            o_ref[:, c * LANES:(c + 1) * LANES] = (yc * cos + partner * sin).astype(o_ref.dtype)


def _rms_proj(x, g, cos, sin, ws, ropes, out_dtypes, *, seq):
    t, d = x.shape
    tm = ROW_TILE
    nseq = seq // tm
    return pl.pallas_call(
        functools.partial(_rms_proj_kernel, ropes=tuple(ropes)),
        grid=(t // tm,),
        in_specs=[
            pl.BlockSpec((tm, d), lambda i: (i, 0)),
            pl.BlockSpec((1, d), lambda i: (0, 0)),
            pl.BlockSpec((tm, LANES), lambda i: (i % nseq, 0)),
            pl.BlockSpec((tm, LANES), lambda i: (i % nseq, 0)),
        ] + [pl.BlockSpec((d, w.shape[1]), lambda i: (0, 0)) for w in ws],
        out_specs=[pl.BlockSpec((tm, w.shape[1]), lambda i: (i, 0)) for w in ws],
        out_shape=[jax.ShapeDtypeStruct((t, w.shape[1]), dt) for w, dt in zip(ws, out_dtypes)],
        compiler_params=_params("parallel"),
        name="rms_proj",
    )(x, g, cos, sin, *ws)


def _cumsum_rows(x):
    rows = x.shape[0]
    row = lax.broadcasted_iota(jnp.int32, x.shape, 0)
    k = 1
    while k < rows:
        x = x + jnp.where(row >= k, pltpu.roll(x, k, 0), 0.0)
        k *= 2
    return x


def _fox_gate_kernel(z_ref, b_ref, ccol_ref, crow_ref, *, tk):
    z = z_ref[...] + b_ref[...]
    logf = jnp.minimum(z, 0.0) - jnp.log(1.0 + jnp.exp(-jnp.abs(z)))
    c = _cumsum_rows(logf)
    ccol_ref[...] = c
    for j in range(c.shape[0] // tk):
        ct = c[j * tk:(j + 1) * tk, :].T
        crow_ref[0, :, j * tk:(j + 1) * tk] = ct[:SUBLANES, :]


def _fox_gate(z, b_pad, *, batch, seq):
    tk = ATTN_TILE
    return pl.pallas_call(
        functools.partial(_fox_gate_kernel, tk=tk),
        grid=(batch,),
        in_specs=[
            pl.BlockSpec((seq, LANES), lambda b: (b, 0)),
            pl.BlockSpec((1, LANES), lambda b: (0, 0)),
        ],
        out_specs=[
            pl.BlockSpec((seq, LANES), lambda b: (b, 0)),
            pl.BlockSpec((1, SUBLANES, seq), lambda b: (b, 0, 0)),
        ],
        out_shape=[
            jax.ShapeDtypeStruct((batch * seq, LANES), F32),
            jax.ShapeDtypeStruct((batch, SUBLANES, seq), F32),
        ],
        compiler_params=_params("parallel"),
        name="fox_gate",
    )(z, b_pad)


def _softmax_pv(s, v):
    m = jnp.max(s, axis=-1, keepdims=True)
    p = jnp.exp(s - m)
    l = jnp.sum(p, axis=-1, keepdims=True)
    return _dot(p.astype(BF16), v) / l


def _causal_attend(qe, k_ref, v_ref, n, tq, row_bias=None, key_bias=None):
    causal = lax.broadcasted_iota(jnp.int32, (tq, tq), 1) <= lax.broadcasted_iota(jnp.int32, (tq, tq), 0)
    parts = []
    for lo, hi in ((0, n - tq), (n - tq, n)):
        if hi == lo:
            continue
        s = _dot_nt(qe, k_ref[lo:hi, :])
        if key_bias is not None:
            s = s - key_bias[:, lo:hi]
        if hi == n:
            s = jnp.where(causal, s, NEG_INF)
        parts.append((s, v_ref[lo:hi, :]))
    m = functools.reduce(jnp.maximum, [jnp.max(s, axis=-1, keepdims=True) for s, _ in parts])
    shift = -m if row_bias is None else row_bias - (m + row_bias)
    acc = 0.0
    l = 0.0
    for s, v in parts:
        p = jnp.exp(s + shift)
        l = l + jnp.sum(p, axis=-1, keepdims=True)
        acc = acc + _dot(p.astype(BF16), v)
    return acc / l


def _fox_kernel(q_ref, k_ref, v_ref, cc_ref, cr_ref, o_ref, *, tq, scale):
    hp = pl.program_id(1)
    seq = q_ref.shape[0]
    half = _lane_half((1, LANES))
    lane = lax.broadcasted_iota(jnp.int32, (tq, LANES), 1)
    for i in range(seq // tq):
        n = (i + 1) * tq
        q = q_ref[i * tq:n, :] * scale
        cc = cc_ref[i * tq:n, :]
        outs = []
        for e in range(2):
            h = 2 * hp + e
            qe = jnp.where(half == e, q, jnp.zeros_like(q))
            cq = jnp.sum(jnp.where(lane == h, cc, 0.0), axis=-1, keepdims=True)
            ck = cr_ref[0, pl.ds(h, 1), 0:n]
            outs.append(_causal_attend(qe, k_ref, v_ref, n, tq, row_bias=cq, key_bias=ck))
        o_ref[i * tq:n, :] = jnp.where(half == 0, outs[0], outs[1]).astype(o_ref.dtype)


def _fox_attention(p1, ccol, crow, *, batch, seq):
    tq = min(QUERY_TILE, seq)
    ng = FOX_HEADS // 2
    return pl.pallas_call(
        functools.partial(_fox_kernel, tq=tq, scale=HEAD_DIM ** -0.5),
        grid=(batch, ng),
        in_specs=[
            pl.BlockSpec((seq, LANES), lambda b, g: (b, g)),
            pl.BlockSpec((seq, LANES), lambda b, g: (b, ng + g)),
            pl.BlockSpec((seq, LANES), lambda b, g: (b, 2 * ng + g)),
            pl.BlockSpec((seq, LANES), lambda b, g: (b, 0)),
            pl.BlockSpec((1, SUBLANES, seq), lambda b, g: (b, 0, 0)),
        ],
        out_specs=pl.BlockSpec((seq, LANES), lambda b, g: (b, g)),
        out_shape=jax.ShapeDtypeStruct((batch * seq, ng * LANES), BF16),
        compiler_params=_params("parallel", "parallel"),
        name="fox_attention",
    )(p1, p1, p1, ccol, crow)


def _diff_kernel(q_ref, k_ref, v_ref, lam_ref, g_ref, o_ref, *, tq, scale, lam_init):
    seq = q_ref.shape[0]
    half = _lane_half((1, LANES))
    lv = lam_ref[...]
    lam = (jnp.exp(jnp.sum(lv[0:1, :] * lv[1:2, :], axis=-1, keepdims=True))
           - jnp.exp(jnp.sum(lv[2:3, :] * lv[3:4, :], axis=-1, keepdims=True)) + lam_init)
    gn = g_ref[...] * (1.0 - lam_init)
    for i in range(seq // tq):
        n = (i + 1) * tq
        q = q_ref[i * tq:n, :] * scale
        outs = []
        for e in range(2):
            qe = jnp.where(half == e, q, jnp.zeros_like(q))
            outs.append(_causal_attend(qe, k_ref, v_ref, n, tq))
        o = outs[0] - lam * outs[1]
        var = jnp.mean(o * o, axis=-1, keepdims=True)
        o_ref[i * tq:n, :] = (o * lax.rsqrt(var + NORM_EPS) * gn).astype(o_ref.dtype)


def _diff_attention(pqk, pv, lam_vecs, g, *, batch, seq, lam_init):
    tq = min(QUERY_TILE, seq)
    ng = DIFF_HEADS
    return pl.pallas_call(
        functools.partial(_diff_kernel, tq=tq, scale=DIFF_DIM ** -0.5, lam_init=lam_init),
        grid=(batch, ng),
        in_specs=[
            pl.BlockSpec((seq, LANES), lambda b, g: (b, g)),
            pl.BlockSpec((seq, LANES), lambda b, g: (b, ng + g)),
            pl.BlockSpec((seq, LANES), lambda b, g: (b, g)),
            pl.BlockSpec(lam_vecs.shape, lambda b, g: (0, 0)),
            pl.BlockSpec((1, LANES), lambda b, g: (0, 0)),
        ],
        out_specs=pl.BlockSpec((seq, LANES), lambda b, g: (b, g)),
        out_shape=jax.ShapeDtypeStruct((batch * seq, ng * LANES), BF16),
        compiler_params=_params("parallel", "parallel"),
        name="diff_attention",
    )(pqk, pqk, pv, lam_vecs, g)


def _sortable_key(x):
    bits = pltpu.bitcast(x, jnp.int32)
    return jnp.where(bits < 0, bits ^ jnp.int32(0x7FFFFFFF), bits)


def _dsa_kernel(q_ref, kd_ref, iq_ref, ik_ref, vd_ref, z_ref, o_ref, key_ref, madd_ref,
                *, tq, n_sel, scale, idx_scale, w_scale):
    i = pl.program_id(1)
    seq = key_ref.shape[1]
    nq = seq // tq
    span = 2 if nq % 2 == 0 else 1
    half = _lane_half((1, LANES))
    row = lax.broadcasted_iota(jnp.int32, (tq, tq), 0)
    col = lax.broadcasted_iota(jnp.int32, (tq, tq), 1)
    row_g = row + i * tq
    tri = jnp.where(row < col, 1.0, 0.0).astype(BF16)

    def block(n):
        z = z_ref[...]
        iq = iq_ref[...]
        ik = ik_ref[0:n, :]
        sc = jnp.zeros((tq, n), F32)
        for h in range(IDX_HEADS):
            grp = iq[:, (h // 2) * LANES:(h // 2 + 1) * LANES]
            iq_h = jnp.where(half == h % 2, grp, jnp.zeros_like(grp))
            w_h = z[:, SUBLANES + h:SUBLANES + h + 1] * w_scale
            sc = sc + w_h * jnp.maximum(_dot_nt(iq_h, ik) * idx_scale, 0.0)
        for j in range(n // tq):
            scj = jnp.where(col + j * tq <= row_g, sc[:, j * tq:(j + 1) * tq], -jnp.inf)
            key_ref[:, j * tq:(j + 1) * tq] = _sortable_key(scj)

        def count(pred):
            return jnp.sum(jnp.where(pred, 1.0, 0.0), axis=-1, keepdims=True)

        thr0 = jnp.where(count(key_ref[:, 0:n] >= 0) >= n_sel, 0, INT_MIN).astype(jnp.int32)

        def bit_body(it, thr):
            cand = thr | jnp.left_shift(jnp.int32(1), 30 - it)
            return jnp.where(count(key_ref[:, 0:n] >= cand) >= n_sel, cand, thr)

        thr = lax.fori_loop(0, 31, bit_body, thr0)
        need = n_sel - count(key_ref[:, 0:n] > thr)

        off = jnp.zeros((tq, 1), F32)
        for j in range(n // tq):
            kj = key_ref[:, j * tq:(j + 1) * tq]
            eq = kj == thr
            eqf = jnp.where(eq, 1.0, 0.0)
            rank = _dot(eqf.astype(BF16), tri) + off
            take = jnp.where(kj > thr, 1.0, jnp.where(eq, jnp.where(rank < need, 1.0, 0.0), 0.0))
            valid = col + j * tq <= row_g
            madd_ref[:, j * tq:(j + 1) * tq] = jnp.where(valid, jnp.where(take > 0.0, 0.0, NEG_INF), NEG_INF)
            off = off + jnp.sum(eqf, axis=-1, keepdims=True)

        q = q_ref[...] * scale
        for p in range(DSA_HEADS // 2):
            qg = q[:, p * LANES:(p + 1) * LANES]
            g = p // (DSA_HEADS // DSA_KV_HEADS // 2)
            k = kd_ref[0:n, g * LANES:(g + 1) * LANES]
            v = vd_ref[0:n, g * LANES:(g + 1) * LANES]
            outs = []
            for e in range(2):
                qe = jnp.where(half == e, qg, jnp.zeros_like(qg))
                outs.append(_softmax_pv(_dot_nt(qe, k) + madd_ref[:, 0:n], v))
            o_ref[:, p * LANES:(p + 1) * LANES] = jnp.where(half == 0, outs[0], outs[1]).astype(o_ref.dtype)

    for c in range(nq // span):
        pl.when(i // span == c)(functools.partial(block, (c + 1) * span * tq))


def _dsa_attention(p2, p1, z, *, batch, seq):
    tq = min(ATTN_TILE, seq)
    nq = seq // tq
    n_sel = min(DSA_TOPK, seq // 4)
    qw = DSA_HEADS * HEAD_DIM
    return pl.pallas_call(
        functools.partial(_dsa_kernel, tq=tq, n_sel=float(n_sel), scale=HEAD_DIM ** -0.5,
                          idx_scale=IDX_DIM ** -0.5, w_scale=IDX_HEADS ** -0.5),
        grid=(batch, nq),
        in_specs=[
            pl.BlockSpec((tq, qw), lambda b, i: (b * nq + i, 0)),
            pl.BlockSpec((seq, 2 * LANES), lambda b, i: (b, 2)),
            pl.BlockSpec((tq, 2 * LANES), lambda b, i: (b * nq + i, 3)),
            pl.BlockSpec((seq, LANES), lambda b, i: (b, 8)),
            pl.BlockSpec((seq, 2 * LANES), lambda b, i: (b, 6)),
            pl.BlockSpec((tq, LANES), lambda b, i: (b * nq + i, 0)),
        ],
        out_specs=pl.BlockSpec((tq, qw), lambda b, i: (b * nq + i, 0)),
        out_shape=jax.ShapeDtypeStruct((batch * seq, qw), BF16),
        scratch_shapes=[pltpu.VMEM((tq, seq), jnp.int32), pltpu.VMEM((tq, seq), F32)],
        compiler_params=_params("parallel", "arbitrary"),
        name="dsa_attention",
    )(p2, p2, p2, p2, p1, z)


def _hgrn_kernel(q_ref, f_ref, i_ref, g_ref, lbl_ref, gn_ref, o_ref, b_scr, *, chunk, layer):
    seq = q_ref.shape[0]
    half = _lane_half((1, LANES))
    logits = lbl_ref[...]
    pe = jnp.exp(logits - jnp.max(logits, axis=0, keepdims=True))
    prob = pe / jnp.sum(pe, axis=0, keepdims=True)
    csum = prob[0:1, :]
    for d in range(1, layer + 1):
        csum = csum + prob[d:d + 1, :]
    lb = csum - prob[0:1, :]

    rowi = lax.broadcasted_iota(jnp.int32, (chunk, 1), 0)
    xor_rc = (lax.broadcasted_iota(jnp.int32, (chunk, chunk), 0)
              ^ lax.broadcasted_iota(jnp.int32, (chunk, chunk), 1))
    same_head = (lax.broadcasted_iota(jnp.int32, (LANES, LANES), 0) // HEAD_DIM
                 == lax.broadcasted_iota(jnp.int32, (LANES, LANES), 1) // HEAD_DIM)
    ones_bd = jnp.where(same_head, 1.0, 0.0).astype(BF16)
    gn = gn_ref[...]

    def chunk_body(c, state_t):
        r0 = pl.multiple_of(c * chunk, chunk)
        q = _silu(q_ref[pl.ds(r0, chunk), :].astype(F32))
        f = lb + (1.0 - lb) * _sigmoid(f_ref[pl.ds(r0, chunk), :])
        kc = 1.0 - f
        v = i_ref[pl.ds(r0, chunk), :].astype(F32)
        b = _cumsum_rows(jnp.log(f))
        b_scr[...] = b
        b_last = b_scr[chunk - 1:chunk, :]

        o = _dot_nt((q * jnp.exp(b)).astype(BF16), state_t.astype(BF16))
        k2 = (kc * jnp.exp(b_last - b)).astype(BF16)
        upd = _dot_tn(v.astype(BF16), k2)
        new_state = state_t * jnp.exp(b_last) + jnp.where(same_head, upd, 0.0)

        s_tot = [jnp.zeros((chunk, chunk), F32), jnp.zeros((chunk, chunk), F32)]
        m = chunk
        while m > SUB:
            hm = m // 2
            pieces = [jnp.broadcast_to(b_scr[blk * m + hm - 1:blk * m + hm, :], (m, LANES))
                      for blk in range(chunk // m)]
            ref = pieces[0] if len(pieces) == 1 else jnp.concatenate(pieces, axis=0)
            right = (rowi % m) >= hm
            qm = jnp.where(right, q * jnp.exp(jnp.where(right, b - ref, 0.0)), 0.0).astype(BF16)
            km = jnp.where(right, 0.0, kc * jnp.exp(jnp.where(right, 0.0, ref - b))).astype(BF16)
            for e in range(2):
                s = _dot_nt(jnp.where(half == e, qm, jnp.zeros_like(qm)), km)
                s_tot[e] = s_tot[e] + (s if m == chunk else jnp.where(xor_rc < m, s, 0.0))
            m = hm
        vb = v.astype(BF16)
        for e in range(2):
            o = o + _dot(s_tot[e].astype(BF16), jnp.where(half == e, vb, jnp.zeros_like(vb)))

        for d in range(SUB):
            ok = (rowi % SUB) >= d
            bs = b if d == 0 else pltpu.roll(b, d, 0)
            ks = kc if d == 0 else pltpu.roll(kc, d, 0)
            vs = v if d == 0 else pltpu.roll(v, d, 0)
            x = jnp.where(ok, q * ks * jnp.exp(jnp.where(ok, b - bs, 0.0)), 0.0)
            o = o + _dot(x.astype(BF16), ones_bd) * vs

        sq = o * o
        s0 = jnp.sum(jnp.where(half == 0, sq, 0.0), axis=-1, keepdims=True)
        s1 = jnp.sum(jnp.where(half == 1, sq, 0.0), axis=-1, keepdims=True)
        var = jnp.where(half == 0, s0, s1) * (1.0 / HEAD_DIM)
        y = o * lax.rsqrt(var + NORM_EPS) * gn * _silu(g_ref[pl.ds(r0, chunk), :].astype(F32))
        o_ref[pl.ds(r0, chunk), :] = y.astype(o_ref.dtype)
        return new_state

    lax.fori_loop(0, seq // chunk, chunk_body, jnp.zeros((LANES, LANES), F32))


def _hgrn(h_qig, h_f, lb_logits, gn, *, batch, seq, layer):
    ng = HGRN_HEADS // 2
    chunk = min(HGRN_CHUNK, seq)
    spec = lambda off: pl.BlockSpec((seq, LANES), lambda b, g: (b, off * ng + g))
    return pl.pallas_call(
        functools.partial(_hgrn_kernel, chunk=chunk, layer=layer),
        grid=(batch, ng),
        in_specs=[spec(0), spec(0), spec(1), spec(2),
                  pl.BlockSpec((lb_logits.shape[0], LANES), lambda b, g: (0, g)),
                  pl.BlockSpec((1, LANES), lambda b, g: (0, 0))],
        out_specs=pl.BlockSpec((seq, LANES), lambda b, g: (b, g)),
        out_shape=jax.ShapeDtypeStruct((batch * seq, ng * LANES), BF16),
        scratch_shapes=[pltpu.VMEM((chunk, LANES), F32)],
        compiler_params=_params("parallel", "parallel"),
        name="hgrn2",
    )(h_qig, h_f, h_qig, h_qig, lb_logits, gn)


def _pack_pairs(x):
    hw = x.shape[1] // 2
    hi = pltpu.bitcast(x[:, :hw].astype(BF16).astype(F32), jnp.int32)
    lo = pltpu.bitcast(x[:, hw:].astype(BF16).astype(F32), jnp.int32)
    return hi | lax.shift_right_logical(lo, 16)


def _unpack_pairs(w):
    hi = pltpu.bitcast(w & jnp.int32(-65536), F32)
    lo = pltpu.bitcast(lax.shift_left(w, 16), F32)
    return hi, lo


def _store_planes(ref, words):
    pw = words.shape[1] // SC_PLANES
    for h in range(SC_PLANES):
        ref[h] = words[:, h * pw:(h + 1) * pw]


def _load_planes(ref):
    return jnp.concatenate([ref[h] for h in range(SC_PLANES)], axis=1)


def _router_kernel(a1_ref, a2_ref, w1_ref, w2_ref, x_ref, g_ref, w_ref, xo_ref, o_ref, xn_ref):
    x = x_ref[...] + _dot(a1_ref[...], w1_ref[...]) + _dot(a2_ref[...], w2_ref[...])
    xo_ref[...] = x
    var = jnp.mean(x * x, axis=-1, keepdims=True)
    xn = x * lax.rsqrt(var + NORM_EPS) * g_ref[...]
    _store_planes(xn_ref, _pack_pairs(xn))
    x_hi = xn.astype(BF16)
    x_lo = (xn - x_hi.astype(F32)).astype(BF16)
    w = w_ref[...]
    w_hi = w.astype(BF16)
    w_lo = (w - w_hi.astype(F32)).astype(BF16)
    logits = _dot(x_hi, w_hi) + _dot(x_hi, w_lo) + _dot(x_lo, w_hi)
    lane = lax.broadcasted_iota(jnp.int32, logits.shape, 1)
    lg = jnp.where(lane < N_EXPERTS, logits, -jnp.inf)
    m1 = jnp.max(lg, axis=-1, keepdims=True)
    i1 = jnp.min(jnp.where(lg == m1, lane, LANES), axis=-1, keepdims=True)
    lg2 = jnp.where(lane == i1, -jnp.inf, lg)
    m2 = jnp.max(lg2, axis=-1, keepdims=True)
    i2 = jnp.min(jnp.where(lg2 == m2, lane, LANES), axis=-1, keepdims=True)
    t = jnp.exp(m2 - m1)
    den = 1.0 + t
    o_ref[...] = (jnp.where(lane == 0, i1.astype(F32), 0.0) + jnp.where(lane == 1, i2.astype(F32), 0.0)
                  + jnp.where(lane == 2, 1.0 / den, 0.0) + jnp.where(lane == 3, t / den, 0.0))


def _mix_router(a1, a2, w1, w2, x, g, w_pad):
    t, d = x.shape
    k1, k2 = a1.shape[1], a2.shape[1]
    tm = ROW_TILE
    return pl.pallas_call(
        _router_kernel,
        grid=(t // tm,),
        in_specs=[pl.BlockSpec((tm, k1), lambda i: (i, 0)),
                  pl.BlockSpec((tm, k2), lambda i: (i, 0)),
                  pl.BlockSpec((k1, d), lambda i: (0, 0)),
                  pl.BlockSpec((k2, d), lambda i: (0, 0)),
                  pl.BlockSpec((tm, d), lambda i: (i, 0)),
                  pl.BlockSpec((1, d), lambda i: (0, 0)),
                  pl.BlockSpec((d, LANES), lambda i: (0, 0))],
        out_specs=[pl.BlockSpec((tm, d), lambda i: (i, 0)),
                   pl.BlockSpec((tm, LANES), lambda i: (i, 0)),
                   pl.BlockSpec((SC_PLANES, tm, d // 2 // SC_PLANES), lambda i: (0, i, 0))],
        out_shape=[jax.ShapeDtypeStruct((t, d), F32),
                   jax.ShapeDtypeStruct((t, LANES), F32),
                   jax.ShapeDtypeStruct((SC_PLANES, t, d // 2 // SC_PLANES), jnp.int32)],
        compiler_params=_params("parallel"),
        name="mix_router",
    )(a1, a2, w1, w2, x, g, w_pad)


def _sc_mesh():
    return plsc.VectorSubcoreMesh(core_axis_name="core", subcore_axis_name="subcore")


def _sc_scatter_rows(x, idx0, idx1, n_out):
    n, d = x.shape

    @functools.partial(pl.kernel, out_type=jax.ShapeDtypeStruct((n_out, d), x.dtype), mesh=_sc_mesh(),
                       scratch_types=[])
    def scatter(x_hbm, i0_hbm, i1_hbm, o_hbm):
        def body(x_vmem, i0_vmem, i1_vmem):
            pltpu.sync_copy(x_vmem, o_hbm.at[i0_vmem.at[0]])
            pltpu.sync_copy(x_vmem, o_hbm.at[i1_vmem.at[0]])

        pltpu.emit_pipeline(
            body,
            grid=(n // SC_WINDOW,),
            in_specs=[pl.BlockSpec((SC_WINDOW, d), index_map=lambda i: (i, 0)),
                      pl.BlockSpec((1, SC_WINDOW), index_map=lambda i: (0, i)),
                      pl.BlockSpec((1, SC_WINDOW), index_map=lambda i: (0, i))],
            out_specs=[],
            core_axis_name=("core", "subcore"),
            dimension_semantics=(pltpu.PARALLEL,),
        )(x_hbm, i0_hbm, i1_hbm)

    return scatter(x, idx0.reshape(1, n), idx1.reshape(1, n))


def _sc_gather_rows(x, idx):
    n = idx.shape[0]
    d = x.shape[1]

    @functools.partial(pl.kernel, out_type=jax.ShapeDtypeStruct((n, d), x.dtype), mesh=_sc_mesh(),
                       scratch_types=[])
    def gather(x_hbm, i_hbm, o_hbm):
        def body(i_vmem, o_vmem):
            pltpu.sync_copy(x_hbm.at[i_vmem.at[0]], o_vmem)

        pltpu.emit_pipeline(
            body,
            grid=(n // SC_WINDOW,),
            in_specs=[pl.BlockSpec((1, SC_WINDOW), index_map=lambda i: (0, i))],
            out_specs=[pl.BlockSpec((SC_WINDOW, d), index_map=lambda i: (i, 0))],
            core_axis_name=("core", "subcore"),
            dimension_semantics=(pltpu.PARALLEL,),
        )(i_hbm, o_hbm)

    return gather(x, idx.reshape(1, n))


def _moe_ffn_packed_kernel(te_ref, act_ref, x_ref, wg_ref, wu_ref, wd_ref, o_ref, xn_ref, acc_ref):
    i = pl.program_id(0)
    f = pl.program_id(1)
    hw = xn_ref.shape[1] // 2

    @pl.when(act_ref[i] == 1)
    def _():
        @pl.when(f == 0)
        def _():
            hi, lo = _unpack_pairs(_load_planes(x_ref))
            xn_ref[:, :hw] = hi.astype(BF16)
            xn_ref[:, hw:] = lo.astype(BF16)
            acc_ref[...] = jnp.zeros_like(acc_ref)

        x = xn_ref[...]
        mid = (_silu(_dot(x, wg_ref[0])) * _dot(x, wu_ref[0])).astype(BF16)
        acc_ref[...] += _dot(mid, wd_ref[0])

        @pl.when(f == pl.num_programs(1) - 1)
        def _():
            _store_planes(o_ref, _pack_pairs(acc_ref[...]))


def _moe_ffn_packed(tile_e, tile_act, xs, wg, wu, wd, *, tf):
    _, p, pw = xs.shape
    d = 2 * pw * SC_PLANES
    ff = wg.shape[2]
    tm = MOE_ROW_TILE
    return pl.pallas_call(
        _moe_ffn_packed_kernel,
        grid_spec=pltpu.PrefetchScalarGridSpec(
            num_scalar_prefetch=2,
            grid=(p // tm, ff // tf),
            in_specs=[pl.BlockSpec((SC_PLANES, tm, pw), lambda i, f, te, ta: (0, i, 0)),
                      pl.BlockSpec((1, d, tf), lambda i, f, te, ta: (te[i], 0, f * ta[i])),
                      pl.BlockSpec((1, d, tf), lambda i, f, te, ta: (te[i], 0, f * ta[i])),
                      pl.BlockSpec((1, tf, d), lambda i, f, te, ta: (te[i], f * ta[i], 0))],
            out_specs=pl.BlockSpec((SC_PLANES, tm, pw), lambda i, f, te, ta: (0, i, 0)),
            scratch_shapes=[pltpu.VMEM((tm, d), BF16), pltpu.VMEM((tm, d), F32)],
        ),
        out_shape=jax.ShapeDtypeStruct((SC_PLANES, p, pw), jnp.int32),
        compiler_params=_params("arbitrary", "arbitrary"),
        name="moe_ffn",
    )(tile_e, tile_act, xs, wg, wu, wd)


def _moe_sum_kernel(x_ref, info_ref, y_ref, gf_ref, o_ref, *, final_norm):
    hw = x_ref.shape[1] // 2
    info = info_ref[...]
    g1 = info[:, 2:3]
    g2 = info[:, 3:4]
    hi1, lo1 = _unpack_pairs(jnp.concatenate([y_ref[h] for h in range(SC_PLANES)], axis=1))
    hi2, lo2 = _unpack_pairs(jnp.concatenate([y_ref[SC_PLANES + h] for h in range(SC_PLANES)], axis=1))
    x = x_ref[...]
    out_a = x[:, :hw] + g1 * hi1 + g2 * hi2
    out_b = x[:, hw:] + g1 * lo1 + g2 * lo2
    if final_norm:
        ss = jnp.sum(out_a * out_a, axis=-1, keepdims=True) + jnp.sum(out_b * out_b, axis=-1, keepdims=True)
        r = lax.rsqrt(ss * (1.0 / (2 * hw)) + NORM_EPS)
        gf = gf_ref[...]
        out_a = out_a * r * gf[:, :hw]
        out_b = out_b * r * gf[:, hw:]
    o_ref[:, :hw] = out_a
    o_ref[:, hw:] = out_b


def _moe_sum(x, info, y_pairs, gf, *, final_norm):
    t, d = x.shape
    tm = ROW_TILE
    return pl.pallas_call(
        functools.partial(_moe_sum_kernel, final_norm=final_norm),
        grid=(t // tm,),
        in_specs=[pl.BlockSpec((tm, d), lambda i: (i, 0)),
                  pl.BlockSpec((tm, LANES), lambda i: (i, 0)),
                  pl.BlockSpec((2 * SC_PLANES, tm, y_pairs.shape[2]), lambda i: (0, i, 0)),
                  pl.BlockSpec((1, d), lambda i: (0, 0))],
        out_specs=pl.BlockSpec((tm, d), lambda i: (i, 0)),
        out_shape=jax.ShapeDtypeStruct((t, d), F32),
        compiler_params=_params("parallel"),
        name="moe_sum",
    )(x, info, y_pairs, gf)


def _moe_positions(info, t):
    tm = MOE_ROW_TILE
    n_e = N_EXPERTS
    a = 2 * t
    p = a + n_e * tm
    e_flat = info[:, 0:2].astype(jnp.int32).reshape(a)
    onehot = (e_flat[:, None] == jnp.arange(n_e, dtype=jnp.int32)[None, :]).astype(jnp.int32)
    csum = jnp.cumsum(onehot, axis=0)
    rank = jnp.sum(csum * onehot, axis=1) - 1
    counts = csum[-1]
    padded = ((counts + tm - 1) // tm) * tm
    ends = jnp.cumsum(padded)
    starts = ends - padded
    pos = jnp.sum(onehot * starts[None, :], axis=1) + rank
    tile_start = jnp.arange(p // tm, dtype=jnp.int32) * tm
    tile_e = jnp.minimum(jnp.sum((tile_start[:, None] >= ends[None, :]).astype(jnp.int32), axis=1), n_e - 1)
    tile_act = (tile_start < ends[-1]).astype(jnp.int32)
    return pos.astype(jnp.int32).reshape(t, 2), tile_e.astype(jnp.int32), tile_act, p


def _mix_ffn_kernel(a1_ref, a2_ref, w1_ref, w2_ref, x_ref, g_ref, wg_ref, wu_ref, wd_ref, gf_ref, o_ref,
                    *, final_norm):
    x = x_ref[...] + _dot(a1_ref[...], w1_ref[...]) + _dot(a2_ref[...], w2_ref[...])
    var = jnp.mean(x * x, axis=-1, keepdims=True)
    xn = (x * lax.rsqrt(var + NORM_EPS) * g_ref[...]).astype(BF16)
    mid = (_silu(_dot(xn, wg_ref[...])) * _dot(xn, wu_ref[...])).astype(BF16)
    out = x + _dot(mid, wd_ref[...])
    if final_norm:
        var = jnp.mean(out * out, axis=-1, keepdims=True)
        out = out * lax.rsqrt(var + NORM_EPS) * gf_ref[...]
    o_ref[...] = out


def _resident(shape):
    return pl.BlockSpec(shape, lambda i: (0,) * len(shape), pipeline_mode=pl.Buffered(1))


def _mix_ffn(a1, a2, w1, w2, x, g, wg, wu, wd, gf, *, final_norm):
    t, d = x.shape
    ff = wg.shape[1]
    k1, k2 = a1.shape[1], a2.shape[1]
    tm = ROW_TILE
    return pl.pallas_call(
        functools.partial(_mix_ffn_kernel, final_norm=final_norm),
        grid=(t // tm,),
        in_specs=[
            pl.BlockSpec((tm, k1), lambda i: (i, 0)),
            pl.BlockSpec((tm, k2), lambda i: (i, 0)),
            _resident((k1, d)),
            _resident((k2, d)),
            pl.BlockSpec((tm, d), lambda i: (i, 0)),
            _resident((1, d)),
            _resident((d, ff)),
            _resident((d, ff)),
            _resident((ff, d)),
            _resident((1, d)),
        ],
        out_specs=pl.BlockSpec((tm, d), lambda i: (i, 0)),
        out_shape=jax.ShapeDtypeStruct((t, d), F32),
        compiler_params=_params("parallel"),
        name="mix_ffn",
    )(a1, a2, w1, w2, x, g, wg, wu, wd, gf)


def _rope_tables(seq):
    half = HEAD_DIM // 2
    inv = ROPE_THETA ** (-jnp.arange(half, dtype=F32) * 2.0 / HEAD_DIM)
    ang = jnp.arange(seq, dtype=F32)[:, None] * inv[None, :]
    cos = jnp.cos(ang)
    sin = jnp.sin(ang)
    cos_t = jnp.concatenate([cos, cos, cos, cos], axis=-1)
    sin_t = jnp.concatenate([-sin, sin, -sin, sin], axis=-1)
    return cos_t, sin_t


def _dup_heads(w):
    d, n = w.shape
    w = w.reshape(d, n // HEAD_DIM, 1, HEAD_DIM)
    return jnp.broadcast_to(w, (d, n // HEAD_DIM, 2, HEAD_DIM)).reshape(d, 2 * n)


def kernel(x, ln_mix_e, w_in_e, b_fox_f, w_out_e, ln_ffn_e, w_ffn_gate, w_ffn_up, w_ffn_down,
           ln_mix_o, w_in_o, hgrn_lb_logits, hgrn_norm_g, diff_lambda, diff_norm_g, w_out_o,
           ln_ffn_o, w_router, w_exp_gate, w_exp_up, w_exp_down, ln_final):
    batch, seq, d = x.shape
    t = batch * seq
    depth = ln_mix_e.shape[0] + ln_mix_o.shape[0]
    xs = x.reshape(t, d)
    cos_t, sin_t = _rope_tables(seq)
    row = lambda v: v.reshape(1, -1)
    fw = FOX_HEADS * HEAD_DIM
    dw = DSA_HEADS * HEAD_DIM
    kvw = DSA_KV_HEADS * HEAD_DIM
    hw = HGRN_HEADS * HEAD_DIM
    qw = DIFF_HEADS * 2 * DIFF_DIM

    for layer in range(depth):
        j = layer // 2
        if layer % 2 == 0:
            w = w_in_e[j]
            o = 0
            fq, fk, fv = w[:, o:o + fw], w[:, o + fw:o + 2 * fw], w[:, o + 2 * fw:o + 3 * fw]
            o += 3 * fw
            ffw = w[:, o:o + FOX_HEADS]
            o += FOX_HEADS
            dq = w[:, o:o + dw]
            o += dw
            dk = w[:, o:o + kvw]
            o += kvw
            dv = w[:, o:o + kvw]
            o += kvw
            iqw = w[:, o:o + IDX_HEADS * IDX_DIM]
            o += IDX_HEADS * IDX_DIM
            ikw = w[:, o:o + IDX_DIM]
            o += IDX_DIM
            iww = w[:, o:o + IDX_HEADS]
            w1 = jnp.concatenate([fq, fk, fv, _dup_heads(dv)], axis=1).astype(BF16)
            w2 = jnp.concatenate([dq, _dup_heads(dk), iqw, _dup_heads(ikw)], axis=1).astype(BF16)
            w3 = jnp.concatenate([ffw, iww, jnp.zeros((d, LANES - FOX_HEADS - IDX_HEADS), F32)], axis=1).astype(BF16)
            g = row(ln_mix_e[j])
            p1, p2, z = _rms_proj(xs, g, cos_t, sin_t, (w1, w2, w3), (False, True, False), (BF16, BF16, F32), seq=seq)
            b_pad = jnp.concatenate([b_fox_f[j], jnp.zeros((LANES - FOX_HEADS,), F32)]).reshape(1, LANES)
            ccol, crow = _fox_gate(z, b_pad, batch=batch, seq=seq)
            fox_o = _fox_attention(p1, ccol, crow, batch=batch, seq=seq)
            dsa_o = _dsa_attention(p2, p1, z, batch=batch, seq=seq)
            wo = w_out_e[j].astype(BF16)
            xs = _mix_ffn(fox_o, dsa_o, wo[:fw], wo[fw:], xs, row(ln_ffn_e[j]), w_ffn_gate[j].astype(BF16),
                          w_ffn_up[j].astype(BF16), w_ffn_down[j].astype(BF16), row(ln_final),
                          final_norm=(layer == depth - 1))
        else:
            w = w_in_o[j]
            lam_init = 0.8 - 0.6 * math.exp(-0.3 * layer)
            g = row(ln_mix_o[j])
            w_qig = jnp.concatenate([w[:, :hw], w[:, 2 * hw:4 * hw]], axis=1).astype(BF16)
            w_f = w[:, hw:2 * hw].astype(BF16)
            w_qk = w[:, 4 * hw:4 * hw + 2 * qw].astype(BF16)
            w_v = w[:, 4 * hw + 2 * qw:].astype(BF16)
            h_qig, h_f, pqk, pv = _rms_proj(xs, g, cos_t, sin_t, (w_qig, w_f, w_qk, w_v),
                                            (False, False, True, False), (BF16, F32, BF16, BF16), seq=seq)
            gn = jnp.concatenate([hgrn_norm_g[j], hgrn_norm_g[j]]).reshape(1, LANES)
            o_h = _hgrn(h_qig, h_f, hgrn_lb_logits, gn, batch=batch, seq=seq, layer=layer)
            o_d = _diff_attention(pqk, pv, diff_lambda[j], row(diff_norm_g[j]), batch=batch, seq=seq,
                                  lam_init=lam_init)
            wo = w_out_o[j].astype(BF16)
            wr = jnp.concatenate([w_router[j], jnp.zeros((d, LANES - N_EXPERTS), F32)], axis=1)
            xs, info, xn = _mix_router(o_h, o_d, wo[:hw], wo[hw:], xs, row(ln_ffn_o[j]), wr)
            pos, tile_e, tile_act, n_rows = _moe_positions(info, t)
            pw = xn.shape[2]
            plane = jnp.arange(SC_PLANES, dtype=jnp.int32)[:, None] * n_rows
            dst0 = (plane + pos[None, :, 0]).reshape(SC_PLANES * t)
            dst1 = (plane + pos[None, :, 1]).reshape(SC_PLANES * t)
            rows = _sc_scatter_rows(xn.reshape(SC_PLANES * t, pw), dst0, dst1, SC_PLANES * n_rows)
            y = _moe_ffn_packed(tile_e, tile_act, rows.reshape(SC_PLANES, n_rows, pw), w_exp_gate[j].astype(BF16),
                                w_exp_up[j].astype(BF16), w_exp_down[j].astype(BF16), tf=1792)
            src = jnp.concatenate([(plane + pos[None, :, s]).reshape(SC_PLANES * t) for s in range(2)])
            y_pairs = _sc_gather_rows(y.reshape(SC_PLANES * n_rows, pw), src).reshape(2 * SC_PLANES, t, pw)
            xs = _moe_sum(xs, info, y_pairs, row(ln_final), final_norm=(layer == depth - 1))
    if depth % 2 == 1:
        pass
    return xs.reshape(batch, seq, d)
```

```python
import functools
import math

import jax
import jax.numpy as jnp
from jax import lax
from jax.experimental import pallas as pl
from jax.experimental.pallas import tpu as pltpu
from jax.experimental.pallas import tpu_sc as plsc

F32 = jnp.float32
BF16 = jnp.bfloat16

HEAD_DIM = 64
FOX_HEADS = 8
DSA_HEADS = 8
DSA_KV_HEADS = 2
IDX_HEADS = 4
IDX_DIM = 64
DSA_TOPK = 256
HGRN_HEADS = 8
DIFF_HEADS = 4
DIFF_DIM = 64
ROPE_THETA = 10000.0
N_EXPERTS = 8
NORM_EPS = 1e-6
NEG_INF = -1e30

LANES = 128
SUBLANES = 8
VMEM_LIMIT_BYTES = 56 * 1024 * 1024

ROW_TILE = 512
ATTN_TILE = 256
QUERY_TILE = 512
HGRN_CHUNK = 256
SUB = 8
MOE_ROW_TILE = 512
SC_WINDOW = 128
SC_PLANES = 2

INT_MIN = -(2 ** 31)


def _params(*sem):
    return pltpu.CompilerParams(dimension_semantics=sem, vmem_limit_bytes=VMEM_LIMIT_BYTES)


def _dot(a, b):
    return jnp.dot(a, b, preferred_element_type=F32)


def _dot_nt(a, b):
    return lax.dot_general(a, b, (((1,), (1,)), ((), ())), preferred_element_type=F32)


def _dot_tn(a, b):
    return lax.dot_general(a, b, (((0,), (0,)), ((), ())), preferred_element_type=F32)


def _sigmoid(x):
    return 1.0 / (1.0 + jnp.exp(-x))


def _silu(x):
    return x * _sigmoid(x)


def _lane_half(shape):
    return lax.broadcasted_iota(jnp.int32, shape, len(shape) - 1) // HEAD_DIM


def _rms_proj_kernel(x_ref, g_ref, cos_ref, sin_ref, *refs, ropes):
    n = len(ropes)
    w_refs, o_refs = refs[:n], refs[n:]
    x = x_ref[...]
    var = jnp.mean(x * x, axis=-1, keepdims=True)
    xn = (x * lax.rsqrt(var + NORM_EPS) * g_ref[...]).astype(BF16)
    cos = cos_ref[...]
    sin = sin_ref[...]
    first = (lax.broadcasted_iota(jnp.int32, cos.shape, 1) % HEAD_DIM) < (HEAD_DIM // 2)
    for w_ref, o_ref, rope in zip(w_refs, o_refs, ropes):
        y = _dot(xn, w_ref[...])
        if not rope:
            o_ref[...] = y.astype(o_ref.dtype)
            continue
        for c in range(y.shape[1] // LANES):
            yc = y[:, c * LANES:(c + 1) * LANES]
            partner = jnp.where(first, pltpu.roll(yc, LANES - HEAD_DIM // 2, 1), pltpu.roll(yc, HEAD_DIM // 2, 1))
            o_ref[:, c * LANES:(c + 1) * LANES] = (yc * cos + partner * sin).astype(o_ref.dtype)


def _rms_proj(x, g, cos, sin, ws, ropes, out_dtypes, *, seq):
    t, d = x.shape
    tm = ROW_TILE
    nseq = seq // tm
    return pl.pallas_call(
        functools.partial(_rms_proj_kernel, ropes=tuple(ropes)),
        grid=(t // tm,),
        in_specs=[
            pl.BlockSpec((tm, d), lambda i: (i, 0)),
            pl.BlockSpec((1, d), lambda i: (0, 0)),
            pl.BlockSpec((tm, LANES), lambda i: (i % nseq, 0)),
            pl.BlockSpec((tm, LANES), lambda i: (i % nseq, 0)),
        ] + [pl.BlockSpec((d, w.shape[1]), lambda i: (0, 0)) for w in ws],
        out_specs=[pl.BlockSpec((tm, w.shape[1]), lambda i: (i, 0)) for w in ws],
        out_shape=[jax.ShapeDtypeStruct((t, w.shape[1]), dt) for w, dt in zip(ws, out_dtypes)],
        compiler_params=_params("parallel"),
        name="rms_proj",
    )(x, g, cos, sin, *ws)


def _cumsum_rows(x):
    rows = x.shape[0]
    row = lax.broadcasted_iota(jnp.int32, x.shape, 0)
    k = 1
    while k < rows:
        x = x + jnp.where(row >= k, pltpu.roll(x, k, 0), 0.0)
        k *= 2
    return x


def _fox_gate_kernel(z_ref, b_ref, ccol_ref, crow_ref, *, tk):
    z = z_ref[...] + b_ref[...]
    logf = jnp.minimum(z, 0.0) - jnp.log(1.0 + jnp.exp(-jnp.abs(z)))
    c = _cumsum_rows(logf)
    ccol_ref[...] = c
    for j in range(c.shape[0] // tk):
        ct = c[j * tk:(j + 1) * tk, :].T
        crow_ref[0, :, j * tk:(j + 1) * tk] = ct[:SUBLANES, :]


def _fox_gate(z, b_pad, *, batch, seq):
    tk = ATTN_TILE
    return pl.pallas_call(
        functools.partial(_fox_gate_kernel, tk=tk),
        grid=(batch,),
        in_specs=[
            pl.BlockSpec((seq, LANES), lambda b: (b, 0)),
            pl.BlockSpec((1, LANES), lambda b: (0, 0)),
        ],
        out_specs=[
            pl.BlockSpec((seq, LANES), lambda b: (b, 0)),
            pl.BlockSpec((1, SUBLANES, seq), lambda b: (b, 0, 0)),
        ],
        out_shape=[
            jax.ShapeDtypeStruct((batch * seq, LANES), F32),
            jax.ShapeDtypeStruct((batch, SUBLANES, seq), F32),
        ],
        compiler_params=_params("parallel"),
        name="fox_gate",
    )(z, b_pad)


def _masked_attend(qes, k, v, madd, tq):
    s_all = _dot_nt(jnp.concatenate(qes, axis=0), k)
    ps, ls = [], []
    for h in range(len(qes)):
        s = s_all[h * tq:(h + 1) * tq, :] + madd
        p = jnp.exp(s - jnp.max(s, axis=-1, keepdims=True))
        ls.append(jnp.sum(p, axis=-1, keepdims=True))
        ps.append(p.astype(BF16))
    o_all = _dot(jnp.concatenate(ps, axis=0), v)
    return [o_all[h * tq:(h + 1) * tq, :] / ls[h] for h in range(len(qes))]


def _causal_attend(qes, k_ref, v_ref, n, tq, row_biases=None, key_biases=None):
    nh = len(qes)
    causal = lax.broadcasted_iota(jnp.int32, (tq, tq), 1) <= lax.broadcasted_iota(jnp.int32, (tq, tq), 0)
    q_all = jnp.concatenate(qes, axis=0)
    spans = [(lo, hi) for lo, hi in ((0, n - tq), (n - tq, n)) if hi > lo]
    logits = [[] for _ in range(nh)]
    for lo, hi in spans:
        s_all = _dot_nt(q_all, k_ref[lo:hi, :])
        for h in range(nh):
            s = s_all[h * tq:(h + 1) * tq, :]
            if key_biases is not None:
                s = s - key_biases[h][:, lo:hi]
            if hi == n:
                s = jnp.where(causal, s, NEG_INF)
            logits[h].append(s)
    shifts = []
    for h in range(nh):
        m = functools.reduce(jnp.maximum, [jnp.max(s, axis=-1, keepdims=True) for s in logits[h]])
        shifts.append(-m if row_biases is None else row_biases[h] - (m + row_biases[h]))
    acc = 0.0
    ls = [0.0] * nh
    for j, (lo, hi) in enumerate(spans):
        ps = []
        for h in range(nh):
            p = jnp.exp(logits[h][j] + shifts[h])
            ls[h] = ls[h] + jnp.sum(p, axis=-1, keepdims=True)
            ps.append(p.astype(BF16))
        acc = acc + _dot(jnp.concatenate(ps, axis=0), v_ref[lo:hi, :])
    return [acc[h * tq:(h + 1) * tq, :] / ls[h] for h in range(nh)]


def _fox_kernel(q_ref, k_ref, v_ref, cc_ref, cr_ref, o_ref, *, tq, scale):
    hp = pl.program_id(1)
    seq = q_ref.shape[0]
    half = _lane_half((1, LANES))
    lane = lax.broadcasted_iota(jnp.int32, (tq, LANES), 1)
    for i in range(seq // tq):
        n = (i + 1) * tq
        q = q_ref[i * tq:n, :] * scale
        cc = cc_ref[i * tq:n, :]
        qes, cqs, cks = [], [], []
        for e in range(2):
            h = 2 * hp + e
            qes.append(jnp.where(half == e, q, jnp.zeros_like(q)))
            cqs.append(jnp.sum(jnp.where(lane == h, cc, 0.0), axis=-1, keepdims=True))
            cks.append(cr_ref[0, pl.ds(h, 1), 0:n])
        outs = _causal_attend(qes, k_ref, v_ref, n, tq, row_biases=cqs, key_biases=cks)
        o_ref[i * tq:n, :] = jnp.where(half == 0, outs[0], outs[1]).astype(o_ref.dtype)


def _fox_attention(p1, ccol, crow, *, batch, seq):
    tq = min(QUERY_TILE, seq)
    ng = FOX_HEADS // 2
    return pl.pallas_call(
        functools.partial(_fox_kernel, tq=tq, scale=HEAD_DIM ** -0.5),
        grid=(batch, ng),
        in_specs=[
            pl.BlockSpec((seq, LANES), lambda b, g: (b, g)),
            pl.BlockSpec((seq, LANES), lambda b, g: (b, ng + g)),
            pl.BlockSpec((seq, LANES), lambda b, g: (b, 2 * ng + g)),
            pl.BlockSpec((seq, LANES), lambda b, g: (b, 0)),
            pl.BlockSpec((1, SUBLANES, seq), lambda b, g: (b, 0, 0)),
        ],
        out_specs=pl.BlockSpec((seq, LANES), lambda b, g: (b, g)),
        out_shape=jax.ShapeDtypeStruct((batch * seq, ng * LANES), BF16),
        compiler_params=_params("parallel", "parallel"),
        name="fox_attention",
    )(p1, p1, p1, ccol, crow)


def _diff_kernel(q_ref, k_ref, v_ref, lam_ref, g_ref, o_ref, *, tq, scale, lam_init):
    seq = q_ref.shape[0]
    half = _lane_half((1, LANES))
    lv = lam_ref[...]
    lam = (jnp.exp(jnp.sum(lv[0:1, :] * lv[1:2, :], axis=-1, keepdims=True))
           - jnp.exp(jnp.sum(lv[2:3, :] * lv[3:4, :], axis=-1, keepdims=True)) + lam_init)
    gn = g_ref[...] * (1.0 - lam_init)
    for i in range(seq // tq):
        n = (i + 1) * tq
        q = q_ref[i * tq:n, :] * scale
        outs = _causal_attend([jnp.where(half == e, q, jnp.zeros_like(q)) for e in range(2)], k_ref, v_ref, n, tq)
        o = outs[0] - lam * outs[1]
        var = jnp.mean(o * o, axis=-1, keepdims=True)
        o_ref[i * tq:n, :] = (o * lax.rsqrt(var + NORM_EPS) * gn).astype(o_ref.dtype)


def _diff_attention(pqk, pv, lam_vecs, g, *, batch, seq, lam_init):
    tq = min(QUERY_TILE, seq)
    ng = DIFF_HEADS
    return pl.pallas_call(
        functools.partial(_diff_kernel, tq=tq, scale=DIFF_DIM ** -0.5, lam_init=lam_init),
        grid=(batch, ng),
        in_specs=[
            pl.BlockSpec((seq, LANES), lambda b, g: (b, g)),
            pl.BlockSpec((seq, LANES), lambda b, g: (b, ng + g)),
            pl.BlockSpec((seq, LANES), lambda b, g: (b, g)),
            pl.BlockSpec(lam_vecs.shape, lambda b, g: (0, 0)),
            pl.BlockSpec((1, LANES), lambda b, g: (0, 0)),
        ],
        out_specs=pl.BlockSpec((seq, LANES), lambda b, g: (b, g)),
        out_shape=jax.ShapeDtypeStruct((batch * seq, ng * LANES), BF16),
        compiler_params=_params("parallel", "parallel"),
        name="diff_attention",
    )(pqk, pqk, pv, lam_vecs, g)


def _sortable_key(x):
    bits = pltpu.bitcast(x, jnp.int32)
    return jnp.where(bits < 0, bits ^ jnp.int32(0x7FFFFFFF), bits)


def _dsa_kernel(q_ref, kd_ref, iq_ref, ik_ref, vd_ref, z_ref, o_ref, key_ref, madd_ref,
                *, tq, n_sel, scale, idx_scale, w_scale):
    i = pl.program_id(1)
    seq = key_ref.shape[1]
    nq = seq // tq
    span = 2 if nq % 2 == 0 else 1
    half = _lane_half((1, LANES))
    row = lax.broadcasted_iota(jnp.int32, (tq, tq), 0)
    col = lax.broadcasted_iota(jnp.int32, (tq, tq), 1)
    row_g = row + i * tq
    tri = jnp.where(row < col, 1.0, 0.0).astype(BF16)

    def block(n):
        z = z_ref[...]
        iq = iq_ref[...]
        ik = ik_ref[0:n, :]
        iq_all = []
        for h in range(IDX_HEADS):
            grp = iq[:, (h // 2) * LANES:(h // 2 + 1) * LANES]
            iq_all.append(jnp.where(half == h % 2, grp, jnp.zeros_like(grp)))
        logit_all = _dot_nt(jnp.concatenate(iq_all, axis=0), ik)
        sc = jnp.zeros((tq, n), F32)
        for h in range(IDX_HEADS):
            w_h = z[:, SUBLANES + h:SUBLANES + h + 1] * w_scale
            sc = sc + w_h * jnp.maximum(logit_all[h * tq:(h + 1) * tq, :] * idx_scale, 0.0)
        for j in range(n // tq):
            scj = jnp.where(col + j * tq <= row_g, sc[:, j * tq:(j + 1) * tq], -jnp.inf)
            key_ref[:, j * tq:(j + 1) * tq] = _sortable_key(scj)

        def count(pred):
            return jnp.sum(jnp.where(pred, 1.0, 0.0), axis=-1, keepdims=True)

        thr0 = jnp.where(count(key_ref[:, 0:n] >= 0) >= n_sel, 0, INT_MIN).astype(jnp.int32)

        def bit_body(it, thr):
            cand = thr | jnp.left_shift(jnp.int32(1), 30 - it)
            return jnp.where(count(key_ref[:, 0:n] >= cand) >= n_sel, cand, thr)

        thr = lax.fori_loop(0, 31, bit_body, thr0)
        need = n_sel - count(key_ref[:, 0:n] > thr)

        off = jnp.zeros((tq, 1), F32)
        for j in range(n // tq):
            kj = key_ref[:, j * tq:(j + 1) * tq]
            eq = kj == thr
            eqf = jnp.where(eq, 1.0, 0.0)
            rank = _dot(eqf.astype(BF16), tri) + off
            take = jnp.where(kj > thr, 1.0, jnp.where(eq, jnp.where(rank < need, 1.0, 0.0), 0.0))
            valid = col + j * tq <= row_g
            madd_ref[:, j * tq:(j + 1) * tq] = jnp.where(valid, jnp.where(take > 0.0, 0.0, NEG_INF), NEG_INF)
            off = off + jnp.sum(eqf, axis=-1, keepdims=True)

        q = q_ref[...] * scale
        groups_per_kv = DSA_HEADS // DSA_KV_HEADS // 2
        for g in range(DSA_KV_HEADS):
            k = kd_ref[0:n, g * LANES:(g + 1) * LANES]
            v = vd_ref[0:n, g * LANES:(g + 1) * LANES]
            qes = []
            for p in range(g * groups_per_kv, (g + 1) * groups_per_kv):
                qg = q[:, p * LANES:(p + 1) * LANES]
                qes += [jnp.where(half == e, qg, jnp.zeros_like(qg)) for e in range(2)]
            outs = _masked_attend(qes, k, v, madd_ref[:, 0:n], tq)
            for j in range(groups_per_kv):
                p = g * groups_per_kv + j
                o_ref[:, p * LANES:(p + 1) * LANES] = jnp.where(
                    half == 0, outs[2 * j], outs[2 * j + 1]).astype(o_ref.dtype)

    for c in range(nq // span):
        pl.when(i // span == c)(functools.partial(block, (c + 1) * span * tq))


def _dsa_attention(p2, p1, z, *, batch, seq):
    tq = min(ATTN_TILE, seq)
    nq = seq // tq
    n_sel = min(DSA_TOPK, seq // 4)
    qw = DSA_HEADS * HEAD_DIM
    return pl.pallas_call(
        functools.partial(_dsa_kernel, tq=tq, n_sel=float(n_sel), scale=HEAD_DIM ** -0.5,
                          idx_scale=IDX_DIM ** -0.5, w_scale=IDX_HEADS ** -0.5),
        grid=(batch, nq),
        in_specs=[
            pl.BlockSpec((tq, qw), lambda b, i: (b * nq + i, 0)),
            pl.BlockSpec((seq, 2 * LANES), lambda b, i: (b, 2)),
            pl.BlockSpec((tq, 2 * LANES), lambda b, i: (b * nq + i, 3)),
            pl.BlockSpec((seq, LANES), lambda b, i: (b, 8)),
            pl.BlockSpec((seq, 2 * LANES), lambda b, i: (b, 6)),
            pl.BlockSpec((tq, LANES), lambda b, i: (b * nq + i, 0)),
        ],
        out_specs=pl.BlockSpec((tq, qw), lambda b, i: (b * nq + i, 0)),
        out_shape=jax.ShapeDtypeStruct((batch * seq, qw), BF16),
        scratch_shapes=[pltpu.VMEM((tq, seq), jnp.int32), pltpu.VMEM((tq, seq), F32)],
        compiler_params=_params("parallel", "arbitrary"),
        name="dsa_attention",
    )(p2, p2, p2, p2, p1, z)


def _hgrn_kernel(q_ref, f_ref, i_ref, g_ref, lbl_ref, gn_ref, o_ref, b_scr, *, chunk, layer):
    seq = q_ref.shape[0]
    half = _lane_half((1, LANES))
    logits = lbl_ref[...]
    pe = jnp.exp(logits - jnp.max(logits, axis=0, keepdims=True))
    prob = pe / jnp.sum(pe, axis=0, keepdims=True)
    csum = prob[0:1, :]
    for d in range(1, layer + 1):
        csum = csum + prob[d:d + 1, :]
    lb = csum - prob[0:1, :]

    rowi = lax.broadcasted_iota(jnp.int32, (chunk, 1), 0)
    xor_rc = (lax.broadcasted_iota(jnp.int32, (chunk, chunk), 0)
              ^ lax.broadcasted_iota(jnp.int32, (chunk, chunk), 1))
    same_head = (lax.broadcasted_iota(jnp.int32, (LANES, LANES), 0) // HEAD_DIM
                 == lax.broadcasted_iota(jnp.int32, (LANES, LANES), 1) // HEAD_DIM)
    ones_bd = jnp.where(same_head, 1.0, 0.0).astype(BF16)
    gn = gn_ref[...]

    def chunk_body(c, state_t):
        r0 = pl.multiple_of(c * chunk, chunk)
        q = _silu(q_ref[pl.ds(r0, chunk), :].astype(F32))
        f = lb + (1.0 - lb) * _sigmoid(f_ref[pl.ds(r0, chunk), :])
        kc = 1.0 - f
        v = i_ref[pl.ds(r0, chunk), :].astype(F32)
        b = _cumsum_rows(jnp.log(f))
        b_scr[...] = b
        b_last = b_scr[chunk - 1:chunk, :]

        o = _dot_nt((q * jnp.exp(b)).astype(BF16), state_t.astype(BF16))
        k2 = (kc * jnp.exp(b_last - b)).astype(BF16)
        upd = _dot_tn(v.astype(BF16), k2)
        new_state = state_t * jnp.exp(b_last) + jnp.where(same_head, upd, 0.0)

        s_tot = [jnp.zeros((chunk, chunk), F32), jnp.zeros((chunk, chunk), F32)]
        m = chunk
        while m > SUB:
            hm = m // 2
            pieces = [jnp.broadcast_to(b_scr[blk * m + hm - 1:blk * m + hm, :], (m, LANES))
                      for blk in range(chunk // m)]
            ref = pieces[0] if len(pieces) == 1 else jnp.concatenate(pieces, axis=0)
            right = (rowi % m) >= hm
            qm = jnp.where(right, q * jnp.exp(jnp.where(right, b - ref, 0.0)), 0.0).astype(BF16)
            km = jnp.where(right, 0.0, kc * jnp.exp(jnp.where(right, 0.0, ref - b))).astype(BF16)
            for e in range(2):
                s = _dot_nt(jnp.where(half == e, qm, jnp.zeros_like(qm)), km)
                s_tot[e] = s_tot[e] + (s if m == chunk else jnp.where(xor_rc < m, s, 0.0))
            m = hm
        vb = v.astype(BF16)
        for e in range(2):
            o = o + _dot(s_tot[e].astype(BF16), jnp.where(half == e, vb, jnp.zeros_like(vb)))

        for d in range(SUB):
            ok = (rowi % SUB) >= d
            bs = b if d == 0 else pltpu.roll(b, d, 0)
            ks = kc if d == 0 else pltpu.roll(kc, d, 0)
            vs = v if d == 0 else pltpu.roll(v, d, 0)
            x = jnp.where(ok, q * ks * jnp.exp(jnp.where(ok, b - bs, 0.0)), 0.0)
            o = o + _dot(x.astype(BF16), ones_bd) * vs

        sq = o * o
        s0 = jnp.sum(jnp.where(half == 0, sq, 0.0), axis=-1, keepdims=True)
        s1 = jnp.sum(jnp.where(half == 1, sq, 0.0), axis=-1, keepdims=True)
        var = jnp.where(half == 0, s0, s1) * (1.0 / HEAD_DIM)
        y = o * lax.rsqrt(var + NORM_EPS) * gn * _silu(g_ref[pl.ds(r0, chunk), :].astype(F32))
        o_ref[pl.ds(r0, chunk), :] = y.astype(o_ref.dtype)
        return new_state

    lax.fori_loop(0, seq // chunk, chunk_body, jnp.zeros((LANES, LANES), F32))


def _hgrn(h_qig, h_f, lb_logits, gn, *, batch, seq, layer):
    ng = HGRN_HEADS // 2
    chunk = min(HGRN_CHUNK, seq)
    spec = lambda off: pl.BlockSpec((seq, LANES), lambda b, g: (b, off * ng + g))
    return pl.pallas_call(
        functools.partial(_hgrn_kernel, chunk=chunk, layer=layer),
        grid=(batch, ng),
        in_specs=[spec(0), spec(0), spec(1), spec(2),
                  pl.BlockSpec((lb_logits.shape[0], LANES), lambda b, g: (0, g)),
                  pl.BlockSpec((1, LANES), lambda b, g: (0, 0))],
        out_specs=pl.BlockSpec((seq, LANES), lambda b, g: (b, g)),
        out_shape=jax.ShapeDtypeStruct((batch * seq, ng * LANES), BF16),
        scratch_shapes=[pltpu.VMEM((chunk, LANES), F32)],
        compiler_params=_params("parallel", "parallel"),
        name="hgrn2",
    )(h_qig, h_f, h_qig, h_qig, lb_logits, gn)


def _pack_pairs(x):
    hw = x.shape[1] // 2
    hi = pltpu.bitcast(x[:, :hw].astype(BF16).astype(F32), jnp.int32)
    lo = pltpu.bitcast(x[:, hw:].astype(BF16).astype(F32), jnp.int32)
    return hi | lax.shift_right_logical(lo, 16)


def _unpack_pairs(w):
    hi = pltpu.bitcast(w & jnp.int32(-65536), F32)
    lo = pltpu.bitcast(lax.shift_left(w, 16), F32)
    return hi, lo


def _store_planes(ref, words):
    pw = words.shape[1] // SC_PLANES
    for h in range(SC_PLANES):
        ref[h] = words[:, h * pw:(h + 1) * pw]


def _load_planes(ref):
    return jnp.concatenate([ref[h] for h in range(SC_PLANES)], axis=1)


def _router_kernel(a1_ref, a2_ref, w1_ref, w2_ref, x_ref, g_ref, w_ref, xo_ref, o_ref, xn_ref):
    x = x_ref[...] + _dot(a1_ref[...], w1_ref[...]) + _dot(a2_ref[...], w2_ref[...])
    xo_ref[...] = x
    var = jnp.mean(x * x, axis=-1, keepdims=True)
    xn = x * lax.rsqrt(var + NORM_EPS) * g_ref[...]
    _store_planes(xn_ref, _pack_pairs(xn))
    x_hi = xn.astype(BF16)
    x_lo = (xn - x_hi.astype(F32)).astype(BF16)
    w = w_ref[...]
    w_hi = w.astype(BF16)
    w_lo = (w - w_hi.astype(F32)).astype(BF16)
    logits = _dot(x_hi, w_hi) + _dot(x_hi, w_lo) + _dot(x_lo, w_hi)
    lane = lax.broadcasted_iota(jnp.int32, logits.shape, 1)
    lg = jnp.where(lane < N_EXPERTS, logits, -jnp.inf)
    m1 = jnp.max(lg, axis=-1, keepdims=True)
    i1 = jnp.min(jnp.where(lg == m1, lane, LANES), axis=-1, keepdims=True)
    lg2 = jnp.where(lane == i1, -jnp.inf, lg)
    m2 = jnp.max(lg2, axis=-1, keepdims=True)
    i2 = jnp.min(jnp.where(lg2 == m2, lane, LANES), axis=-1, keepdims=True)
    t = jnp.exp(m2 - m1)
    den = 1.0 + t
    o_ref[...] = (jnp.where(lane == 0, i1.astype(F32), 0.0) + jnp.where(lane == 1, i2.astype(F32), 0.0)
                  + jnp.where(lane == 2, 1.0 / den, 0.0) + jnp.where(lane == 3, t / den, 0.0))


def _mix_router(a1, a2, w1, w2, x, g, w_pad):
    t, d = x.shape
    k1, k2 = a1.shape[1], a2.shape[1]
    tm = ROW_TILE
    return pl.pallas_call(
        _router_kernel,
        grid=(t // tm,),
        in_specs=[pl.BlockSpec((tm, k1), lambda i: (i, 0)),
                  pl.BlockSpec((tm, k2), lambda i: (i, 0)),
                  pl.BlockSpec((k1, d), lambda i: (0, 0)),
                  pl.BlockSpec((k2, d), lambda i: (0, 0)),
                  pl.BlockSpec((tm, d), lambda i: (i, 0)),
                  pl.BlockSpec((1, d), lambda i: (0, 0)),
                  pl.BlockSpec((d, LANES), lambda i: (0, 0))],
        out_specs=[pl.BlockSpec((tm, d), lambda i: (i, 0)),
                   pl.BlockSpec((tm, LANES), lambda i: (i, 0)),
                   pl.BlockSpec((SC_PLANES, tm, d // 2 // SC_PLANES), lambda i: (0, i, 0))],
        out_shape=[jax.ShapeDtypeStruct((t, d), F32),
                   jax.ShapeDtypeStruct((t, LANES), F32),
                   jax.ShapeDtypeStruct((SC_PLANES, t, d // 2 // SC_PLANES), jnp.int32)],
        compiler_params=_params("parallel"),
        name="mix_router",
    )(a1, a2, w1, w2, x, g, w_pad)


def _sc_mesh():
    return plsc.VectorSubcoreMesh(core_axis_name="core", subcore_axis_name="subcore")


def _sc_scatter_rows(x, idx0, idx1, n_out):
    n, d = x.shape

    @functools.partial(pl.kernel, out_type=jax.ShapeDtypeStruct((n_out, d), x.dtype), mesh=_sc_mesh(),
                       scratch_types=[])
    def scatter(x_hbm, i0_hbm, i1_hbm, o_hbm):
        def body(x_vmem, i0_vmem, i1_vmem):
            pltpu.sync_copy(x_vmem, o_hbm.at[i0_vmem.at[0]])
            pltpu.sync_copy(x_vmem, o_hbm.at[i1_vmem.at[0]])

        pltpu.emit_pipeline(
            body,
            grid=(n // SC_WINDOW,),
            in_specs=[pl.BlockSpec((SC_WINDOW, d), index_map=lambda i: (i, 0)),
                      pl.BlockSpec((1, SC_WINDOW), index_map=lambda i: (0, i)),
                      pl.BlockSpec((1, SC_WINDOW), index_map=lambda i: (0, i))],
            out_specs=[],
            core_axis_name=("core", "subcore"),
            dimension_semantics=(pltpu.PARALLEL,),
        )(x_hbm, i0_hbm, i1_hbm)

    return scatter(x, idx0.reshape(1, n), idx1.reshape(1, n))


def _sc_gather_rows(x, idx):
    n = idx.shape[0]
    d = x.shape[1]

    @functools.partial(pl.kernel, out_type=jax.ShapeDtypeStruct((n, d), x.dtype), mesh=_sc_mesh(),
                       scratch_types=[])
    def gather(x_hbm, i_hbm, o_hbm):
        def body(i_vmem, o_vmem):
            pltpu.sync_copy(x_hbm.at[i_vmem.at[0]], o_vmem)

        pltpu.emit_pipeline(
            body,
            grid=(n // SC_WINDOW,),
            in_specs=[pl.BlockSpec((1, SC_WINDOW), index_map=lambda i: (0, i))],
            out_specs=[pl.BlockSpec((SC_WINDOW, d), index_map=lambda i: (i, 0))],
            core_axis_name=("core", "subcore"),
            dimension_semantics=(pltpu.PARALLEL,),
        )(i_hbm, o_hbm)

    return gather(x, idx.reshape(1, n))


def _moe_ffn_packed_kernel(te_ref, act_ref, x_ref, wg_ref, wu_ref, wd_ref, o_ref, xn_ref, acc_ref):
    i = pl.program_id(0)
    f = pl.program_id(1)
    hw = xn_ref.shape[1] // 2

    @pl.when(act_ref[i] == 1)
    def _():
        @pl.when(f == 0)
        def _():
            hi, lo = _unpack_pairs(_load_planes(x_ref))
            xn_ref[:, :hw] = hi.astype(BF16)
            xn_ref[:, hw:] = lo.astype(BF16)
            acc_ref[...] = jnp.zeros_like(acc_ref)

        x = xn_ref[...]
        mid = (_silu(_dot(x, wg_ref[0])) * _dot(x, wu_ref[0])).astype(BF16)
        acc_ref[...] += _dot(mid, wd_ref[0])

        @pl.when(f == pl.num_programs(1) - 1)
        def _():
            _store_planes(o_ref, _pack_pairs(acc_ref[...]))


def _moe_ffn_packed(tile_e, tile_act, xs, wg, wu, wd, *, tf):
    _, p, pw = xs.shape
    d = 2 * pw * SC_PLANES
    ff = wg.shape[2]
    tm = MOE_ROW_TILE
    return pl.pallas_call(
        _moe_ffn_packed_kernel,
        grid_spec=pltpu.PrefetchScalarGridSpec(
            num_scalar_prefetch=2,
            grid=(p // tm, ff // tf),
            in_specs=[pl.BlockSpec((SC_PLANES, tm, pw), lambda i, f, te, ta: (0, i, 0)),
                      pl.BlockSpec((1, d, tf), lambda i, f, te, ta: (te[i], 0, f * ta[i])),
                      pl.BlockSpec((1, d, tf), lambda i, f, te, ta: (te[i], 0, f * ta[i])),
                      pl.BlockSpec((1, tf, d), lambda i, f, te, ta: (te[i], f * ta[i], 0))],
            out_specs=pl.BlockSpec((SC_PLANES, tm, pw), lambda i, f, te, ta: (0, i, 0)),
            scratch_shapes=[pltpu.VMEM((tm, d), BF16), pltpu.VMEM((tm, d), F32)],
        ),
        out_shape=jax.ShapeDtypeStruct((SC_PLANES, p, pw), jnp.int32),
        compiler_params=_params("arbitrary", "arbitrary"),
        name="moe_ffn",
    )(tile_e, tile_act, xs, wg, wu, wd)


def _moe_sum_kernel(x_ref, info_ref, y_ref, gf_ref, o_ref, *, final_norm):
    hw = x_ref.shape[1] // 2
    info = info_ref[...]
    g1 = info[:, 2:3]
    g2 = info[:, 3:4]
    hi1, lo1 = _unpack_pairs(jnp.concatenate([y_ref[h] for h in range(SC_PLANES)], axis=1))
    hi2, lo2 = _unpack_pairs(jnp.concatenate([y_ref[SC_PLANES + h] for h in range(SC_PLANES)], axis=1))
    x = x_ref[...]
    out_a = x[:, :hw] + g1 * hi1 + g2 * hi2
    out_b = x[:, hw:] + g1 * lo1 + g2 * lo2
    if final_norm:
        ss = jnp.sum(out_a * out_a, axis=-1, keepdims=True) + jnp.sum(out_b * out_b, axis=-1, keepdims=True)
        r = lax.rsqrt(ss * (1.0 / (2 * hw)) + NORM_EPS)
        gf = gf_ref[...]
        out_a = out_a * r * gf[:, :hw]
        out_b = out_b * r * gf[:, hw:]
    o_ref[:, :hw] = out_a
    o_ref[:, hw:] = out_b


def _moe_sum(x, info, y_pairs, gf, *, final_norm):
    t, d = x.shape
    tm = ROW_TILE
    return pl.pallas_call(
        functools.partial(_moe_sum_kernel, final_norm=final_norm),
        grid=(t // tm,),
        in_specs=[pl.BlockSpec((tm, d), lambda i: (i, 0)),
                  pl.BlockSpec((tm, LANES), lambda i: (i, 0)),
                  pl.BlockSpec((2 * SC_PLANES, tm, y_pairs.shape[2]), lambda i: (0, i, 0)),
                  pl.BlockSpec((1, d), lambda i: (0, 0))],
        out_specs=pl.BlockSpec((tm, d), lambda i: (i, 0)),
        out_shape=jax.ShapeDtypeStruct((t, d), F32),
        compiler_params=_params("parallel"),
        name="moe_sum",
    )(x, info, y_pairs, gf)


def _moe_positions(info, t):
    tm = MOE_ROW_TILE
    n_e = N_EXPERTS
    a = 2 * t
    p = a + n_e * tm
    e_flat = info[:, 0:2].astype(jnp.int32).reshape(a)
    onehot = (e_flat[:, None] == jnp.arange(n_e, dtype=jnp.int32)[None, :]).astype(jnp.int32)
    csum = jnp.cumsum(onehot, axis=0)
    rank = jnp.sum(csum * onehot, axis=1) - 1
    counts = csum[-1]
    padded = ((counts + tm - 1) // tm) * tm
    ends = jnp.cumsum(padded)
    starts = ends - padded
    pos = jnp.sum(onehot * starts[None, :], axis=1) + rank
    tile_start = jnp.arange(p // tm, dtype=jnp.int32) * tm
    tile_e = jnp.minimum(jnp.sum((tile_start[:, None] >= ends[None, :]).astype(jnp.int32), axis=1), n_e - 1)
    tile_act = (tile_start < ends[-1]).astype(jnp.int32)
    return pos.astype(jnp.int32).reshape(t, 2), tile_e.astype(jnp.int32), tile_act, p


def _mix_ffn_kernel(a1_ref, a2_ref, w1_ref, w2_ref, x_ref, g_ref, wg_ref, wu_ref, wd_ref, gf_ref, o_ref,
                    *, final_norm):
    x = x_ref[...] + _dot(a1_ref[...], w1_ref[...]) + _dot(a2_ref[...], w2_ref[...])
    var = jnp.mean(x * x, axis=-1, keepdims=True)
    xn = (x * lax.rsqrt(var + NORM_EPS) * g_ref[...]).astype(BF16)
    mid = (_silu(_dot(xn, wg_ref[...])) * _dot(xn, wu_ref[...])).astype(BF16)
    out = x + _dot(mid, wd_ref[...])
    if final_norm:
        var = jnp.mean(out * out, axis=-1, keepdims=True)
        out = out * lax.rsqrt(var + NORM_EPS) * gf_ref[...]
    o_ref[...] = out


def _resident(shape):
    return pl.BlockSpec(shape, lambda i: (0,) * len(shape), pipeline_mode=pl.Buffered(1))


def _mix_ffn(a1, a2, w1, w2, x, g, wg, wu, wd, gf, *, final_norm):
    t, d = x.shape
    ff = wg.shape[1]
    k1, k2 = a1.shape[1], a2.shape[1]
    tm = ROW_TILE
    return pl.pallas_call(
        functools.partial(_mix_ffn_kernel, final_norm=final_norm),
        grid=(t // tm,),
        in_specs=[
            pl.BlockSpec((tm, k1), lambda i: (i, 0)),
            pl.BlockSpec((tm, k2), lambda i: (i, 0)),
            _resident((k1, d)),
            _resident((k2, d)),
            pl.BlockSpec((tm, d), lambda i: (i, 0)),
            _resident((1, d)),
            _resident((d, ff)),
            _resident((d, ff)),
            _resident((ff, d)),
            _resident((1, d)),
        ],
        out_specs=pl.BlockSpec((tm, d), lambda i: (i, 0)),
        out_shape=jax.ShapeDtypeStruct((t, d), F32),
        compiler_params=_params("parallel"),
        name="mix_ffn",
    )(a1, a2, w1, w2, x, g, wg, wu, wd, gf)


def _rope_tables(seq):
    half = HEAD_DIM // 2
    inv = ROPE_THETA ** (-jnp.arange(half, dtype=F32) * 2.0 / HEAD_DIM)
    ang = jnp.arange(seq, dtype=F32)[:, None] * inv[None, :]
    cos = jnp.cos(ang)
    sin = jnp.sin(ang)
    cos_t = jnp.concatenate([cos, cos, cos, cos], axis=-1)
    sin_t = jnp.concatenate([-sin, sin, -sin, sin], axis=-1)
    return cos_t, sin_t


def _dup_heads(w):
    d, n = w.shape
    w = w.reshape(d, n // HEAD_DIM, 1, HEAD_DIM)
    return jnp.broadcast_to(w, (d, n // HEAD_DIM, 2, HEAD_DIM)).reshape(d, 2 * n)


def kernel(x, ln_mix_e, w_in_e, b_fox_f, w_out_e, ln_ffn_e, w_ffn_gate, w_ffn_up, w_ffn_down,
           ln_mix_o, w_in_o, hgrn_lb_logits, hgrn_norm_g, diff_lambda, diff_norm_g, w_out_o,
           ln_ffn_o, w_router, w_exp_gate, w_exp_up, w_exp_down, ln_final):
    batch, seq, d = x.shape
    t = batch * seq
    depth = ln_mix_e.shape[0] + ln_mix_o.shape[0]
    xs = x.reshape(t, d)
    cos_t, sin_t = _rope_tables(seq)
    row = lambda v: v.reshape(1, -1)
    fw = FOX_HEADS * HEAD_DIM
    dw = DSA_HEADS * HEAD_DIM
    kvw = DSA_KV_HEADS * HEAD_DIM
    hw = HGRN_HEADS * HEAD_DIM
    qw = DIFF_HEADS * 2 * DIFF_DIM

    for layer in range(depth):
        j = layer // 2
        if layer % 2 == 0:
            w = w_in_e[j]
            o = 0
            fq, fk, fv = w[:, o:o + fw], w[:, o + fw:o + 2 * fw], w[:, o + 2 * fw:o + 3 * fw]
            o += 3 * fw
            ffw = w[:, o:o + FOX_HEADS]
            o += FOX_HEADS
            dq = w[:, o:o + dw]
            o += dw
            dk = w[:, o:o + kvw]
            o += kvw
            dv = w[:, o:o + kvw]
            o += kvw
            iqw = w[:, o:o + IDX_HEADS * IDX_DIM]
            o += IDX_HEADS * IDX_DIM
            ikw = w[:, o:o + IDX_DIM]
            o += IDX_DIM
            iww = w[:, o:o + IDX_HEADS]
            w1 = jnp.concatenate([fq, fk, fv, _dup_heads(dv)], axis=1).astype(BF16)
            w2 = jnp.concatenate([dq, _dup_heads(dk), iqw, _dup_heads(ikw)], axis=1).astype(BF16)
            w3 = jnp.concatenate([ffw, iww, jnp.zeros((d, LANES - FOX_HEADS - IDX_HEADS), F32)], axis=1).astype(BF16)
            g = row(ln_mix_e[j])
            p1, p2, z = _rms_proj(xs, g, cos_t, sin_t, (w1, w2, w3), (False, True, False), (BF16, BF16, F32), seq=seq)
            b_pad = jnp.concatenate([b_fox_f[j], jnp.zeros((LANES - FOX_HEADS,), F32)]).reshape(1, LANES)
            ccol, crow = _fox_gate(z, b_pad, batch=batch, seq=seq)
            fox_o = _fox_attention(p1, ccol, crow, batch=batch, seq=seq)
            dsa_o = _dsa_attention(p2, p1, z, batch=batch, seq=seq)
            wo = w_out_e[j].astype(BF16)
            xs = _mix_ffn(fox_o, dsa_o, wo[:fw], wo[fw:], xs, row(ln_ffn_e[j]), w_ffn_gate[j].astype(BF16),
                          w_ffn_up[j].astype(BF16), w_ffn_down[j].astype(BF16), row(ln_final),
                          final_norm=(layer == depth - 1))
        else:
            w = w_in_o[j]
            lam_init = 0.8 - 0.6 * math.exp(-0.3 * layer)
            g = row(ln_mix_o[j])
            w_qig = jnp.concatenate([w[:, :hw], w[:, 2 * hw:4 * hw]], axis=1).astype(BF16)
            w_f = w[:, hw:2 * hw].astype(BF16)
            w_qk = w[:, 4 * hw:4 * hw + 2 * qw].astype(BF16)
            w_v = w[:, 4 * hw + 2 * qw:].astype(BF16)
            h_qig, h_f, pqk, pv = _rms_proj(xs, g, cos_t, sin_t, (w_qig, w_f, w_qk, w_v),
                                            (False, False, True, False), (BF16, F32, BF16, BF16), seq=seq)
            gn = jnp.concatenate([hgrn_norm_g[j], hgrn_norm_g[j]]).reshape(1, LANES)
            o_h = _hgrn(h_qig, h_f, hgrn_lb_logits, gn, batch=batch, seq=seq, layer=layer)
            o_d = _diff_attention(pqk, pv, diff_lambda[j], row(diff_norm_g[j]), batch=batch, seq=seq,
                                  lam_init=lam_init)
            wo = w_out_o[j].astype(BF16)
            wr = jnp.concatenate([w_router[j], jnp.zeros((d, LANES - N_EXPERTS), F32)], axis=1)
            xs, info, xn = _mix_router(o_h, o_d, wo[:hw], wo[hw:], xs, row(ln_ffn_o[j]), wr)
            pos, tile_e, tile_act, n_rows = _moe_positions(info, t)
            pw = xn.shape[2]
            plane = jnp.arange(SC_PLANES, dtype=jnp.int32)[:, None] * n_rows
            dst0 = (plane + pos[None, :, 0]).reshape(SC_PLANES * t)
            dst1 = (plane + pos[None, :, 1]).reshape(SC_PLANES * t)
            rows = _sc_scatter_rows(xn.reshape(SC_PLANES * t, pw), dst0, dst1, SC_PLANES * n_rows)
            y = _moe_ffn_packed(tile_e, tile_act, rows.reshape(SC_PLANES, n_rows, pw), w_exp_gate[j].astype(BF16),
                                w_exp_up[j].astype(BF16), w_exp_down[j].astype(BF16), tf=1792)
            src = jnp.concatenate([(plane + pos[None, :, s]).reshape(SC_PLANES * t) for s in range(2)])
            y_pairs = _sc_gather_rows(y.reshape(SC_PLANES * n_rows, pw), src).reshape(2 * SC_PLANES, t, pw)
            xs = _moe_sum(xs, info, y_pairs, row(ln_final), final_norm=(layer == depth - 1))
    if depth % 2 == 1:
        pass
    return xs.reshape(batch, seq, d)
```

```python
import functools
import math

import jax
import jax.numpy as jnp
from jax import lax
from jax.experimental import pallas as pl
from jax.experimental.pallas import tpu as pltpu
from jax.experimental.pallas import tpu_sc as plsc

F32 = jnp.float32
BF16 = jnp.bfloat16

HEAD_DIM = 64
FOX_HEADS = 8
DSA_HEADS = 8
DSA_KV_HEADS = 2
IDX_HEADS = 4
IDX_DIM = 64
DSA_TOPK = 256
HGRN_HEADS = 8
DIFF_HEADS = 4
DIFF_DIM = 64
ROPE_THETA = 10000.0
N_EXPERTS = 8
NORM_EPS = 1e-6
NEG_INF = -1e30

LANES = 128
SUBLANES = 8
VMEM_LIMIT_BYTES = 56 * 1024 * 1024

ROW_TILE = 512
ATTN_TILE = 256
QUERY_TILE = 512
HGRN_CHUNK = 256
SUB = 8
MOE_ROW_TILE = 512
SC_WINDOW = 128
SC_PLANES = 2

INT_MIN = -(2 ** 31)


def _params(*sem):
    return pltpu.CompilerParams(dimension_semantics=sem, vmem_limit_bytes=VMEM_LIMIT_BYTES)


def _dot(a, b):
    return jnp.dot(a, b, preferred_element_type=F32)


def _dot_nt(a, b):
    return lax.dot_general(a, b, (((1,), (1,)), ((), ())), preferred_element_type=F32)


def _dot_tn(a, b):
    return lax.dot_general(a, b, (((0,), (0,)), ((), ())), preferred_element_type=F32)


def _sigmoid(x):
    return 1.0 / (1.0 + jnp.exp(-x))


def _silu(x):
    return x * _sigmoid(x)


def _lane_half(shape):
    return lax.broadcasted_iota(jnp.int32, shape, len(shape) - 1) // HEAD_DIM


def _rms_proj_kernel(x_ref, g_ref, cos_ref, sin_ref, *refs, ropes):
    n = len(ropes)
    w_refs, o_refs = refs[:n], refs[n:]
    x = x_ref[...]
    var = jnp.mean(x * x, axis=-1, keepdims=True)
    xn = (x * lax.rsqrt(var + NORM_EPS) * g_ref[...]).astype(BF16)
    cos = cos_ref[...]
    sin = sin_ref[...]
    first = (lax.broadcasted_iota(jnp.int32, cos.shape, 1) % HEAD_DIM) < (HEAD_DIM // 2)
    for w_ref, o_ref, rope in zip(w_refs, o_refs, ropes):
        y = _dot(xn, w_ref[...])
        if not rope:
            o_ref[...] = y.astype(o_ref.dtype)
            continue
        for c in range(y.shape[1] // LANES):
            yc = y[:, c * LANES:(c + 1) * LANES]
            partner = jnp.where(first, pltpu.roll(yc, LANES - HEAD_DIM // 2, 1), pltpu.roll(yc, HEAD_DIM // 2, 1))
            o_ref[:, c * LANES:(c + 1) * LANES] = (yc * cos + partner * sin).astype(o_ref.dtype)


def _rms_proj(x, g, cos, sin, ws, ropes, out_dtypes, *, seq):
    t, d = x.shape
    tm = ROW_TILE
    nseq = seq // tm
    return pl.pallas_call(
        functools.partial(_rms_proj_kernel, ropes=tuple(ropes)),
        grid=(t // tm,),
        in_specs=[
            pl.BlockSpec((tm, d), lambda i: (i, 0)),
            pl.BlockSpec((1, d), lambda i: (0, 0)),
            pl.BlockSpec((tm, LANES), lambda i: (i % nseq, 0)),
            pl.BlockSpec((tm, LANES), lambda i: (i % nseq, 0)),
        ] + [pl.BlockSpec((d, w.shape[1]), lambda i: (0, 0)) for w in ws],
        out_specs=[pl.BlockSpec((tm, w.shape[1]), lambda i: (i, 0)) for w in ws],
        out_shape=[jax.ShapeDtypeStruct((t, w.shape[1]), dt) for w, dt in zip(ws, out_dtypes)],
        compiler_params=_params("parallel"),
        name="rms_proj",
    )(x, g, cos, sin, *ws)


def _cumsum_rows(x):
    rows = x.shape[0]
    row = lax.broadcasted_iota(jnp.int32, x.shape, 0)
    k = 1
    while k < rows:
        x = x + jnp.where(row >= k, pltpu.roll(x, k, 0), 0.0)
        k *= 2
    return x


def _fox_gate_kernel(z_ref, b_ref, ccol_ref, crow_ref, *, tk):
    z = z_ref[...] + b_ref[...]
    logf = jnp.minimum(z, 0.0) - jnp.log(1.0 + jnp.exp(-jnp.abs(z)))
    c = _cumsum_rows(logf)
    ccol_ref[...] = c
    for j in range(c.shape[0] // tk):
        ct = c[j * tk:(j + 1) * tk, :].T
        crow_ref[0, :, j * tk:(j + 1) * tk] = ct[:SUBLANES, :]


def _fox_gate(z, b_pad, *, batch, seq):
    tk = ATTN_TILE
    return pl.pallas_call(
        functools.partial(_fox_gate_kernel, tk=tk),
        grid=(batch,),
        in_specs=[
            pl.BlockSpec((seq, LANES), lambda b: (b, 0)),
            pl.BlockSpec((1, LANES), lambda b: (0, 0)),
        ],
        out_specs=[
            pl.BlockSpec((seq, LANES), lambda b: (b, 0)),
            pl.BlockSpec((1, SUBLANES, seq), lambda b: (b, 0, 0)),
        ],
        out_shape=[
            jax.ShapeDtypeStruct((batch * seq, LANES), F32),
            jax.ShapeDtypeStruct((batch, SUBLANES, seq), F32),
        ],
        compiler_params=_params("parallel"),
        name="fox_gate",
    )(z, b_pad)


def _softmax_pv(s, v):
    m = jnp.max(s, axis=-1, keepdims=True)
    p = jnp.exp(s - m)
    l = jnp.sum(p, axis=-1, keepdims=True)
    return _dot(p.astype(BF16), v) / l


def _causal_attend(qe, k_ref, v_ref, n, tq, row_bias=None, key_bias=None):
    causal = lax.broadcasted_iota(jnp.int32, (tq, tq), 1) <= lax.broadcasted_iota(jnp.int32, (tq, tq), 0)
    parts = []
    for lo, hi in ((0, n - tq), (n - tq, n)):
        if hi == lo:
            continue
        s = _dot_nt(qe, k_ref[lo:hi, :])
        if key_bias is not None:
            s = s - key_bias[:, lo:hi]
        if hi == n:
            s = jnp.where(causal, s, NEG_INF)
        parts.append((s, v_ref[lo:hi, :]))
    m = functools.reduce(jnp.maximum, [jnp.max(s, axis=-1, keepdims=True) for s, _ in parts])
    shift = -m if row_bias is None else row_bias - (m + row_bias)
    acc = 0.0
    l = 0.0
    for s, v in parts:
        p = jnp.exp(s + shift)
        l = l + jnp.sum(p, axis=-1, keepdims=True)
        acc = acc + _dot(p.astype(BF16), v)
    return acc / l


def _fox_kernel(q_ref, k_ref, v_ref, cc_ref, cr_ref, o_ref, *, tq, scale):
    hp = pl.program_id(1)
    seq = q_ref.shape[0]
    half = _lane_half((1, LANES))
    lane = lax.broadcasted_iota(jnp.int32, (tq, LANES), 1)
    for i in range(seq // tq):
        n = (i + 1) * tq
        q = q_ref[i * tq:n, :] * scale
        cc = cc_ref[i * tq:n, :]
        outs = []
        for e in range(2):
            h = 2 * hp + e
            qe = jnp.where(half == e, q, jnp.zeros_like(q))
            cq = jnp.sum(jnp.where(lane == h, cc, 0.0), axis=-1, keepdims=True)
            ck = cr_ref[0, pl.ds(h, 1), 0:n]
            outs.append(_causal_attend(qe, k_ref, v_ref, n, tq, row_bias=cq, key_bias=ck))
        o_ref[i * tq:n, :] = jnp.where(half == 0, outs[0], outs[1]).astype(o_ref.dtype)


def _fox_attention(p1, ccol, crow, *, batch, seq):
    tq = min(QUERY_TILE, seq)
    ng = FOX_HEADS // 2
    return pl.pallas_call(
        functools.partial(_fox_kernel, tq=tq, scale=HEAD_DIM ** -0.5),
        grid=(batch, ng),
        in_specs=[
            pl.BlockSpec((seq, LANES), lambda b, g: (b, g)),
            pl.BlockSpec((seq, LANES), lambda b, g: (b, ng + g)),
            pl.BlockSpec((seq, LANES), lambda b, g: (b, 2 * ng + g)),
            pl.BlockSpec((seq, LANES), lambda b, g: (b, 0)),
            pl.BlockSpec((1, SUBLANES, seq), lambda b, g: (b, 0, 0)),
        ],
        out_specs=pl.BlockSpec((seq, LANES), lambda b, g: (b, g)),
        out_shape=jax.ShapeDtypeStruct((batch * seq, ng * LANES), BF16),
        compiler_params=_params("parallel", "parallel"),
        name="fox_attention",
    )(p1, p1, p1, ccol, crow)


def _diff_kernel(q_ref, k_ref, v_ref, lam_ref, g_ref, o_ref, *, tq, scale, lam_init):
    seq = q_ref.shape[0]
    half = _lane_half((1, LANES))
    lv = lam_ref[...]
    lam = (jnp.exp(jnp.sum(lv[0:1, :] * lv[1:2, :], axis=-1, keepdims=True))
           - jnp.exp(jnp.sum(lv[2:3, :] * lv[3:4, :], axis=-1, keepdims=True)) + lam_init)
    gn = g_ref[...] * (1.0 - lam_init)
    for i in range(seq // tq):
        n = (i + 1) * tq
        q = q_ref[i * tq:n, :] * scale
        outs = []
        for e in range(2):
            qe = jnp.where(half == e, q, jnp.zeros_like(q))
            outs.append(_causal_attend(qe, k_ref, v_ref, n, tq))
        o = outs[0] - lam * outs[1]
        var = jnp.mean(o * o, axis=-1, keepdims=True)
        o_ref[i * tq:n, :] = (o * lax.rsqrt(var + NORM_EPS) * gn).astype(o_ref.dtype)


def _diff_attention(pqk, pv, lam_vecs, g, *, batch, seq, lam_init):
    tq = min(QUERY_TILE, seq)
    ng = DIFF_HEADS
    return pl.pallas_call(
        functools.partial(_diff_kernel, tq=tq, scale=DIFF_DIM ** -0.5, lam_init=lam_init),
        grid=(batch, ng),
        in_specs=[
            pl.BlockSpec((seq, LANES), lambda b, g: (b, g)),
            pl.BlockSpec((seq, LANES), lambda b, g: (b, ng + g)),
            pl.BlockSpec((seq, LANES), lambda b, g: (b, g)),
            pl.BlockSpec(lam_vecs.shape, lambda b, g: (0, 0)),
            pl.BlockSpec((1, LANES), lambda b, g: (0, 0)),
        ],
        out_specs=pl.BlockSpec((seq, LANES), lambda b, g: (b, g)),
        out_shape=jax.ShapeDtypeStruct((batch * seq, ng * LANES), BF16),
        compiler_params=_params("parallel", "parallel"),
        name="diff_attention",
    )(pqk, pqk, pv, lam_vecs, g)


def _sortable_key(x):
    bits = pltpu.bitcast(x, jnp.int32)
    return jnp.where(bits < 0, bits ^ jnp.int32(0x7FFFFFFF), bits)


def _dsa_kernel(q_ref, kd_ref, iq_ref, ik_ref, vd_ref, z_ref, o_ref, key_ref, madd_ref,
                *, tq, n_sel, scale, idx_scale, w_scale):
    i = pl.program_id(1)
    seq = key_ref.shape[1]
    nq = seq // tq
    span = 1
    half = _lane_half((1, LANES))
    row = lax.broadcasted_iota(jnp.int32, (tq, tq), 0)
    col = lax.broadcasted_iota(jnp.int32, (tq, tq), 1)
    row_g = row + i * tq
    tri = jnp.where(row < col, 1.0, 0.0).astype(BF16)

    def block(n):
        z = z_ref[...]
        iq = iq_ref[...]
        ik = ik_ref[0:n, :]
        sc = jnp.zeros((tq, n), F32)
        for h in range(IDX_HEADS):
            grp = iq[:, (h // 2) * LANES:(h // 2 + 1) * LANES]
            iq_h = jnp.where(half == h % 2, grp, jnp.zeros_like(grp))
            w_h = z[:, SUBLANES + h:SUBLANES + h + 1] * w_scale
            sc = sc + w_h * jnp.maximum(_dot_nt(iq_h, ik) * idx_scale, 0.0)
        for j in range(n // tq):
            scj = jnp.where(col + j * tq <= row_g, sc[:, j * tq:(j + 1) * tq], -jnp.inf)
            key_ref[:, j * tq:(j + 1) * tq] = _sortable_key(scj)

        def count(pred):
            return jnp.sum(jnp.where(pred, 1.0, 0.0), axis=-1, keepdims=True)

        thr0 = jnp.where(count(key_ref[:, 0:n] >= 0) >= n_sel, 0, INT_MIN).astype(jnp.int32)

        def bit_body(it, thr):
            cand = thr | jnp.left_shift(jnp.int32(1), 30 - it)
            return jnp.where(count(key_ref[:, 0:n] >= cand) >= n_sel, cand, thr)

        thr = lax.fori_loop(0, 31, bit_body, thr0)
        need = n_sel - count(key_ref[:, 0:n] > thr)

        off = jnp.zeros((tq, 1), F32)
        for j in range(n // tq):
            kj = key_ref[:, j * tq:(j + 1) * tq]
            eq = kj == thr
            eqf = jnp.where(eq, 1.0, 0.0)
            rank = _dot(eqf.astype(BF16), tri) + off
            take = jnp.where(kj > thr, 1.0, jnp.where(eq, jnp.where(rank < need, 1.0, 0.0), 0.0))
            valid = col + j * tq <= row_g
            madd_ref[:, j * tq:(j + 1) * tq] = jnp.where(valid, jnp.where(take > 0.0, 0.0, NEG_INF), NEG_INF)
            off = off + jnp.sum(eqf, axis=-1, keepdims=True)

        q = q_ref[...] * scale
        for p in range(DSA_HEADS // 2):
            qg = q[:, p * LANES:(p + 1) * LANES]
            g = p // (DSA_HEADS // DSA_KV_HEADS // 2)
            k = kd_ref[0:n, g * LANES:(g + 1) * LANES]
            v = vd_ref[0:n, g * LANES:(g + 1) * LANES]
            outs = []
            for e in range(2):
                qe = jnp.where(half == e, qg, jnp.zeros_like(qg))
                outs.append(_softmax_pv(_dot_nt(qe, k) + madd_ref[:, 0:n], v))
            o_ref[:, p * LANES:(p + 1) * LANES] = jnp.where(half == 0, outs[0], outs[1]).astype(o_ref.dtype)

    for c in range(nq // span):
        pl.when(i // span == c)(functools.partial(block, (c + 1) * span * tq))


def _dsa_attention(p2, p1, z, *, batch, seq):
    tq = min(QUERY_TILE, seq)
    nq = seq // tq
    n_sel = min(DSA_TOPK, seq // 4)
    qw = DSA_HEADS * HEAD_DIM
    return pl.pallas_call(
        functools.partial(_dsa_kernel, tq=tq, n_sel=float(n_sel), scale=HEAD_DIM ** -0.5,
                          idx_scale=IDX_DIM ** -0.5, w_scale=IDX_HEADS ** -0.5),
        grid=(batch, nq),
        in_specs=[
            pl.BlockSpec((tq, qw), lambda b, i: (b * nq + i, 0)),
            pl.BlockSpec((seq, 2 * LANES), lambda b, i: (b, 2)),
            pl.BlockSpec((tq, 2 * LANES), lambda b, i: (b * nq + i, 3)),
            pl.BlockSpec((seq, LANES), lambda b, i: (b, 8)),
            pl.BlockSpec((seq, 2 * LANES), lambda b, i: (b, 6)),
            pl.BlockSpec((tq, LANES), lambda b, i: (b * nq + i, 0)),
        ],
        out_specs=pl.BlockSpec((tq, qw), lambda b, i: (b * nq + i, 0)),
        out_shape=jax.ShapeDtypeStruct((batch * seq, qw), BF16),
        scratch_shapes=[pltpu.VMEM((tq, seq), jnp.int32), pltpu.VMEM((tq, seq), F32)],
        compiler_params=_params("parallel", "arbitrary"),
        name="dsa_attention",
    )(p2, p2, p2, p2, p1, z)


def _hgrn_kernel(q_ref, f_ref, i_ref, g_ref, lbl_ref, gn_ref, o_ref, b_scr, *, chunk, layer):
    seq = q_ref.shape[0]
    half = _lane_half((1, LANES))
    logits = lbl_ref[...]
    pe = jnp.exp(logits - jnp.max(logits, axis=0, keepdims=True))
    prob = pe / jnp.sum(pe, axis=0, keepdims=True)
    csum = prob[0:1, :]
    for d in range(1, layer + 1):
        csum = csum + prob[d:d + 1, :]
    lb = csum - prob[0:1, :]

    rowi = lax.broadcasted_iota(jnp.int32, (chunk, 1), 0)
    xor_rc = (lax.broadcasted_iota(jnp.int32, (chunk, chunk), 0)
              ^ lax.broadcasted_iota(jnp.int32, (chunk, chunk), 1))
    same_head = (lax.broadcasted_iota(jnp.int32, (LANES, LANES), 0) // HEAD_DIM
                 == lax.broadcasted_iota(jnp.int32, (LANES, LANES), 1) // HEAD_DIM)
    ones_bd = jnp.where(same_head, 1.0, 0.0).astype(BF16)
    gn = gn_ref[...]

    def chunk_body(c, state_t):
        r0 = pl.multiple_of(c * chunk, chunk)
        q = _silu(q_ref[pl.ds(r0, chunk), :].astype(F32))
        f = lb + (1.0 - lb) * _sigmoid(f_ref[pl.ds(r0, chunk), :])
        kc = 1.0 - f
        v = i_ref[pl.ds(r0, chunk), :].astype(F32)
        b = _cumsum_rows(jnp.log(f))
        b_scr[...] = b
        b_last = b_scr[chunk - 1:chunk, :]

        o = _dot_nt((q * jnp.exp(b)).astype(BF16), state_t.astype(BF16))
        k2 = (kc * jnp.exp(b_last - b)).astype(BF16)
        upd = _dot_tn(v.astype(BF16), k2)
        new_state = state_t * jnp.exp(b_last) + jnp.where(same_head, upd, 0.0)

        s_tot = [jnp.zeros((chunk, chunk), F32), jnp.zeros((chunk, chunk), F32)]
        m = chunk
        while m > SUB:
            hm = m // 2
            pieces = [jnp.broadcast_to(b_scr[blk * m + hm - 1:blk * m + hm, :], (m, LANES))
                      for blk in range(chunk // m)]
            ref = pieces[0] if len(pieces) == 1 else jnp.concatenate(pieces, axis=0)
            right = (rowi % m) >= hm
            qm = jnp.where(right, q * jnp.exp(jnp.where(right, b - ref, 0.0)), 0.0).astype(BF16)
            km = jnp.where(right, 0.0, kc * jnp.exp(jnp.where(right, 0.0, ref - b))).astype(BF16)
            for e in range(2):
                s = _dot_nt(jnp.where(half == e, qm, jnp.zeros_like(qm)), km)
                s_tot[e] = s_tot[e] + (s if m == chunk else jnp.where(xor_rc < m, s, 0.0))
            m = hm
        vb = v.astype(BF16)
        for e in range(2):
            o = o + _dot(s_tot[e].astype(BF16), jnp.where(half == e, vb, jnp.zeros_like(vb)))

        for d in range(SUB):
            ok = (rowi % SUB) >= d
            bs = b if d == 0 else pltpu.roll(b, d, 0)
            ks = kc if d == 0 else pltpu.roll(kc, d, 0)
            vs = v if d == 0 else pltpu.roll(v, d, 0)
            x = jnp.where(ok, q * ks * jnp.exp(jnp.where(ok, b - bs, 0.0)), 0.0)
            o = o + _dot(x.astype(BF16), ones_bd) * vs

        sq = o * o
        s0 = jnp.sum(jnp.where(half == 0, sq, 0.0), axis=-1, keepdims=True)
        s1 = jnp.sum(jnp.where(half == 1, sq, 0.0), axis=-1, keepdims=True)
        var = jnp.where(half == 0, s0, s1) * (1.0 / HEAD_DIM)
        y = o * lax.rsqrt(var + NORM_EPS) * gn * _silu(g_ref[pl.ds(r0, chunk), :].astype(F32))
        o_ref[pl.ds(r0, chunk), :] = y.astype(o_ref.dtype)
        return new_state

    lax.fori_loop(0, seq // chunk, chunk_body, jnp.zeros((LANES, LANES), F32))


def _hgrn(h_qig, h_f, lb_logits, gn, *, batch, seq, layer):
    ng = HGRN_HEADS // 2
    chunk = min(HGRN_CHUNK, seq)
    spec = lambda off: pl.BlockSpec((seq, LANES), lambda b, g: (b, off * ng + g))
    return pl.pallas_call(
        functools.partial(_hgrn_kernel, chunk=chunk, layer=layer),
        grid=(batch, ng),
        in_specs=[spec(0), spec(0), spec(1), spec(2),
                  pl.BlockSpec((lb_logits.shape[0], LANES), lambda b, g: (0, g)),
                  pl.BlockSpec((1, LANES), lambda b, g: (0, 0))],
        out_specs=pl.BlockSpec((seq, LANES), lambda b, g: (b, g)),
        out_shape=jax.ShapeDtypeStruct((batch * seq, ng * LANES), BF16),
        scratch_shapes=[pltpu.VMEM((chunk, LANES), F32)],
        compiler_params=_params("parallel", "parallel"),
        name="hgrn2",
    )(h_qig, h_f, h_qig, h_qig, lb_logits, gn)


def _pack_pairs(x):
    hw = x.shape[1] // 2
    hi = pltpu.bitcast(x[:, :hw].astype(BF16).astype(F32), jnp.int32)
    lo = pltpu.bitcast(x[:, hw:].astype(BF16).astype(F32), jnp.int32)
    return hi | lax.shift_right_logical(lo, 16)


def _unpack_pairs(w):
    hi = pltpu.bitcast(w & jnp.int32(-65536), F32)
    lo = pltpu.bitcast(lax.shift_left(w, 16), F32)
    return hi, lo


def _store_planes(ref, words):
    pw = words.shape[1] // SC_PLANES
    for h in range(SC_PLANES):
        ref[h] = words[:, h * pw:(h + 1) * pw]


def _load_planes(ref):
    return jnp.concatenate([ref[h] for h in range(SC_PLANES)], axis=1)


def _router_kernel(a1_ref, a2_ref, w1_ref, w2_ref, x_ref, g_ref, w_ref, xo_ref, o_ref, xn_ref):
    x = x_ref[...] + _dot(a1_ref[...], w1_ref[...]) + _dot(a2_ref[...], w2_ref[...])
    xo_ref[...] = x
    var = jnp.mean(x * x, axis=-1, keepdims=True)
    xn = x * lax.rsqrt(var + NORM_EPS) * g_ref[...]
    _store_planes(xn_ref, _pack_pairs(xn))
    x_hi = xn.astype(BF16)
    x_lo = (xn - x_hi.astype(F32)).astype(BF16)
    w = w_ref[...]
    w_hi = w.astype(BF16)
    w_lo = (w - w_hi.astype(F32)).astype(BF16)
    logits = _dot(x_hi, w_hi) + _dot(x_hi, w_lo) + _dot(x_lo, w_hi)
    lane = lax.broadcasted_iota(jnp.int32, logits.shape, 1)
    lg = jnp.where(lane < N_EXPERTS, logits, -jnp.inf)
    m1 = jnp.max(lg, axis=-1, keepdims=True)
    i1 = jnp.min(jnp.where(lg == m1, lane, LANES), axis=-1, keepdims=True)
    lg2 = jnp.where(lane == i1, -jnp.inf, lg)
    m2 = jnp.max(lg2, axis=-1, keepdims=True)
    i2 = jnp.min(jnp.where(lg2 == m2, lane, LANES), axis=-1, keepdims=True)
    t = jnp.exp(m2 - m1)
    den = 1.0 + t
    o_ref[...] = (jnp.where(lane == 0, i1.astype(F32), 0.0) + jnp.where(lane == 1, i2.astype(F32), 0.0)
                  + jnp.where(lane == 2, 1.0 / den, 0.0) + jnp.where(lane == 3, t / den, 0.0))


def _mix_router(a1, a2, w1, w2, x, g, w_pad):
    t, d = x.shape
    k1, k2 = a1.shape[1], a2.shape[1]
    tm = ROW_TILE
    return pl.pallas_call(
        _router_kernel,
        grid=(t // tm,),
        in_specs=[pl.BlockSpec((tm, k1), lambda i: (i, 0)),
                  pl.BlockSpec((tm, k2), lambda i: (i, 0)),
                  pl.BlockSpec((k1, d), lambda i: (0, 0)),
                  pl.BlockSpec((k2, d), lambda i: (0, 0)),
                  pl.BlockSpec((tm, d), lambda i: (i, 0)),
                  pl.BlockSpec((1, d), lambda i: (0, 0)),
                  pl.BlockSpec((d, LANES), lambda i: (0, 0))],
        out_specs=[pl.BlockSpec((tm, d), lambda i: (i, 0)),
                   pl.BlockSpec((tm, LANES), lambda i: (i, 0)),
                   pl.BlockSpec((SC_PLANES, tm, d // 2 // SC_PLANES), lambda i: (0, i, 0))],
        out_shape=[jax.ShapeDtypeStruct((t, d), F32),
                   jax.ShapeDtypeStruct((t, LANES), F32),
                   jax.ShapeDtypeStruct((SC_PLANES, t, d // 2 // SC_PLANES), jnp.int32)],
        compiler_params=_params("parallel"),
        name="mix_router",
    )(a1, a2, w1, w2, x, g, w_pad)


def _sc_mesh():
    return plsc.VectorSubcoreMesh(core_axis_name="core", subcore_axis_name="subcore")


def _sc_scatter_rows(x, idx0, idx1, n_out):
    n, d = x.shape

    @functools.partial(pl.kernel, out_type=jax.ShapeDtypeStruct((n_out, d), x.dtype), mesh=_sc_mesh(),
                       scratch_types=[])
    def scatter(x_hbm, i0_hbm, i1_hbm, o_hbm):
        def body(x_vmem, i0_vmem, i1_vmem):
            pltpu.sync_copy(x_vmem, o_hbm.at[i0_vmem.at[0]])
            pltpu.sync_copy(x_vmem, o_hbm.at[i1_vmem.at[0]])

        pltpu.emit_pipeline(
            body,
            grid=(n // SC_WINDOW,),
            in_specs=[pl.BlockSpec((SC_WINDOW, d), index_map=lambda i: (i, 0)),
                      pl.BlockSpec((1, SC_WINDOW), index_map=lambda i: (0, i)),
                      pl.BlockSpec((1, SC_WINDOW), index_map=lambda i: (0, i))],
            out_specs=[],
            core_axis_name=("core", "subcore"),
            dimension_semantics=(pltpu.PARALLEL,),
        )(x_hbm, i0_hbm, i1_hbm)

    return scatter(x, idx0.reshape(1, n), idx1.reshape(1, n))


def _sc_gather_rows(x, idx):
    n = idx.shape[0]
    d = x.shape[1]

    @functools.partial(pl.kernel, out_type=jax.ShapeDtypeStruct((n, d), x.dtype), mesh=_sc_mesh(),
                       scratch_types=[])
    def gather(x_hbm, i_hbm, o_hbm):
        def body(i_vmem, o_vmem):
            pltpu.sync_copy(x_hbm.at[i_vmem.at[0]], o_vmem)

        pltpu.emit_pipeline(
            body,
            grid=(n // SC_WINDOW,),
            in_specs=[pl.BlockSpec((1, SC_WINDOW), index_map=lambda i: (0, i))],
            out_specs=[pl.BlockSpec((SC_WINDOW, d), index_map=lambda i: (i, 0))],
            core_axis_name=("core", "subcore"),
            dimension_semantics=(pltpu.PARALLEL,),
        )(i_hbm, o_hbm)

    return gather(x, idx.reshape(1, n))


def _moe_ffn_packed_kernel(te_ref, act_ref, x_ref, wg_ref, wu_ref, wd_ref, o_ref, xn_ref, acc_ref):
    i = pl.program_id(0)
    f = pl.program_id(1)
    hw = xn_ref.shape[1] // 2

    @pl.when(act_ref[i] == 1)
    def _():
        @pl.when(f == 0)
        def _():
            hi, lo = _unpack_pairs(_load_planes(x_ref))
            xn_ref[:, :hw] = hi.astype(BF16)
            xn_ref[:, hw:] = lo.astype(BF16)
            acc_ref[...] = jnp.zeros_like(acc_ref)

        x = xn_ref[...]
        mid = (_silu(_dot(x, wg_ref[0])) * _dot(x, wu_ref[0])).astype(BF16)
        acc_ref[...] += _dot(mid, wd_ref[0])

        @pl.when(f == pl.num_programs(1) - 1)
        def _():
            _store_planes(o_ref, _pack_pairs(acc_ref[...]))


def _moe_ffn_packed(tile_e, tile_act, xs, wg, wu, wd, *, tf):
    _, p, pw = xs.shape
    d = 2 * pw * SC_PLANES
    ff = wg.shape[2]
    tm = MOE_ROW_TILE
    return pl.pallas_call(
        _moe_ffn_packed_kernel,
        grid_spec=pltpu.PrefetchScalarGridSpec(
            num_scalar_prefetch=2,
            grid=(p // tm, ff // tf),
            in_specs=[pl.BlockSpec((SC_PLANES, tm, pw), lambda i, f, te, ta: (0, i, 0)),
                      pl.BlockSpec((1, d, tf), lambda i, f, te, ta: (te[i], 0, f * ta[i])),
                      pl.BlockSpec((1, d, tf), lambda i, f, te, ta: (te[i], 0, f * ta[i])),
                      pl.BlockSpec((1, tf, d), lambda i, f, te, ta: (te[i], f * ta[i], 0))],
            out_specs=pl.BlockSpec((SC_PLANES, tm, pw), lambda i, f, te, ta: (0, i, 0)),
            scratch_shapes=[pltpu.VMEM((tm, d), BF16), pltpu.VMEM((tm, d), F32)],
        ),
        out_shape=jax.ShapeDtypeStruct((SC_PLANES, p, pw), jnp.int32),
        compiler_params=_params("arbitrary", "arbitrary"),
        name="moe_ffn",
    )(tile_e, tile_act, xs, wg, wu, wd)


def _moe_sum_kernel(x_ref, info_ref, y_ref, gf_ref, o_ref, *, final_norm):
    hw = x_ref.shape[1] // 2
    info = info_ref[...]
    g1 = info[:, 2:3]
    g2 = info[:, 3:4]
    hi1, lo1 = _unpack_pairs(jnp.concatenate([y_ref[h] for h in range(SC_PLANES)], axis=1))
    hi2, lo2 = _unpack_pairs(jnp.concatenate([y_ref[SC_PLANES + h] for h in range(SC_PLANES)], axis=1))
    x = x_ref[...]
    out_a = x[:, :hw] + g1 * hi1 + g2 * hi2
    out_b = x[:, hw:] + g1 * lo1 + g2 * lo2
    if final_norm:
        ss = jnp.sum(out_a * out_a, axis=-1, keepdims=True) + jnp.sum(out_b * out_b, axis=-1, keepdims=True)
        r = lax.rsqrt(ss * (1.0 / (2 * hw)) + NORM_EPS)
        gf = gf_ref[...]
        out_a = out_a * r * gf[:, :hw]
        out_b = out_b * r * gf[:, hw:]
    o_ref[:, :hw] = out_a
    o_ref[:, hw:] = out_b


def _moe_sum(x, info, y_pairs, gf, *, final_norm):
    t, d = x.shape
    tm = ROW_TILE
    return pl.pallas_call(
        functools.partial(_moe_sum_kernel, final_norm=final_norm),
        grid=(t // tm,),
        in_specs=[pl.BlockSpec((tm, d), lambda i: (i, 0)),
                  pl.BlockSpec((tm, LANES), lambda i: (i, 0)),
                  pl.BlockSpec((2 * SC_PLANES, tm, y_pairs.shape[2]), lambda i: (0, i, 0)),
                  pl.BlockSpec((1, d), lambda i: (0, 0))],
        out_specs=pl.BlockSpec((tm, d), lambda i: (i, 0)),
        out_shape=jax.ShapeDtypeStruct((t, d), F32),
        compiler_params=_params("parallel"),
        name="moe_sum",
    )(x, info, y_pairs, gf)


def _moe_positions(info, t):
    tm = MOE_ROW_TILE
    n_e = N_EXPERTS
    a = 2 * t
    p = a + n_e * tm
    e_flat = info[:, 0:2].astype(jnp.int32).reshape(a)
    onehot = (e_flat[:, None] == jnp.arange(n_e, dtype=jnp.int32)[None, :]).astype(jnp.int32)
    csum = jnp.cumsum(onehot, axis=0)
    rank = jnp.sum(csum * onehot, axis=1) - 1
    counts = csum[-1]
    padded = ((counts + tm - 1) // tm) * tm
    ends = jnp.cumsum(padded)
    starts = ends - padded
    pos = jnp.sum(onehot * starts[None, :], axis=1) + rank
    tile_start = jnp.arange(p // tm, dtype=jnp.int32) * tm
    tile_e = jnp.minimum(jnp.sum((tile_start[:, None] >= ends[None, :]).astype(jnp.int32), axis=1), n_e - 1)
    tile_act = (tile_start < ends[-1]).astype(jnp.int32)
    return pos.astype(jnp.int32).reshape(t, 2), tile_e.astype(jnp.int32), tile_act, p


def _mix_ffn_kernel(a1_ref, a2_ref, w1_ref, w2_ref, x_ref, g_ref, wg_ref, wu_ref, wd_ref, gf_ref, o_ref,
                    *, final_norm):
    x = x_ref[...] + _dot(a1_ref[...], w1_ref[...]) + _dot(a2_ref[...], w2_ref[...])
    var = jnp.mean(x * x, axis=-1, keepdims=True)
    xn = (x * lax.rsqrt(var + NORM_EPS) * g_ref[...]).astype(BF16)
    mid = (_silu(_dot(xn, wg_ref[...])) * _dot(xn, wu_ref[...])).astype(BF16)
    out = x + _dot(mid, wd_ref[...])
    if final_norm:
        var = jnp.mean(out * out, axis=-1, keepdims=True)
        out = out * lax.rsqrt(var + NORM_EPS) * gf_ref[...]
    o_ref[...] = out


def _resident(shape):
    return pl.BlockSpec(shape, lambda i: (0,) * len(shape), pipeline_mode=pl.Buffered(1))


def _mix_ffn(a1, a2, w1, w2, x, g, wg, wu, wd, gf, *, final_norm):
    t, d = x.shape
    ff = wg.shape[1]
    k1, k2 = a1.shape[1], a2.shape[1]
    tm = ROW_TILE
    return pl.pallas_call(
        functools.partial(_mix_ffn_kernel, final_norm=final_norm),
        grid=(t // tm,),
        in_specs=[
            pl.BlockSpec((tm, k1), lambda i: (i, 0)),
            pl.BlockSpec((tm, k2), lambda i: (i, 0)),
            _resident((k1, d)),
            _resident((k2, d)),
            pl.BlockSpec((tm, d), lambda i: (i, 0)),
            _resident((1, d)),
            _resident((d, ff)),
            _resident((d, ff)),
            _resident((ff, d)),
            _resident((1, d)),
        ],
        out_specs=pl.BlockSpec((tm, d), lambda i: (i, 0)),
        out_shape=jax.ShapeDtypeStruct((t, d), F32),
        compiler_params=_params("parallel"),
        name="mix_ffn",
    )(a1, a2, w1, w2, x, g, wg, wu, wd, gf)


def _rope_tables(seq):
    half = HEAD_DIM // 2
    inv = ROPE_THETA ** (-jnp.arange(half, dtype=F32) * 2.0 / HEAD_DIM)
    ang = jnp.arange(seq, dtype=F32)[:, None] * inv[None, :]
    cos = jnp.cos(ang)
    sin = jnp.sin(ang)
    cos_t = jnp.concatenate([cos, cos, cos, cos], axis=-1)
    sin_t = jnp.concatenate([-sin, sin, -sin, sin], axis=-1)
    return cos_t, sin_t


def _dup_heads(w):
    d, n = w.shape
    w = w.reshape(d, n // HEAD_DIM, 1, HEAD_DIM)
    return jnp.broadcast_to(w, (d, n // HEAD_DIM, 2, HEAD_DIM)).reshape(d, 2 * n)


def kernel(x, ln_mix_e, w_in_e, b_fox_f, w_out_e, ln_ffn_e, w_ffn_gate, w_ffn_up, w_ffn_down,
           ln_mix_o, w_in_o, hgrn_lb_logits, hgrn_norm_g, diff_lambda, diff_norm_g, w_out_o,
           ln_ffn_o, w_router, w_exp_gate, w_exp_up, w_exp_down, ln_final):
    batch, seq, d = x.shape
    t = batch * seq
    depth = ln_mix_e.shape[0] + ln_mix_o.shape[0]
    xs = x.reshape(t, d)
    cos_t, sin_t = _rope_tables(seq)
    row = lambda v: v.reshape(1, -1)
    fw = FOX_HEADS * HEAD_DIM
    dw = DSA_HEADS * HEAD_DIM
    kvw = DSA_KV_HEADS * HEAD_DIM
    hw = HGRN_HEADS * HEAD_DIM
    qw = DIFF_HEADS * 2 * DIFF_DIM

    for layer in range(depth):
        j = layer // 2
        if layer % 2 == 0:
            w = w_in_e[j]
            o = 0
            fq, fk, fv = w[:, o:o + fw], w[:, o + fw:o + 2 * fw], w[:, o + 2 * fw:o + 3 * fw]
            o += 3 * fw
            ffw = w[:, o:o + FOX_HEADS]
            o += FOX_HEADS
            dq = w[:, o:o + dw]
            o += dw
            dk = w[:, o:o + kvw]
            o += kvw
            dv = w[:, o:o + kvw]
            o += kvw
            iqw = w[:, o:o + IDX_HEADS * IDX_DIM]
            o += IDX_HEADS * IDX_DIM
            ikw = w[:, o:o + IDX_DIM]
            o += IDX_DIM
            iww = w[:, o:o + IDX_HEADS]
            w1 = jnp.concatenate([fq, fk, fv, _dup_heads(dv)], axis=1).astype(BF16)
            w2 = jnp.concatenate([dq, _dup_heads(dk), iqw, _dup_heads(ikw)], axis=1).astype(BF16)
            w3 = jnp.concatenate([ffw, iww, jnp.zeros((d, LANES - FOX_HEADS - IDX_HEADS), F32)], axis=1).astype(BF16)
            g = row(ln_mix_e[j])
            p1, p2, z = _rms_proj(xs, g, cos_t, sin_t, (w1, w2, w3), (False, True, False), (BF16, BF16, F32), seq=seq)
            b_pad = jnp.concatenate([b_fox_f[j], jnp.zeros((LANES - FOX_HEADS,), F32)]).reshape(1, LANES)
            ccol, crow = _fox_gate(z, b_pad, batch=batch, seq=seq)
            fox_o = _fox_attention(p1, ccol, crow, batch=batch, seq=seq)
            dsa_o = _dsa_attention(p2, p1, z, batch=batch, seq=seq)
            wo = w_out_e[j].astype(BF16)
            xs = _mix_ffn(fox_o, dsa_o, wo[:fw], wo[fw:], xs, row(ln_ffn_e[j]), w_ffn_gate[j].astype(BF16),
                          w_ffn_up[j].astype(BF16), w_ffn_down[j].astype(BF16), row(ln_final),
                          final_norm=(layer == depth - 1))
        else:
            w = w_in_o[j]
            lam_init = 0.8 - 0.6 * math.exp(-0.3 * layer)
            g = row(ln_mix_o[j])
            w_qig = jnp.concatenate([w[:, :hw], w[:, 2 * hw:4 * hw]], axis=1).astype(BF16)
            w_f = w[:, hw:2 * hw].astype(BF16)
            w_qk = w[:, 4 * hw:4 * hw + 2 * qw].astype(BF16)
            w_v = w[:, 4 * hw + 2 * qw:].astype(BF16)
            h_qig, h_f, pqk, pv = _rms_proj(xs, g, cos_t, sin_t, (w_qig, w_f, w_qk, w_v),
                                            (False, False, True, False), (BF16, F32, BF16, BF16), seq=seq)
            gn = jnp.concatenate([hgrn_norm_g[j], hgrn_norm_g[j]]).reshape(1, LANES)
            o_h = _hgrn(h_qig, h_f, hgrn_lb_logits, gn, batch=batch, seq=seq, layer=layer)
            o_d = _diff_attention(pqk, pv, diff_lambda[j], row(diff_norm_g[j]), batch=batch, seq=seq,
                                  lam_init=lam_init)
            wo = w_out_o[j].astype(BF16)
            wr = jnp.concatenate([w_router[j], jnp.zeros((d, LANES - N_EXPERTS), F32)], axis=1)
            xs, info, xn = _mix_router(o_h, o_d, wo[:hw], wo[hw:], xs, row(ln_ffn_o[j]), wr)
            pos, tile_e, tile_act, n_rows = _moe_positions(info, t)
            pw = xn.shape[2]
            plane = jnp.arange(SC_PLANES, dtype=jnp.int32)[:, None] * n_rows
            dst0 = (plane + pos[None, :, 0]).reshape(SC_PLANES * t)
            dst1 = (plane + pos[None, :, 1]).reshape(SC_PLANES * t)
            rows = _sc_scatter_rows(xn.reshape(SC_PLANES * t, pw), dst0, dst1, SC_PLANES * n_rows)
            y = _moe_ffn_packed(tile_e, tile_act, rows.reshape(SC_PLANES, n_rows, pw), w_exp_gate[j].astype(BF16),
                                w_exp_up[j].astype(BF16), w_exp_down[j].astype(BF16), tf=1792)
            src = jnp.concatenate([(plane + pos[None, :, s]).reshape(SC_PLANES * t) for s in range(2)])
            y_pairs = _sc_gather_rows(y.reshape(SC_PLANES * n_rows, pw), src).reshape(2 * SC_PLANES, t, pw)
            xs = _moe_sum(xs, info, y_pairs, row(ln_final), final_norm=(layer == depth - 1))
    if depth % 2 == 1:
        pass
    return xs.reshape(batch, seq, d)
```

```python
import functools
import math

import jax
import jax.numpy as jnp
from jax import lax
from jax.experimental import pallas as pl
from jax.experimental.pallas import tpu as pltpu
from jax.experimental.pallas import tpu_sc as plsc

F32 = jnp.float32
BF16 = jnp.bfloat16

HEAD_DIM = 64
FOX_HEADS = 8
DSA_HEADS = 8
DSA_KV_HEADS = 2
IDX_HEADS = 4
IDX_DIM = 64
DSA_TOPK = 256
HGRN_HEADS = 8
DIFF_HEADS = 4
DIFF_DIM = 64
ROPE_THETA = 10000.0
N_EXPERTS = 8
NORM_EPS = 1e-6
NEG_INF = -1e30

LANES = 128
SUBLANES = 8
VMEM_LIMIT_BYTES = 56 * 1024 * 1024

ROW_TILE = 512
ATTN_TILE = 256
QUERY_TILE = 512
HGRN_CHUNK = 256
SUB = 8
MOE_ROW_TILE = 512
SC_WINDOW = 128
SC_PLANES = 2

INT_MIN = -(2 ** 31)


def _params(*sem):
    return pltpu.CompilerParams(dimension_semantics=sem, vmem_limit_bytes=VMEM_LIMIT_BYTES)


def _dot(a, b):
    return jnp.dot(a, b, preferred_element_type=F32)


def _dot_nt(a, b):
    return lax.dot_general(a, b, (((1,), (1,)), ((), ())), preferred_element_type=F32)


def _dot_tn(a, b):
    return lax.dot_general(a, b, (((0,), (0,)), ((), ())), preferred_element_type=F32)


def _sigmoid(x):
    return 1.0 / (1.0 + jnp.exp(-x))


def _silu(x):
    return x * _sigmoid(x)


def _lane_half(shape):
    return lax.broadcasted_iota(jnp.int32, shape, len(shape) - 1) // HEAD_DIM


def _rms_proj_kernel(x_ref, g_ref, cos_ref, sin_ref, *refs, ropes):
    n = len(ropes)
    w_refs, o_refs = refs[:n], refs[n:]
    x = x_ref[...]
    var = jnp.mean(x * x, axis=-1, keepdims=True)
    xn = (x * lax.rsqrt(var + NORM_EPS) * g_ref[...]).astype(BF16)
    cos = cos_ref[...]
    sin = sin_ref[...]
    first = (lax.broadcasted_iota(jnp.int32, cos.shape, 1) % HEAD_DIM) < (HEAD_DIM // 2)
    for w_ref, o_ref, rope in zip(w_refs, o_refs, ropes):
        y = _dot(xn, w_ref[...])
        if not rope:
            o_ref[...] = y.astype(o_ref.dtype)
            continue
        for c in range(y.shape[1] // LANES):
            yc = y[:, c * LANES:(c + 1) * LANES]
            partner = jnp.where(first, pltpu.roll(yc, LANES - HEAD_DIM // 2, 1), pltpu.roll(yc, HEAD_DIM // 2, 1))
            o_ref[:, c * LANES:(c + 1) * LANES] = (yc * cos + partner * sin).astype(o_ref.dtype)


def _rms_proj(x, g, cos, sin, ws, ropes, out_dtypes, *, seq):
    t, d = x.shape
    tm = ROW_TILE
    nseq = seq // tm
    return pl.pallas_call(
        functools.partial(_rms_proj_kernel, ropes=tuple(ropes)),
        grid=(t // tm,),
        in_specs=[
            pl.BlockSpec((tm, d), lambda i: (i, 0)),
            pl.BlockSpec((1, d), lambda i: (0, 0)),
            pl.BlockSpec((tm, LANES), lambda i: (i % nseq, 0)),
            pl.BlockSpec((tm, LANES), lambda i: (i % nseq, 0)),
        ] + [pl.BlockSpec((d, w.shape[1]), lambda i: (0, 0)) for w in ws],
        out_specs=[pl.BlockSpec((tm, w.shape[1]), lambda i: (i, 0)) for w in ws],
        out_shape=[jax.ShapeDtypeStruct((t, w.shape[1]), dt) for w, dt in zip(ws, out_dtypes)],
        compiler_params=_params("parallel"),
        name="rms_proj",
    )(x, g, cos, sin, *ws)


def _cumsum_rows(x):
    rows = x.shape[0]
    row = lax.broadcasted_iota(jnp.int32, x.shape, 0)
    k = 1
    while k < rows:
        x = x + jnp.where(row >= k, pltpu.roll(x, k, 0), 0.0)
        k *= 2
    return x


def _fox_gate_kernel(z_ref, b_ref, ccol_ref, crow_ref, *, tk):
    z = z_ref[...] + b_ref[...]
    logf = jnp.minimum(z, 0.0) - jnp.log(1.0 + jnp.exp(-jnp.abs(z)))
    c = _cumsum_rows(logf)
    ccol_ref[...] = c
    for j in range(c.shape[0] // tk):
        ct = c[j * tk:(j + 1) * tk, :].T
        crow_ref[0, :, j * tk:(j + 1) * tk] = ct[:SUBLANES, :]


def _fox_gate(z, b_pad, *, batch, seq):
    tk = ATTN_TILE
    return pl.pallas_call(
        functools.partial(_fox_gate_kernel, tk=tk),
        grid=(batch,),
        in_specs=[
            pl.BlockSpec((seq, LANES), lambda b: (b, 0)),
            pl.BlockSpec((1, LANES), lambda b: (0, 0)),
        ],
        out_specs=[
            pl.BlockSpec((seq, LANES), lambda b: (b, 0)),
            pl.BlockSpec((1, SUBLANES, seq), lambda b: (b, 0, 0)),
        ],
        out_shape=[
            jax.ShapeDtypeStruct((batch * seq, LANES), F32),
            jax.ShapeDtypeStruct((batch, SUBLANES, seq), F32),
        ],
        compiler_params=_params("parallel"),
        name="fox_gate",
    )(z, b_pad)


def _softmax_pv(s, v):
    m = jnp.max(s, axis=-1, keepdims=True)
    p = jnp.exp(s - m)
    l = jnp.sum(p, axis=-1, keepdims=True)
    return _dot(p.astype(BF16), v) / l


def _causal_attend(qe, k_ref, v_ref, n, tq, row_bias=None, key_bias=None):
    causal = lax.broadcasted_iota(jnp.int32, (tq, tq), 1) <= lax.broadcasted_iota(jnp.int32, (tq, tq), 0)
    parts = []
    for lo, hi in ((0, n - tq), (n - tq, n)):
        if hi == lo:
            continue
        s = _dot_nt(qe, k_ref[lo:hi, :])
        if key_bias is not None:
            s = s - key_bias[:, lo:hi]
        if hi == n:
            s = jnp.where(causal, s, NEG_INF)
        parts.append((s, v_ref[lo:hi, :]))
    m = functools.reduce(jnp.maximum, [jnp.max(s, axis=-1, keepdims=True) for s, _ in parts])
    shift = -m if row_bias is None else row_bias - (m + row_bias)
    acc = 0.0
    l = 0.0
    for s, v in parts:
        p = jnp.exp(s + shift)
        l = l + jnp.sum(p, axis=-1, keepdims=True)
        acc = acc + _dot(p.astype(BF16), v)
    return acc / l


def _fox_kernel(q_ref, k_ref, v_ref, cc_ref, cr_ref, o_ref, *, tq, scale):
    hp = pl.program_id(1)
    seq = q_ref.shape[0]
    half = _lane_half((1, LANES))
    lane = lax.broadcasted_iota(jnp.int32, (tq, LANES), 1)
    for i in range(seq // tq):
        n = (i + 1) * tq
        q = q_ref[i * tq:n, :] * scale
        cc = cc_ref[i * tq:n, :]
        outs = []
        for e in range(2):
            h = 2 * hp + e
            qe = jnp.where(half == e, q, jnp.zeros_like(q))
            cq = jnp.sum(jnp.where(lane == h, cc, 0.0), axis=-1, keepdims=True)
            ck = cr_ref[0, pl.ds(h, 1), 0:n]
            outs.append(_causal_attend(qe, k_ref, v_ref, n, tq, row_bias=cq, key_bias=ck))
        o_ref[i * tq:n, :] = jnp.where(half == 0, outs[0], outs[1]).astype(o_ref.dtype)


def _fox_attention(p1, ccol, crow, *, batch, seq):
    tq = min(QUERY_TILE, seq)
    ng = FOX_HEADS // 2
    return pl.pallas_call(
        functools.partial(_fox_kernel, tq=tq, scale=HEAD_DIM ** -0.5),
        grid=(batch, ng),
        in_specs=[
            pl.BlockSpec((seq, LANES), lambda b, g: (b, g)),
            pl.BlockSpec((seq, LANES), lambda b, g: (b, ng + g)),
            pl.BlockSpec((seq, LANES), lambda b, g: (b, 2 * ng + g)),
            pl.BlockSpec((seq, LANES), lambda b, g: (b, 0)),
            pl.BlockSpec((1, SUBLANES, seq), lambda b, g: (b, 0, 0)),
        ],
        out_specs=pl.BlockSpec((seq, LANES), lambda b, g: (b, g)),
        out_shape=jax.ShapeDtypeStruct((batch * seq, ng * LANES), BF16),
        compiler_params=_params("parallel", "parallel"),
        name="fox_attention",
    )(p1, p1, p1, ccol, crow)


def _diff_kernel(q_ref, k_ref, v_ref, lam_ref, g_ref, o_ref, *, tq, scale, lam_init):
    seq = q_ref.shape[0]
    half = _lane_half((1, LANES))
    lv = lam_ref[...]
    lam = (jnp.exp(jnp.sum(lv[0:1, :] * lv[1:2, :], axis=-1, keepdims=True))
           - jnp.exp(jnp.sum(lv[2:3, :] * lv[3:4, :], axis=-1, keepdims=True)) + lam_init)
    gn = g_ref[...] * (1.0 - lam_init)
    for i in range(seq // tq):
        n = (i + 1) * tq
        q = q_ref[i * tq:n, :] * scale
        outs = []
        for e in range(2):
            qe = jnp.where(half == e, q, jnp.zeros_like(q))
            outs.append(_causal_attend(qe, k_ref, v_ref, n, tq))
        o = outs[0] - lam * outs[1]
        var = jnp.mean(o * o, axis=-1, keepdims=True)
        o_ref[i * tq:n, :] = (o * lax.rsqrt(var + NORM_EPS) * gn).astype(o_ref.dtype)


def _diff_attention(pqk, pv, lam_vecs, g, *, batch, seq, lam_init):
    tq = min(QUERY_TILE, seq)
    ng = DIFF_HEADS
    return pl.pallas_call(
        functools.partial(_diff_kernel, tq=tq, scale=DIFF_DIM ** -0.5, lam_init=lam_init),
        grid=(batch, ng),
        in_specs=[
            pl.BlockSpec((seq, LANES), lambda b, g: (b, g)),
            pl.BlockSpec((seq, LANES), lambda b, g: (b, ng + g)),
            pl.BlockSpec((seq, LANES), lambda b, g: (b, g)),
            pl.BlockSpec(lam_vecs.shape, lambda b, g: (0, 0)),
            pl.BlockSpec((1, LANES), lambda b, g: (0, 0)),
        ],
        out_specs=pl.BlockSpec((seq, LANES), lambda b, g: (b, g)),
        out_shape=jax.ShapeDtypeStruct((batch * seq, ng * LANES), BF16),
        compiler_params=_params("parallel", "parallel"),
        name="diff_attention",
    )(pqk, pqk, pv, lam_vecs, g)


def _sortable_key(x):
    bits = pltpu.bitcast(x, jnp.int32)
    return jnp.where(bits < 0, bits ^ jnp.int32(0x7FFFFFFF), bits)


def _dsa_kernel(q_ref, k0_ref, k1_ref, iq_ref, ik_ref, v0_ref, v1_ref, z_ref, o_ref, key_ref, madd_ref,
                qs_ref, os_ref, *, tq, n_sel, scale, idx_scale, w_scale):
    i = pl.program_id(1)
    seq = key_ref.shape[1]
    nq = seq // tq
    span = 2 if nq % 2 == 0 else 1
    half = _lane_half((1, LANES))
    row = lax.broadcasted_iota(jnp.int32, (tq, tq), 0)
    col = lax.broadcasted_iota(jnp.int32, (tq, tq), 1)
    row_g = row + i * tq
    tri = jnp.where(row < col, 1.0, 0.0).astype(BF16)

    def block(n):
        z = z_ref[...]
        iq = iq_ref[...]
        ik = ik_ref[0:n, :]
        sc = jnp.zeros((tq, n), F32)
        for h in range(IDX_HEADS):
            grp = iq[:, (h // 2) * LANES:(h // 2 + 1) * LANES]
            iq_h = jnp.where(half == h % 2, grp, jnp.zeros_like(grp))
            w_h = z[:, SUBLANES + h:SUBLANES + h + 1] * w_scale
            sc = sc + w_h * jnp.maximum(_dot_nt(iq_h, ik) * idx_scale, 0.0)
        for j in range(n // tq):
            scj = jnp.where(col + j * tq <= row_g, sc[:, j * tq:(j + 1) * tq], -jnp.inf)
            key_ref[:, j * tq:(j + 1) * tq] = _sortable_key(scj)

        def count(pred):
            return jnp.sum(jnp.where(pred, 1.0, 0.0), axis=-1, keepdims=True)

        thr0 = jnp.where(count(key_ref[:, 0:n] >= 0) >= n_sel, 0, INT_MIN).astype(jnp.int32)

        def bit_body(it, thr):
            cand = thr | jnp.left_shift(jnp.int32(1), 30 - it)
            return jnp.where(count(key_ref[:, 0:n] >= cand) >= n_sel, cand, thr)

        thr = lax.fori_loop(0, 31, bit_body, thr0)
        need = n_sel - count(key_ref[:, 0:n] > thr)

        off = jnp.zeros((tq, 1), F32)
        for j in range(n // tq):
            kj = key_ref[:, j * tq:(j + 1) * tq]
            eq = kj == thr
            eqf = jnp.where(eq, 1.0, 0.0)
            rank = _dot(eqf.astype(BF16), tri) + off
            take = jnp.where(kj > thr, 1.0, jnp.where(eq, jnp.where(rank < need, 1.0, 0.0), 0.0))
            valid = col + j * tq <= row_g
            madd_ref[:, j * tq:(j + 1) * tq] = jnp.where(valid, jnp.where(take > 0.0, 0.0, NEG_INF), NEG_INF)
            off = off + jnp.sum(eqf, axis=-1, keepdims=True)

        q = q_ref[...] * scale
        n_groups = DSA_HEADS // 2
        for p in range(n_groups):
            qs_ref[p] = q[:, p * LANES:(p + 1) * LANES]

        def group_body(p, carry):
            first_kv = p < n_groups // DSA_KV_HEADS
            qg = qs_ref[p]
            k = jnp.where(first_kv, k0_ref[0:n, :], k1_ref[0:n, :])
            v = jnp.where(first_kv, v0_ref[0:n, :], v1_ref[0:n, :])
            outs = []
            for e in range(2):
                qe = jnp.where(half == e, qg, jnp.zeros_like(qg))
                outs.append(_softmax_pv(_dot_nt(qe, k) + madd_ref[:, 0:n], v))
            os_ref[p] = jnp.where(half == 0, outs[0], outs[1]).astype(os_ref.dtype)
            return carry

        lax.fori_loop(0, n_groups, group_body, 0)
        for p in range(n_groups):
            o_ref[:, p * LANES:(p + 1) * LANES] = os_ref[p]

    for c in range(nq // span):
        pl.when(i // span == c)(functools.partial(block, (c + 1) * span * tq))


def _dsa_attention(p2, p1, z, *, batch, seq):
    tq = min(ATTN_TILE, seq)
    nq = seq // tq
    n_sel = min(DSA_TOPK, seq // 4)
    qw = DSA_HEADS * HEAD_DIM
    return pl.pallas_call(
        functools.partial(_dsa_kernel, tq=tq, n_sel=float(n_sel), scale=HEAD_DIM ** -0.5,
                          idx_scale=IDX_DIM ** -0.5, w_scale=IDX_HEADS ** -0.5),
        grid=(batch, nq),
        in_specs=[
            pl.BlockSpec((tq, qw), lambda b, i: (b * nq + i, 0)),
            pl.BlockSpec((seq, LANES), lambda b, i: (b, 4)),
            pl.BlockSpec((seq, LANES), lambda b, i: (b, 5)),
            pl.BlockSpec((tq, 2 * LANES), lambda b, i: (b * nq + i, 3)),
            pl.BlockSpec((seq, LANES), lambda b, i: (b, 8)),
            pl.BlockSpec((seq, LANES), lambda b, i: (b, 12)),
            pl.BlockSpec((seq, LANES), lambda b, i: (b, 13)),
            pl.BlockSpec((tq, LANES), lambda b, i: (b * nq + i, 0)),
        ],
        out_specs=pl.BlockSpec((tq, qw), lambda b, i: (b * nq + i, 0)),
        out_shape=jax.ShapeDtypeStruct((batch * seq, qw), BF16),
        scratch_shapes=[pltpu.VMEM((tq, seq), jnp.int32), pltpu.VMEM((tq, seq), F32),
                        pltpu.VMEM((DSA_HEADS // 2, tq, LANES), BF16),
                        pltpu.VMEM((DSA_HEADS // 2, tq, LANES), BF16)],
        compiler_params=_params("parallel", "arbitrary"),
        name="dsa_attention",
    )(p2, p2, p2, p2, p2, p1, p1, z)


def _hgrn_kernel(q_ref, f_ref, i_ref, g_ref, lbl_ref, gn_ref, o_ref, b_scr, *, chunk, layer):
    seq = q_ref.shape[0]
    half = _lane_half((1, LANES))
    logits = lbl_ref[...]
    pe = jnp.exp(logits - jnp.max(logits, axis=0, keepdims=True))
    prob = pe / jnp.sum(pe, axis=0, keepdims=True)
    csum = prob[0:1, :]
    for d in range(1, layer + 1):
        csum = csum + prob[d:d + 1, :]
    lb = csum - prob[0:1, :]

    rowi = lax.broadcasted_iota(jnp.int32, (chunk, 1), 0)
    xor_rc = (lax.broadcasted_iota(jnp.int32, (chunk, chunk), 0)
              ^ lax.broadcasted_iota(jnp.int32, (chunk, chunk), 1))
    same_head = (lax.broadcasted_iota(jnp.int32, (LANES, LANES), 0) // HEAD_DIM
                 == lax.broadcasted_iota(jnp.int32, (LANES, LANES), 1) // HEAD_DIM)
    ones_bd = jnp.where(same_head, 1.0, 0.0).astype(BF16)
    gn = gn_ref[...]

    def chunk_body(c, state_t):
        r0 = pl.multiple_of(c * chunk, chunk)
        q = _silu(q_ref[pl.ds(r0, chunk), :].astype(F32))
        f = lb + (1.0 - lb) * _sigmoid(f_ref[pl.ds(r0, chunk), :])
        kc = 1.0 - f
        v = i_ref[pl.ds(r0, chunk), :].astype(F32)
        b = _cumsum_rows(jnp.log(f))
        b_scr[...] = b
        b_last = b_scr[chunk - 1:chunk, :]

        o = _dot_nt((q * jnp.exp(b)).astype(BF16), state_t.astype(BF16))
        k2 = (kc * jnp.exp(b_last - b)).astype(BF16)
        upd = _dot_tn(v.astype(BF16), k2)
        new_state = state_t * jnp.exp(b_last) + jnp.where(same_head, upd, 0.0)

        s_tot = [jnp.zeros((chunk, chunk), F32), jnp.zeros((chunk, chunk), F32)]
        m = chunk
        while m > SUB:
            hm = m // 2
            pieces = [jnp.broadcast_to(b_scr[blk * m + hm - 1:blk * m + hm, :], (m, LANES))
                      for blk in range(chunk // m)]
            ref = pieces[0] if len(pieces) == 1 else jnp.concatenate(pieces, axis=0)
            right = (rowi % m) >= hm
            qm = jnp.where(right, q * jnp.exp(jnp.where(right, b - ref, 0.0)), 0.0).astype(BF16)
            km = jnp.where(right, 0.0, kc * jnp.exp(jnp.where(right, 0.0, ref - b))).astype(BF16)
            for e in range(2):
                s = _dot_nt(jnp.where(half == e, qm, jnp.zeros_like(qm)), km)
                s_tot[e] = s_tot[e] + (s if m == chunk else jnp.where(xor_rc < m, s, 0.0))
            m = hm
        vb = v.astype(BF16)
        for e in range(2):
            o = o + _dot(s_tot[e].astype(BF16), jnp.where(half == e, vb, jnp.zeros_like(vb)))

        for d in range(SUB):
            ok = (rowi % SUB) >= d
            bs = b if d == 0 else pltpu.roll(b, d, 0)
            ks = kc if d == 0 else pltpu.roll(kc, d, 0)
            vs = v if d == 0 else pltpu.roll(v, d, 0)
            x = jnp.where(ok, q * ks * jnp.exp(jnp.where(ok, b - bs, 0.0)), 0.0)
            o = o + _dot(x.astype(BF16), ones_bd) * vs

        sq = o * o
        s0 = jnp.sum(jnp.where(half == 0, sq, 0.0), axis=-1, keepdims=True)
        s1 = jnp.sum(jnp.where(half == 1, sq, 0.0), axis=-1, keepdims=True)
        var = jnp.where(half == 0, s0, s1) * (1.0 / HEAD_DIM)
        y = o * lax.rsqrt(var + NORM_EPS) * gn * _silu(g_ref[pl.ds(r0, chunk), :].astype(F32))
        o_ref[pl.ds(r0, chunk), :] = y.astype(o_ref.dtype)
        return new_state

    lax.fori_loop(0, seq // chunk, chunk_body, jnp.zeros((LANES, LANES), F32))


def _hgrn(h_qig, h_f, lb_logits, gn, *, batch, seq, layer):
    ng = HGRN_HEADS // 2
    chunk = min(HGRN_CHUNK, seq)
    spec = lambda off: pl.BlockSpec((seq, LANES), lambda b, g: (b, off * ng + g))
    return pl.pallas_call(
        functools.partial(_hgrn_kernel, chunk=chunk, layer=layer),
        grid=(batch, ng),
        in_specs=[spec(0), spec(0), spec(1), spec(2),
                  pl.BlockSpec((lb_logits.shape[0], LANES), lambda b, g: (0, g)),
                  pl.BlockSpec((1, LANES), lambda b, g: (0, 0))],
        out_specs=pl.BlockSpec((seq, LANES), lambda b, g: (b, g)),
        out_shape=jax.ShapeDtypeStruct((batch * seq, ng * LANES), BF16),
        scratch_shapes=[pltpu.VMEM((chunk, LANES), F32)],
        compiler_params=_params("parallel", "parallel"),
        name="hgrn2",
    )(h_qig, h_f, h_qig, h_qig, lb_logits, gn)


def _pack_pairs(x):
    hw = x.shape[1] // 2
    hi = pltpu.bitcast(x[:, :hw].astype(BF16).astype(F32), jnp.int32)
    lo = pltpu.bitcast(x[:, hw:].astype(BF16).astype(F32), jnp.int32)
    return hi | lax.shift_right_logical(lo, 16)


def _unpack_pairs(w):
    hi = pltpu.bitcast(w & jnp.int32(-65536), F32)
    lo = pltpu.bitcast(lax.shift_left(w, 16), F32)
    return hi, lo


def _store_planes(ref, words):
    pw = words.shape[1] // SC_PLANES
    for h in range(SC_PLANES):
        ref[h] = words[:, h * pw:(h + 1) * pw]


def _load_planes(ref):
    return jnp.concatenate([ref[h] for h in range(SC_PLANES)], axis=1)


def _router_kernel(a1_ref, a2_ref, w1_ref, w2_ref, x_ref, g_ref, w_ref, xo_ref, o_ref, xn_ref):
    x = x_ref[...] + _dot(a1_ref[...], w1_ref[...]) + _dot(a2_ref[...], w2_ref[...])
    xo_ref[...] = x
    var = jnp.mean(x * x, axis=-1, keepdims=True)
    xn = x * lax.rsqrt(var + NORM_EPS) * g_ref[...]
    _store_planes(xn_ref, _pack_pairs(xn))
    x_hi = xn.astype(BF16)
    x_lo = (xn - x_hi.astype(F32)).astype(BF16)
    w = w_ref[...]
    w_hi = w.astype(BF16)
    w_lo = (w - w_hi.astype(F32)).astype(BF16)
    logits = _dot(x_hi, w_hi) + _dot(x_hi, w_lo) + _dot(x_lo, w_hi)
    lane = lax.broadcasted_iota(jnp.int32, logits.shape, 1)
    lg = jnp.where(lane < N_EXPERTS, logits, -jnp.inf)
    m1 = jnp.max(lg, axis=-1, keepdims=True)
    i1 = jnp.min(jnp.where(lg == m1, lane, LANES), axis=-1, keepdims=True)
    lg2 = jnp.where(lane == i1, -jnp.inf, lg)
    m2 = jnp.max(lg2, axis=-1, keepdims=True)
    i2 = jnp.min(jnp.where(lg2 == m2, lane, LANES), axis=-1, keepdims=True)
    t = jnp.exp(m2 - m1)
    den = 1.0 + t
    o_ref[...] = (jnp.where(lane == 0, i1.astype(F32), 0.0) + jnp.where(lane == 1, i2.astype(F32), 0.0)
                  + jnp.where(lane == 2, 1.0 / den, 0.0) + jnp.where(lane == 3, t / den, 0.0))


def _mix_router(a1, a2, w1, w2, x, g, w_pad):
    t, d = x.shape
    k1, k2 = a1.shape[1], a2.shape[1]
    tm = ROW_TILE
    return pl.pallas_call(
        _router_kernel,
        grid=(t // tm,),
        in_specs=[pl.BlockSpec((tm, k1), lambda i: (i, 0)),
                  pl.BlockSpec((tm, k2), lambda i: (i, 0)),
                  pl.BlockSpec((k1, d), lambda i: (0, 0)),
                  pl.BlockSpec((k2, d), lambda i: (0, 0)),
                  pl.BlockSpec((tm, d), lambda i: (i, 0)),
                  pl.BlockSpec((1, d), lambda i: (0, 0)),
                  pl.BlockSpec((d, LANES), lambda i: (0, 0))],
        out_specs=[pl.BlockSpec((tm, d), lambda i: (i, 0)),
                   pl.BlockSpec((tm, LANES), lambda i: (i, 0)),
                   pl.BlockSpec((SC_PLANES, tm, d // 2 // SC_PLANES), lambda i: (0, i, 0))],
        out_shape=[jax.ShapeDtypeStruct((t, d), F32),
                   jax.ShapeDtypeStruct((t, LANES), F32),
                   jax.ShapeDtypeStruct((SC_PLANES, t, d // 2 // SC_PLANES), jnp.int32)],
        compiler_params=_params("parallel"),
        name="mix_router",
    )(a1, a2, w1, w2, x, g, w_pad)


def _sc_mesh():
    return plsc.VectorSubcoreMesh(core_axis_name="core", subcore_axis_name="subcore")


def _sc_scatter_rows(x, idx0, idx1, n_out):
    n, d = x.shape

    @functools.partial(pl.kernel, out_type=jax.ShapeDtypeStruct((n_out, d), x.dtype), mesh=_sc_mesh(),
                       scratch_types=[])
    def scatter(x_hbm, i0_hbm, i1_hbm, o_hbm):
        def body(x_vmem, i0_vmem, i1_vmem):
            pltpu.sync_copy(x_vmem, o_hbm.at[i0_vmem.at[0]])
            pltpu.sync_copy(x_vmem, o_hbm.at[i1_vmem.at[0]])

        pltpu.emit_pipeline(
            body,
            grid=(n // SC_WINDOW,),
            in_specs=[pl.BlockSpec((SC_WINDOW, d), index_map=lambda i: (i, 0)),
                      pl.BlockSpec((1, SC_WINDOW), index_map=lambda i: (0, i)),
                      pl.BlockSpec((1, SC_WINDOW), index_map=lambda i: (0, i))],
            out_specs=[],
            core_axis_name=("core", "subcore"),
            dimension_semantics=(pltpu.PARALLEL,),
        )(x_hbm, i0_hbm, i1_hbm)

    return scatter(x, idx0.reshape(1, n), idx1.reshape(1, n))


def _sc_gather_rows(x, idx):
    n = idx.shape[0]
    d = x.shape[1]

    @functools.partial(pl.kernel, out_type=jax.ShapeDtypeStruct((n, d), x.dtype), mesh=_sc_mesh(),
                       scratch_types=[])
    def gather(x_hbm, i_hbm, o_hbm):
        def body(i_vmem, o_vmem):
            pltpu.sync_copy(x_hbm.at[i_vmem.at[0]], o_vmem)

        pltpu.emit_pipeline(
            body,
            grid=(n // SC_WINDOW,),
            in_specs=[pl.BlockSpec((1, SC_WINDOW), index_map=lambda i: (0, i))],
            out_specs=[pl.BlockSpec((SC_WINDOW, d), index_map=lambda i: (i, 0))],
            core_axis_name=("core", "subcore"),
            dimension_semantics=(pltpu.PARALLEL,),
        )(i_hbm, o_hbm)

    return gather(x, idx.reshape(1, n))


def _moe_ffn_packed_kernel(te_ref, act_ref, x_ref, wg_ref, wu_ref, wd_ref, o_ref, xn_ref, acc_ref):
    i = pl.program_id(0)
    f = pl.program_id(1)
    hw = xn_ref.shape[1] // 2

    @pl.when(act_ref[i] == 1)
    def _():
        @pl.when(f == 0)
        def _():
            hi, lo = _unpack_pairs(_load_planes(x_ref))
            xn_ref[:, :hw] = hi.astype(BF16)
            xn_ref[:, hw:] = lo.astype(BF16)
            acc_ref[...] = jnp.zeros_like(acc_ref)

        x = xn_ref[...]
        mid = (_silu(_dot(x, wg_ref[0])) * _dot(x, wu_ref[0])).astype(BF16)
        acc_ref[...] += _dot(mid, wd_ref[0])

        @pl.when(f == pl.num_programs(1) - 1)
        def _():
            _store_planes(o_ref, _pack_pairs(acc_ref[...]))


def _moe_ffn_packed(tile_e, tile_act, xs, wg, wu, wd, *, tf):
    _, p, pw = xs.shape
    d = 2 * pw * SC_PLANES
    ff = wg.shape[2]
    tm = MOE_ROW_TILE
    return pl.pallas_call(
        _moe_ffn_packed_kernel,
        grid_spec=pltpu.PrefetchScalarGridSpec(
            num_scalar_prefetch=2,
            grid=(p // tm, ff // tf),
            in_specs=[pl.BlockSpec((SC_PLANES, tm, pw), lambda i, f, te, ta: (0, i, 0)),
                      pl.BlockSpec((1, d, tf), lambda i, f, te, ta: (te[i], 0, f * ta[i])),
                      pl.BlockSpec((1, d, tf), lambda i, f, te, ta: (te[i], 0, f * ta[i])),
                      pl.BlockSpec((1, tf, d), lambda i, f, te, ta: (te[i], f * ta[i], 0))],
            out_specs=pl.BlockSpec((SC_PLANES, tm, pw), lambda i, f, te, ta: (0, i, 0)),
            scratch_shapes=[pltpu.VMEM((tm, d), BF16), pltpu.VMEM((tm, d), F32)],
        ),
        out_shape=jax.ShapeDtypeStruct((SC_PLANES, p, pw), jnp.int32),
        compiler_params=_params("arbitrary", "arbitrary"),
        name="moe_ffn",
    )(tile_e, tile_act, xs, wg, wu, wd)


def _moe_sum_kernel(x_ref, info_ref, y_ref, gf_ref, o_ref, *, final_norm):
    hw = x_ref.shape[1] // 2
    info = info_ref[...]
    g1 = info[:, 2:3]
    g2 = info[:, 3:4]
    hi1, lo1 = _unpack_pairs(jnp.concatenate([y_ref[h] for h in range(SC_PLANES)], axis=1))
    hi2, lo2 = _unpack_pairs(jnp.concatenate([y_ref[SC_PLANES + h] for h in range(SC_PLANES)], axis=1))
    x = x_ref[...]
    out_a = x[:, :hw] + g1 * hi1 + g2 * hi2
    out_b = x[:, hw:] + g1 * lo1 + g2 * lo2
    if final_norm:
        ss = jnp.sum(out_a * out_a, axis=-1, keepdims=True) + jnp.sum(out_b * out_b, axis=-1, keepdims=True)
        r = lax.rsqrt(ss * (1.0 / (2 * hw)) + NORM_EPS)
        gf = gf_ref[...]
        out_a = out_a * r * gf[:, :hw]
        out_b = out_b * r * gf[:, hw:]
    o_ref[:, :hw] = out_a
    o_ref[:, hw:] = out_b


def _moe_sum(x, info, y_pairs, gf, *, final_norm):
    t, d = x.shape
    tm = ROW_TILE
    return pl.pallas_call(
        functools.partial(_moe_sum_kernel, final_norm=final_norm),
        grid=(t // tm,),
        in_specs=[pl.BlockSpec((tm, d), lambda i: (i, 0)),
                  pl.BlockSpec((tm, LANES), lambda i: (i, 0)),
                  pl.BlockSpec((2 * SC_PLANES, tm, y_pairs.shape[2]), lambda i: (0, i, 0)),
                  pl.BlockSpec((1, d), lambda i: (0, 0))],
        out_specs=pl.BlockSpec((tm, d), lambda i: (i, 0)),
        out_shape=jax.ShapeDtypeStruct((t, d), F32),
        compiler_params=_params("parallel"),
        name="moe_sum",
    )(x, info, y_pairs, gf)


def _moe_positions(info, t):
    tm = MOE_ROW_TILE
    n_e = N_EXPERTS
    a = 2 * t
    p = a + n_e * tm
    e_flat = info[:, 0:2].astype(jnp.int32).reshape(a)
    onehot = (e_flat[:, None] == jnp.arange(n_e, dtype=jnp.int32)[None, :]).astype(jnp.int32)
    csum = jnp.cumsum(onehot, axis=0)
    rank = jnp.sum(csum * onehot, axis=1) - 1
    counts = csum[-1]
    padded = ((counts + tm - 1) // tm) * tm
    ends = jnp.cumsum(padded)
    starts = ends - padded
    pos = jnp.sum(onehot * starts[None, :], axis=1) + rank
    tile_start = jnp.arange(p // tm, dtype=jnp.int32) * tm
    tile_e = jnp.minimum(jnp.sum((tile_start[:, None] >= ends[None, :]).astype(jnp.int32), axis=1), n_e - 1)
    tile_act = (tile_start < ends[-1]).astype(jnp.int32)
    return pos.astype(jnp.int32).reshape(t, 2), tile_e.astype(jnp.int32), tile_act, p


def _mix_ffn_kernel(a1_ref, a2_ref, w1_ref, w2_ref, x_ref, g_ref, wg_ref, wu_ref, wd_ref, gf_ref, o_ref,
                    *, final_norm):
    x = x_ref[...] + _dot(a1_ref[...], w1_ref[...]) + _dot(a2_ref[...], w2_ref[...])
    var = jnp.mean(x * x, axis=-1, keepdims=True)
    xn = (x * lax.rsqrt(var + NORM_EPS) * g_ref[...]).astype(BF16)
    mid = (_silu(_dot(xn, wg_ref[...])) * _dot(xn, wu_ref[...])).astype(BF16)
    out = x + _dot(mid, wd_ref[...])
    if final_norm:
        var = jnp.mean(out * out, axis=-1, keepdims=True)
        out = out * lax.rsqrt(var + NORM_EPS) * gf_ref[...]
    o_ref[...] = out


def _resident(shape):
    return pl.BlockSpec(shape, lambda i: (0,) * len(shape), pipeline_mode=pl.Buffered(1))


def _mix_ffn(a1, a2, w1, w2, x, g, wg, wu, wd, gf, *, final_norm):
    t, d = x.shape
    ff = wg.shape[1]
    k1, k2 = a1.shape[1], a2.shape[1]
    tm = ROW_TILE
    return pl.pallas_call(
        functools.partial(_mix_ffn_kernel, final_norm=final_norm),
        grid=(t // tm,),
        in_specs=[
            pl.BlockSpec((tm, k1), lambda i: (i, 0)),
            pl.BlockSpec((tm, k2), lambda i: (i, 0)),
            _resident((k1, d)),
            _resident((k2, d)),
            pl.BlockSpec((tm, d), lambda i: (i, 0)),
            _resident((1, d)),
            _resident((d, ff)),
            _resident((d, ff)),
            _resident((ff, d)),
            _resident((1, d)),
        ],
        out_specs=pl.BlockSpec((tm, d), lambda i: (i, 0)),
        out_shape=jax.ShapeDtypeStruct((t, d), F32),
        compiler_params=_params("parallel"),
        name="mix_ffn",
    )(a1, a2, w1, w2, x, g, wg, wu, wd, gf)


def _rope_tables(seq):
    half = HEAD_DIM // 2
    inv = ROPE_THETA ** (-jnp.arange(half, dtype=F32) * 2.0 / HEAD_DIM)
    ang = jnp.arange(seq, dtype=F32)[:, None] * inv[None, :]
    cos = jnp.cos(ang)
    sin = jnp.sin(ang)
    cos_t = jnp.concatenate([cos, cos, cos, cos], axis=-1)
    sin_t = jnp.concatenate([-sin, sin, -sin, sin], axis=-1)
    return cos_t, sin_t


def _dup_heads(w):
    d, n = w.shape
    w = w.reshape(d, n // HEAD_DIM, 1, HEAD_DIM)
    return jnp.broadcast_to(w, (d, n // HEAD_DIM, 2, HEAD_DIM)).reshape(d, 2 * n)


def kernel(x, ln_mix_e, w_in_e, b_fox_f, w_out_e, ln_ffn_e, w_ffn_gate, w_ffn_up, w_ffn_down,
           ln_mix_o, w_in_o, hgrn_lb_logits, hgrn_norm_g, diff_lambda, diff_norm_g, w_out_o,
           ln_ffn_o, w_router, w_exp_gate, w_exp_up, w_exp_down, ln_final):
    batch, seq, d = x.shape
    t = batch * seq
    depth = ln_mix_e.shape[0] + ln_mix_o.shape[0]
    xs = x.reshape(t, d)
    cos_t, sin_t = _rope_tables(seq)
    row = lambda v: v.reshape(1, -1)
    fw = FOX_HEADS * HEAD_DIM
    dw = DSA_HEADS * HEAD_DIM
    kvw = DSA_KV_HEADS * HEAD_DIM
    hw = HGRN_HEADS * HEAD_DIM
    qw = DIFF_HEADS * 2 * DIFF_DIM

    for layer in range(depth):
        j = layer // 2
        if layer % 2 == 0:
            w = w_in_e[j]
            o = 0
            fq, fk, fv = w[:, o:o + fw], w[:, o + fw:o + 2 * fw], w[:, o + 2 * fw:o + 3 * fw]
            o += 3 * fw
            ffw = w[:, o:o + FOX_HEADS]
            o += FOX_HEADS
            dq = w[:, o:o + dw]
            o += dw
            dk = w[:, o:o + kvw]
            o += kvw
            dv = w[:, o:o + kvw]
            o += kvw
            iqw = w[:, o:o + IDX_HEADS * IDX_DIM]
            o += IDX_HEADS * IDX_DIM
            ikw = w[:, o:o + IDX_DIM]
            o += IDX_DIM
            iww = w[:, o:o + IDX_HEADS]
            w1 = jnp.concatenate([fq, fk, fv, _dup_heads(dv)], axis=1).astype(BF16)
            w2 = jnp.concatenate([dq, _dup_heads(dk), iqw, _dup_heads(ikw)], axis=1).astype(BF16)
            w3 = jnp.concatenate([ffw, iww, jnp.zeros((d, LANES - FOX_HEADS - IDX_HEADS), F32)], axis=1).astype(BF16)
            g = row(ln_mix_e[j])
            p1, p2, z = _rms_proj(xs, g, cos_t, sin_t, (w1, w2, w3), (False, True, False), (BF16, BF16, F32), seq=seq)
            b_pad = jnp.concatenate([b_fox_f[j], jnp.zeros((LANES - FOX_HEADS,), F32)]).reshape(1, LANES)
            ccol, crow = _fox_gate(z, b_pad, batch=batch, seq=seq)
            fox_o = _fox_attention(p1, ccol, crow, batch=batch, seq=seq)
            dsa_o = _dsa_attention(p2, p1, z, batch=batch, seq=seq)
            wo = w_out_e[j].astype(BF16)
            xs = _mix_ffn(fox_o, dsa_o, wo[:fw], wo[fw:], xs, row(ln_ffn_e[j]), w_ffn_gate[j].astype(BF16),
                          w_ffn_up[j].astype(BF16), w_ffn_down[j].astype(BF16), row(ln_final),
                          final_norm=(layer == depth - 1))
        else:
            w = w_in_o[j]
            lam_init = 0.8 - 0.6 * math.exp(-0.3 * layer)
            g = row(ln_mix_o[j])
            w_qig = jnp.concatenate([w[:, :hw], w[:, 2 * hw:4 * hw]], axis=1).astype(BF16)
            w_f = w[:, hw:2 * hw].astype(BF16)
            w_qk = w[:, 4 * hw:4 * hw + 2 * qw].astype(BF16)
            w_v = w[:, 4 * hw + 2 * qw:].astype(BF16)
            h_qig, h_f, pqk, pv = _rms_proj(xs, g, cos_t, sin_t, (w_qig, w_f, w_qk, w_v),
                                            (False, False, True, False), (BF16, F32, BF16, BF16), seq=seq)
            gn = jnp.concatenate([hgrn_norm_g[j], hgrn_norm_g[j]]).reshape(1, LANES)
            o_h = _hgrn(h_qig, h_f, hgrn_lb_logits, gn, batch=batch, seq=seq, layer=layer)
            o_d = _diff_attention(pqk, pv, diff_lambda[j], row(diff_norm_g[j]), batch=batch, seq=seq,
                                  lam_init=lam_init)
            wo = w_out_o[j].astype(BF16)
            wr = jnp.concatenate([w_router[j], jnp.zeros((d, LANES - N_EXPERTS), F32)], axis=1)
            xs, info, xn = _mix_router(o_h, o_d, wo[:hw], wo[hw:], xs, row(ln_ffn_o[j]), wr)
            pos, tile_e, tile_act, n_rows = _moe_positions(info, t)
            pw = xn.shape[2]
            plane = jnp.arange(SC_PLANES, dtype=jnp.int32)[:, None] * n_rows
            dst0 = (plane + pos[None, :, 0]).reshape(SC_PLANES * t)
            dst1 = (plane + pos[None, :, 1]).reshape(SC_PLANES * t)
            rows = _sc_scatter_rows(xn.reshape(SC_PLANES * t, pw), dst0, dst1, SC_PLANES * n_rows)
            y = _moe_ffn_packed(tile_e, tile_act, rows.reshape(SC_PLANES, n_rows, pw), w_exp_gate[j].astype(BF16),
                                w_exp_up[j].astype(BF16), w_exp_down[j].astype(BF16), tf=1792)
            src = jnp.concatenate([(plane + pos[None, :, s]).reshape(SC_PLANES * t) for s in range(2)])
            y_pairs = _sc_gather_rows(y.reshape(SC_PLANES * n_rows, pw), src).reshape(2 * SC_PLANES, t, pw)
            xs = _moe_sum(xs, info, y_pairs, row(ln_final), final_norm=(layer == depth - 1))
    if depth % 2 == 1:
        pass
    return xs.reshape(batch, seq, d)
```

```python
import functools
import math

import jax
import jax.numpy as jnp
from jax import lax
from jax.experimental import pallas as pl
from jax.experimental.pallas import tpu as pltpu
from jax.experimental.pallas import tpu_sc as plsc

F32 = jnp.float32
BF16 = jnp.bfloat16

HEAD_DIM = 64
FOX_HEADS = 8
DSA_HEADS = 8
DSA_KV_HEADS = 2
IDX_HEADS = 4
IDX_DIM = 64
DSA_TOPK = 256
HGRN_HEADS = 8
DIFF_HEADS = 4
DIFF_DIM = 64
ROPE_THETA = 10000.0
N_EXPERTS = 8
NORM_EPS = 1e-6
NEG_INF = -1e30

LANES = 128
SUBLANES = 8
VMEM_LIMIT_BYTES = 56 * 1024 * 1024

ROW_TILE = 512
ATTN_TILE = 256
QUERY_TILE = 512
HGRN_CHUNK = 256
SUB = 8
MOE_ROW_TILE = 512
SC_WINDOW = 128
SC_PLANES = 2

INT_MIN = -(2 ** 31)


def _params(*sem):
    return pltpu.CompilerParams(dimension_semantics=sem, vmem_limit_bytes=VMEM_LIMIT_BYTES)


def _dot(a, b):
    return jnp.dot(a, b, preferred_element_type=F32)


def _dot_nt(a, b):
    return lax.dot_general(a, b, (((1,), (1,)), ((), ())), preferred_element_type=F32)


def _dot_tn(a, b):
    return lax.dot_general(a, b, (((0,), (0,)), ((), ())), preferred_element_type=F32)


def _sigmoid(x):
    return 1.0 / (1.0 + jnp.exp(-x))


def _silu(x):
    return x * _sigmoid(x)


def _lane_half(shape):
    return lax.broadcasted_iota(jnp.int32, shape, len(shape) - 1) // HEAD_DIM


def _rms_proj_kernel(x_ref, g_ref, cos_ref, sin_ref, *refs, ropes):
    n = len(ropes)
    w_refs, o_refs = refs[:n], refs[n:]
    x = x_ref[...]
    var = jnp.mean(x * x, axis=-1, keepdims=True)
    xn = (x * lax.rsqrt(var + NORM_EPS) * g_ref[...]).astype(BF16)
    cos = cos_ref[...]
    sin = sin_ref[...]
    first = (lax.broadcasted_iota(jnp.int32, cos.shape, 1) % HEAD_DIM) < (HEAD_DIM // 2)
    for w_ref, o_ref, rope in zip(w_refs, o_refs, ropes):
        y = _dot(xn, w_ref[...])
        if not rope:
            o_ref[...] = y.astype(o_ref.dtype)
            continue
        for c in range(y.shape[1] // LANES):
            yc = y[:, c * LANES:(c + 1) * LANES]
            partner = jnp.where(first, pltpu.roll(yc, LANES - HEAD_DIM // 2, 1), pltpu.roll(yc, HEAD_DIM // 2, 1))
            o_ref[:, c * LANES:(c + 1) * LANES] = (yc * cos + partner * sin).astype(o_ref.dtype)


def _rms_proj(x, g, cos, sin, ws, ropes, out_dtypes, *, seq):
    t, d = x.shape
    tm = ROW_TILE
    nseq = seq // tm
    return pl.pallas_call(
        functools.partial(_rms_proj_kernel, ropes=tuple(ropes)),
        grid=(t // tm,),
        in_specs=[
            pl.BlockSpec((tm, d), lambda i: (i, 0)),
            pl.BlockSpec((1, d), lambda i: (0, 0)),
            pl.BlockSpec((tm, LANES), lambda i: (i % nseq, 0)),
            pl.BlockSpec((tm, LANES), lambda i: (i % nseq, 0)),
        ] + [pl.BlockSpec((d, w.shape[1]), lambda i: (0, 0)) for w in ws],
        out_specs=[pl.BlockSpec((tm, w.shape[1]), lambda i: (i, 0)) for w in ws],
        out_shape=[jax.ShapeDtypeStruct((t, w.shape[1]), dt) for w, dt in zip(ws, out_dtypes)],
        compiler_params=_params("parallel"),
        name="rms_proj",
    )(x, g, cos, sin, *ws)


def _cumsum_rows(x):
    rows = x.shape[0]
    row = lax.broadcasted_iota(jnp.int32, x.shape, 0)
    k = 1
    while k < rows:
        x = x + jnp.where(row >= k, pltpu.roll(x, k, 0), 0.0)
        k *= 2
    return x


def _fox_gate_kernel(z_ref, b_ref, ccol_ref, crow_ref, *, tk):
    z = z_ref[...] + b_ref[...]
    logf = jnp.minimum(z, 0.0) - jnp.log(1.0 + jnp.exp(-jnp.abs(z)))
    c = _cumsum_rows(logf)
    ccol_ref[...] = c
    for j in range(c.shape[0] // tk):
        ct = c[j * tk:(j + 1) * tk, :].T
        crow_ref[0, :, j * tk:(j + 1) * tk] = ct[:SUBLANES, :]


def _fox_gate(z, b_pad, *, batch, seq):
    tk = ATTN_TILE
    return pl.pallas_call(
        functools.partial(_fox_gate_kernel, tk=tk),
        grid=(batch,),
        in_specs=[
            pl.BlockSpec((seq, LANES), lambda b: (b, 0)),
            pl.BlockSpec((1, LANES), lambda b: (0, 0)),
        ],
        out_specs=[
            pl.BlockSpec((seq, LANES), lambda b: (b, 0)),
            pl.BlockSpec((1, SUBLANES, seq), lambda b: (b, 0, 0)),
        ],
        out_shape=[
            jax.ShapeDtypeStruct((batch * seq, LANES), F32),
            jax.ShapeDtypeStruct((batch, SUBLANES, seq), F32),
        ],
        compiler_params=_params("parallel"),
        name="fox_gate",
    )(z, b_pad)


def _softmax_pv(s, v):
    m = jnp.max(s, axis=-1, keepdims=True)
    p = jnp.exp(s - m)
    l = jnp.sum(p, axis=-1, keepdims=True)
    return _dot(p.astype(BF16), v) / l


def _causal_attend(qe, k_ref, v_ref, n, tq, row_bias=None, key_bias=None):
    causal = lax.broadcasted_iota(jnp.int32, (tq, tq), 1) <= lax.broadcasted_iota(jnp.int32, (tq, tq), 0)
    parts = []
    for lo, hi in ((0, n - tq), (n - tq, n)):
        if hi == lo:
            continue
        s = _dot_nt(qe, k_ref[lo:hi, :])
        if key_bias is not None:
            s = s - key_bias[:, lo:hi]
        if hi == n:
            s = jnp.where(causal, s, NEG_INF)
        parts.append((s, v_ref[lo:hi, :]))
    m = functools.reduce(jnp.maximum, [jnp.max(s, axis=-1, keepdims=True) for s, _ in parts])
    shift = -m if row_bias is None else row_bias - (m + row_bias)
    acc = 0.0
    l = 0.0
    for s, v in parts:
        p = jnp.exp(s + shift)
        l = l + jnp.sum(p, axis=-1, keepdims=True)
        acc = acc + _dot(p.astype(BF16), v)
    return acc / l


def _fox_kernel(q_ref, k_ref, v_ref, cc_ref, cr_ref, o_ref, *, tq, scale):
    hp = pl.program_id(1)
    seq = q_ref.shape[0]
    half = _lane_half((1, LANES))
    lane = lax.broadcasted_iota(jnp.int32, (tq, LANES), 1)
    for i in range(seq // tq):
        n = (i + 1) * tq
        q = q_ref[i * tq:n, :] * scale
        cc = cc_ref[i * tq:n, :]
        outs = []
        for e in range(2):
            h = 2 * hp + e
            qe = jnp.where(half == e, q, jnp.zeros_like(q))
            cq = jnp.sum(jnp.where(lane == h, cc, 0.0), axis=-1, keepdims=True)
            ck = cr_ref[0, pl.ds(h, 1), 0:n]
            outs.append(_causal_attend(qe, k_ref, v_ref, n, tq, row_bias=cq, key_bias=ck))
        o_ref[i * tq:n, :] = jnp.where(half == 0, outs[0], outs[1]).astype(o_ref.dtype)


def _fox_attention(p1, ccol, crow, *, batch, seq):
    tq = min(QUERY_TILE, seq)
    ng = FOX_HEADS // 2
    return pl.pallas_call(
        functools.partial(_fox_kernel, tq=tq, scale=HEAD_DIM ** -0.5),
        grid=(batch, ng),
        in_specs=[
            pl.BlockSpec((seq, LANES), lambda b, g: (b, g)),
            pl.BlockSpec((seq, LANES), lambda b, g: (b, ng + g)),
            pl.BlockSpec((seq, LANES), lambda b, g: (b, 2 * ng + g)),
            pl.BlockSpec((seq, LANES), lambda b, g: (b, 0)),
            pl.BlockSpec((1, SUBLANES, seq), lambda b, g: (b, 0, 0)),
        ],
        out_specs=pl.BlockSpec((seq, LANES), lambda b, g: (b, g)),
        out_shape=jax.ShapeDtypeStruct((batch * seq, ng * LANES), BF16),
        compiler_params=_params("parallel", "parallel"),
        name="fox_attention",
    )(p1, p1, p1, ccol, crow)


def _diff_kernel(q_ref, k_ref, v_ref, lam_ref, g_ref, o_ref, *, tq, scale, lam_init):
    seq = q_ref.shape[0]
    half = _lane_half((1, LANES))
    lv = lam_ref[...]
    lam = (jnp.exp(jnp.sum(lv[0:1, :] * lv[1:2, :], axis=-1, keepdims=True))
           - jnp.exp(jnp.sum(lv[2:3, :] * lv[3:4, :], axis=-1, keepdims=True)) + lam_init)
    gn = g_ref[...] * (1.0 - lam_init)
    for i in range(seq // tq):
        n = (i + 1) * tq
        q = q_ref[i * tq:n, :] * scale
        outs = []
        for e in range(2):
            qe = jnp.where(half == e, q, jnp.zeros_like(q))
            outs.append(_causal_attend(qe, k_ref, v_ref, n, tq))
        o = outs[0] - lam * outs[1]
        var = jnp.mean(o * o, axis=-1, keepdims=True)
        o_ref[i * tq:n, :] = (o * lax.rsqrt(var + NORM_EPS) * gn).astype(o_ref.dtype)


def _diff_attention(pqk, pv, lam_vecs, g, *, batch, seq, lam_init):
    tq = min(QUERY_TILE, seq)
    ng = DIFF_HEADS
    return pl.pallas_call(
        functools.partial(_diff_kernel, tq=tq, scale=DIFF_DIM ** -0.5, lam_init=lam_init),
        grid=(batch, ng),
        in_specs=[
            pl.BlockSpec((seq, LANES), lambda b, g: (b, g)),
            pl.BlockSpec((seq, LANES), lambda b, g: (b, ng + g)),
            pl.BlockSpec((seq, LANES), lambda b, g: (b, g)),
            pl.BlockSpec(lam_vecs.shape, lambda b, g: (0, 0)),
            pl.BlockSpec((1, LANES), lambda b, g: (0, 0)),
        ],
        out_specs=pl.BlockSpec((seq, LANES), lambda b, g: (b, g)),
        out_shape=jax.ShapeDtypeStruct((batch * seq, ng * LANES), BF16),
        compiler_params=_params("parallel", "parallel"),
        name="diff_attention",
    )(pqk, pqk, pv, lam_vecs, g)


def _sortable_key(x):
    bits = pltpu.bitcast(x, jnp.int32)
    return jnp.where(bits < 0, bits ^ jnp.int32(0x7FFFFFFF), bits)


def _dsa_kernel(q_ref, k0_ref, k1_ref, iq_ref, ik_ref, v0_ref, v1_ref, z_ref, o_ref, key_ref, madd_ref,
                qs_ref, os_ref, half_ref, *, tq, n_sel, scale, idx_scale, w_scale):
    i = pl.program_id(1)
    seq = key_ref.shape[1]
    nq = seq // tq
    span = 2 if nq % 2 == 0 else 1
    half = _lane_half((1, LANES))
    row = lax.broadcasted_iota(jnp.int32, (tq, tq), 0)
    col = lax.broadcasted_iota(jnp.int32, (tq, tq), 1)
    row_g = row + i * tq
    tri = jnp.where(row < col, 1.0, 0.0).astype(BF16)

    def block(n):
        z = z_ref[...]
        iq = iq_ref[...]
        ik = ik_ref[0:n, :]
        sc = jnp.zeros((tq, n), F32)
        for h in range(IDX_HEADS):
            grp = iq[:, (h // 2) * LANES:(h // 2 + 1) * LANES]
            iq_h = jnp.where(half == h % 2, grp, jnp.zeros_like(grp))
            w_h = z[:, SUBLANES + h:SUBLANES + h + 1] * w_scale
            sc = sc + w_h * jnp.maximum(_dot_nt(iq_h, ik) * idx_scale, 0.0)
        for j in range(n // tq):
            scj = jnp.where(col + j * tq <= row_g, sc[:, j * tq:(j + 1) * tq], -jnp.inf)
            kj = _sortable_key(scj)
            key_ref[:, j * tq:(j + 1) * tq] = kj
            half_ref[1, :, j * tq:(j + 1) * tq] = lax.shift_right_arithmetic(kj, 16).astype(jnp.int16)
            half_ref[0, :, j * tq:(j + 1) * tq] = ((kj & 65535) - 32768).astype(jnp.int16)

        one = jnp.ones((tq, LANES), jnp.int16)
        zero = jnp.zeros((tq, LANES), jnp.int16)

        def count(b, pred):
            acc = zero
            for c in range(n // LANES):
                acc = acc + jnp.where(pred(half_ref[b, :, c * LANES:(c + 1) * LANES]), one, zero)
            return jnp.sum(acc.astype(jnp.int32).astype(F32), axis=-1, keepdims=True)

        above = jnp.zeros((tq, 1), F32)
        halves = []
        for b in (1, 0):
            def bit_body(it, tb, b=b, above=above):
                cand = tb + lax.shift_left(jnp.int32(1), 15 - it)
                cand_b = jnp.broadcast_to(cand, (tq, LANES)).astype(jnp.int16)
                return jnp.where(above + count(b, lambda m: m >= cand_b) >= n_sel, cand, tb)

            tb = lax.fori_loop(0, 16, bit_body, jnp.full((tq, 1), -32768, jnp.int32))
            tb_b = jnp.broadcast_to(tb, (tq, LANES)).astype(jnp.int16)
            above = above + count(b, lambda m: m > tb_b)
            if b == 1:
                for c in range(n // LANES):
                    cs = slice(c * LANES, (c + 1) * LANES)
                    half_ref[0, :, cs] = jnp.where(half_ref[1, :, cs] == tb_b, half_ref[0, :, cs], -32768 * one)
            halves.append(tb)
        thr = lax.shift_left(halves[0], 16) | ((halves[1] + 32768) & 65535)
        need = n_sel - above

        off = jnp.zeros((tq, 1), F32)
        for j in range(n // tq):
            kj = key_ref[:, j * tq:(j + 1) * tq]
            eq = kj == thr
            eqf = jnp.where(eq, 1.0, 0.0)
            rank = _dot(eqf.astype(BF16), tri) + off
            take = jnp.where(kj > thr, 1.0, jnp.where(eq, jnp.where(rank < need, 1.0, 0.0), 0.0))
            valid = col + j * tq <= row_g
            madd_ref[:, j * tq:(j + 1) * tq] = jnp.where(valid, jnp.where(take > 0.0, 0.0, NEG_INF), NEG_INF)
            off = off + jnp.sum(eqf, axis=-1, keepdims=True)

        q = q_ref[...] * scale
        n_groups = DSA_HEADS // 2
        for p in range(n_groups):
            qs_ref[p] = q[:, p * LANES:(p + 1) * LANES]

        def group_body(p, carry):
            first_kv = p < n_groups // DSA_KV_HEADS
            qg = qs_ref[p]
            k = jnp.where(first_kv, k0_ref[0:n, :], k1_ref[0:n, :])
            v = jnp.where(first_kv, v0_ref[0:n, :], v1_ref[0:n, :])
            outs = []
            for e in range(2):
                qe = jnp.where(half == e, qg, jnp.zeros_like(qg))
                outs.append(_softmax_pv(_dot_nt(qe, k) + madd_ref[:, 0:n], v))
            os_ref[p] = jnp.where(half == 0, outs[0], outs[1]).astype(os_ref.dtype)
            return carry

        lax.fori_loop(0, n_groups, group_body, 0)
        for p in range(n_groups):
            o_ref[:, p * LANES:(p + 1) * LANES] = os_ref[p]

    for c in range(nq // span):
        pl.when(i // span == c)(functools.partial(block, (c + 1) * span * tq))


def _dsa_attention(p2, p1, z, *, batch, seq):
    tq = min(ATTN_TILE, seq)
    nq = seq // tq
    n_sel = min(DSA_TOPK, seq // 4)
    qw = DSA_HEADS * HEAD_DIM
    return pl.pallas_call(
        functools.partial(_dsa_kernel, tq=tq, n_sel=float(n_sel), scale=HEAD_DIM ** -0.5,
                          idx_scale=IDX_DIM ** -0.5, w_scale=IDX_HEADS ** -0.5),
        grid=(batch, nq),
        in_specs=[
            pl.BlockSpec((tq, qw), lambda b, i: (b * nq + i, 0)),
            pl.BlockSpec((seq, LANES), lambda b, i: (b, 4)),
            pl.BlockSpec((seq, LANES), lambda b, i: (b, 5)),
            pl.BlockSpec((tq, 2 * LANES), lambda b, i: (b * nq + i, 3)),
            pl.BlockSpec((seq, LANES), lambda b, i: (b, 8)),
            pl.BlockSpec((seq, LANES), lambda b, i: (b, 12)),
            pl.BlockSpec((seq, LANES), lambda b, i: (b, 13)),
            pl.BlockSpec((tq, LANES), lambda b, i: (b * nq + i, 0)),
        ],
        out_specs=pl.BlockSpec((tq, qw), lambda b, i: (b * nq + i, 0)),
        out_shape=jax.ShapeDtypeStruct((batch * seq, qw), BF16),
        scratch_shapes=[pltpu.VMEM((tq, seq), jnp.int32), pltpu.VMEM((tq, seq), F32),
                        pltpu.VMEM((DSA_HEADS // 2, tq, LANES), BF16),
                        pltpu.VMEM((DSA_HEADS // 2, tq, LANES), BF16),
                        pltpu.VMEM((2, tq, seq), jnp.int16)],
        compiler_params=_params("parallel", "arbitrary"),
        name="dsa_attention",
    )(p2, p2, p2, p2, p2, p1, p1, z)


def _hgrn_kernel(q_ref, f_ref, i_ref, g_ref, lbl_ref, gn_ref, o_ref, b_scr, *, chunk, layer):
    seq = q_ref.shape[0]
    half = _lane_half((1, LANES))
    logits = lbl_ref[...]
    pe = jnp.exp(logits - jnp.max(logits, axis=0, keepdims=True))
    prob = pe / jnp.sum(pe, axis=0, keepdims=True)
    csum = prob[0:1, :]
    for d in range(1, layer + 1):
        csum = csum + prob[d:d + 1, :]
    lb = csum - prob[0:1, :]

    rowi = lax.broadcasted_iota(jnp.int32, (chunk, 1), 0)
    xor_rc = (lax.broadcasted_iota(jnp.int32, (chunk, chunk), 0)
              ^ lax.broadcasted_iota(jnp.int32, (chunk, chunk), 1))
    same_head = (lax.broadcasted_iota(jnp.int32, (LANES, LANES), 0) // HEAD_DIM
                 == lax.broadcasted_iota(jnp.int32, (LANES, LANES), 1) // HEAD_DIM)
    ones_bd = jnp.where(same_head, 1.0, 0.0).astype(BF16)
    gn = gn_ref[...]

    def chunk_body(c, state_t):
        r0 = pl.multiple_of(c * chunk, chunk)
        q = _silu(q_ref[pl.ds(r0, chunk), :].astype(F32))
        f = lb + (1.0 - lb) * _sigmoid(f_ref[pl.ds(r0, chunk), :])
        kc = 1.0 - f
        v = i_ref[pl.ds(r0, chunk), :].astype(F32)
        b = _cumsum_rows(jnp.log(f))
        b_scr[...] = b
        b_last = b_scr[chunk - 1:chunk, :]

        o = _dot_nt((q * jnp.exp(b)).astype(BF16), state_t.astype(BF16))
        k2 = (kc * jnp.exp(b_last - b)).astype(BF16)
        upd = _dot_tn(v.astype(BF16), k2)
        new_state = state_t * jnp.exp(b_last) + jnp.where(same_head, upd, 0.0)

        s_tot = [jnp.zeros((chunk, chunk), F32), jnp.zeros((chunk, chunk), F32)]
        m = chunk
        while m > SUB:
            hm = m // 2
            pieces = [jnp.broadcast_to(b_scr[blk * m + hm - 1:blk * m + hm, :], (m, LANES))
                      for blk in range(chunk // m)]
            ref = pieces[0] if len(pieces) == 1 else jnp.concatenate(pieces, axis=0)
            right = (rowi % m) >= hm
            qm = jnp.where(right, q * jnp.exp(jnp.where(right, b - ref, 0.0)), 0.0).astype(BF16)
            km = jnp.where(right, 0.0, kc * jnp.exp(jnp.where(right, 0.0, ref - b))).astype(BF16)
            for e in range(2):
                s = _dot_nt(jnp.where(half == e, qm, jnp.zeros_like(qm)), km)
                s_tot[e] = s_tot[e] + (s if m == chunk else jnp.where(xor_rc < m, s, 0.0))
            m = hm
        vb = v.astype(BF16)
        for e in range(2):
            o = o + _dot(s_tot[e].astype(BF16), jnp.where(half == e, vb, jnp.zeros_like(vb)))

        for d in range(SUB):
            ok = (rowi % SUB) >= d
            bs = b if d == 0 else pltpu.roll(b, d, 0)
            ks = kc if d == 0 else pltpu.roll(kc, d, 0)
            vs = v if d == 0 else pltpu.roll(v, d, 0)
            x = jnp.where(ok, q * ks * jnp.exp(jnp.where(ok, b - bs, 0.0)), 0.0)
            o = o + _dot(x.astype(BF16), ones_bd) * vs

        sq = o * o
        s0 = jnp.sum(jnp.where(half == 0, sq, 0.0), axis=-1, keepdims=True)
        s1 = jnp.sum(jnp.where(half == 1, sq, 0.0), axis=-1, keepdims=True)
        var = jnp.where(half == 0, s0, s1) * (1.0 / HEAD_DIM)
        y = o * lax.rsqrt(var + NORM_EPS) * gn * _silu(g_ref[pl.ds(r0, chunk), :].astype(F32))
        o_ref[pl.ds(r0, chunk), :] = y.astype(o_ref.dtype)
        return new_state

    lax.fori_loop(0, seq // chunk, chunk_body, jnp.zeros((LANES, LANES), F32))


def _hgrn(h_qig, h_f, lb_logits, gn, *, batch, seq, layer):
    ng = HGRN_HEADS // 2
    chunk = min(HGRN_CHUNK, seq)
    spec = lambda off: pl.BlockSpec((seq, LANES), lambda b, g: (b, off * ng + g))
    return pl.pallas_call(
        functools.partial(_hgrn_kernel, chunk=chunk, layer=layer),
        grid=(batch, ng),
        in_specs=[spec(0), spec(0), spec(1), spec(2),
                  pl.BlockSpec((lb_logits.shape[0], LANES), lambda b, g: (0, g)),
                  pl.BlockSpec((1, LANES), lambda b, g: (0, 0))],
        out_specs=pl.BlockSpec((seq, LANES), lambda b, g: (b, g)),
        out_shape=jax.ShapeDtypeStruct((batch * seq, ng * LANES), BF16),
        scratch_shapes=[pltpu.VMEM((chunk, LANES), F32)],
        compiler_params=_params("parallel", "parallel"),
        name="hgrn2",
    )(h_qig, h_f, h_qig, h_qig, lb_logits, gn)


def _pack_pairs(x):
    hw = x.shape[1] // 2
    hi = pltpu.bitcast(x[:, :hw].astype(BF16).astype(F32), jnp.int32)
    lo = pltpu.bitcast(x[:, hw:].astype(BF16).astype(F32), jnp.int32)
    return hi | lax.shift_right_logical(lo, 16)


def _unpack_pairs(w):
    hi = pltpu.bitcast(w & jnp.int32(-65536), F32)
    lo = pltpu.bitcast(lax.shift_left(w, 16), F32)
    return hi, lo


def _store_planes(ref, words):
    pw = words.shape[1] // SC_PLANES
    for h in range(SC_PLANES):
        ref[h] = words[:, h * pw:(h + 1) * pw]


def _load_planes(ref):
    return jnp.concatenate([ref[h] for h in range(SC_PLANES)], axis=1)


def _router_kernel(a1_ref, a2_ref, w1_ref, w2_ref, x_ref, g_ref, w_ref, xo_ref, o_ref, xn_ref):
    x = x_ref[...] + _dot(a1_ref[...], w1_ref[...]) + _dot(a2_ref[...], w2_ref[...])
    xo_ref[...] = x
    var = jnp.mean(x * x, axis=-1, keepdims=True)
    xn = x * lax.rsqrt(var + NORM_EPS) * g_ref[...]
    _store_planes(xn_ref, _pack_pairs(xn))
    x_hi = xn.astype(BF16)
    x_lo = (xn - x_hi.astype(F32)).astype(BF16)
    w = w_ref[...]
    w_hi = w.astype(BF16)
    w_lo = (w - w_hi.astype(F32)).astype(BF16)
    logits = _dot(x_hi, w_hi) + _dot(x_hi, w_lo) + _dot(x_lo, w_hi)
    lane = lax.broadcasted_iota(jnp.int32, logits.shape, 1)
    lg = jnp.where(lane < N_EXPERTS, logits, -jnp.inf)
    m1 = jnp.max(lg, axis=-1, keepdims=True)
    i1 = jnp.min(jnp.where(lg == m1, lane, LANES), axis=-1, keepdims=True)
    lg2 = jnp.where(lane == i1, -jnp.inf, lg)
    m2 = jnp.max(lg2, axis=-1, keepdims=True)
    i2 = jnp.min(jnp.where(lg2 == m2, lane, LANES), axis=-1, keepdims=True)
    t = jnp.exp(m2 - m1)
    den = 1.0 + t
    o_ref[...] = (jnp.where(lane == 0, i1.astype(F32), 0.0) + jnp.where(lane == 1, i2.astype(F32), 0.0)
                  + jnp.where(lane == 2, 1.0 / den, 0.0) + jnp.where(lane == 3, t / den, 0.0))


def _mix_router(a1, a2, w1, w2, x, g, w_pad):
    t, d = x.shape
    k1, k2 = a1.shape[1], a2.shape[1]
    tm = ROW_TILE
    return pl.pallas_call(
        _router_kernel,
        grid=(t // tm,),
        in_specs=[pl.BlockSpec((tm, k1), lambda i: (i, 0)),
                  pl.BlockSpec((tm, k2), lambda i: (i, 0)),
                  pl.BlockSpec((k1, d), lambda i: (0, 0)),
                  pl.BlockSpec((k2, d), lambda i: (0, 0)),
                  pl.BlockSpec((tm, d), lambda i: (i, 0)),
                  pl.BlockSpec((1, d), lambda i: (0, 0)),
                  pl.BlockSpec((d, LANES), lambda i: (0, 0))],
        out_specs=[pl.BlockSpec((tm, d), lambda i: (i, 0)),
                   pl.BlockSpec((tm, LANES), lambda i: (i, 0)),
                   pl.BlockSpec((SC_PLANES, tm, d // 2 // SC_PLANES), lambda i: (0, i, 0))],
        out_shape=[jax.ShapeDtypeStruct((t, d), F32),
                   jax.ShapeDtypeStruct((t, LANES), F32),
                   jax.ShapeDtypeStruct((SC_PLANES, t, d // 2 // SC_PLANES), jnp.int32)],
        compiler_params=_params("parallel"),
        name="mix_router",
    )(a1, a2, w1, w2, x, g, w_pad)


def _sc_mesh():
    return plsc.VectorSubcoreMesh(core_axis_name="core", subcore_axis_name="subcore")


def _sc_scatter_rows(x, idx0, idx1, n_out):
    n, d = x.shape

    @functools.partial(pl.kernel, out_type=jax.ShapeDtypeStruct((n_out, d), x.dtype), mesh=_sc_mesh(),
                       scratch_types=[])
    def scatter(x_hbm, i0_hbm, i1_hbm, o_hbm):
        def body(x_vmem, i0_vmem, i1_vmem):
            pltpu.sync_copy(x_vmem, o_hbm.at[i0_vmem.at[0]])
            pltpu.sync_copy(x_vmem, o_hbm.at[i1_vmem.at[0]])

        pltpu.emit_pipeline(
            body,
            grid=(n // SC_WINDOW,),
            in_specs=[pl.BlockSpec((SC_WINDOW, d), index_map=lambda i: (i, 0)),
                      pl.BlockSpec((1, SC_WINDOW), index_map=lambda i: (0, i)),
                      pl.BlockSpec((1, SC_WINDOW), index_map=lambda i: (0, i))],
            out_specs=[],
            core_axis_name=("core", "subcore"),
            dimension_semantics=(pltpu.PARALLEL,),
        )(x_hbm, i0_hbm, i1_hbm)

    return scatter(x, idx0.reshape(1, n), idx1.reshape(1, n))


def _sc_gather_rows(x, idx):
    n = idx.shape[0]
    d = x.shape[1]

    @functools.partial(pl.kernel, out_type=jax.ShapeDtypeStruct((n, d), x.dtype), mesh=_sc_mesh(),
                       scratch_types=[])
    def gather(x_hbm, i_hbm, o_hbm):
        def body(i_vmem, o_vmem):
            pltpu.sync_copy(x_hbm.at[i_vmem.at[0]], o_vmem)

        pltpu.emit_pipeline(
            body,
            grid=(n // SC_WINDOW,),
            in_specs=[pl.BlockSpec((1, SC_WINDOW), index_map=lambda i: (0, i))],
            out_specs=[pl.BlockSpec((SC_WINDOW, d), index_map=lambda i: (i, 0))],
            core_axis_name=("core", "subcore"),
            dimension_semantics=(pltpu.PARALLEL,),
        )(i_hbm, o_hbm)

    return gather(x, idx.reshape(1, n))


def _moe_ffn_packed_kernel(te_ref, act_ref, x_ref, wg_ref, wu_ref, wd_ref, o_ref, xn_ref, acc_ref):
    i = pl.program_id(0)
    f = pl.program_id(1)
    hw = xn_ref.shape[1] // 2

    @pl.when(act_ref[i] == 1)
    def _():
        @pl.when(f == 0)
        def _():
            hi, lo = _unpack_pairs(_load_planes(x_ref))
            xn_ref[:, :hw] = hi.astype(BF16)
            xn_ref[:, hw:] = lo.astype(BF16)
            acc_ref[...] = jnp.zeros_like(acc_ref)

        x = xn_ref[...]
        mid = (_silu(_dot(x, wg_ref[0])) * _dot(x, wu_ref[0])).astype(BF16)
        acc_ref[...] += _dot(mid, wd_ref[0])

        @pl.when(f == pl.num_programs(1) - 1)
        def _():
            _store_planes(o_ref, _pack_pairs(acc_ref[...]))


def _moe_ffn_packed(tile_e, tile_act, xs, wg, wu, wd, *, tf):
    _, p, pw = xs.shape
    d = 2 * pw * SC_PLANES
    ff = wg.shape[2]
    tm = MOE_ROW_TILE
    return pl.pallas_call(
        _moe_ffn_packed_kernel,
        grid_spec=pltpu.PrefetchScalarGridSpec(
            num_scalar_prefetch=2,
            grid=(p // tm, ff // tf),
            in_specs=[pl.BlockSpec((SC_PLANES, tm, pw), lambda i, f, te, ta: (0, i, 0)),
                      pl.BlockSpec((1, d, tf), lambda i, f, te, ta: (te[i], 0, f * ta[i])),
                      pl.BlockSpec((1, d, tf), lambda i, f, te, ta: (te[i], 0, f * ta[i])),
                      pl.BlockSpec((1, tf, d), lambda i, f, te, ta: (te[i], f * ta[i], 0))],
            out_specs=pl.BlockSpec((SC_PLANES, tm, pw), lambda i, f, te, ta: (0, i, 0)),
            scratch_shapes=[pltpu.VMEM((tm, d), BF16), pltpu.VMEM((tm, d), F32)],
        ),
        out_shape=jax.ShapeDtypeStruct((SC_PLANES, p, pw), jnp.int32),
        compiler_params=_params("arbitrary", "arbitrary"),
        name="moe_ffn",
    )(tile_e, tile_act, xs, wg, wu, wd)


def _moe_sum_kernel(x_ref, info_ref, y_ref, gf_ref, o_ref, *, final_norm):
    hw = x_ref.shape[1] // 2
    info = info_ref[...]
    g1 = info[:, 2:3]
    g2 = info[:, 3:4]
    hi1, lo1 = _unpack_pairs(jnp.concatenate([y_ref[h] for h in range(SC_PLANES)], axis=1))
    hi2, lo2 = _unpack_pairs(jnp.concatenate([y_ref[SC_PLANES + h] for h in range(SC_PLANES)], axis=1))
    x = x_ref[...]
    out_a = x[:, :hw] + g1 * hi1 + g2 * hi2
    out_b = x[:, hw:] + g1 * lo1 + g2 * lo2
    if final_norm:
        ss = jnp.sum(out_a * out_a, axis=-1, keepdims=True) + jnp.sum(out_b * out_b, axis=-1, keepdims=True)
        r = lax.rsqrt(ss * (1.0 / (2 * hw)) + NORM_EPS)
        gf = gf_ref[...]
        out_a = out_a * r * gf[:, :hw]
        out_b = out_b * r * gf[:, hw:]
    o_ref[:, :hw] = out_a
    o_ref[:, hw:] = out_b


def _moe_sum(x, info, y_pairs, gf, *, final_norm):
    t, d = x.shape
    tm = ROW_TILE
    return pl.pallas_call(
        functools.partial(_moe_sum_kernel, final_norm=final_norm),
        grid=(t // tm,),
        in_specs=[pl.BlockSpec((tm, d), lambda i: (i, 0)),
                  pl.BlockSpec((tm, LANES), lambda i: (i, 0)),
                  pl.BlockSpec((2 * SC_PLANES, tm, y_pairs.shape[2]), lambda i: (0, i, 0)),
                  pl.BlockSpec((1, d), lambda i: (0, 0))],
        out_specs=pl.BlockSpec((tm, d), lambda i: (i, 0)),
        out_shape=jax.ShapeDtypeStruct((t, d), F32),
        compiler_params=_params("parallel"),
        name="moe_sum",
    )(x, info, y_pairs, gf)


def _moe_positions(info, t):
    tm = MOE_ROW_TILE
    n_e = N_EXPERTS
    a = 2 * t
    p = a + n_e * tm
    e_flat = info[:, 0:2].astype(jnp.int32).reshape(a)
    onehot = (e_flat[:, None] == jnp.arange(n_e, dtype=jnp.int32)[None, :]).astype(jnp.int32)
    csum = jnp.cumsum(onehot, axis=0)
    rank = jnp.sum(csum * onehot, axis=1) - 1
    counts = csum[-1]
    padded = ((counts + tm - 1) // tm) * tm
    ends = jnp.cumsum(padded)
    starts = ends - padded
    pos = jnp.sum(onehot * starts[None, :], axis=1) + rank
    tile_start = jnp.arange(p // tm, dtype=jnp.int32) * tm
    tile_e = jnp.minimum(jnp.sum((tile_start[:, None] >= ends[None, :]).astype(jnp.int32), axis=1), n_e - 1)
    tile_act = (tile_start < ends[-1]).astype(jnp.int32)
    return pos.astype(jnp.int32).reshape(t, 2), tile_e.astype(jnp.int32), tile_act, p


def _mix_ffn_kernel(a1_ref, a2_ref, w1_ref, w2_ref, x_ref, g_ref, wg_ref, wu_ref, wd_ref, gf_ref, o_ref,
                    *, final_norm):
    x = x_ref[...] + _dot(a1_ref[...], w1_ref[...]) + _dot(a2_ref[...], w2_ref[...])
    var = jnp.mean(x * x, axis=-1, keepdims=True)
    xn = (x * lax.rsqrt(var + NORM_EPS) * g_ref[...]).astype(BF16)
    mid = (_silu(_dot(xn, wg_ref[...])) * _dot(xn, wu_ref[...])).astype(BF16)
    out = x + _dot(mid, wd_ref[...])
    if final_norm:
        var = jnp.mean(out * out, axis=-1, keepdims=True)
        out = out * lax.rsqrt(var + NORM_EPS) * gf_ref[...]
    o_ref[...] = out


def _resident(shape):
    return pl.BlockSpec(shape, lambda i: (0,) * len(shape), pipeline_mode=pl.Buffered(1))


def _mix_ffn(a1, a2, w1, w2, x, g, wg, wu, wd, gf, *, final_norm):
    t, d = x.shape
    ff = wg.shape[1]
    k1, k2 = a1.shape[1], a2.shape[1]
    tm = ROW_TILE
    return pl.pallas_call(
        functools.partial(_mix_ffn_kernel, final_norm=final_norm),
        grid=(t // tm,),
        in_specs=[
            pl.BlockSpec((tm, k1), lambda i: (i, 0)),
            pl.BlockSpec((tm, k2), lambda i: (i, 0)),
            _resident((k1, d)),
            _resident((k2, d)),
            pl.BlockSpec((tm, d), lambda i: (i, 0)),
            _resident((1, d)),
            _resident((d, ff)),
            _resident((d, ff)),
            _resident((ff, d)),
            _resident((1, d)),
        ],
        out_specs=pl.BlockSpec((tm, d), lambda i: (i, 0)),
        out_shape=jax.ShapeDtypeStruct((t, d), F32),
        compiler_params=_params("parallel"),
        name="mix_ffn",
    )(a1, a2, w1, w2, x, g, wg, wu, wd, gf)


def _rope_tables(seq):
    half = HEAD_DIM // 2
    inv = ROPE_THETA ** (-jnp.arange(half, dtype=F32) * 2.0 / HEAD_DIM)
    ang = jnp.arange(seq, dtype=F32)[:, None] * inv[None, :]
    cos = jnp.cos(ang)
    sin = jnp.sin(ang)
    cos_t = jnp.concatenate([cos, cos, cos, cos], axis=-1)
    sin_t = jnp.concatenate([-sin, sin, -sin, sin], axis=-1)
    return cos_t, sin_t


def _dup_heads(w):
    d, n = w.shape
    w = w.reshape(d, n // HEAD_DIM, 1, HEAD_DIM)
    return jnp.broadcast_to(w, (d, n // HEAD_DIM, 2, HEAD_DIM)).reshape(d, 2 * n)


def kernel(x, ln_mix_e, w_in_e, b_fox_f, w_out_e, ln_ffn_e, w_ffn_gate, w_ffn_up, w_ffn_down,
           ln_mix_o, w_in_o, hgrn_lb_logits, hgrn_norm_g, diff_lambda, diff_norm_g, w_out_o,
           ln_ffn_o, w_router, w_exp_gate, w_exp_up, w_exp_down, ln_final):
    batch, seq, d = x.shape
    t = batch * seq
    depth = ln_mix_e.shape[0] + ln_mix_o.shape[0]
    xs = x.reshape(t, d)
    cos_t, sin_t = _rope_tables(seq)
    row = lambda v: v.reshape(1, -1)
    fw = FOX_HEADS * HEAD_DIM
    dw = DSA_HEADS * HEAD_DIM
    kvw = DSA_KV_HEADS * HEAD_DIM
    hw = HGRN_HEADS * HEAD_DIM
    qw = DIFF_HEADS * 2 * DIFF_DIM

    for layer in range(depth):
        j = layer // 2
        if layer % 2 == 0:
            w = w_in_e[j]
            o = 0
            fq, fk, fv = w[:, o:o + fw], w[:, o + fw:o + 2 * fw], w[:, o + 2 * fw:o + 3 * fw]
            o += 3 * fw
            ffw = w[:, o:o + FOX_HEADS]
            o += FOX_HEADS
            dq = w[:, o:o + dw]
            o += dw
            dk = w[:, o:o + kvw]
            o += kvw
            dv = w[:, o:o + kvw]
            o += kvw
            iqw = w[:, o:o + IDX_HEADS * IDX_DIM]
            o += IDX_HEADS * IDX_DIM
            ikw = w[:, o:o + IDX_DIM]
            o += IDX_DIM
            iww = w[:, o:o + IDX_HEADS]
            w1 = jnp.concatenate([fq, fk, fv, _dup_heads(dv)], axis=1).astype(BF16)
            w2 = jnp.concatenate([dq, _dup_heads(dk), iqw, _dup_heads(ikw)], axis=1).astype(BF16)
            w3 = jnp.concatenate([ffw, iww, jnp.zeros((d, LANES - FOX_HEADS - IDX_HEADS), F32)], axis=1).astype(BF16)
            g = row(ln_mix_e[j])
            p1, p2, z = _rms_proj(xs, g, cos_t, sin_t, (w1, w2, w3), (False, True, False), (BF16, BF16, F32), seq=seq)
            b_pad = jnp.concatenate([b_fox_f[j], jnp.zeros((LANES - FOX_HEADS,), F32)]).reshape(1, LANES)
            ccol, crow = _fox_gate(z, b_pad, batch=batch, seq=seq)
            fox_o = _fox_attention(p1, ccol, crow, batch=batch, seq=seq)
            dsa_o = _dsa_attention(p2, p1, z, batch=batch, seq=seq)
            wo = w_out_e[j].astype(BF16)
            xs = _mix_ffn(fox_o, dsa_o, wo[:fw], wo[fw:], xs, row(ln_ffn_e[j]), w_ffn_gate[j].astype(BF16),
                          w_ffn_up[j].astype(BF16), w_ffn_down[j].astype(BF16), row(ln_final),
                          final_norm=(layer == depth - 1))
        else:
            w = w_in_o[j]
            lam_init = 0.8 - 0.6 * math.exp(-0.3 * layer)
            g = row(ln_mix_o[j])
            w_qig = jnp.concatenate([w[:, :hw], w[:, 2 * hw:4 * hw]], axis=1).astype(BF16)
            w_f = w[:, hw:2 * hw].astype(BF16)
            w_qk = w[:, 4 * hw:4 * hw + 2 * qw].astype(BF16)
            w_v = w[:, 4 * hw + 2 * qw:].astype(BF16)
            h_qig, h_f, pqk, pv = _rms_proj(xs, g, cos_t, sin_t, (w_qig, w_f, w_qk, w_v),
                                            (False, False, True, False), (BF16, F32, BF16, BF16), seq=seq)
            gn = jnp.concatenate([hgrn_norm_g[j], hgrn_norm_g[j]]).reshape(1, LANES)
            o_h = _hgrn(h_qig, h_f, hgrn_lb_logits, gn, batch=batch, seq=seq, layer=layer)
            o_d = _diff_attention(pqk, pv, diff_lambda[j], row(diff_norm_g[j]), batch=batch, seq=seq,
                                  lam_init=lam_init)
            wo = w_out_o[j].astype(BF16)
            wr = jnp.concatenate([w_router[j], jnp.zeros((d, LANES - N_EXPERTS), F32)], axis=1)
            xs, info, xn = _mix_router(o_h, o_d, wo[:hw], wo[hw:], xs, row(ln_ffn_o[j]), wr)
            pos, tile_e, tile_act, n_rows = _moe_positions(info, t)
            pw = xn.shape[2]
            plane = jnp.arange(SC_PLANES, dtype=jnp.int32)[:, None] * n_rows
            dst0 = (plane + pos[None, :, 0]).reshape(SC_PLANES * t)
            dst1 = (plane + pos[None, :, 1]).reshape(SC_PLANES * t)
            rows = _sc_scatter_rows(xn.reshape(SC_PLANES * t, pw), dst0, dst1, SC_PLANES * n_rows)
            y = _moe_ffn_packed(tile_e, tile_act, rows.reshape(SC_PLANES, n_rows, pw), w_exp_gate[j].astype(BF16),
                                w_exp_up[j].astype(BF16), w_exp_down[j].astype(BF16), tf=1792)
            src = jnp.concatenate([(plane + pos[None, :, s]).reshape(SC_PLANES * t) for s in range(2)])
            y_pairs = _sc_gather_rows(y.reshape(SC_PLANES * n_rows, pw), src).reshape(2 * SC_PLANES, t, pw)
            xs = _moe_sum(xs, info, y_pairs, row(ln_final), final_norm=(layer == depth - 1))
    if depth % 2 == 1:
        pass
    return xs.reshape(batch, seq, d)
```

```python
import functools
import math

import jax
import jax.numpy as jnp
from jax import lax
from jax.experimental import pallas as pl
from jax.experimental.pallas import tpu as pltpu
from jax.experimental.pallas import tpu_sc as plsc

F32 = jnp.float32
BF16 = jnp.bfloat16

HEAD_DIM = 64
FOX_HEADS = 8
DSA_HEADS = 8
DSA_KV_HEADS = 2
IDX_HEADS = 4
IDX_DIM = 64
DSA_TOPK = 256
HGRN_HEADS = 8
DIFF_HEADS = 4
DIFF_DIM = 64
ROPE_THETA = 10000.0
N_EXPERTS = 8
NORM_EPS = 1e-6
NEG_INF = -1e30

LANES = 128
SUBLANES = 8
VMEM_LIMIT_BYTES = 56 * 1024 * 1024

ROW_TILE = 512
PROJ_ROW_TILE = 1024
ATTN_TILE = 256
QUERY_TILE = 512
HGRN_CHUNK = 256
SUB = 8
MOE_ROW_TILE = 512
SC_WINDOW = 128
SC_PLANES = 2

INT_MIN = -(2 ** 31)


def _params(*sem):
    return pltpu.CompilerParams(dimension_semantics=sem, vmem_limit_bytes=VMEM_LIMIT_BYTES)


def _dot(a, b):
    return jnp.dot(a, b, preferred_element_type=F32)


def _dot_nt(a, b):
    return lax.dot_general(a, b, (((1,), (1,)), ((), ())), preferred_element_type=F32)


def _dot_tn(a, b):
    return lax.dot_general(a, b, (((0,), (0,)), ((), ())), preferred_element_type=F32)


def _sigmoid(x):
    return 1.0 / (1.0 + jnp.exp(-x))


def _silu(x):
    return x * _sigmoid(x)


def _lane_half(shape):
    return lax.broadcasted_iota(jnp.int32, shape, len(shape) - 1) // HEAD_DIM


def _rms_proj_kernel(x_ref, g_ref, cos_ref, sin_ref, *refs, ropes):
    n = len(ropes)
    w_refs, o_refs = refs[:n], refs[n:]
    x = x_ref[...]
    var = jnp.mean(x * x, axis=-1, keepdims=True)
    xn = (x * lax.rsqrt(var + NORM_EPS) * g_ref[...]).astype(BF16)
    cos = cos_ref[...]
    sin = sin_ref[...]
    first = (lax.broadcasted_iota(jnp.int32, cos.shape, 1) % HEAD_DIM) < (HEAD_DIM // 2)
    for w_ref, o_ref, rope in zip(w_refs, o_refs, ropes):
        y = _dot(xn, w_ref[...])
        if not rope:
            o_ref[...] = y.astype(o_ref.dtype)
            continue
        for c in range(y.shape[1] // LANES):
            yc = y[:, c * LANES:(c + 1) * LANES]
            partner = jnp.where(first, pltpu.roll(yc, LANES - HEAD_DIM // 2, 1), pltpu.roll(yc, HEAD_DIM // 2, 1))
            o_ref[:, c * LANES:(c + 1) * LANES] = (yc * cos + partner * sin).astype(o_ref.dtype)


def _rms_proj(x, g, cos, sin, ws, ropes, out_dtypes, *, seq):
    t, d = x.shape
    tm = min(PROJ_ROW_TILE, seq)
    nseq = seq // tm
    return pl.pallas_call(
        functools.partial(_rms_proj_kernel, ropes=tuple(ropes)),
        grid=(t // tm,),
        in_specs=[
            pl.BlockSpec((tm, d), lambda i: (i, 0)),
            pl.BlockSpec((1, d), lambda i: (0, 0)),
            pl.BlockSpec((tm, LANES), lambda i: (i % nseq, 0)),
            pl.BlockSpec((tm, LANES), lambda i: (i % nseq, 0)),
        ] + [pl.BlockSpec((d, w.shape[1]), lambda i: (0, 0)) for w in ws],
        out_specs=[pl.BlockSpec((tm, w.shape[1]), lambda i: (i, 0)) for w in ws],
        out_shape=[jax.ShapeDtypeStruct((t, w.shape[1]), dt) for w, dt in zip(ws, out_dtypes)],
        compiler_params=_params("parallel"),
        name="rms_proj",
    )(x, g, cos, sin, *ws)


def _cumsum_rows(x):
    rows = x.shape[0]
    row = lax.broadcasted_iota(jnp.int32, x.shape, 0)
    k = 1
    while k < rows:
        x = x + jnp.where(row >= k, pltpu.roll(x, k, 0), 0.0)
        k *= 2
    return x


def _fox_gate_kernel(z_ref, b_ref, ccol_ref, crow_ref, *, tk):
    z = z_ref[...] + b_ref[...]
    logf = jnp.minimum(z, 0.0) - jnp.log(1.0 + jnp.exp(-jnp.abs(z)))
    c = _cumsum_rows(logf)
    ccol_ref[...] = c
    for j in range(c.shape[0] // tk):
        ct = c[j * tk:(j + 1) * tk, :].T
        crow_ref[0, :, j * tk:(j + 1) * tk] = ct[:SUBLANES, :]


def _fox_gate(z, b_pad, *, batch, seq):
    tk = ATTN_TILE
    return pl.pallas_call(
        functools.partial(_fox_gate_kernel, tk=tk),
        grid=(batch,),
        in_specs=[
            pl.BlockSpec((seq, LANES), lambda b: (b, 0)),
            pl.BlockSpec((1, LANES), lambda b: (0, 0)),
        ],
        out_specs=[
            pl.BlockSpec((seq, LANES), lambda b: (b, 0)),
            pl.BlockSpec((1, SUBLANES, seq), lambda b: (b, 0, 0)),
        ],
        out_shape=[
            jax.ShapeDtypeStruct((batch * seq, LANES), F32),
            jax.ShapeDtypeStruct((batch, SUBLANES, seq), F32),
        ],
        compiler_params=_params("parallel"),
        name="fox_gate",
    )(z, b_pad)


def _softmax_pv(s, v):
    m = jnp.max(s, axis=-1, keepdims=True)
    p = jnp.exp(s - m)
    l = jnp.sum(p, axis=-1, keepdims=True)
    return _dot(p.astype(BF16), v) / l


def _causal_attend(qe, k_ref, v_ref, n, tq, row_bias=None, key_bias=None):
    causal = lax.broadcasted_iota(jnp.int32, (tq, tq), 1) <= lax.broadcasted_iota(jnp.int32, (tq, tq), 0)
    parts = []
    for lo, hi in ((0, n - tq), (n - tq, n)):
        if hi == lo:
            continue
        s = _dot_nt(qe, k_ref[lo:hi, :])
        if key_bias is not None:
            s = s - key_bias[:, lo:hi]
        if hi == n:
            s = jnp.where(causal, s, NEG_INF)
        parts.append((s, v_ref[lo:hi, :]))
    m = functools.reduce(jnp.maximum, [jnp.max(s, axis=-1, keepdims=True) for s, _ in parts])
    shift = -m if row_bias is None else row_bias - (m + row_bias)
    acc = 0.0
    l = 0.0
    for s, v in parts:
        p = jnp.exp(s + shift)
        l = l + jnp.sum(p, axis=-1, keepdims=True)
        acc = acc + _dot(p.astype(BF16), v)
    return acc / l


def _fox_kernel(q_ref, k_ref, v_ref, cc_ref, cr_ref, o_ref, *, tq, scale):
    hp = pl.program_id(1)
    seq = q_ref.shape[0]
    half = _lane_half((1, LANES))
    lane = lax.broadcasted_iota(jnp.int32, (tq, LANES), 1)
    for i in range(seq // tq):
        n = (i + 1) * tq
        q = q_ref[i * tq:n, :] * scale
        cc = cc_ref[i * tq:n, :]
        outs = []
        for e in range(2):
            h = 2 * hp + e
            qe = jnp.where(half == e, q, jnp.zeros_like(q))
            cq = jnp.sum(jnp.where(lane == h, cc, 0.0), axis=-1, keepdims=True)
            ck = cr_ref[0, pl.ds(h, 1), 0:n]
            outs.append(_causal_attend(qe, k_ref, v_ref, n, tq, row_bias=cq, key_bias=ck))
        o_ref[i * tq:n, :] = jnp.where(half == 0, outs[0], outs[1]).astype(o_ref.dtype)


def _fox_attention(p1, ccol, crow, *, batch, seq):
    tq = min(QUERY_TILE, seq)
    ng = FOX_HEADS // 2
    return pl.pallas_call(
        functools.partial(_fox_kernel, tq=tq, scale=HEAD_DIM ** -0.5),
        grid=(batch, ng),
        in_specs=[
            pl.BlockSpec((seq, LANES), lambda b, g: (b, g)),
            pl.BlockSpec((seq, LANES), lambda b, g: (b, ng + g)),
            pl.BlockSpec((seq, LANES), lambda b, g: (b, 2 * ng + g)),
            pl.BlockSpec((seq, LANES), lambda b, g: (b, 0)),
            pl.BlockSpec((1, SUBLANES, seq), lambda b, g: (b, 0, 0)),
        ],
        out_specs=pl.BlockSpec((seq, LANES), lambda b, g: (b, g)),
        out_shape=jax.ShapeDtypeStruct((batch * seq, ng * LANES), BF16),
        compiler_params=_params("parallel", "parallel"),
        name="fox_attention",
    )(p1, p1, p1, ccol, crow)


def _diff_kernel(q_ref, k_ref, v_ref, lam_ref, g_ref, o_ref, *, tq, scale, lam_init):
    seq = q_ref.shape[0]
    half = _lane_half((1, LANES))
    lv = lam_ref[...]
    lam = (jnp.exp(jnp.sum(lv[0:1, :] * lv[1:2, :], axis=-1, keepdims=True))
           - jnp.exp(jnp.sum(lv[2:3, :] * lv[3:4, :], axis=-1, keepdims=True)) + lam_init)
    gn = g_ref[...] * (1.0 - lam_init)
    for i in range(seq // tq):
        n = (i + 1) * tq
        q = q_ref[i * tq:n, :] * scale
        outs = []
        for e in range(2):
            qe = jnp.where(half == e, q, jnp.zeros_like(q))
            outs.append(_causal_attend(qe, k_ref, v_ref, n, tq))
        o = outs[0] - lam * outs[1]
        var = jnp.mean(o * o, axis=-1, keepdims=True)
        o_ref[i * tq:n, :] = (o * lax.rsqrt(var + NORM_EPS) * gn).astype(o_ref.dtype)


def _diff_attention(pqk, pv, lam_vecs, g, *, batch, seq, lam_init):
    tq = min(QUERY_TILE, seq)
    ng = DIFF_HEADS
    return pl.pallas_call(
        functools.partial(_diff_kernel, tq=tq, scale=DIFF_DIM ** -0.5, lam_init=lam_init),
        grid=(batch, ng),
        in_specs=[
            pl.BlockSpec((seq, LANES), lambda b, g: (b, g)),
            pl.BlockSpec((seq, LANES), lambda b, g: (b, ng + g)),
            pl.BlockSpec((seq, LANES), lambda b, g: (b, g)),
            pl.BlockSpec(lam_vecs.shape, lambda b, g: (0, 0)),
            pl.BlockSpec((1, LANES), lambda b, g: (0, 0)),
        ],
        out_specs=pl.BlockSpec((seq, LANES), lambda b, g: (b, g)),
        out_shape=jax.ShapeDtypeStruct((batch * seq, ng * LANES), BF16),
        compiler_params=_params("parallel", "parallel"),
        name="diff_attention",
    )(pqk, pqk, pv, lam_vecs, g)


def _sortable_key(x):
    bits = pltpu.bitcast(x, jnp.int32)
    return jnp.where(bits < 0, bits ^ jnp.int32(0x7FFFFFFF), bits)


def _dsa_kernel(q_ref, kd_ref, iq_ref, ik_ref, vd_ref, z_ref, o_ref, key_ref, madd_ref,
                *, tq, n_sel, scale, idx_scale, w_scale):
    i = pl.program_id(1)
    seq = key_ref.shape[1]
    nq = seq // tq
    span = 2 if nq % 2 == 0 else 1
    half = _lane_half((1, LANES))
    row = lax.broadcasted_iota(jnp.int32, (tq, tq), 0)
    col = lax.broadcasted_iota(jnp.int32, (tq, tq), 1)
    row_g = row + i * tq
    tri = jnp.where(row < col, 1.0, 0.0).astype(BF16)

    def block(n):
        z = z_ref[...]
        iq = iq_ref[...]
        ik = ik_ref[0:n, :]
        sc = jnp.zeros((tq, n), F32)
        for h in range(IDX_HEADS):
            grp = iq[:, (h // 2) * LANES:(h // 2 + 1) * LANES]
            iq_h = jnp.where(half == h % 2, grp, jnp.zeros_like(grp))
            w_h = z[:, SUBLANES + h:SUBLANES + h + 1] * w_scale
            sc = sc + w_h * jnp.maximum(_dot_nt(iq_h, ik) * idx_scale, 0.0)
        for j in range(n // tq):
            scj = jnp.where(col + j * tq <= row_g, sc[:, j * tq:(j + 1) * tq], -jnp.inf)
            key_ref[:, j * tq:(j + 1) * tq] = _sortable_key(scj)

        def count(pred):
            return jnp.sum(jnp.where(pred, 1.0, 0.0), axis=-1, keepdims=True)

        thr0 = jnp.where(count(key_ref[:, 0:n] >= 0) >= n_sel, 0, INT_MIN).astype(jnp.int32)

        def bit_body(it, thr):
            cand = thr | jnp.left_shift(jnp.int32(1), 30 - it)
            return jnp.where(count(key_ref[:, 0:n] >= cand) >= n_sel, cand, thr)

        thr = lax.fori_loop(0, 31, bit_body, thr0)
        need = n_sel - count(key_ref[:, 0:n] > thr)

        off = jnp.zeros((tq, 1), F32)
        for j in range(n // tq):
            kj = key_ref[:, j * tq:(j + 1) * tq]
            eq = kj == thr
            eqf = jnp.where(eq, 1.0, 0.0)
            rank = _dot(eqf.astype(BF16), tri) + off
            take = jnp.where(kj > thr, 1.0, jnp.where(eq, jnp.where(rank < need, 1.0, 0.0), 0.0))
            valid = col + j * tq <= row_g
            madd_ref[:, j * tq:(j + 1) * tq] = jnp.where(valid, jnp.where(take > 0.0, 0.0, NEG_INF), NEG_INF)
            off = off + jnp.sum(eqf, axis=-1, keepdims=True)

        q = q_ref[...] * scale
        for p in range(DSA_HEADS // 2):
            qg = q[:, p * LANES:(p + 1) * LANES]
            g = p // (DSA_HEADS // DSA_KV_HEADS // 2)
            k = kd_ref[0:n, g * LANES:(g + 1) * LANES]
            v = vd_ref[0:n, g * LANES:(g + 1) * LANES]
            outs = []
            for e in range(2):
                qe = jnp.where(half == e, qg, jnp.zeros_like(qg))
                outs.append(_softmax_pv(_dot_nt(qe, k) + madd_ref[:, 0:n], v))
            o_ref[:, p * LANES:(p + 1) * LANES] = jnp.where(half == 0, outs[0], outs[1]).astype(o_ref.dtype)

    for c in range(nq // span):
        pl.when(i // span == c)(functools.partial(block, (c + 1) * span * tq))


def _dsa_attention(p2, p1, z, *, batch, seq):
    tq = min(ATTN_TILE, seq)
    nq = seq // tq
    n_sel = min(DSA_TOPK, seq // 4)
    qw = DSA_HEADS * HEAD_DIM
    return pl.pallas_call(
        functools.partial(_dsa_kernel, tq=tq, n_sel=float(n_sel), scale=HEAD_DIM ** -0.5,
                          idx_scale=IDX_DIM ** -0.5, w_scale=IDX_HEADS ** -0.5),
        grid=(batch, nq),
        in_specs=[
            pl.BlockSpec((tq, qw), lambda b, i: (b * nq + i, 0)),
            pl.BlockSpec((seq, 2 * LANES), lambda b, i: (b, 2)),
            pl.BlockSpec((tq, 2 * LANES), lambda b, i: (b * nq + i, 3)),
            pl.BlockSpec((seq, LANES), lambda b, i: (b, 8)),
            pl.BlockSpec((seq, 2 * LANES), lambda b, i: (b, 6)),
            pl.BlockSpec((tq, LANES), lambda b, i: (b * nq + i, 0)),
        ],
        out_specs=pl.BlockSpec((tq, qw), lambda b, i: (b * nq + i, 0)),
        out_shape=jax.ShapeDtypeStruct((batch * seq, qw), BF16),
        scratch_shapes=[pltpu.VMEM((tq, seq), jnp.int32), pltpu.VMEM((tq, seq), F32)],
        compiler_params=_params("parallel", "arbitrary"),
        name="dsa_attention",
    )(p2, p2, p2, p2, p1, z)


def _hgrn_kernel(q_ref, f_ref, i_ref, g_ref, lbl_ref, gn_ref, o_ref, b_scr, *, chunk, layer):
    seq = q_ref.shape[0]
    half = _lane_half((1, LANES))
    logits = lbl_ref[...]
    pe = jnp.exp(logits - jnp.max(logits, axis=0, keepdims=True))
    prob = pe / jnp.sum(pe, axis=0, keepdims=True)
    csum = prob[0:1, :]
    for d in range(1, layer + 1):
        csum = csum + prob[d:d + 1, :]
    lb = csum - prob[0:1, :]

    rowi = lax.broadcasted_iota(jnp.int32, (chunk, 1), 0)
    xor_rc = (lax.broadcasted_iota(jnp.int32, (chunk, chunk), 0)
              ^ lax.broadcasted_iota(jnp.int32, (chunk, chunk), 1))
    same_head = (lax.broadcasted_iota(jnp.int32, (LANES, LANES), 0) // HEAD_DIM
                 == lax.broadcasted_iota(jnp.int32, (LANES, LANES), 1) // HEAD_DIM)
    ones_bd = jnp.where(same_head, 1.0, 0.0).astype(BF16)
    gn = gn_ref[...]

    def chunk_body(c, state_t):
        r0 = pl.multiple_of(c * chunk, chunk)
        q = _silu(q_ref[pl.ds(r0, chunk), :].astype(F32))
        f = lb + (1.0 - lb) * _sigmoid(f_ref[pl.ds(r0, chunk), :])
        kc = 1.0 - f
        v = i_ref[pl.ds(r0, chunk), :].astype(F32)
        b = _cumsum_rows(jnp.log(f))
        b_scr[...] = b
        b_last = b_scr[chunk - 1:chunk, :]

        o = _dot_nt((q * jnp.exp(b)).astype(BF16), state_t.astype(BF16))
        k2 = (kc * jnp.exp(b_last - b)).astype(BF16)
        upd = _dot_tn(v.astype(BF16), k2)
        new_state = state_t * jnp.exp(b_last) + jnp.where(same_head, upd, 0.0)

        s_tot = [jnp.zeros((chunk, chunk), F32), jnp.zeros((chunk, chunk), F32)]
        m = chunk
        while m > SUB:
            hm = m // 2
            pieces = [jnp.broadcast_to(b_scr[blk * m + hm - 1:blk * m + hm, :], (m, LANES))
                      for blk in range(chunk // m)]
            ref = pieces[0] if len(pieces) == 1 else jnp.concatenate(pieces, axis=0)
            right = (rowi % m) >= hm
            qm = jnp.where(right, q * jnp.exp(jnp.where(right, b - ref, 0.0)), 0.0).astype(BF16)
            km = jnp.where(right, 0.0, kc * jnp.exp(jnp.where(right, 0.0, ref - b))).astype(BF16)
            for e in range(2):
                s = _dot_nt(jnp.where(half == e, qm, jnp.zeros_like(qm)), km)
                s_tot[e] = s_tot[e] + (s if m == chunk else jnp.where(xor_rc < m, s, 0.0))
            m = hm
        vb = v.astype(BF16)
        for e in range(2):
            o = o + _dot(s_tot[e].astype(BF16), jnp.where(half == e, vb, jnp.zeros_like(vb)))

        for d in range(SUB):
            ok = (rowi % SUB) >= d
            bs = b if d == 0 else pltpu.roll(b, d, 0)
            ks = kc if d == 0 else pltpu.roll(kc, d, 0)
            vs = v if d == 0 else pltpu.roll(v, d, 0)
            x = jnp.where(ok, q * ks * jnp.exp(jnp.where(ok, b - bs, 0.0)), 0.0)
            o = o + _dot(x.astype(BF16), ones_bd) * vs

        sq = o * o
        s0 = jnp.sum(jnp.where(half == 0, sq, 0.0), axis=-1, keepdims=True)
        s1 = jnp.sum(jnp.where(half == 1, sq, 0.0), axis=-1, keepdims=True)
        var = jnp.where(half == 0, s0, s1) * (1.0 / HEAD_DIM)
        y = o * lax.rsqrt(var + NORM_EPS) * gn * _silu(g_ref[pl.ds(r0, chunk), :].astype(F32))
        o_ref[pl.ds(r0, chunk), :] = y.astype(o_ref.dtype)
        return new_state

    lax.fori_loop(0, seq // chunk, chunk_body, jnp.zeros((LANES, LANES), F32))


def _hgrn(h_qig, h_f, lb_logits, gn, *, batch, seq, layer):
    ng = HGRN_HEADS // 2
    chunk = min(HGRN_CHUNK, seq)
    spec = lambda off: pl.BlockSpec((seq, LANES), lambda b, g: (b, off * ng + g))
    return pl.pallas_call(
        functools.partial(_hgrn_kernel, chunk=chunk, layer=layer),
        grid=(batch, ng),
        in_specs=[spec(0), spec(0), spec(1), spec(2),
                  pl.BlockSpec((lb_logits.shape[0], LANES), lambda b, g: (0, g)),
                  pl.BlockSpec((1, LANES), lambda b, g: (0, 0))],
        out_specs=pl.BlockSpec((seq, LANES), lambda b, g: (b, g)),
        out_shape=jax.ShapeDtypeStruct((batch * seq, ng * LANES), BF16),
        scratch_shapes=[pltpu.VMEM((chunk, LANES), F32)],
        compiler_params=_params("parallel", "parallel"),
        name="hgrn2",
    )(h_qig, h_f, h_qig, h_qig, lb_logits, gn)


def _pack_pairs(x):
    hw = x.shape[1] // 2
    hi = pltpu.bitcast(x[:, :hw].astype(BF16).astype(F32), jnp.int32)
    lo = pltpu.bitcast(x[:, hw:].astype(BF16).astype(F32), jnp.int32)
    return hi | lax.shift_right_logical(lo, 16)


def _unpack_pairs(w):
    hi = pltpu.bitcast(w & jnp.int32(-65536), F32)
    lo = pltpu.bitcast(lax.shift_left(w, 16), F32)
    return hi, lo


def _store_planes(ref, words):
    pw = words.shape[1] // SC_PLANES
    for h in range(SC_PLANES):
        ref[h] = words[:, h * pw:(h + 1) * pw]


def _load_planes(ref):
    return jnp.concatenate([ref[h] for h in range(SC_PLANES)], axis=1)


def _router_kernel(a1_ref, a2_ref, w1_ref, w2_ref, x_ref, g_ref, w_ref, xo_ref, o_ref, xn_ref):
    x = x_ref[...] + _dot(a1_ref[...], w1_ref[...]) + _dot(a2_ref[...], w2_ref[...])
    xo_ref[...] = x
    var = jnp.mean(x * x, axis=-1, keepdims=True)
    xn = x * lax.rsqrt(var + NORM_EPS) * g_ref[...]
    _store_planes(xn_ref, _pack_pairs(xn))
    x_hi = xn.astype(BF16)
    x_lo = (xn - x_hi.astype(F32)).astype(BF16)
    w = w_ref[...]
    w_hi = w.astype(BF16)
    w_lo = (w - w_hi.astype(F32)).astype(BF16)
    logits = _dot(x_hi, w_hi) + _dot(x_hi, w_lo) + _dot(x_lo, w_hi)
    lane = lax.broadcasted_iota(jnp.int32, logits.shape, 1)
    lg = jnp.where(lane < N_EXPERTS, logits, -jnp.inf)
    m1 = jnp.max(lg, axis=-1, keepdims=True)
    i1 = jnp.min(jnp.where(lg == m1, lane, LANES), axis=-1, keepdims=True)
    lg2 = jnp.where(lane == i1, -jnp.inf, lg)
    m2 = jnp.max(lg2, axis=-1, keepdims=True)
    i2 = jnp.min(jnp.where(lg2 == m2, lane, LANES), axis=-1, keepdims=True)
    t = jnp.exp(m2 - m1)
    den = 1.0 + t
    o_ref[...] = (jnp.where(lane == 0, i1.astype(F32), 0.0) + jnp.where(lane == 1, i2.astype(F32), 0.0)
                  + jnp.where(lane == 2, 1.0 / den, 0.0) + jnp.where(lane == 3, t / den, 0.0))


def _mix_router(a1, a2, w1, w2, x, g, w_pad):
    t, d = x.shape
    k1, k2 = a1.shape[1], a2.shape[1]
    tm = ROW_TILE
    return pl.pallas_call(
        _router_kernel,
        grid=(t // tm,),
        in_specs=[pl.BlockSpec((tm, k1), lambda i: (i, 0)),
                  pl.BlockSpec((tm, k2), lambda i: (i, 0)),
                  pl.BlockSpec((k1, d), lambda i: (0, 0)),
                  pl.BlockSpec((k2, d), lambda i: (0, 0)),
                  pl.BlockSpec((tm, d), lambda i: (i, 0)),
                  pl.BlockSpec((1, d), lambda i: (0, 0)),
                  pl.BlockSpec((d, LANES), lambda i: (0, 0))],
        out_specs=[pl.BlockSpec((tm, d), lambda i: (i, 0)),
                   pl.BlockSpec((tm, LANES), lambda i: (i, 0)),
                   pl.BlockSpec((SC_PLANES, tm, d // 2 // SC_PLANES), lambda i: (0, i, 0))],
        out_shape=[jax.ShapeDtypeStruct((t, d), F32),
                   jax.ShapeDtypeStruct((t, LANES), F32),
                   jax.ShapeDtypeStruct((SC_PLANES, t, d // 2 // SC_PLANES), jnp.int32)],
        compiler_params=_params("parallel"),
        name="mix_router",
    )(a1, a2, w1, w2, x, g, w_pad)


def _sc_mesh():
    return plsc.VectorSubcoreMesh(core_axis_name="core", subcore_axis_name="subcore")


def _sc_scatter_rows(x, idx0, idx1, n_out):
    n, d = x.shape

    @functools.partial(pl.kernel, out_type=jax.ShapeDtypeStruct((n_out, d), x.dtype), mesh=_sc_mesh(),
                       scratch_types=[])
    def scatter(x_hbm, i0_hbm, i1_hbm, o_hbm):
        def body(x_vmem, i0_vmem, i1_vmem):
            pltpu.sync_copy(x_vmem, o_hbm.at[i0_vmem.at[0]])
            pltpu.sync_copy(x_vmem, o_hbm.at[i1_vmem.at[0]])

        pltpu.emit_pipeline(
            body,
            grid=(n // SC_WINDOW,),
            in_specs=[pl.BlockSpec((SC_WINDOW, d), index_map=lambda i: (i, 0)),
                      pl.BlockSpec((1, SC_WINDOW), index_map=lambda i: (0, i)),
                      pl.BlockSpec((1, SC_WINDOW), index_map=lambda i: (0, i))],
            out_specs=[],
            core_axis_name=("core", "subcore"),
            dimension_semantics=(pltpu.PARALLEL,),
        )(x_hbm, i0_hbm, i1_hbm)

    return scatter(x, idx0.reshape(1, n), idx1.reshape(1, n))


def _sc_gather_rows(x, idx):
    n = idx.shape[0]
    d = x.shape[1]

    @functools.partial(pl.kernel, out_type=jax.ShapeDtypeStruct((n, d), x.dtype), mesh=_sc_mesh(),
                       scratch_types=[])
    def gather(x_hbm, i_hbm, o_hbm):
        def body(i_vmem, o_vmem):
            pltpu.sync_copy(x_hbm.at[i_vmem.at[0]], o_vmem)

        pltpu.emit_pipeline(
            body,
            grid=(n // SC_WINDOW,),
            in_specs=[pl.BlockSpec((1, SC_WINDOW), index_map=lambda i: (0, i))],
            out_specs=[pl.BlockSpec((SC_WINDOW, d), index_map=lambda i: (i, 0))],
            core_axis_name=("core", "subcore"),
            dimension_semantics=(pltpu.PARALLEL,),
        )(i_hbm, o_hbm)

    return gather(x, idx.reshape(1, n))


def _moe_ffn_packed_kernel(te_ref, act_ref, x_ref, wg_ref, wu_ref, wd_ref, o_ref, xn_ref, acc_ref):
    i = pl.program_id(0)
    f = pl.program_id(1)
    hw = xn_ref.shape[1] // 2

    @pl.when(act_ref[i] == 1)
    def _():
        @pl.when(f == 0)
        def _():
            hi, lo = _unpack_pairs(_load_planes(x_ref))
            xn_ref[:, :hw] = hi.astype(BF16)
            xn_ref[:, hw:] = lo.astype(BF16)
            acc_ref[...] = jnp.zeros_like(acc_ref)

        x = xn_ref[...]
        mid = (_silu(_dot(x, wg_ref[0])) * _dot(x, wu_ref[0])).astype(BF16)
        acc_ref[...] += _dot(mid, wd_ref[0])

        @pl.when(f == pl.num_programs(1) - 1)
        def _():
            _store_planes(o_ref, _pack_pairs(acc_ref[...]))


def _moe_ffn_packed(tile_e, tile_act, xs, wg, wu, wd, *, tf):
    _, p, pw = xs.shape
    d = 2 * pw * SC_PLANES
    ff = wg.shape[2]
    tm = MOE_ROW_TILE
    return pl.pallas_call(
        _moe_ffn_packed_kernel,
        grid_spec=pltpu.PrefetchScalarGridSpec(
            num_scalar_prefetch=2,
            grid=(p // tm, ff // tf),
            in_specs=[pl.BlockSpec((SC_PLANES, tm, pw), lambda i, f, te, ta: (0, i, 0)),
                      pl.BlockSpec((1, d, tf), lambda i, f, te, ta: (te[i], 0, f * ta[i])),
                      pl.BlockSpec((1, d, tf), lambda i, f, te, ta: (te[i], 0, f * ta[i])),
                      pl.BlockSpec((1, tf, d), lambda i, f, te, ta: (te[i], f * ta[i], 0))],
            out_specs=pl.BlockSpec((SC_PLANES, tm, pw), lambda i, f, te, ta: (0, i, 0)),
            scratch_shapes=[pltpu.VMEM((tm, d), BF16), pltpu.VMEM((tm, d), F32)],
        ),
        out_shape=jax.ShapeDtypeStruct((SC_PLANES, p, pw), jnp.int32),
        compiler_params=_params("arbitrary", "arbitrary"),
        name="moe_ffn",
    )(tile_e, tile_act, xs, wg, wu, wd)


def _moe_sum_kernel(x_ref, info_ref, y_ref, gf_ref, o_ref, *, final_norm):
    hw = x_ref.shape[1] // 2
    info = info_ref[...]
    g1 = info[:, 2:3]
    g2 = info[:, 3:4]
    hi1, lo1 = _unpack_pairs(jnp.concatenate([y_ref[h] for h in range(SC_PLANES)], axis=1))
    hi2, lo2 = _unpack_pairs(jnp.concatenate([y_ref[SC_PLANES + h] for h in range(SC_PLANES)], axis=1))
    x = x_ref[...]
    out_a = x[:, :hw] + g1 * hi1 + g2 * hi2
    out_b = x[:, hw:] + g1 * lo1 + g2 * lo2
    if final_norm:
        ss = jnp.sum(out_a * out_a, axis=-1, keepdims=True) + jnp.sum(out_b * out_b, axis=-1, keepdims=True)
        r = lax.rsqrt(ss * (1.0 / (2 * hw)) + NORM_EPS)
        gf = gf_ref[...]
        out_a = out_a * r * gf[:, :hw]
        out_b = out_b * r * gf[:, hw:]
    o_ref[:, :hw] = out_a
    o_ref[:, hw:] = out_b


def _moe_sum(x, info, y_pairs, gf, *, final_norm):
    t, d = x.shape
    tm = ROW_TILE
    return pl.pallas_call(
        functools.partial(_moe_sum_kernel, final_norm=final_norm),
        grid=(t // tm,),
        in_specs=[pl.BlockSpec((tm, d), lambda i: (i, 0)),
                  pl.BlockSpec((tm, LANES), lambda i: (i, 0)),
                  pl.BlockSpec((2 * SC_PLANES, tm, y_pairs.shape[2]), lambda i: (0, i, 0)),
                  pl.BlockSpec((1, d), lambda i: (0, 0))],
        out_specs=pl.BlockSpec((tm, d), lambda i: (i, 0)),
        out_shape=jax.ShapeDtypeStruct((t, d), F32),
        compiler_params=_params("parallel"),
        name="moe_sum",
    )(x, info, y_pairs, gf)


def _moe_positions(info, t):
    tm = MOE_ROW_TILE
    n_e = N_EXPERTS
    a = 2 * t
    p = a + n_e * tm
    e_flat = info[:, 0:2].astype(jnp.int32).reshape(a)
    onehot = (e_flat[:, None] == jnp.arange(n_e, dtype=jnp.int32)[None, :]).astype(jnp.int32)
    csum = jnp.cumsum(onehot, axis=0)
    rank = jnp.sum(csum * onehot, axis=1) - 1
    counts = csum[-1]
    padded = ((counts + tm - 1) // tm) * tm
    ends = jnp.cumsum(padded)
    starts = ends - padded
    pos = jnp.sum(onehot * starts[None, :], axis=1) + rank
    tile_start = jnp.arange(p // tm, dtype=jnp.int32) * tm
    tile_e = jnp.minimum(jnp.sum((tile_start[:, None] >= ends[None, :]).astype(jnp.int32), axis=1), n_e - 1)
    tile_act = (tile_start < ends[-1]).astype(jnp.int32)
    return pos.astype(jnp.int32).reshape(t, 2), tile_e.astype(jnp.int32), tile_act, p


def _mix_ffn_kernel(a1_ref, a2_ref, w1_ref, w2_ref, x_ref, g_ref, wg_ref, wu_ref, wd_ref, gf_ref, o_ref,
                    *, final_norm):
    x = x_ref[...] + _dot(a1_ref[...], w1_ref[...]) + _dot(a2_ref[...], w2_ref[...])
    var = jnp.mean(x * x, axis=-1, keepdims=True)
    xn = (x * lax.rsqrt(var + NORM_EPS) * g_ref[...]).astype(BF16)
    mid = (_silu(_dot(xn, wg_ref[...])) * _dot(xn, wu_ref[...])).astype(BF16)
    out = x + _dot(mid, wd_ref[...])
    if final_norm:
        var = jnp.mean(out * out, axis=-1, keepdims=True)
        out = out * lax.rsqrt(var + NORM_EPS) * gf_ref[...]
    o_ref[...] = out


def _resident(shape):
    return pl.BlockSpec(shape, lambda i: (0,) * len(shape), pipeline_mode=pl.Buffered(1))


def _mix_ffn(a1, a2, w1, w2, x, g, wg, wu, wd, gf, *, final_norm):
    t, d = x.shape
    ff = wg.shape[1]
    k1, k2 = a1.shape[1], a2.shape[1]
    tm = ROW_TILE
    return pl.pallas_call(
        functools.partial(_mix_ffn_kernel, final_norm=final_norm),
        grid=(t // tm,),
        in_specs=[
            pl.BlockSpec((tm, k1), lambda i: (i, 0)),
            pl.BlockSpec((tm, k2), lambda i: (i, 0)),
            _resident((k1, d)),
            _resident((k2, d)),
            pl.BlockSpec((tm, d), lambda i: (i, 0)),
            _resident((1, d)),
            _resident((d, ff)),
            _resident((d, ff)),
            _resident((ff, d)),
            _resident((1, d)),
        ],
        out_specs=pl.BlockSpec((tm, d), lambda i: (i, 0)),
        out_shape=jax.ShapeDtypeStruct((t, d), F32),
        compiler_params=_params("parallel"),
        name="mix_ffn",
    )(a1, a2, w1, w2, x, g, wg, wu, wd, gf)


def _rope_tables(seq):
    half = HEAD_DIM // 2
    inv = ROPE_THETA ** (-jnp.arange(half, dtype=F32) * 2.0 / HEAD_DIM)
    ang = jnp.arange(seq, dtype=F32)[:, None] * inv[None, :]
    cos = jnp.cos(ang)
    sin = jnp.sin(ang)
    cos_t = jnp.concatenate([cos, cos, cos, cos], axis=-1)
    sin_t = jnp.concatenate([-sin, sin, -sin, sin], axis=-1)
    return cos_t, sin_t


def _dup_heads(w):
    d, n = w.shape
    w = w.reshape(d, n // HEAD_DIM, 1, HEAD_DIM)
    return jnp.broadcast_to(w, (d, n // HEAD_DIM, 2, HEAD_DIM)).reshape(d, 2 * n)


def kernel(x, ln_mix_e, w_in_e, b_fox_f, w_out_e, ln_ffn_e, w_ffn_gate, w_ffn_up, w_ffn_down,
           ln_mix_o, w_in_o, hgrn_lb_logits, hgrn_norm_g, diff_lambda, diff_norm_g, w_out_o,
           ln_ffn_o, w_router, w_exp_gate, w_exp_up, w_exp_down, ln_final):
    batch, seq, d = x.shape
    t = batch * seq
    depth = ln_mix_e.shape[0] + ln_mix_o.shape[0]
    xs = x.reshape(t, d)
    cos_t, sin_t = _rope_tables(seq)
    row = lambda v: v.reshape(1, -1)
    fw = FOX_HEADS * HEAD_DIM
    dw = DSA_HEADS * HEAD_DIM
    kvw = DSA_KV_HEADS * HEAD_DIM
    hw = HGRN_HEADS * HEAD_DIM
    qw = DIFF_HEADS * 2 * DIFF_DIM

    for layer in range(depth):
        j = layer // 2
        if layer % 2 == 0:
            w = w_in_e[j]
            o = 0
            fq, fk, fv = w[:, o:o + fw], w[:, o + fw:o + 2 * fw], w[:, o + 2 * fw:o + 3 * fw]
            o += 3 * fw
            ffw = w[:, o:o + FOX_HEADS]
            o += FOX_HEADS
            dq = w[:, o:o + dw]
            o += dw
            dk = w[:, o:o + kvw]
            o += kvw
            dv = w[:, o:o + kvw]
            o += kvw
            iqw = w[:, o:o + IDX_HEADS * IDX_DIM]
            o += IDX_HEADS * IDX_DIM
            ikw = w[:, o:o + IDX_DIM]
            o += IDX_DIM
            iww = w[:, o:o + IDX_HEADS]
            w1 = jnp.concatenate([fq, fk, fv, _dup_heads(dv)], axis=1).astype(BF16)
            w2 = jnp.concatenate([dq, _dup_heads(dk), iqw, _dup_heads(ikw)], axis=1).astype(BF16)
            w3 = jnp.concatenate([ffw, iww, jnp.zeros((d, LANES - FOX_HEADS - IDX_HEADS), F32)], axis=1).astype(BF16)
            g = row(ln_mix_e[j])
            p1, p2, z = _rms_proj(xs, g, cos_t, sin_t, (w1, w2, w3), (False, True, False), (BF16, BF16, F32), seq=seq)
            b_pad = jnp.concatenate([b_fox_f[j], jnp.zeros((LANES - FOX_HEADS,), F32)]).reshape(1, LANES)
            ccol, crow = _fox_gate(z, b_pad, batch=batch, seq=seq)
            fox_o = _fox_attention(p1, ccol, crow, batch=batch, seq=seq)
            dsa_o = _dsa_attention(p2, p1, z, batch=batch, seq=seq)
            wo = w_out_e[j].astype(BF16)
            xs = _mix_ffn(fox_o, dsa_o, wo[:fw], wo[fw:], xs, row(ln_ffn_e[j]), w_ffn_gate[j].astype(BF16),
                          w_ffn_up[j].astype(BF16), w_ffn_down[j].astype(BF16), row(ln_final),
                          final_norm=(layer == depth - 1))
        else:
            w = w_in_o[j]
            lam_init = 0.8 - 0.6 * math.exp(-0.3 * layer)
            g = row(ln_mix_o[j])
            w_qig = jnp.concatenate([w[:, :hw], w[:, 2 * hw:4 * hw]], axis=1).astype(BF16)
            w_f = w[:, hw:2 * hw].astype(BF16)
            w_qk = w[:, 4 * hw:4 * hw + 2 * qw].astype(BF16)
            w_v = w[:, 4 * hw + 2 * qw:].astype(BF16)
            h_qig, h_f, pqk, pv = _rms_proj(xs, g, cos_t, sin_t, (w_qig, w_f, w_qk, w_v),
                                            (False, False, True, False), (BF16, F32, BF16, BF16), seq=seq)
            gn = jnp.concatenate([hgrn_norm_g[j], hgrn_norm_g[j]]).reshape(1, LANES)
            o_h = _hgrn(h_qig, h_f, hgrn_lb_logits, gn, batch=batch, seq=seq, layer=layer)
            o_d = _diff_attention(pqk, pv, diff_lambda[j], row(diff_norm_g[j]), batch=batch, seq=seq,
                                  lam_init=lam_init)
            wo = w_out_o[j].astype(BF16)
            wr = jnp.concatenate([w_router[j], jnp.zeros((d, LANES - N_EXPERTS), F32)], axis=1)
            xs, info, xn = _mix_router(o_h, o_d, wo[:hw], wo[hw:], xs, row(ln_ffn_o[j]), wr)
            pos, tile_e, tile_act, n_rows = _moe_positions(info, t)
            pw = xn.shape[2]
            plane = jnp.arange(SC_PLANES, dtype=jnp.int32)[:, None] * n_rows
            dst0 = (plane + pos[None, :, 0]).reshape(SC_PLANES * t)
            dst1 = (plane + pos[None, :, 1]).reshape(SC_PLANES * t)
            rows = _sc_scatter_rows(xn.reshape(SC_PLANES * t, pw), dst0, dst1, SC_PLANES * n_rows)
            y = _moe_ffn_packed(tile_e, tile_act, rows.reshape(SC_PLANES, n_rows, pw), w_exp_gate[j].astype(BF16),
                                w_exp_up[j].astype(BF16), w_exp_down[j].astype(BF16), tf=1792)
            src = jnp.concatenate([(plane + pos[None, :, s]).reshape(SC_PLANES * t) for s in range(2)])
            y_pairs = _sc_gather_rows(y.reshape(SC_PLANES * n_rows, pw), src).reshape(2 * SC_PLANES, t, pw)
            xs = _moe_sum(xs, info, y_pairs, row(ln_final), final_norm=(layer == depth - 1))
    if depth % 2 == 1:
        pass
    return xs.reshape(batch, seq, d)
```

```python
import functools
import math

import jax
import jax.numpy as jnp
from jax import lax
from jax.experimental import pallas as pl
from jax.experimental.pallas import tpu as pltpu
from jax.experimental.pallas import tpu_sc as plsc

F32 = jnp.float32
BF16 = jnp.bfloat16

HEAD_DIM = 64
FOX_HEADS = 8
DSA_HEADS = 8
DSA_KV_HEADS = 2
IDX_HEADS = 4
IDX_DIM = 64
DSA_TOPK = 256
HGRN_HEADS = 8
DIFF_HEADS = 4
DIFF_DIM = 64
ROPE_THETA = 10000.0
N_EXPERTS = 8
NORM_EPS = 1e-6
NEG_INF = -1e30

LANES = 128
SUBLANES = 8
VMEM_LIMIT_BYTES = 56 * 1024 * 1024

ROW_TILE = 512
PROJ_ROW_TILE = 1024
ATTN_TILE = 256
QUERY_TILE = 512
HGRN_CHUNK = 256
SUB = 8
MOE_ROW_TILE = 512
SC_WINDOW = 128
SC_PLANES = 2

INT_MIN = -(2 ** 31)


def _params(*sem):
    return pltpu.CompilerParams(dimension_semantics=sem, vmem_limit_bytes=VMEM_LIMIT_BYTES)


def _dot(a, b):
    return jnp.dot(a, b, preferred_element_type=F32)


def _dot_nt(a, b):
    return lax.dot_general(a, b, (((1,), (1,)), ((), ())), preferred_element_type=F32)


def _dot_tn(a, b):
    return lax.dot_general(a, b, (((0,), (0,)), ((), ())), preferred_element_type=F32)


def _sigmoid(x):
    return 1.0 / (1.0 + jnp.exp(-x))


def _silu(x):
    return x * _sigmoid(x)


def _lane_half(shape):
    return lax.broadcasted_iota(jnp.int32, shape, len(shape) - 1) // HEAD_DIM


def _rms_proj_kernel(x_ref, g_ref, cos_ref, sin_ref, *refs, ropes):
    n = len(ropes)
    w_refs, o_refs = refs[:n], refs[n:]
    x = x_ref[...]
    var = jnp.mean(x * x, axis=-1, keepdims=True)
    xn = (x * lax.rsqrt(var + NORM_EPS) * g_ref[...]).astype(BF16)
    cos = cos_ref[...]
    sin = sin_ref[...]
    first = (lax.broadcasted_iota(jnp.int32, cos.shape, 1) % HEAD_DIM) < (HEAD_DIM // 2)
    for w_ref, o_ref, rope in zip(w_refs, o_refs, ropes):
        y = _dot(xn, w_ref[...])
        if not rope:
            o_ref[...] = y.astype(o_ref.dtype)
            continue
        for c in range(y.shape[1] // LANES):
            yc = y[:, c * LANES:(c + 1) * LANES]
            partner = jnp.where(first, pltpu.roll(yc, LANES - HEAD_DIM // 2, 1), pltpu.roll(yc, HEAD_DIM // 2, 1))
            o_ref[:, c * LANES:(c + 1) * LANES] = (yc * cos + partner * sin).astype(o_ref.dtype)


def _rms_proj(x, g, cos, sin, ws, ropes, out_dtypes, *, seq):
    t, d = x.shape
    tm = min(PROJ_ROW_TILE, seq)
    nseq = seq // tm
    return pl.pallas_call(
        functools.partial(_rms_proj_kernel, ropes=tuple(ropes)),
        grid=(t // tm,),
        in_specs=[
            pl.BlockSpec((tm, d), lambda i: (i, 0)),
            pl.BlockSpec((1, d), lambda i: (0, 0)),
            pl.BlockSpec((tm, LANES), lambda i: (i % nseq, 0)),
            pl.BlockSpec((tm, LANES), lambda i: (i % nseq, 0)),
        ] + [pl.BlockSpec((d, w.shape[1]), lambda i: (0, 0)) for w in ws],
        out_specs=[pl.BlockSpec((tm, w.shape[1]), lambda i: (i, 0)) for w in ws],
        out_shape=[jax.ShapeDtypeStruct((t, w.shape[1]), dt) for w, dt in zip(ws, out_dtypes)],
        compiler_params=_params("parallel"),
        name="rms_proj",
    )(x, g, cos, sin, *ws)


def _cumsum_rows(x):
    rows = x.shape[0]
    row = lax.broadcasted_iota(jnp.int32, x.shape, 0)
    k = 1
    while k < rows:
        x = x + jnp.where(row >= k, pltpu.roll(x, k, 0), 0.0)
        k *= 2
    return x


def _fox_gate_kernel(z_ref, b_ref, ccol_ref, crow_ref, *, tk):
    z = z_ref[...] + b_ref[...]
    logf = jnp.minimum(z, 0.0) - jnp.log(1.0 + jnp.exp(-jnp.abs(z)))
    c = _cumsum_rows(logf)
    ccol_ref[...] = c
    for j in range(c.shape[0] // tk):
        ct = c[j * tk:(j + 1) * tk, :].T
        crow_ref[0, :, j * tk:(j + 1) * tk] = ct[:SUBLANES, :]


def _fox_gate(z, b_pad, *, batch, seq):
    tk = ATTN_TILE
    return pl.pallas_call(
        functools.partial(_fox_gate_kernel, tk=tk),
        grid=(batch,),
        in_specs=[
            pl.BlockSpec((seq, LANES), lambda b: (b, 0)),
            pl.BlockSpec((1, LANES), lambda b: (0, 0)),
        ],
        out_specs=[
            pl.BlockSpec((seq, LANES), lambda b: (b, 0)),
            pl.BlockSpec((1, SUBLANES, seq), lambda b: (b, 0, 0)),
        ],
        out_shape=[
            jax.ShapeDtypeStruct((batch * seq, LANES), F32),
            jax.ShapeDtypeStruct((batch, SUBLANES, seq), F32),
        ],
        compiler_params=_params("parallel"),
        name="fox_gate",
    )(z, b_pad)


def _softmax_pv(s, v):
    m = jnp.max(s, axis=-1, keepdims=True)
    p = jnp.exp(s - m)
    l = jnp.sum(p, axis=-1, keepdims=True)
    return _dot(p.astype(BF16), v) / l


def _causal_attend(qe, k_ref, v_ref, n, tq, row_bias=None, key_bias=None):
    causal = lax.broadcasted_iota(jnp.int32, (tq, tq), 1) <= lax.broadcasted_iota(jnp.int32, (tq, tq), 0)
    parts = []
    for lo, hi in ((0, n - tq), (n - tq, n)):
        if hi == lo:
            continue
        s = _dot_nt(qe, k_ref[lo:hi, :])
        if key_bias is not None:
            s = s - key_bias[:, lo:hi]
        if hi == n:
            s = jnp.where(causal, s, NEG_INF)
        parts.append((s, v_ref[lo:hi, :]))
    m = functools.reduce(jnp.maximum, [jnp.max(s, axis=-1, keepdims=True) for s, _ in parts])
    shift = -m if row_bias is None else row_bias - (m + row_bias)
    acc = 0.0
    l = 0.0
    for s, v in parts:
        p = jnp.exp(s + shift)
        l = l + jnp.sum(p, axis=-1, keepdims=True)
        acc = acc + _dot(p.astype(BF16), v)
    return acc / l


def _fox_kernel(q_ref, k_ref, v_ref, cc_ref, cr_ref, o_ref, *, tq, scale):
    hp = pl.program_id(1)
    seq = q_ref.shape[0]
    half = _lane_half((1, LANES))
    lane = lax.broadcasted_iota(jnp.int32, (tq, LANES), 1)
    for i in range(seq // tq):
        n = (i + 1) * tq
        q = q_ref[i * tq:n, :] * scale
        cc = cc_ref[i * tq:n, :]
        outs = []
        for e in range(2):
            h = 2 * hp + e
            qe = jnp.where(half == e, q, jnp.zeros_like(q))
            cq = jnp.sum(jnp.where(lane == h, cc, 0.0), axis=-1, keepdims=True)
            ck = cr_ref[0, pl.ds(h, 1), 0:n]
            outs.append(_causal_attend(qe, k_ref, v_ref, n, tq, row_bias=cq, key_bias=ck))
        o_ref[i * tq:n, :] = jnp.where(half == 0, outs[0], outs[1]).astype(o_ref.dtype)


def _fox_attention(p1, ccol, crow, *, batch, seq):
    tq = min(QUERY_TILE, seq)
    ng = FOX_HEADS // 2
    return pl.pallas_call(
        functools.partial(_fox_kernel, tq=tq, scale=HEAD_DIM ** -0.5),
        grid=(batch, ng),
        in_specs=[
            pl.BlockSpec((seq, LANES), lambda b, g: (b, g)),
            pl.BlockSpec((seq, LANES), lambda b, g: (b, ng + g)),
            pl.BlockSpec((seq, LANES), lambda b, g: (b, 2 * ng + g)),
            pl.BlockSpec((seq, LANES), lambda b, g: (b, 0)),
            pl.BlockSpec((1, SUBLANES, seq), lambda b, g: (b, 0, 0)),
        ],
        out_specs=pl.BlockSpec((seq, LANES), lambda b, g: (b, g)),
        out_shape=jax.ShapeDtypeStruct((batch * seq, ng * LANES), BF16),
        compiler_params=_params("parallel", "parallel"),
        name="fox_attention",
    )(p1, p1, p1, ccol, crow)


def _diff_kernel(q_ref, k_ref, v_ref, lam_ref, g_ref, o_ref, *, tq, scale, lam_init):
    seq = q_ref.shape[0]
    half = _lane_half((1, LANES))
    lv = lam_ref[...]
    lam = (jnp.exp(jnp.sum(lv[0:1, :] * lv[1:2, :], axis=-1, keepdims=True))
           - jnp.exp(jnp.sum(lv[2:3, :] * lv[3:4, :], axis=-1, keepdims=True)) + lam_init)
    gn = g_ref[...] * (1.0 - lam_init)
    for i in range(seq // tq):
        n = (i + 1) * tq
        q = q_ref[i * tq:n, :] * scale
        outs = []
        for e in range(2):
            qe = jnp.where(half == e, q, jnp.zeros_like(q))
            outs.append(_causal_attend(qe, k_ref, v_ref, n, tq))
        o = outs[0] - lam * outs[1]
        var = jnp.mean(o * o, axis=-1, keepdims=True)
        o_ref[i * tq:n, :] = (o * lax.rsqrt(var + NORM_EPS) * gn).astype(o_ref.dtype)


def _diff_attention(pqk, pv, lam_vecs, g, *, batch, seq, lam_init):
    tq = min(QUERY_TILE, seq)
    ng = DIFF_HEADS
    return pl.pallas_call(
        functools.partial(_diff_kernel, tq=tq, scale=DIFF_DIM ** -0.5, lam_init=lam_init),
        grid=(batch, ng),
        in_specs=[
            pl.BlockSpec((seq, LANES), lambda b, g: (b, g)),
            pl.BlockSpec((seq, LANES), lambda b, g: (b, ng + g)),
            pl.BlockSpec((seq, LANES), lambda b, g: (b, g)),
            pl.BlockSpec(lam_vecs.shape, lambda b, g: (0, 0)),
            pl.BlockSpec((1, LANES), lambda b, g: (0, 0)),
        ],
        out_specs=pl.BlockSpec((seq, LANES), lambda b, g: (b, g)),
        out_shape=jax.ShapeDtypeStruct((batch * seq, ng * LANES), BF16),
        compiler_params=_params("parallel", "parallel"),
        name="diff_attention",
    )(pqk, pqk, pv, lam_vecs, g)


def _sortable_key(x):
    bits = pltpu.bitcast(x, jnp.int32)
    return jnp.where(bits < 0, bits ^ jnp.int32(0x7FFFFFFF), bits)


def _dsa_kernel(q_ref, kd_ref, iq_ref, ik_ref, vd_ref, z_ref, o_ref, key_ref, madd_ref,
                *, tq, n_sel, scale, idx_scale, w_scale):
    i = pl.program_id(1)
    seq = key_ref.shape[1]
    nq = seq // tq
    span = 2 if nq % 2 == 0 else 1
    half = _lane_half((1, LANES))
    row = lax.broadcasted_iota(jnp.int32, (tq, tq), 0)
    col = lax.broadcasted_iota(jnp.int32, (tq, tq), 1)
    row_g = row + i * tq
    tri = jnp.where(row < col, 1.0, 0.0).astype(BF16)

    def block(n):
        z = z_ref[...]
        iq = iq_ref[...]
        ik = ik_ref[0:n, :]
        sc = jnp.zeros((tq, n), F32)
        for h in range(IDX_HEADS):
            grp = iq[:, (h // 2) * LANES:(h // 2 + 1) * LANES]
            iq_h = jnp.where(half == h % 2, grp, jnp.zeros_like(grp))
            w_h = z[:, SUBLANES + h:SUBLANES + h + 1] * w_scale
            sc = sc + w_h * jnp.maximum(_dot_nt(iq_h, ik) * idx_scale, 0.0)
        for j in range(n // tq):
            scj = jnp.where(col + j * tq <= row_g, sc[:, j * tq:(j + 1) * tq], -jnp.inf)
            key_ref[:, j * tq:(j + 1) * tq] = _sortable_key(scj)

        def count(pred):
            return jnp.sum(jnp.where(pred, 1.0, 0.0), axis=-1, keepdims=True)

        thr0 = jnp.where(count(key_ref[:, 0:n] >= 0) >= n_sel, 0, INT_MIN).astype(jnp.int32)

        def bit_body(it, thr):
            cand = thr | jnp.left_shift(jnp.int32(1), 30 - it)
            return jnp.where(count(key_ref[:, 0:n] >= cand) >= n_sel, cand, thr)

        thr = lax.fori_loop(0, 31, bit_body, thr0)
        need = n_sel - count(key_ref[:, 0:n] > thr)

        off = jnp.zeros((tq, 1), F32)
        for j in range(n // tq):
            kj = key_ref[:, j * tq:(j + 1) * tq]
            eq = kj == thr
            eqf = jnp.where(eq, 1.0, 0.0)
            rank = _dot(eqf.astype(BF16), tri) + off
            take = jnp.where(kj > thr, 1.0, jnp.where(eq, jnp.where(rank < need, 1.0, 0.0), 0.0))
            valid = col + j * tq <= row_g
            madd_ref[:, j * tq:(j + 1) * tq] = jnp.where(valid, jnp.where(take > 0.0, 0.0, NEG_INF), NEG_INF)
            off = off + jnp.sum(eqf, axis=-1, keepdims=True)

        q = q_ref[...] * scale
        for p in range(DSA_HEADS // 2):
            qg = q[:, p * LANES:(p + 1) * LANES]
            g = p // (DSA_HEADS // DSA_KV_HEADS // 2)
            k = kd_ref[0:n, g * LANES:(g + 1) * LANES]
            v = vd_ref[0:n, g * LANES:(g + 1) * LANES]
            outs = []
            for e in range(2):
                qe = jnp.where(half == e, qg, jnp.zeros_like(qg))
                outs.append(_softmax_pv(_dot_nt(qe, k) + madd_ref[:, 0:n], v))
            o_ref[:, p * LANES:(p + 1) * LANES] = jnp.where(half == 0, outs[0], outs[1]).astype(o_ref.dtype)

    for c in range(nq // span):
        pl.when(i // span == c)(functools.partial(block, (c + 1) * span * tq))


def _dsa_attention(p2, p1, z, *, batch, seq):
    tq = min(ATTN_TILE, seq)
    nq = seq // tq
    n_sel = min(DSA_TOPK, seq // 4)
    qw = DSA_HEADS * HEAD_DIM
    return pl.pallas_call(
        functools.partial(_dsa_kernel, tq=tq, n_sel=float(n_sel), scale=HEAD_DIM ** -0.5,
                          idx_scale=IDX_DIM ** -0.5, w_scale=IDX_HEADS ** -0.5),
        grid=(batch, nq),
        in_specs=[
            pl.BlockSpec((tq, qw), lambda b, i: (b * nq + i, 0)),
            pl.BlockSpec((seq, 2 * LANES), lambda b, i: (b, 2)),
            pl.BlockSpec((tq, 2 * LANES), lambda b, i: (b * nq + i, 3)),
            pl.BlockSpec((seq, LANES), lambda b, i: (b, 8)),
            pl.BlockSpec((seq, 2 * LANES), lambda b, i: (b, 6)),
            pl.BlockSpec((tq, LANES), lambda b, i: (b * nq + i, 0)),
        ],
        out_specs=pl.BlockSpec((tq, qw), lambda b, i: (b * nq + i, 0)),
        out_shape=jax.ShapeDtypeStruct((batch * seq, qw), BF16),
        scratch_shapes=[pltpu.VMEM((tq, seq), jnp.int32), pltpu.VMEM((tq, seq), F32)],
        compiler_params=_params("parallel", "arbitrary"),
        name="dsa_attention",
    )(p2, p2, p2, p2, p1, z)


def _hgrn_kernel(q_ref, f_ref, i_ref, g_ref, lbl_ref, gn_ref, o_ref, b_scr, *, chunk, layer):
    seq = q_ref.shape[0]
    half = _lane_half((1, LANES))
    logits = lbl_ref[...]
    pe = jnp.exp(logits - jnp.max(logits, axis=0, keepdims=True))
    prob = pe / jnp.sum(pe, axis=0, keepdims=True)
    csum = prob[0:1, :]
    for d in range(1, layer + 1):
        csum = csum + prob[d:d + 1, :]
    lb = csum - prob[0:1, :]

    rowi = lax.broadcasted_iota(jnp.int32, (chunk, 1), 0)
    xor_rc = (lax.broadcasted_iota(jnp.int32, (chunk, chunk), 0)
              ^ lax.broadcasted_iota(jnp.int32, (chunk, chunk), 1))
    same_head = (lax.broadcasted_iota(jnp.int32, (LANES, LANES), 0) // HEAD_DIM
                 == lax.broadcasted_iota(jnp.int32, (LANES, LANES), 1) // HEAD_DIM)
    ones_bd = jnp.where(same_head, 1.0, 0.0).astype(BF16)
    gn = gn_ref[...]

    def chunk_body(c, state_t):
        r0 = pl.multiple_of(c * chunk, chunk)
        q = _silu(q_ref[pl.ds(r0, chunk), :].astype(F32))
        f = lb + (1.0 - lb) * _sigmoid(f_ref[pl.ds(r0, chunk), :])
        kc = 1.0 - f
        v = i_ref[pl.ds(r0, chunk), :].astype(F32)
        b = _cumsum_rows(jnp.log(f))
        b_scr[...] = b
        b_last = b_scr[chunk - 1:chunk, :]

        o = _dot_nt((q * jnp.exp(b)).astype(BF16), state_t.astype(BF16))
        k2 = (kc * jnp.exp(b_last - b)).astype(BF16)
        upd = _dot_tn(v.astype(BF16), k2)
        new_state = state_t * jnp.exp(b_last) + jnp.where(same_head, upd, 0.0)

        s_tot = [jnp.zeros((chunk, chunk), F32), jnp.zeros((chunk, chunk), F32)]
        m = chunk
        while m > SUB:
            hm = m // 2
            pieces = [jnp.broadcast_to(b_scr[blk * m + hm - 1:blk * m + hm, :], (m, LANES))
                      for blk in range(chunk // m)]
            ref = pieces[0] if len(pieces) == 1 else jnp.concatenate(pieces, axis=0)
            right = (rowi % m) >= hm
            qm = jnp.where(right, q * jnp.exp(jnp.where(right, b - ref, 0.0)), 0.0).astype(BF16)
            km = jnp.where(right, 0.0, kc * jnp.exp(jnp.where(right, 0.0, ref - b))).astype(BF16)
            for e in range(2):
                s = _dot_nt(jnp.where(half == e, qm, jnp.zeros_like(qm)), km)
                s_tot[e] = s_tot[e] + (s if m == chunk else jnp.where(xor_rc < m, s, 0.0))
            m = hm
        vb = v.astype(BF16)
        for e in range(2):
            o = o + _dot(s_tot[e].astype(BF16), jnp.where(half == e, vb, jnp.zeros_like(vb)))

        for d in range(SUB):
            ok = (rowi % SUB) >= d
            bs = b if d == 0 else pltpu.roll(b, d, 0)
            ks = kc if d == 0 else pltpu.roll(kc, d, 0)
            vs = v if d == 0 else pltpu.roll(v, d, 0)
            x = jnp.where(ok, q * ks * jnp.exp(jnp.where(ok, b - bs, 0.0)), 0.0)
            o = o + _dot(x.astype(BF16), ones_bd) * vs

        sq = o * o
        s0 = jnp.sum(jnp.where(half == 0, sq, 0.0), axis=-1, keepdims=True)
        s1 = jnp.sum(jnp.where(half == 1, sq, 0.0), axis=-1, keepdims=True)
        var = jnp.where(half == 0, s0, s1) * (1.0 / HEAD_DIM)
        y = o * lax.rsqrt(var + NORM_EPS) * gn * _silu(g_ref[pl.ds(r0, chunk), :].astype(F32))
        o_ref[pl.ds(r0, chunk), :] = y.astype(o_ref.dtype)
        return new_state

    lax.fori_loop(0, seq // chunk, chunk_body, jnp.zeros((LANES, LANES), F32))


def _hgrn(h_qig, h_f, lb_logits, gn, *, batch, seq, layer):
    ng = HGRN_HEADS // 2
    chunk = min(HGRN_CHUNK, seq)
    spec = lambda off: pl.BlockSpec((seq, LANES), lambda b, g: (b, off * ng + g))
    return pl.pallas_call(
        functools.partial(_hgrn_kernel, chunk=chunk, layer=layer),
        grid=(batch, ng),
        in_specs=[spec(0), spec(0), spec(1), spec(2),
                  pl.BlockSpec((lb_logits.shape[0], LANES), lambda b, g: (0, g)),
                  pl.BlockSpec((1, LANES), lambda b, g: (0, 0))],
        out_specs=pl.BlockSpec((seq, LANES), lambda b, g: (b, g)),
        out_shape=jax.ShapeDtypeStruct((batch * seq, ng * LANES), BF16),
        scratch_shapes=[pltpu.VMEM((chunk, LANES), F32)],
        compiler_params=_params("parallel", "parallel"),
        name="hgrn2",
    )(h_qig, h_f, h_qig, h_qig, lb_logits, gn)


def _pack_pairs(x):
    hw = x.shape[1] // 2
    hi = pltpu.bitcast(x[:, :hw].astype(BF16).astype(F32), jnp.int32)
    lo = pltpu.bitcast(x[:, hw:].astype(BF16).astype(F32), jnp.int32)
    return hi | lax.shift_right_logical(lo, 16)


def _unpack_pairs(w):
    hi = pltpu.bitcast(w & jnp.int32(-65536), F32)
    lo = pltpu.bitcast(lax.shift_left(w, 16), F32)
    return hi, lo


def _store_planes(ref, words):
    pw = words.shape[1] // SC_PLANES
    for h in range(SC_PLANES):
        ref[h] = words[:, h * pw:(h + 1) * pw]


def _load_planes(ref):
    return jnp.concatenate([ref[h] for h in range(SC_PLANES)], axis=1)


def _router_kernel(a1_ref, a2_ref, w1_ref, w2_ref, x_ref, g_ref, w_ref, xo_ref, o_ref, xn_ref):
    x = x_ref[...] + _dot(a1_ref[...], w1_ref[...]) + _dot(a2_ref[...], w2_ref[...])
    xo_ref[...] = x
    var = jnp.mean(x * x, axis=-1, keepdims=True)
    xn = x * lax.rsqrt(var + NORM_EPS) * g_ref[...]
    _store_planes(xn_ref, _pack_pairs(xn))
    x_hi = xn.astype(BF16)
    x_lo = (xn - x_hi.astype(F32)).astype(BF16)
    w = w_ref[...]
    w_hi = w.astype(BF16)
    w_lo = (w - w_hi.astype(F32)).astype(BF16)
    logits = _dot(x_hi, w_hi) + _dot(x_hi, w_lo) + _dot(x_lo, w_hi)
    lane = lax.broadcasted_iota(jnp.int32, logits.shape, 1)
    lg = jnp.where(lane < N_EXPERTS, logits, -jnp.inf)
    m1 = jnp.max(lg, axis=-1, keepdims=True)
    i1 = jnp.min(jnp.where(lg == m1, lane, LANES), axis=-1, keepdims=True)
    lg2 = jnp.where(lane == i1, -jnp.inf, lg)
    m2 = jnp.max(lg2, axis=-1, keepdims=True)
    i2 = jnp.min(jnp.where(lg2 == m2, lane, LANES), axis=-1, keepdims=True)
    t = jnp.exp(m2 - m1)
    den = 1.0 + t
    o_ref[...] = (jnp.where(lane == 0, i1.astype(F32), 0.0) + jnp.where(lane == 1, i2.astype(F32), 0.0)
                  + jnp.where(lane == 2, 1.0 / den, 0.0) + jnp.where(lane == 3, t / den, 0.0))


def _mix_router(a1, a2, w1, w2, x, g, w_pad):
    t, d = x.shape
    k1, k2 = a1.shape[1], a2.shape[1]
    tm = min(PROJ_ROW_TILE, t)
    return pl.pallas_call(
        _router_kernel,
        grid=(t // tm,),
        in_specs=[pl.BlockSpec((tm, k1), lambda i: (i, 0)),
                  pl.BlockSpec((tm, k2), lambda i: (i, 0)),
                  pl.BlockSpec((k1, d), lambda i: (0, 0)),
                  pl.BlockSpec((k2, d), lambda i: (0, 0)),
                  pl.BlockSpec((tm, d), lambda i: (i, 0)),
                  pl.BlockSpec((1, d), lambda i: (0, 0)),
                  pl.BlockSpec((d, LANES), lambda i: (0, 0))],
        out_specs=[pl.BlockSpec((tm, d), lambda i: (i, 0)),
                   pl.BlockSpec((tm, LANES), lambda i: (i, 0)),
                   pl.BlockSpec((SC_PLANES, tm, d // 2 // SC_PLANES), lambda i: (0, i, 0))],
        out_shape=[jax.ShapeDtypeStruct((t, d), F32),
                   jax.ShapeDtypeStruct((t, LANES), F32),
                   jax.ShapeDtypeStruct((SC_PLANES, t, d // 2 // SC_PLANES), jnp.int32)],
        compiler_params=_params("parallel"),
        name="mix_router",
    )(a1, a2, w1, w2, x, g, w_pad)


def _sc_mesh():
    return plsc.VectorSubcoreMesh(core_axis_name="core", subcore_axis_name="subcore")


def _sc_scatter_rows(x, idx0, idx1, n_out):
    n, d = x.shape

    @functools.partial(pl.kernel, out_type=jax.ShapeDtypeStruct((n_out, d), x.dtype), mesh=_sc_mesh(),
                       scratch_types=[])
    def scatter(x_hbm, i0_hbm, i1_hbm, o_hbm):
        def body(x_vmem, i0_vmem, i1_vmem):
            pltpu.sync_copy(x_vmem, o_hbm.at[i0_vmem.at[0]])
            pltpu.sync_copy(x_vmem, o_hbm.at[i1_vmem.at[0]])

        pltpu.emit_pipeline(
            body,
            grid=(n // SC_WINDOW,),
            in_specs=[pl.BlockSpec((SC_WINDOW, d), index_map=lambda i: (i, 0)),
                      pl.BlockSpec((1, SC_WINDOW), index_map=lambda i: (0, i)),
                      pl.BlockSpec((1, SC_WINDOW), index_map=lambda i: (0, i))],
            out_specs=[],
            core_axis_name=("core", "subcore"),
            dimension_semantics=(pltpu.PARALLEL,),
        )(x_hbm, i0_hbm, i1_hbm)

    return scatter(x, idx0.reshape(1, n), idx1.reshape(1, n))


def _sc_gather_rows(x, idx):
    n = idx.shape[0]
    d = x.shape[1]

    @functools.partial(pl.kernel, out_type=jax.ShapeDtypeStruct((n, d), x.dtype), mesh=_sc_mesh(),
                       scratch_types=[])
    def gather(x_hbm, i_hbm, o_hbm):
        def body(i_vmem, o_vmem):
            pltpu.sync_copy(x_hbm.at[i_vmem.at[0]], o_vmem)

        pltpu.emit_pipeline(
            body,
            grid=(n // SC_WINDOW,),
            in_specs=[pl.BlockSpec((1, SC_WINDOW), index_map=lambda i: (0, i))],
            out_specs=[pl.BlockSpec((SC_WINDOW, d), index_map=lambda i: (i, 0))],
            core_axis_name=("core", "subcore"),
            dimension_semantics=(pltpu.PARALLEL,),
        )(i_hbm, o_hbm)

    return gather(x, idx.reshape(1, n))


def _moe_ffn_packed_kernel(te_ref, act_ref, x_ref, wg_ref, wu_ref, wd_ref, o_ref, xn_ref, acc_ref):
    i = pl.program_id(0)
    f = pl.program_id(1)
    hw = xn_ref.shape[1] // 2

    @pl.when(act_ref[i] == 1)
    def _():
        @pl.when(f == 0)
        def _():
            hi, lo = _unpack_pairs(_load_planes(x_ref))
            xn_ref[:, :hw] = hi.astype(BF16)
            xn_ref[:, hw:] = lo.astype(BF16)
            acc_ref[...] = jnp.zeros_like(acc_ref)

        x = xn_ref[...]
        mid = (_silu(_dot(x, wg_ref[0])) * _dot(x, wu_ref[0])).astype(BF16)
        acc_ref[...] += _dot(mid, wd_ref[0])

        @pl.when(f == pl.num_programs(1) - 1)
        def _():
            _store_planes(o_ref, _pack_pairs(acc_ref[...]))


def _moe_ffn_packed(tile_e, tile_act, xs, wg, wu, wd, *, tf):
    _, p, pw = xs.shape
    d = 2 * pw * SC_PLANES
    ff = wg.shape[2]
    tm = MOE_ROW_TILE
    return pl.pallas_call(
        _moe_ffn_packed_kernel,
        grid_spec=pltpu.PrefetchScalarGridSpec(
            num_scalar_prefetch=2,
            grid=(p // tm, ff // tf),
            in_specs=[pl.BlockSpec((SC_PLANES, tm, pw), lambda i, f, te, ta: (0, i, 0)),
                      pl.BlockSpec((1, d, tf), lambda i, f, te, ta: (te[i], 0, f * ta[i])),
                      pl.BlockSpec((1, d, tf), lambda i, f, te, ta: (te[i], 0, f * ta[i])),
                      pl.BlockSpec((1, tf, d), lambda i, f, te, ta: (te[i], f * ta[i], 0))],
            out_specs=pl.BlockSpec((SC_PLANES, tm, pw), lambda i, f, te, ta: (0, i, 0)),
            scratch_shapes=[pltpu.VMEM((tm, d), BF16), pltpu.VMEM((tm, d), F32)],
        ),
        out_shape=jax.ShapeDtypeStruct((SC_PLANES, p, pw), jnp.int32),
        compiler_params=_params("arbitrary", "arbitrary"),
        name="moe_ffn",
    )(tile_e, tile_act, xs, wg, wu, wd)


def _moe_sum_kernel(x_ref, info_ref, y_ref, gf_ref, o_ref, *, final_norm):
    hw = x_ref.shape[1] // 2
    info = info_ref[...]
    g1 = info[:, 2:3]
    g2 = info[:, 3:4]
    hi1, lo1 = _unpack_pairs(jnp.concatenate([y_ref[h] for h in range(SC_PLANES)], axis=1))
    hi2, lo2 = _unpack_pairs(jnp.concatenate([y_ref[SC_PLANES + h] for h in range(SC_PLANES)], axis=1))
    x = x_ref[...]
    out_a = x[:, :hw] + g1 * hi1 + g2 * hi2
    out_b = x[:, hw:] + g1 * lo1 + g2 * lo2
    if final_norm:
        ss = jnp.sum(out_a * out_a, axis=-1, keepdims=True) + jnp.sum(out_b * out_b, axis=-1, keepdims=True)
        r = lax.rsqrt(ss * (1.0 / (2 * hw)) + NORM_EPS)
        gf = gf_ref[...]
        out_a = out_a * r * gf[:, :hw]
        out_b = out_b * r * gf[:, hw:]
    o_ref[:, :hw] = out_a
    o_ref[:, hw:] = out_b


def _moe_sum(x, info, y_pairs, gf, *, final_norm):
    t, d = x.shape
    tm = min(PROJ_ROW_TILE, t)
    return pl.pallas_call(
        functools.partial(_moe_sum_kernel, final_norm=final_norm),
        grid=(t // tm,),
        in_specs=[pl.BlockSpec((tm, d), lambda i: (i, 0)),
                  pl.BlockSpec((tm, LANES), lambda i: (i, 0)),
                  pl.BlockSpec((2 * SC_PLANES, tm, y_pairs.shape[2]), lambda i: (0, i, 0)),
                  pl.BlockSpec((1, d), lambda i: (0, 0))],
        out_specs=pl.BlockSpec((tm, d), lambda i: (i, 0)),
        out_shape=jax.ShapeDtypeStruct((t, d), F32),
        compiler_params=_params("parallel"),
        name="moe_sum",
    )(x, info, y_pairs, gf)


def _moe_positions(info, t):
    tm = MOE_ROW_TILE
    n_e = N_EXPERTS
    a = 2 * t
    p = a + n_e * tm
    e_flat = info[:, 0:2].astype(jnp.int32).reshape(a)
    onehot = (e_flat[:, None] == jnp.arange(n_e, dtype=jnp.int32)[None, :]).astype(jnp.int32)
    csum = jnp.cumsum(onehot, axis=0)
    rank = jnp.sum(csum * onehot, axis=1) - 1
    counts = csum[-1]
    padded = ((counts + tm - 1) // tm) * tm
    ends = jnp.cumsum(padded)
    starts = ends - padded
    pos = jnp.sum(onehot * starts[None, :], axis=1) + rank
    tile_start = jnp.arange(p // tm, dtype=jnp.int32) * tm
    tile_e = jnp.minimum(jnp.sum((tile_start[:, None] >= ends[None, :]).astype(jnp.int32), axis=1), n_e - 1)
    tile_act = (tile_start < ends[-1]).astype(jnp.int32)
    return pos.astype(jnp.int32).reshape(t, 2), tile_e.astype(jnp.int32), tile_act, p


def _mix_ffn_kernel(a1_ref, a2_ref, w1_ref, w2_ref, x_ref, g_ref, wg_ref, wu_ref, wd_ref, gf_ref, o_ref,
                    *, final_norm):
    x = x_ref[...] + _dot(a1_ref[...], w1_ref[...]) + _dot(a2_ref[...], w2_ref[...])
    var = jnp.mean(x * x, axis=-1, keepdims=True)
    xn = (x * lax.rsqrt(var + NORM_EPS) * g_ref[...]).astype(BF16)
    mid = (_silu(_dot(xn, wg_ref[...])) * _dot(xn, wu_ref[...])).astype(BF16)
    out = x + _dot(mid, wd_ref[...])
    if final_norm:
        var = jnp.mean(out * out, axis=-1, keepdims=True)
        out = out * lax.rsqrt(var + NORM_EPS) * gf_ref[...]
    o_ref[...] = out


def _resident(shape):
    return pl.BlockSpec(shape, lambda i: (0,) * len(shape), pipeline_mode=pl.Buffered(1))


def _mix_ffn(a1, a2, w1, w2, x, g, wg, wu, wd, gf, *, final_norm):
    t, d = x.shape
    ff = wg.shape[1]
    k1, k2 = a1.shape[1], a2.shape[1]
    tm = ROW_TILE
    return pl.pallas_call(
        functools.partial(_mix_ffn_kernel, final_norm=final_norm),
        grid=(t // tm,),
        in_specs=[
            pl.BlockSpec((tm, k1), lambda i: (i, 0)),
            pl.BlockSpec((tm, k2), lambda i: (i, 0)),
            _resident((k1, d)),
            _resident((k2, d)),
            pl.BlockSpec((tm, d), lambda i: (i, 0)),
            _resident((1, d)),
            _resident((d, ff)),
            _resident((d, ff)),
            _resident((ff, d)),
            _resident((1, d)),
        ],
        out_specs=pl.BlockSpec((tm, d), lambda i: (i, 0)),
        out_shape=jax.ShapeDtypeStruct((t, d), F32),
        compiler_params=_params("parallel"),
        name="mix_ffn",
    )(a1, a2, w1, w2, x, g, wg, wu, wd, gf)


def _rope_tables(seq):
    half = HEAD_DIM // 2
    inv = ROPE_THETA ** (-jnp.arange(half, dtype=F32) * 2.0 / HEAD_DIM)
    ang = jnp.arange(seq, dtype=F32)[:, None] * inv[None, :]
    cos = jnp.cos(ang)
    sin = jnp.sin(ang)
    cos_t = jnp.concatenate([cos, cos, cos, cos], axis=-1)
    sin_t = jnp.concatenate([-sin, sin, -sin, sin], axis=-1)
    return cos_t, sin_t


def _dup_heads(w):
    d, n = w.shape
    w = w.reshape(d, n // HEAD_DIM, 1, HEAD_DIM)
    return jnp.broadcast_to(w, (d, n // HEAD_DIM, 2, HEAD_DIM)).reshape(d, 2 * n)


def kernel(x, ln_mix_e, w_in_e, b_fox_f, w_out_e, ln_ffn_e, w_ffn_gate, w_ffn_up, w_ffn_down,
           ln_mix_o, w_in_o, hgrn_lb_logits, hgrn_norm_g, diff_lambda, diff_norm_g, w_out_o,
           ln_ffn_o, w_router, w_exp_gate, w_exp_up, w_exp_down, ln_final):
    batch, seq, d = x.shape
    t = batch * seq
    depth = ln_mix_e.shape[0] + ln_mix_o.shape[0]
    xs = x.reshape(t, d)
    cos_t, sin_t = _rope_tables(seq)
    row = lambda v: v.reshape(1, -1)
    fw = FOX_HEADS * HEAD_DIM
    dw = DSA_HEADS * HEAD_DIM
    kvw = DSA_KV_HEADS * HEAD_DIM
    hw = HGRN_HEADS * HEAD_DIM
    qw = DIFF_HEADS * 2 * DIFF_DIM

    for layer in range(depth):
        j = layer // 2
        if layer % 2 == 0:
            w = w_in_e[j]
            o = 0
            fq, fk, fv = w[:, o:o + fw], w[:, o + fw:o + 2 * fw], w[:, o + 2 * fw:o + 3 * fw]
            o += 3 * fw
            ffw = w[:, o:o + FOX_HEADS]
            o += FOX_HEADS
            dq = w[:, o:o + dw]
            o += dw
            dk = w[:, o:o + kvw]
            o += kvw
            dv = w[:, o:o + kvw]
            o += kvw
            iqw = w[:, o:o + IDX_HEADS * IDX_DIM]
            o += IDX_HEADS * IDX_DIM
            ikw = w[:, o:o + IDX_DIM]
            o += IDX_DIM
            iww = w[:, o:o + IDX_HEADS]
            w1 = jnp.concatenate([fq, fk, fv, _dup_heads(dv)], axis=1).astype(BF16)
            w2 = jnp.concatenate([dq, _dup_heads(dk), iqw, _dup_heads(ikw)], axis=1).astype(BF16)
            w3 = jnp.concatenate([ffw, iww, jnp.zeros((d, LANES - FOX_HEADS - IDX_HEADS), F32)], axis=1).astype(BF16)
            g = row(ln_mix_e[j])
            p1, p2, z = _rms_proj(xs, g, cos_t, sin_t, (w1, w2, w3), (False, True, False), (BF16, BF16, F32), seq=seq)
            b_pad = jnp.concatenate([b_fox_f[j], jnp.zeros((LANES - FOX_HEADS,), F32)]).reshape(1, LANES)
            ccol, crow = _fox_gate(z, b_pad, batch=batch, seq=seq)
            fox_o = _fox_attention(p1, ccol, crow, batch=batch, seq=seq)
            dsa_o = _dsa_attention(p2, p1, z, batch=batch, seq=seq)
            wo = w_out_e[j].astype(BF16)
            xs = _mix_ffn(fox_o, dsa_o, wo[:fw], wo[fw:], xs, row(ln_ffn_e[j]), w_ffn_gate[j].astype(BF16),
                          w_ffn_up[j].astype(BF16), w_ffn_down[j].astype(BF16), row(ln_final),
                          final_norm=(layer == depth - 1))
        else:
            w = w_in_o[j]
            lam_init = 0.8 - 0.6 * math.exp(-0.3 * layer)
            g = row(ln_mix_o[j])
            w_qig = jnp.concatenate([w[:, :hw], w[:, 2 * hw:4 * hw]], axis=1).astype(BF16)
            w_f = w[:, hw:2 * hw].astype(BF16)
            w_qk = w[:, 4 * hw:4 * hw + 2 * qw].astype(BF16)
            w_v = w[:, 4 * hw + 2 * qw:].astype(BF16)
            h_qig, h_f, pqk, pv = _rms_proj(xs, g, cos_t, sin_t, (w_qig, w_f, w_qk, w_v),
                                            (False, False, True, False), (BF16, F32, BF16, BF16), seq=seq)
            gn = jnp.concatenate([hgrn_norm_g[j], hgrn_norm_g[j]]).reshape(1, LANES)
            o_h = _hgrn(h_qig, h_f, hgrn_lb_logits, gn, batch=batch, seq=seq, layer=layer)
            o_d = _diff_attention(pqk, pv, diff_lambda[j], row(diff_norm_g[j]), batch=batch, seq=seq,
                                  lam_init=lam_init)
            wo = w_out_o[j].astype(BF16)
            wr = jnp.concatenate([w_router[j], jnp.zeros((d, LANES - N_EXPERTS), F32)], axis=1)
            xs, info, xn = _mix_router(o_h, o_d, wo[:hw], wo[hw:], xs, row(ln_ffn_o[j]), wr)
            pos, tile_e, tile_act, n_rows = _moe_positions(info, t)
            pw = xn.shape[2]
            plane = jnp.arange(SC_PLANES, dtype=jnp.int32)[:, None] * n_rows
            dst0 = (plane + pos[None, :, 0]).reshape(SC_PLANES * t)
            dst1 = (plane + pos[None, :, 1]).reshape(SC_PLANES * t)
            rows = _sc_scatter_rows(xn.reshape(SC_PLANES * t, pw), dst0, dst1, SC_PLANES * n_rows)
            y = _moe_ffn_packed(tile_e, tile_act, rows.reshape(SC_PLANES, n_rows, pw), w_exp_gate[j].astype(BF16),
                                w_exp_up[j].astype(BF16), w_exp_down[j].astype(BF16), tf=1792)
            src = jnp.concatenate([(plane + pos[None, :, s]).reshape(SC_PLANES * t) for s in range(2)])
            y_pairs = _sc_gather_rows(y.reshape(SC_PLANES * n_rows, pw), src).reshape(2 * SC_PLANES, t, pw)
            xs = _moe_sum(xs, info, y_pairs, row(ln_final), final_norm=(layer == depth - 1))
    if depth % 2 == 1:
        pass
    return xs.reshape(batch, seq, d)
```
